```python
import math, functools
import jax, jax.numpy as jnp
from jax import lax
import numpy as np

D_MODEL = 2048
BATCH = 1
SEQ = 8192
DEPTH = 1
DEC_BATCH = 32
DEC_SEQ = 1
PAST_LEN = 16384
PAGE_SIZE = 128

HEAD_DIM = 128
ATT_HEADS = D_MODEL // 256
N_KV_HEADS = ATT_HEADS // 2
KV_GROUP = ATT_HEADS // N_KV_HEADS
ATT_WIDTH = ATT_HEADS * HEAD_DIM
KV_WIDTH = N_KV_HEADS * HEAD_DIM
MOBA_BLOCK = 256
MOBA_TOPK = 3
Q_BLOCK = 128
D_INNER = D_MODEL // 2
SSM_HEAD_DIM = 64
SSM_HEADS = D_INNER // SSM_HEAD_DIM
SSM_GROUPS = 2
D_STATE = 128
CONV_WIDTH = 4
CONV_DIM = D_INNER + 2 * SSM_GROUPS * D_STATE
SSD_CHUNK = 128
MIX_WIDTH = ATT_WIDTH + D_INNER
IN_WIDTH = ATT_WIDTH + 2 * KV_WIDTH + D_INNER + CONV_DIM + SSM_HEADS
D_FF = -(-8 * D_MODEL // (3 * 256)) * 256
EPS = 1e-6

kernel_name = "hymba_moba_ssd_decoder_step"


def rms_norm(x, g):
    xf = x.astype(jnp.float32)
    y = xf * lax.rsqrt(jnp.mean(xf * xf, axis=-1, keepdims=True) + EPS)
    return (y * g.astype(jnp.float32)).astype(x.dtype)


def gated_group_rms(y, z, g):
    yz = y.astype(jnp.float32) * jax.nn.silu(z.astype(jnp.float32))
    sh = yz.shape
    yg = yz.reshape(sh[:-1] + (SSM_GROUPS, D_INNER // SSM_GROUPS))
    yg = yg * lax.rsqrt(jnp.mean(yg * yg, axis=-1, keepdims=True) + EPS)
    return (yg.reshape(sh) * g.astype(jnp.float32)).astype(y.dtype)


def alibi_slopes(n):
    return jnp.exp2(-8.0 * jnp.arange(1, n + 1, dtype=jnp.float32) / n)


def moba_attend(q, q_pos, k_own, v_own, own_pos, sel):
    n, nq, h, dh = q.shape
    lo = k_own.shape[1]
    scale = dh ** -0.5
    slopes = alibi_slopes(h)
    qg = q.reshape(n, nq, N_KV_HEADS, KV_GROUP, dh)
    s_own = jnp.einsum("nqkgd,nlkd->nqkgl", qg, k_own).astype(jnp.float32).reshape(n, nq, h, lo) * scale
    d_own = (q_pos[:, None] - own_pos[None, :]).astype(jnp.float32)[None, :, None, :]
    s_own = jnp.where(d_own >= 0, s_own - slopes[None, None, :, None] * d_own, -jnp.inf)
    if sel is None:
        p_own = jax.nn.softmax(s_own, axis=-1).astype(v_own.dtype)
        o = jnp.zeros((n, nq, h, dh), v_own.dtype)
    else:
        k_sel, v_sel, sel_pos, sel_valid = sel
        ks = k_sel.shape[3]
        s_sel = jnp.einsum("nqhd,nqhkjd->nqhkj", q, k_sel).astype(jnp.float32) * scale
        d_sel = (q_pos[None, :, None, None, None] - sel_pos).astype(jnp.float32)
        s_sel = s_sel - slopes[None, None, :, None, None] * d_sel
        if sel_valid is not None:
            s_sel = jnp.where(sel_valid[..., None], s_sel, -jnp.inf)
        s = jnp.concatenate([s_sel.reshape(n, nq, h, ks * MOBA_BLOCK), s_own], axis=-1)
        p = jax.nn.softmax(s, axis=-1).astype(v_own.dtype)
        p_sel = p[..., :ks * MOBA_BLOCK].reshape(n, nq, h, ks, MOBA_BLOCK)
        p_own = p[..., ks * MOBA_BLOCK:]
        o = jnp.einsum("nqhkj,nqhkjd->nqhd", p_sel, v_sel)
    o_own = jnp.einsum("nqkgl,nlkd->nqkgd", p_own.reshape(n, nq, N_KV_HEADS, KV_GROUP, lo), v_own)
    return o + o_own.reshape(n, nq, h, dh)


def moba_prompt(q, k, v):
    n, L, h, dh = q.shape
    n_blk = -(-L // MOBA_BLOCK)
    pad = n_blk * MOBA_BLOCK - L
    k_blk = jnp.pad(k, ((0, 0), (0, pad), (0, 0), (0, 0))).reshape(n, n_blk, MOBA_BLOCK, N_KV_HEADS, dh)
    v_blk = jnp.pad(v, ((0, 0), (0, pad), (0, 0), (0, 0))).reshape(n, n_blk, MOBA_BLOCK, N_KV_HEADS, dh)
    k_mean = jnp.mean(k_blk.astype(jnp.float32), axis=2)
    ks = min(MOBA_TOPK, n_blk - 1)
    bidx = jnp.arange(n)[:, None, None, None]
    kvh = (jnp.arange(h) // KV_GROUP)[None, None, :, None]
    blk_ids = jnp.arange(n_blk)
    offs = jnp.arange(MOBA_BLOCK)

    def one_block(ci):
        start = ci * Q_BLOCK
        qc = lax.dynamic_slice_in_dim(q, start, Q_BLOCK, axis=1)
        q_pos = start + jnp.arange(Q_BLOCK)
        ob = start // MOBA_BLOCK
        k_own = lax.dynamic_index_in_dim(k_blk, ob, axis=1, keepdims=False)
        v_own = lax.dynamic_index_in_dim(v_blk, ob, axis=1, keepdims=False)
        own_pos = ob * MOBA_BLOCK + offs
        sel = None
        if ks > 0:
            qg = qc.reshape(n, Q_BLOCK, N_KV_HEADS, KV_GROUP, dh).astype(jnp.float32)
            gate = jnp.einsum("nqkgd,nbkd->nqkgb", qg, k_mean).reshape(n, Q_BLOCK, h, n_blk)
            gate = jnp.where(blk_ids < ob, gate, -jnp.inf)
            _, idx = lax.top_k(gate, ks)
            sel = (k_blk[bidx, idx, :, kvh], v_blk[bidx, idx, :, kvh],
                   idx[..., None] * MOBA_BLOCK + offs, idx < ob)
        return moba_attend(qc, q_pos, k_own, v_own, own_pos, sel)

    o = lax.map(one_block, jnp.arange(L // Q_BLOCK))
    return jnp.moveaxis(o, 0, 1).reshape(n, L, h, dh)


def moba_sample(q, k_new, v_new, k_pool, v_pool, page_table):
    n, nq, h, dh = q.shape
    n_pages = page_table.shape[1]
    past = n_pages * PAGE_SIZE
    ppb = MOBA_BLOCK // PAGE_SIZE
    n_past_blk = past // MOBA_BLOCK
    own_start = n_past_blk * MOBA_BLOCK
    own_rows = past - own_start
    assert own_rows + nq <= MOBA_BLOCK
    q_pos = past + jnp.arange(nq)
    own_pt = page_table[:, own_start // PAGE_SIZE:]
    k_own = jnp.concatenate([k_pool[own_pt].reshape(n, own_rows, N_KV_HEADS, dh).astype(k_new.dtype), k_new], axis=1)
    v_own = jnp.concatenate([v_pool[own_pt].reshape(n, own_rows, N_KV_HEADS, dh).astype(v_new.dtype), v_new], axis=1)
    own_pos = own_start + jnp.arange(own_rows + nq)
    ks = min(MOBA_TOPK, n_past_blk)
    sel = None
    if ks > 0:
        k_past = k_pool[page_table[:, :n_past_blk * ppb]].reshape(n, n_past_blk, MOBA_BLOCK, N_KV_HEADS, dh)
        k_mean = jnp.mean(k_past.astype(jnp.float32), axis=2)
        qg = q.reshape(n, nq, N_KV_HEADS, KV_GROUP, dh).astype(jnp.float32)
        gate = jnp.einsum("nqkgd,nbkd->nqkgb", qg, k_mean).reshape(n, nq, h, n_past_blk)
        _, idx = lax.top_k(gate, ks)
        logical = idx[..., None] * ppb + jnp.arange(ppb)
        phys = page_table[jnp.arange(n)[:, None, None, None, None], logical]
        kvh = (jnp.arange(h) // KV_GROUP)[None, None, :, None, None]
        k_sel = k_pool[phys, :, kvh].reshape(n, nq, h, ks, MOBA_BLOCK, dh).astype(q.dtype)
        v_sel = v_pool[phys, :, kvh].reshape(n, nq, h, ks, MOBA_BLOCK, dh).astype(v_new.dtype)
        sel = (k_sel, v_sel, idx[..., None] * MOBA_BLOCK + jnp.arange(MOBA_BLOCK), None)
    return moba_attend(q, q_pos, k_own, v_own, own_pos, sel)


def causal_conv(xbc, buf, w, b):
    L = xbc.shape[1]
    xp = jnp.concatenate([buf.astype(xbc.dtype), xbc], axis=1)
    out = b + xp[:, 0:L] * w[0]
    for i in range(1, CONV_WIDTH):
        out = out + xp[:, i:i + L] * w[i]
    return out, xp[:, L:]


def ssd_scan(x, dt, a, bm, cm, h0):
    n, L, h, p = x.shape
    g, s = bm.shape[2], bm.shape[3]
    r = h // g
    cs = min(SSD_CHUNK, L)
    nc = L // cs
    f32 = jnp.float32
    x = x.reshape(n, nc, cs, g, r, p).astype(f32)
    dt = dt.reshape(n, nc, cs, g, r)
    bm = bm.reshape(n, nc, cs, g, s).astype(f32)
    cm = cm.reshape(n, nc, cs, g, s).astype(f32)
    acum = jnp.cumsum(dt * a.reshape(g, r), axis=2)
    xdt = x * dt[..., None]
    causal = jnp.tril(jnp.ones((cs, cs), bool))[:, :, None, None]
    seg = acum[:, :, :, None] - acum[:, :, None]
    lmat = jnp.exp(jnp.where(causal, seg, -jnp.inf))
    cb = jnp.einsum("nclgs,ncmgs->nclmg", cm, bm)
    y_intra = jnp.einsum("nclmg,nclmgr,ncmgrp->nclgrp", cb, lmat, xdt)
    decay_end = jnp.exp(acum[:, :, -1:] - acum)
    states = jnp.einsum("ncmgs,ncmgr,ncmgrp->ncgrps", bm, decay_end, xdt)
    chunk_decay = jnp.exp(acum[:, :, -1])

    def step(hc, inp):
        st, dec = inp
        return dec[..., None, None] * hc + st, hc

    h_T, h_in = lax.scan(step, h0.astype(f32).reshape(n, g, r, p, s),
                         (jnp.moveaxis(states, 1, 0), jnp.moveaxis(chunk_decay, 1, 0)))
    h_in = jnp.moveaxis(h_in, 0, 1)
    y_inter = jnp.einsum("nclgs,nclgr,ncgrps->nclgrp", cm, jnp.exp(acum), h_in)
    return (y_intra + y_inter).reshape(n, L, h, p), h_T.reshape(n, h, p, s)


def trunk_layer(x, c, conv_buf, ssm_h, attn_fn, w_ada, b_ada, g_mix, w_in, q_gain, k_gain, g_att_out,
                conv_w, conv_b, dt_bias, a_log, d_skip, g_ssm_out, w_out, g_ffn, w_gate, w_up, w_down):
    n, L, _ = x.shape
    mod = (jax.nn.silu(c) @ w_ada + b_ada)[:, None, :]
    sh_m, sc_m, gt_m, sh_f, sc_f, gt_f = jnp.split(mod, 6, axis=-1)
    hn = rms_norm(x, g_mix) * (1 + sc_m) + sh_m
    proj = hn @ w_in
    cuts = [ATT_WIDTH, ATT_WIDTH + KV_WIDTH, ATT_WIDTH + 2 * KV_WIDTH,
            ATT_WIDTH + 2 * KV_WIDTH + D_INNER, ATT_WIDTH + 2 * KV_WIDTH + D_INNER + CONV_DIM]
    q, k, v, z, xbc, dt_raw = jnp.split(proj, cuts, axis=-1)
    q = rms_norm(q.reshape(n, L, ATT_HEADS, HEAD_DIM), q_gain)
    k = rms_norm(k.reshape(n, L, N_KV_HEADS, HEAD_DIM), k_gain)
    v = v.reshape(n, L, N_KV_HEADS, HEAD_DIM)
    o_att = rms_norm(attn_fn(q, k, v).reshape(n, L, ATT_WIDTH), g_att_out)
    xbc_c, new_buf = causal_conv(xbc, conv_buf, conv_w, conv_b)
    xbc_c = jax.nn.silu(xbc_c)
    xs, bm, cm = jnp.split(xbc_c, [D_INNER, D_INNER + SSM_GROUPS * D_STATE], axis=-1)
    dt = jax.nn.softplus(dt_raw.astype(jnp.float32) + dt_bias.astype(jnp.float32))
    a = -jnp.exp(a_log.astype(jnp.float32))
    xs = xs.reshape(n, L, SSM_HEADS, SSM_HEAD_DIM)
    y, new_h = ssd_scan(xs, dt, a, bm.reshape(n, L, SSM_GROUPS, D_STATE),
                        cm.reshape(n, L, SSM_GROUPS, D_STATE), ssm_h)
    y = (y + d_skip.astype(jnp.float32)[:, None] * xs.astype(jnp.float32)).astype(x.dtype)
    y = gated_group_rms(y.reshape(n, L, D_INNER), z, g_ssm_out)
    mix = jnp.concatenate([o_att, y], axis=-1) @ w_out
    x = x + gt_m * mix
    h2 = rms_norm(x, g_ffn) * (1 + sc_f) + sh_f
    x = x + gt_f * ((jax.nn.silu(h2 @ w_gate) * (h2 @ w_up)) @ w_down)
    return x, k, v, new_buf, new_h


def setup_inputs(seed: int = 0) -> dict:
    key = jax.random.key(seed)
    ks = jax.random.split(key, 32)
    f32 = jnp.float32
    n_pages = PAST_LEN // PAGE_SIZE
    n_pool = (5 * DEC_BATCH * n_pages) // 4

    def nrm(k, shape, scale):
        return jax.random.normal(k, shape, f32) * scale

    dt0 = jnp.exp(jax.random.uniform(ks[20], (DEPTH, SSM_HEADS), f32, math.log(1e-3), math.log(1e-1)))
    return {
        "x_prompt": nrm(ks[0], (BATCH, SEQ, D_MODEL), 1.0),
        "x_sample": nrm(ks[1], (DEC_BATCH, DEC_SEQ, D_MODEL), 1.0),
        "cache_k": nrm(ks[2], (DEPTH, n_pool, PAGE_SIZE, N_KV_HEADS, HEAD_DIM), 1.0),
        "cache_v": nrm(ks[3], (DEPTH, n_pool, PAGE_SIZE, N_KV_HEADS, HEAD_DIM), 1.0),
        "state_conv": nrm(ks[4], (DEPTH, DEC_BATCH, CONV_WIDTH - 1, CONV_DIM), 1.0),
        "state_ssm": nrm(ks[5], (DEPTH, DEC_BATCH, SSM_HEADS, SSM_HEAD_DIM, D_STATE), 0.5),
        "page_table": jax.random.permutation(ks[6], n_pool)[:DEC_BATCH * n_pages].reshape(DEC_BATCH, n_pages).astype(jnp.int32),
        "c_prompt": nrm(ks[7], (BATCH, D_MODEL), 1.0),
        "c_sample": nrm(ks[8], (DEC_BATCH, D_MODEL), 1.0),
        "w_ada": nrm(ks[9], (DEPTH, D_MODEL, 6 * D_MODEL), 0.5 * D_MODEL ** -0.5),
        "b_ada": nrm(ks[10], (DEPTH, 6 * D_MODEL), 0.02),
        "g_mix_norm": 1.0 + nrm(ks[11], (DEPTH, D_MODEL), 0.02),
        "w_in": nrm(ks[12], (DEPTH, D_MODEL, IN_WIDTH), D_MODEL ** -0.5),
        "q_gain": 1.0 + nrm(ks[13], (DEPTH, HEAD_DIM), 0.02),
        "k_gain": 1.0 + nrm(ks[14], (DEPTH, HEAD_DIM), 0.02),
        "g_att_out": 1.0 + nrm(ks[15], (DEPTH, ATT_WIDTH), 0.02),
        "conv_w": nrm(ks[16], (DEPTH, CONV_WIDTH, CONV_DIM), CONV_WIDTH ** -0.5),
        "conv_b": nrm(ks[17], (DEPTH, CONV_DIM), 0.02),
        "dt_bias": dt0 + jnp.log(-jnp.expm1(-dt0)),
        "a_log": jnp.log(jax.random.uniform(ks[18], (DEPTH, SSM_HEADS), f32, 1.0, 16.0)),
        "d_skip": 1.0 + nrm(ks[19], (DEPTH, SSM_HEADS), 0.02),
        "g_ssm_out": 1.0 + nrm(ks[21], (DEPTH, D_INNER), 0.02),
        "w_out": nrm(ks[22], (DEPTH, MIX_WIDTH, D_MODEL), MIX_WIDTH ** -0.5),
        "g_ffn_norm": 1.0 + nrm(ks[23], (DEPTH, D_MODEL), 0.02),
        "w_gate": nrm(ks[24], (DEPTH, D_MODEL, D_FF), D_MODEL ** -0.5),
        "w_up": nrm(ks[25], (DEPTH, D_MODEL, D_FF), D_MODEL ** -0.5),
        "w_down": nrm(ks[26], (DEPTH, D_FF, D_MODEL), D_FF ** -0.5),
    }


def reference(x_prompt, x_sample, cache_k, cache_v, state_conv, state_ssm, page_table, c_prompt, c_sample,
              w_ada, b_ada, g_mix_norm, w_in, q_gain, k_gain, g_att_out, conv_w, conv_b, dt_bias, a_log,
              d_skip, g_ssm_out, w_out, g_ffn_norm, w_gate, w_up, w_down):
    yp, ys = x_prompt, x_sample
    n_p = x_prompt.shape[0]
    kp_l, vp_l, cp_l, hp_l, ks_l, vs_l, cs_l, hs_l = [], [], [], [], [], [], [], []
    for l in range(DEPTH):
        w = (w_ada[l], b_ada[l], g_mix_norm[l], w_in[l], q_gain[l], k_gain[l], g_att_out[l], conv_w[l],
             conv_b[l], dt_bias[l], a_log[l], d_skip[l], g_ssm_out[l], w_out[l], g_ffn_norm[l],
             w_gate[l], w_up[l], w_down[l])
        buf0 = jnp.zeros((n_p, CONV_WIDTH - 1, CONV_DIM), x_prompt.dtype)
        h0 = jnp.zeros((n_p, SSM_HEADS, SSM_HEAD_DIM, D_STATE), jnp.float32)
        yp, kp, vp, bp, hp = trunk_layer(yp, c_prompt, buf0, h0, moba_prompt, *w)
        attn_s = functools.partial(moba_sample, k_pool=cache_k[l], v_pool=cache_v[l], page_table=page_table)
        ys, kn, vn, bn, hn = trunk_layer(ys, c_sample, state_conv[l], state_ssm[l], attn_s, *w)
        kp_l.append(kp); vp_l.append(vp); cp_l.append(bp); hp_l.append(hp)
        ks_l.append(kn); vs_l.append(vn); cs_l.append(bn); hs_l.append(hn)
    k_prompt = jnp.stack(kp_l); v_prompt = jnp.stack(vp_l)
    conv_prompt = jnp.stack(cp_l); ssm_prompt = jnp.stack(hp_l)
    k_sample = jnp.stack(ks_l); v_sample = jnp.stack(vs_l)
    conv_sample = jnp.stack(cs_l); ssm_sample = jnp.stack(hs_l)
    return (yp, ys, k_prompt, v_prompt, conv_prompt, ssm_prompt, k_sample, v_sample, conv_sample, ssm_sample)
```

```python
import functools

import jax
import jax.numpy as jnp
from jax import lax
from jax.experimental import pallas as pl
from jax.experimental.pallas import tpu as pltpu

F32 = jnp.float32
BF16 = jnp.bfloat16

HEAD_DIM = 128
ATT_HEADS = 8
N_KV_HEADS = 4
KV_GROUP = ATT_HEADS // N_KV_HEADS
ATT_WIDTH = ATT_HEADS * HEAD_DIM
KV_WIDTH = N_KV_HEADS * HEAD_DIM
MOBA_BLOCK = 256
MOBA_TOPK = 3
D_INNER = 1024
SSM_HEAD_DIM = 64
SSM_HEADS = D_INNER // SSM_HEAD_DIM
SSM_GROUPS = 2
D_STATE = 128
CONV_WIDTH = 4
CONV_DIM = D_INNER + 2 * SSM_GROUPS * D_STATE
SSD_CHUNK = 128
EPS = 1e-6
ATT_SCALE = HEAD_DIM ** -0.5

LANES = 128
FEAT_KBLK = 96
FEAT_KOFF = 97
FEAT_ONE_A = 98
FEAT_ONE_B = 99
NEG_BIG = -1e30

NT_DIMS = (((1,), (1,)), ((), ()))
VMEM_LIMIT = 56 * 1024 * 1024


def _params(*sem):
    return pltpu.CompilerParams(dimension_semantics=sem, vmem_limit_bytes=VMEM_LIMIT)


def _silu(x):
    return x / (1.0 + jnp.exp(-x))


def _softplus(x):
    return jnp.maximum(x, 0.0) + jnp.log1p(jnp.exp(-jnp.abs(x)))


def _split_bf16(x):
    hi = x.astype(BF16)
    lo = (x - hi.astype(F32)).astype(BF16)
    return hi, lo


def _dot3(a, b, dims):
    ah, al = _split_bf16(a)
    bh, bl = _split_bf16(b)
    d = lambda x, y: lax.dot_general(x, y, dims, preferred_element_type=F32)
    return d(ah, bh) + d(al, bh) + d(ah, bl)


def _alibi_slope(h):
    return 2.0 ** (-8.0 * (h + 1) / ATT_HEADS)


def _ada_kernel(c_ref, w_ref, b_ref, o_ref):
    a = _silu(c_ref[...]).astype(BF16)
    o_ref[...] = jnp.dot(a, w_ref[...].astype(BF16), preferred_element_type=F32) + b_ref[...]


def _ada(c_all, w, b):
    rows, d = c_all.shape
    n = w.shape[1]
    tn = 1024
    return pl.pallas_call(
        _ada_kernel,
        grid=(n // tn,),
        in_specs=[pl.BlockSpec((rows, d), lambda j: (0, 0)),
                  pl.BlockSpec((d, tn), lambda j: (0, j)),
                  pl.BlockSpec((1, tn), lambda j: (0, j))],
        out_specs=pl.BlockSpec((rows, tn), lambda j: (0, j)),
        out_shape=jax.ShapeDtypeStruct((rows, n), F32),
        compiler_params=_params("arbitrary"),
        name="ada_mod",
    )(c_all, w, b.reshape(1, n))


def _norm_mod_kernel(x_ref, g_ref, sc_ref, sh_ref, o_ref):
    x = x_ref[...]
    y = x * lax.rsqrt(jnp.mean(x * x, axis=-1, keepdims=True) + EPS)
    y = y * g_ref[...]
    o_ref[...] = (y * (1.0 + sc_ref[...]) + sh_ref[...]).astype(o_ref.dtype)


def _norm_mod(x, g, mod, sc_idx, sh_idx):
    m, d = x.shape
    mr = mod.shape[0]
    tm = min(256, m)
    mod_map = (lambda idx: (lambda i: (0, idx))) if mr == 1 else (lambda idx: (lambda i: (i, idx)))
    return pl.pallas_call(
        _norm_mod_kernel,
        grid=(m // tm,),
        in_specs=[pl.BlockSpec((tm, d), lambda i: (i, 0)),
                  pl.BlockSpec((1, d), lambda i: (0, 0)),
                  pl.BlockSpec((1 if mr == 1 else tm, d), mod_map(sc_idx)),
                  pl.BlockSpec((1 if mr == 1 else tm, d), mod_map(sh_idx))],
        out_specs=pl.BlockSpec((tm, d), lambda i: (i, 0)),
        out_shape=jax.ShapeDtypeStruct((m, d), BF16),
        compiler_params=_params("arbitrary"),
        name="norm_mod",
    )(x, g.reshape(1, d), mod, mod)


def _rms_rows_kernel(x_ref, g_ref, o_ref):
    x = x_ref[...]
    y = x * lax.rsqrt(jnp.mean(x * x, axis=-1, keepdims=True) + EPS)
    o_ref[...] = (y * g_ref[...]).astype(o_ref.dtype)


def _rms_rows(x, g):
    m, d = x.shape
    tm = min(512, m)
    return pl.pallas_call(
        _rms_rows_kernel,
        grid=(m // tm,),
        in_specs=[pl.BlockSpec((tm, d), lambda i: (i, 0)), pl.BlockSpec((1, d), lambda i: (0, 0))],
        out_specs=pl.BlockSpec((tm, d), lambda i: (i, 0)),
        out_shape=jax.ShapeDtypeStruct((m, d), BF16),
        compiler_params=_params("arbitrary"),
        name="att_out_norm",
    )(x, g.reshape(1, d))


def _mm_kernel(x_ref, w_ref, o_ref, wbf_ref):
    @pl.when(pl.program_id(1) == 0)
    def _():
        wbf_ref[...] = w_ref[...].astype(BF16)

    o_ref[...] = jnp.dot(x_ref[...], wbf_ref[...], preferred_element_type=F32).astype(o_ref.dtype)


def _mm(x, w, n_cols, *, tn, tm=512, name):
    m, k = x.shape
    tm = min(tm, m)
    return pl.pallas_call(
        _mm_kernel,
        grid=(n_cols // tn, m // tm),
        in_specs=[pl.BlockSpec((tm, k), lambda n, i: (i, 0)),
                  pl.BlockSpec((k, tn), lambda n, i: (0, n))],
        out_specs=pl.BlockSpec((tm, tn), lambda n, i: (i, n)),
        out_shape=jax.ShapeDtypeStruct((m, n_cols), F32),
        scratch_shapes=[pltpu.VMEM((k, tn), BF16)],
        compiler_params=_params("arbitrary", "arbitrary"),
        name=name,
    )(x, w)


def _outproj_kernel(a_ref, b_ref, w_ref, x_ref, gt_ref, o_ref, wbf_ref):
    @pl.when(pl.program_id(1) == 0)
    def _():
        wbf_ref[...] = w_ref[...].astype(BF16)

    ka = a_ref.shape[1]
    acc = jnp.dot(a_ref[...], wbf_ref[:ka, :], preferred_element_type=F32)
    acc = acc + jnp.dot(b_ref[...], wbf_ref[ka:, :], preferred_element_type=F32)
    o_ref[...] = x_ref[...] + gt_ref[...] * acc


def _outproj(a, b, w, x, mod, gt_idx):
    m, ka = a.shape
    kb = b.shape[1]
    d = w.shape[1]
    tn, tm = 512, min(512, m)
    mr = mod.shape[0]
    nb = d // tn
    gt_map = (lambda n, i: (0, gt_idx * nb + n)) if mr == 1 else (lambda n, i: (i, gt_idx * nb + n))
    return pl.pallas_call(
        _outproj_kernel,
        grid=(d // tn, m // tm),
        in_specs=[pl.BlockSpec((tm, ka), lambda n, i: (i, 0)),
                  pl.BlockSpec((tm, kb), lambda n, i: (i, 0)),
                  pl.BlockSpec((ka + kb, tn), lambda n, i: (0, n)),
                  pl.BlockSpec((tm, tn), lambda n, i: (i, n)),
                  pl.BlockSpec((1 if mr == 1 else tm, tn), gt_map)],
        out_specs=pl.BlockSpec((tm, tn), lambda n, i: (i, n)),
        out_shape=jax.ShapeDtypeStruct((m, d), F32),
        scratch_shapes=[pltpu.VMEM((ka + kb, tn), BF16)],
        compiler_params=_params("arbitrary", "arbitrary"),
        name="out_proj",
    )(a, b, w, x, mod)


def _gateup_kernel(x_ref, wg_ref, wu_ref, o_ref, wgb_ref, wub_ref):
    @pl.when(pl.program_id(1) == 0)
    def _():
        wgb_ref[...] = wg_ref[...].astype(BF16)
        wub_ref[...] = wu_ref[...].astype(BF16)

    x = x_ref[...]
    g = jnp.dot(x, wgb_ref[...], preferred_element_type=F32)
    u = jnp.dot(x, wub_ref[...], preferred_element_type=F32)
    o_ref[...] = (_silu(g) * u).astype(o_ref.dtype)


def _gateup(x, wg, wu):
    m, k = x.shape
    f = wg.shape[1]
    tn, tm = 512, min(512, m)
    return pl.pallas_call(
        _gateup_kernel,
        grid=(f // tn, m // tm),
        in_specs=[pl.BlockSpec((tm, k), lambda n, i: (i, 0)),
                  pl.BlockSpec((k, tn), lambda n, i: (0, n)),
                  pl.BlockSpec((k, tn), lambda n, i: (0, n))],
        out_specs=pl.BlockSpec((tm, tn), lambda n, i: (i, n)),
        out_shape=jax.ShapeDtypeStruct((m, f), BF16),
        scratch_shapes=[pltpu.VMEM((k, tn), BF16), pltpu.VMEM((k, tn), BF16)],
        compiler_params=_params("arbitrary", "arbitrary"),
        name="ffn_gate_up",
    )(x, wg, wu)


def _down_kernel(h_ref, w_ref, x_ref, gt_ref, o_ref, wbf_ref):
    @pl.when(pl.program_id(1) == 0)
    def _():
        wbf_ref[...] = w_ref[...].astype(BF16)

    acc = jnp.dot(h_ref[...], wbf_ref[...], preferred_element_type=F32)
    o_ref[...] = x_ref[...] + gt_ref[...] * acc


def _down(h, w, x, mod, gt_idx):
    m, f = h.shape
    d = w.shape[1]
    tn, tm = 512, min(256, m)
    mr = mod.shape[0]
    nb = d // tn
    gt_map = (lambda n, i: (0, gt_idx * nb + n)) if mr == 1 else (lambda n, i: (i, gt_idx * nb + n))
    return pl.pallas_call(
        _down_kernel,
        grid=(d // tn, m // tm),
        in_specs=[pl.BlockSpec((tm, f), lambda n, i: (i, 0)),
                  pl.BlockSpec((f, tn), lambda n, i: (0, n)),
                  pl.BlockSpec((tm, tn), lambda n, i: (i, n)),
                  pl.BlockSpec((1 if mr == 1 else tm, tn), gt_map)],
        out_specs=pl.BlockSpec((tm, tn), lambda n, i: (i, n)),
        out_shape=jax.ShapeDtypeStruct((m, d), F32),
        scratch_shapes=[pltpu.VMEM((f, tn), BF16)],
        compiler_params=_params("arbitrary", "arbitrary"),
        name="ffn_down",
    )(h, w, x, mod)


def _head_norm(x, gain):
    return x * lax.rsqrt(jnp.mean(x * x, axis=-1, keepdims=True) + EPS) * gain


def _top3(g, lane):
    sel = jnp.zeros(g.shape, jnp.int32)
    lane_f = lane.astype(F32)
    firsts = []
    for _ in range(MOBA_TOPK):
        mx = jnp.max(g, axis=1, keepdims=True)
        ismax = jnp.logical_and(g == mx, mx > -jnp.inf)
        first = jnp.min(jnp.where(ismax, lane_f, float(LANES)), axis=1, keepdims=True)
        pick = lane_f == first
        sel = jnp.where(pick, 1, sel)
        g = jnp.where(pick, -jnp.inf, g)
        firsts.append(first.astype(jnp.int32))
    return sel, firsts


def _prompt_prep_kernel(proj_ref, qg_ref, kg_ref, kout_ref, vout_ref, kp_ref, vp_ref, qp_ref, km_ref):
    i = pl.program_id(0)
    blk = MOBA_BLOCK

    @pl.when(i == 0)
    def _():
        km_ref[...] = jnp.zeros_like(km_ref)

    lane = lax.broadcasted_iota(jnp.int32, (blk, LANES), 1)
    row = lax.broadcasted_iota(jnp.int32, (blk, LANES), 0)
    sq_row = lax.broadcasted_iota(jnp.int32, (LANES, LANES), 0)
    i_f = i.astype(F32)
    row_f = row.astype(F32)

    kfeat = jnp.where(lane == i, 1.0, 0.0)
    kfeat = jnp.where(lane == FEAT_KBLK, i_f, kfeat)
    kfeat = jnp.where(lane == FEAT_KOFF, row_f, kfeat)
    kfeat = jnp.where(jnp.logical_or(lane == FEAT_ONE_A, lane == FEAT_ONE_B), 1.0, kfeat).astype(BF16)
    vfeat = jnp.where(lane == 0, 1.0, 0.0).astype(BF16)

    kg = kg_ref[...]
    for kv in range(N_KV_HEADS):
        k = proj_ref[:, ATT_WIDTH + kv * HEAD_DIM:ATT_WIDTH + (kv + 1) * HEAD_DIM]
        kn = _head_norm(k, kg)
        kout_ref[:, kv * HEAD_DIM:(kv + 1) * HEAD_DIM] = kn
        kp_ref[kv, :, :HEAD_DIM] = kn.astype(BF16)
        kp_ref[kv, :, HEAD_DIM:] = kfeat
        ksum = jnp.sum(kn, axis=0, keepdims=True) * (1.0 / blk)
        km_ref[kv] = jnp.where(sq_row == i, jnp.broadcast_to(ksum, (LANES, LANES)), km_ref[kv])
        v = proj_ref[:, ATT_WIDTH + KV_WIDTH + kv * HEAD_DIM:ATT_WIDTH + KV_WIDTH + (kv + 1) * HEAD_DIM]
        vout_ref[:, kv * HEAD_DIM:(kv + 1) * HEAD_DIM] = v
        vp_ref[kv, :, :HEAD_DIM] = v.astype(BF16)
        vp_ref[kv, :, HEAD_DIM:] = vfeat

    qg = qg_ref[...]
    valid = lane < i
    for h in range(ATT_HEADS):
        q = proj_ref[:, h * HEAD_DIM:(h + 1) * HEAD_DIM]
        qn = _head_norm(q, qg)
        gate = _dot3(qn, km_ref[h // KV_GROUP], NT_DIMS)
        sel, _ = _top3(jnp.where(valid, gate, -jnp.inf), lane)
        slope = _alibi_slope(h)
        qfeat = jnp.where(jnp.logical_and(valid, sel == 0), NEG_BIG, 0.0)
        qfeat = jnp.where(lane == FEAT_KBLK, slope * blk, qfeat)
        qfeat = jnp.where(lane == FEAT_KOFF, slope, qfeat)
        qfeat = jnp.where(lane == FEAT_ONE_A, -(slope * blk) * i_f, qfeat)
        qfeat = jnp.where(lane == FEAT_ONE_B, -slope * row_f, qfeat)
        qp_ref[h, :, :HEAD_DIM] = (qn * ATT_SCALE).astype(BF16)
        qp_ref[h, :, HEAD_DIM:] = qfeat.astype(BF16)


def _prompt_prep(proj, q_gain, k_gain):
    m = proj.shape[0]
    nb = m // MOBA_BLOCK
    assert m % MOBA_BLOCK == 0 and nb <= FEAT_KBLK
    blk = MOBA_BLOCK
    qkv_w = ATT_WIDTH + 2 * KV_WIDTH
    return pl.pallas_call(
        _prompt_prep_kernel,
        grid=(nb,),
        in_specs=[pl.BlockSpec((blk, qkv_w), lambda i: (i, 0)),
                  pl.BlockSpec((1, HEAD_DIM), lambda i: (0, 0)),
                  pl.BlockSpec((1, HEAD_DIM), lambda i: (0, 0))],
        out_specs=[pl.BlockSpec((blk, KV_WIDTH), lambda i: (i, 0)),
                   pl.BlockSpec((blk, KV_WIDTH), lambda i: (i, 0)),
                   pl.BlockSpec((N_KV_HEADS, blk, 2 * HEAD_DIM), lambda i: (0, i, 0)),
                   pl.BlockSpec((N_KV_HEADS, blk, 2 * HEAD_DIM), lambda i: (0, i, 0)),
                   pl.BlockSpec((ATT_HEADS, blk, 2 * HEAD_DIM), lambda i: (0, i, 0))],
        out_shape=[jax.ShapeDtypeStruct((m, KV_WIDTH), F32),
                   jax.ShapeDtypeStruct((m, KV_WIDTH), F32),
                   jax.ShapeDtypeStruct((N_KV_HEADS, m, 2 * HEAD_DIM), BF16),
                   jax.ShapeDtypeStruct((N_KV_HEADS, m, 2 * HEAD_DIM), BF16),
                   jax.ShapeDtypeStruct((ATT_HEADS, m, 2 * HEAD_DIM), BF16)],
        scratch_shapes=[pltpu.VMEM((N_KV_HEADS, LANES, LANES), F32)],
        compiler_params=_params("arbitrary"),
        name="prompt_qk_prep",
    )(proj, q_gain.reshape(1, HEAD_DIM), k_gain.reshape(1, HEAD_DIM))


def _prompt_attn_kernel(q_ref, k_ref, v_ref, o_ref, acc_ref, m_ref):
    i = pl.program_id(1)
    blk = MOBA_BLOCK
    rows = KV_GROUP * blk
    qs = q_ref[...].reshape(rows, 2 * HEAD_DIM)

    def scores(j):
        kj = k_ref[pl.ds(pl.multiple_of(j * blk, blk), blk), :]
        return lax.dot_general(qs, kj, NT_DIMS, preferred_element_type=F32)

    def vblock(j):
        return v_ref[pl.ds(pl.multiple_of(j * blk, blk), blk), :]

    def lane_fold(s):
        return jnp.maximum(s[:, :LANES], s[:, LANES:])

    r = lax.broadcasted_iota(jnp.int32, (rows, blk), 0)
    c = lax.broadcasted_iota(jnp.int32, (rows, blk), 1)
    causal = jnp.bitwise_and(r, blk - 1) >= c
    s_own = jnp.where(causal, scores(i), -jnp.inf)

    mfold = lax.fori_loop(0, i, lambda j, mf: jnp.maximum(mf, lane_fold(scores(j))), lane_fold(s_own))
    m_ref[...] = jnp.broadcast_to(jnp.max(mfold, axis=1, keepdims=True), (rows, LANES))

    def probs(s):
        mb = m_ref[...]
        return jnp.concatenate([jnp.exp(s[:, :LANES] - mb), jnp.exp(s[:, LANES:] - mb)], axis=1).astype(BF16)

    acc_ref[...] = jnp.dot(probs(s_own), vblock(i), preferred_element_type=F32)

    def body(j, carry):
        acc_ref[...] += jnp.dot(probs(scores(j)), vblock(j), preferred_element_type=F32)
        return carry

    lax.fori_loop(0, i, body, 0)
    acc = acc_ref[...]
    o = acc[:, :HEAD_DIM] / acc[:, HEAD_DIM:HEAD_DIM + 1]
    for g in range(KV_GROUP):
        o_ref[:, g * HEAD_DIM:(g + 1) * HEAD_DIM] = o[g * blk:(g + 1) * blk]


def _prompt_attn(qp, kp, vp):
    m = kp.shape[1]
    nb = m // MOBA_BLOCK
    blk = MOBA_BLOCK
    return pl.pallas_call(
        _prompt_attn_kernel,
        grid=(N_KV_HEADS, nb),
        in_specs=[pl.BlockSpec((KV_GROUP, blk, 2 * HEAD_DIM), lambda kv, i: (kv, i, 0)),
                  pl.BlockSpec((None, m, 2 * HEAD_DIM), lambda kv, i: (kv, 0, 0)),
                  pl.BlockSpec((None, m, 2 * HEAD_DIM), lambda kv, i: (kv, 0, 0))],
        out_specs=pl.BlockSpec((blk, KV_GROUP * HEAD_DIM), lambda kv, i: (i, kv)),
        out_shape=jax.ShapeDtypeStruct((m, ATT_WIDTH), F32),
        scratch_shapes=[pltpu.VMEM((KV_GROUP * blk, 2 * HEAD_DIM), F32),
                        pltpu.VMEM((KV_GROUP * blk, LANES), F32)],
        compiler_params=_params("arbitrary", "arbitrary"),
        name="prompt_moba_attn",
    )(qp, kp, vp)


def _expand_heads(v, lane_lo):
    r = v.shape[0]
    parts = []
    for k in range(SSM_HEADS // 2):
        a0 = jnp.broadcast_to(v[:, 2 * k:2 * k + 1], (r, LANES))
        a1 = jnp.broadcast_to(v[:, 2 * k + 1:2 * k + 2], (r, LANES))
        parts.append(jnp.where(lane_lo, a0, a1))
    return jnp.concatenate(parts, axis=1)


def _gated_group_norm(y, z, g):
    yz = y * _silu(z)
    gw = D_INNER // SSM_GROUPS
    outs = []
    for grp in range(SSM_GROUPS):
        t = yz[:, grp * gw:(grp + 1) * gw]
        t = t * lax.rsqrt(jnp.mean(t * t, axis=-1, keepdims=True) + EPS)
        outs.append(t * g[:, grp * gw:(grp + 1) * gw])
    return jnp.concatenate(outs, axis=1)


def _ssd_kernel(z_ref, xbc_ref, dt_ref, cw_ref, cb_ref, dtb_ref, alog_ref, dsk_ref, g_ref,
                y_ref, tail_out_ref, st_out_ref, tail_ref, st_ref):
    c = pl.program_id(0)
    cs = SSD_CHUNK
    gw = D_INNER // SSM_GROUPS

    @pl.when(c == 0)
    def _():
        tail_ref[...] = jnp.zeros_like(tail_ref)
        st_ref[...] = jnp.zeros_like(st_ref)

    xr = xbc_ref[...]
    xp = jnp.concatenate([tail_ref[...], xr], axis=0)
    cw = cw_ref[...]
    conv = cb_ref[...] + cw[3:4] * xr
    for t in range(CONV_WIDTH - 1):
        conv = conv + cw[t:t + 1] * xp[8 - (CONV_WIDTH - 1) + t:8 - (CONV_WIDTH - 1) + t + cs]
    tail_ref[...] = xr[cs - 8:]
    tail_out_ref[...] = xr[cs - 8:]
    xc = _silu(conv)
    xs = xc[:, :D_INNER]
    bm = xc[:, D_INNER:D_INNER + SSM_GROUPS * D_STATE]
    cm = xc[:, D_INNER + SSM_GROUPS * D_STATE:]

    lane = lax.broadcasted_iota(jnp.int32, (cs, LANES), 1)
    rowi = lax.broadcasted_iota(jnp.int32, (cs, LANES), 0)
    lane_lo = lane < SSM_HEAD_DIM
    tri = rowi >= lane

    dt = _softplus(dt_ref[...] + dtb_ref[...])
    a = jnp.where(lane[:1] < SSM_HEADS, -jnp.exp(alog_ref[...]), 0.0)
    da = dt * a
    tril = jnp.where(tri, 1.0, 0.0).astype(BF16)
    p1 = da.astype(BF16)
    r1 = da - p1.astype(F32)
    p2 = r1.astype(BF16)
    p3 = (r1 - p2.astype(F32)).astype(BF16)
    acum = (jnp.dot(tril, p1, preferred_element_type=F32) + jnp.dot(tril, p2, preferred_element_type=F32)
            + jnp.dot(tril, p3, preferred_element_type=F32))
    acum_t = acum.T

    dt_e = _expand_heads(dt, lane_lo)
    ac_e = _expand_heads(acum, lane_lo)
    xdt = xs * dt_e
    ea_e = jnp.exp(ac_e)
    dend_e = jnp.exp(ac_e[cs - 1:cs, :] - ac_e)
    cdec = ea_e[cs - 1:cs, :]
    xdt_bf = xdt.astype(BF16)
    xdec_bf = (xdt * dend_e).astype(BF16)

    y_parts = []
    for grp in range(SSM_GROUPS):
        bg = bm[:, grp * D_STATE:(grp + 1) * D_STATE]
        cg = cm[:, grp * D_STATE:(grp + 1) * D_STATE].astype(BF16)
        cb = lax.dot_general(cg, bg.astype(BF16), NT_DIMS, preferred_element_type=F32)
        hpg = SSM_HEADS // SSM_GROUPS
        intra = []
        for k in range(hpg // 2):
            pair = grp * (hpg // 2) + k
            xpair = xdt_bf[:, pair * LANES:(pair + 1) * LANES]
            acc = None
            for hh in range(2):
                h = 2 * pair + hh
                seg = jnp.broadcast_to(acum[:, h:h + 1], (cs, cs)) - acum_t[h:h + 1, :]
                lmat = jnp.exp(jnp.where(tri, seg, -jnp.inf))
                mh = (cb * lmat).astype(BF16)
                xh = jnp.where(lane_lo if hh == 0 else jnp.logical_not(lane_lo), xpair, jnp.zeros_like(xpair))
                part = jnp.dot(mh, xh, preferred_element_type=F32)
                acc = part if acc is None else acc + part
            intra.append(acc)
        y_intra = jnp.concatenate(intra, axis=1)
        st = st_ref[grp]
        y_inter = jnp.dot(cg, st.astype(BF16), preferred_element_type=F32) * ea_e[:, grp * gw:(grp + 1) * gw]
        new_st = cdec[:, grp * gw:(grp + 1) * gw] * st + jnp.dot(
            bg.T.astype(BF16), xdec_bf[:, grp * gw:(grp + 1) * gw], preferred_element_type=F32)
        st_ref[grp] = new_st
        st_out_ref[grp] = new_st
        y_parts.append(y_intra + y_inter)
    y = jnp.concatenate(y_parts, axis=1) + dsk_ref[...] * xs
    y_ref[...] = _gated_group_norm(y, z_ref[...], g_ref[...]).astype(y_ref.dtype)


def _pad_lanes(v):
    return jnp.pad(v.reshape(1, -1), ((0, 0), (0, LANES - v.size)))


def _ssd_prompt(proj, dtp, conv_w, conv_b, dt_bias, a_log, d_skip, g_ssm):
    m = proj.shape[0]
    cs = SSD_CHUNK
    assert m % cs == 0
    gw = D_INNER // SSM_GROUPS
    z_blk = (ATT_WIDTH + 2 * KV_WIDTH) // D_INNER
    x_blk = (ATT_WIDTH + 2 * KV_WIDTH + D_INNER) // CONV_DIM
    assert z_blk * D_INNER == ATT_WIDTH + 2 * KV_WIDTH and x_blk * CONV_DIM == ATT_WIDTH + 2 * KV_WIDTH + D_INNER
    const = lambda shape: pl.BlockSpec(shape, lambda c: tuple(0 for _ in shape))
    return pl.pallas_call(
        _ssd_kernel,
        grid=(m // cs,),
        in_specs=[pl.BlockSpec((cs, D_INNER), lambda c: (c, z_blk)),
                  pl.BlockSpec((cs, CONV_DIM), lambda c: (c, x_blk)),
                  pl.BlockSpec((cs, LANES), lambda c: (c, 0)),
                  const((CONV_WIDTH, CONV_DIM)), const((1, CONV_DIM)), const((1, LANES)), const((1, LANES)),
                  const((1, D_INNER)), const((1, D_INNER))],
        out_specs=[pl.BlockSpec((cs, D_INNER), lambda c: (c, 0)),
                   const((8, CONV_DIM)), const((SSM_GROUPS, D_STATE, gw))],
        out_shape=[jax.ShapeDtypeStruct((m, D_INNER), BF16),
                   jax.ShapeDtypeStruct((8, CONV_DIM), F32),
                   jax.ShapeDtypeStruct((SSM_GROUPS, D_STATE, gw), F32)],
        scratch_shapes=[pltpu.VMEM((8, CONV_DIM), F32), pltpu.VMEM((SSM_GROUPS, D_STATE, gw), F32)],
        compiler_params=_params("arbitrary"),
        name="prompt_ssd",
    )(proj, proj, dtp, conv_w, conv_b.reshape(1, CONV_DIM), _pad_lanes(dt_bias), _pad_lanes(a_log),
      jnp.repeat(d_skip, SSM_HEAD_DIM).reshape(1, D_INNER), g_ssm.reshape(1, D_INNER))


def _sample_prep_kernel(proj_ref, qg_ref, kg_ref, q_ref, k_ref, v_ref):
    for h in range(ATT_HEADS):
        q_ref[:, h * HEAD_DIM:(h + 1) * HEAD_DIM] = _head_norm(proj_ref[:, h * HEAD_DIM:(h + 1) * HEAD_DIM], qg_ref[...])
    for kv in range(N_KV_HEADS):
        lo = ATT_WIDTH + kv * HEAD_DIM
        k_ref[:, kv * HEAD_DIM:(kv + 1) * HEAD_DIM] = _head_norm(proj_ref[:, lo:lo + HEAD_DIM], kg_ref[...])
    v_ref[...] = proj_ref[:, ATT_WIDTH + KV_WIDTH:ATT_WIDTH + 2 * KV_WIDTH]


def _sample_prep(proj, q_gain, k_gain):
    n = proj.shape[0]
    qkv_w = ATT_WIDTH + 2 * KV_WIDTH
    return pl.pallas_call(
        _sample_prep_kernel,
        grid=(1,),
        in_specs=[pl.BlockSpec((n, qkv_w), lambda i: (0, 0)),
                  pl.BlockSpec((1, HEAD_DIM), lambda i: (0, 0)),
                  pl.BlockSpec((1, HEAD_DIM), lambda i: (0, 0))],
        out_specs=[pl.BlockSpec((n, ATT_WIDTH), lambda i: (0, 0)),
                   pl.BlockSpec((n, KV_WIDTH), lambda i: (0, 0)),
                   pl.BlockSpec((n, KV_WIDTH), lambda i: (0, 0))],
        out_shape=[jax.ShapeDtypeStruct((n, ATT_WIDTH), F32),
                   jax.ShapeDtypeStruct((n, KV_WIDTH), F32),
                   jax.ShapeDtypeStruct((n, KV_WIDTH), F32)],
        compiler_params=_params("arbitrary"),
        name="sample_qk_prep",
    )(proj, q_gain.reshape(1, HEAD_DIM), k_gain.reshape(1, HEAD_DIM))


def _sample_gate_kernel(pps, pt_ref, q_ref, *rest):
    page_refs = rest[:pps]
    idx_ref = rest[pps]
    ksum_ref = rest[pps + 1]
    j = pl.program_id(1)
    ppb = MOBA_BLOCK // page_refs[0].shape[1]
    bps = pps // ppb
    for b in range(bps):
        s = jnp.sum(page_refs[b * ppb][0], axis=0, keepdims=True)
        for t in range(1, ppb):
            s = s + jnp.sum(page_refs[b * ppb + t][0], axis=0, keepdims=True)
        ksum_ref[j, b:b + 1, :] = s * (1.0 / MOBA_BLOCK)

    @pl.when(j == pl.num_programs(1) - 1)
    def _():
        nblk = ksum_ref.shape[0] * bps
        km = ksum_ref[...].reshape(nblk, KV_WIDTH)
        q = q_ref[0]
        hrow = lax.broadcasted_iota(jnp.int32, (ATT_HEADS, nblk), 0)
        gate = jnp.zeros((ATT_HEADS, nblk), F32)
        for kv in range(N_KV_HEADS):
            gk = _dot3(q, km[:, kv * HEAD_DIM:(kv + 1) * HEAD_DIM], NT_DIMS)
            gate = jnp.where(hrow // KV_GROUP == kv, gk, gate)
        lane = lax.broadcasted_iota(jnp.int32, (ATT_HEADS, LANES), 1)
        if nblk < LANES:
            gate = jnp.concatenate([gate, jnp.full((ATT_HEADS, LANES - nblk), -jnp.inf, F32)], axis=1)
        _, firsts = _top3(gate, lane)
        out = jnp.zeros((ATT_HEADS, LANES), jnp.int32)
        for t, first in enumerate(firsts):
            out = jnp.where(lane == t, first, out)
        idx_ref[0] = out


def _sample_gate(q3, cache_k3, page_table):
    n, n_pages = page_table.shape
    page = cache_k3.shape[1]
    ppb = MOBA_BLOCK // page
    nblk = n_pages // ppb
    assert n_pages % ppb == 0 and MOBA_TOPK <= nblk <= LANES
    pps = min(16, n_pages)
    assert n_pages % pps == 0 and pps % ppb == 0
    steps = n_pages // pps

    def page_spec(t):
        return pl.BlockSpec((1, page, KV_WIDTH), lambda s, j, pt: (pt[s, j * pps + t], 0, 0))

    grid_spec = pltpu.PrefetchScalarGridSpec(
        num_scalar_prefetch=1,
        grid=(n, steps),
        in_specs=[pl.BlockSpec((1, ATT_HEADS, HEAD_DIM), lambda s, j, pt: (s, 0, 0))]
        + [page_spec(t) for t in range(pps)],
        out_specs=pl.BlockSpec((1, ATT_HEADS, LANES), lambda s, j, pt: (s, 0, 0)),
        scratch_shapes=[pltpu.VMEM((steps, pps // ppb, KV_WIDTH), F32)],
    )
    return pl.pallas_call(
        functools.partial(_sample_gate_kernel, pps),
        grid_spec=grid_spec,
        out_shape=jax.ShapeDtypeStruct((n, ATT_HEADS, LANES), jnp.int32),
        compiler_params=_params("arbitrary", "arbitrary"),
        name="sample_gate_topk",
    )(page_table, q3, *([cache_k3] * pps))


def _sample_attn_kernel(past, ppb, pt_ref, idx_ref, q_ref, kn_ref, vn_ref, ck_ref, cv_ref, o_ref,
                        kbuf, vbuf, sems):
    s = pl.program_id(0)
    page = MOBA_BLOCK // ppb

    def copies(h, t, p):
        kv = h // KV_GROUP
        blk = idx_ref[s, h * MOBA_TOPK + t]
        phys = pt_ref[s, blk * ppb + p]
        dst = pl.ds((t * ppb + p) * page, page)
        src_lanes = pl.ds(kv * HEAD_DIM, HEAD_DIM)
        return (pltpu.make_async_copy(ck_ref.at[phys, :, src_lanes], kbuf.at[h, dst, :], sems.at[0, h, t * ppb + p]),
                pltpu.make_async_copy(cv_ref.at[phys, :, src_lanes], vbuf.at[h, dst, :], sems.at[1, h, t * ppb + p]))

    triples = [(h, t, p) for h in range(ATT_HEADS) for t in range(MOBA_TOPK) for p in range(ppb)]
    for h, t, p in triples:
        ck, cv = copies(h, t, p)
        ck.start()
        cv.start()
    for h, t, p in triples:
        ck, cv = copies(h, t, p)
        ck.wait()
        cv.wait()

    nsel = MOBA_TOPK * MOBA_BLOCK
    rowi = lax.broadcasted_iota(jnp.int32, (nsel, 1), 0)
    off = jnp.bitwise_and(rowi, MOBA_BLOCK - 1)
    for h in range(ATT_HEADS):
        kv = h // KV_GROUP
        slope = _alibi_slope(h)
        q = q_ref[0, h:h + 1, :]
        sc = jnp.sum(kbuf[h] * q, axis=1, keepdims=True) * ATT_SCALE
        pos = jnp.zeros((nsel, 1), jnp.int32)
        for t in range(MOBA_TOPK):
            pos = jnp.where(rowi // MOBA_BLOCK == t, idx_ref[s, h * MOBA_TOPK + t] * MOBA_BLOCK, pos)
        dist = (past - (pos + off)).astype(F32)
        sc = sc - slope * dist
        s_own = jnp.sum(kn_ref[0, kv:kv + 1, :] * q, axis=1, keepdims=True) * ATT_SCALE
        mx = jnp.maximum(jnp.max(sc, axis=0, keepdims=True), s_own)
        p = jnp.exp(sc - mx)
        p_own = jnp.exp(s_own - mx)
        denom = jnp.sum(p, axis=0, keepdims=True) + p_own
        num = jnp.sum(p * vbuf[h], axis=0, keepdims=True) + p_own * vn_ref[0, kv:kv + 1, :]
        o_ref[0, h:h + 1, :] = num / denom


def _sample_attn(q3, k_new3, v_new3, cache_k3, cache_v3, page_table, idx):
    n, n_pages = page_table.shape
    page = cache_k3.shape[1]
    ppb = MOBA_BLOCK // page
    past = n_pages * page
    assert past % MOBA_BLOCK == 0
    nsel = MOBA_TOPK * MOBA_BLOCK
    grid_spec = pltpu.PrefetchScalarGridSpec(
        num_scalar_prefetch=2,
        grid=(n,),
        in_specs=[pl.BlockSpec((1, ATT_HEADS, HEAD_DIM), lambda s, pt, ix: (s, 0, 0)),
                  pl.BlockSpec((1, N_KV_HEADS, HEAD_DIM), lambda s, pt, ix: (s, 0, 0)),
                  pl.BlockSpec((1, N_KV_HEADS, HEAD_DIM), lambda s, pt, ix: (s, 0, 0)),
                  pl.BlockSpec(memory_space=pl.ANY),
                  pl.BlockSpec(memory_space=pl.ANY)],
        out_specs=pl.BlockSpec((1, ATT_HEADS, HEAD_DIM), lambda s, pt, ix: (s, 0, 0)),
        scratch_shapes=[pltpu.VMEM((ATT_HEADS, nsel, HEAD_DIM), F32),
                        pltpu.VMEM((ATT_HEADS, nsel, HEAD_DIM), F32),
                        pltpu.SemaphoreType.DMA((2, ATT_HEADS, MOBA_TOPK * ppb))],
    )
    return pl.pallas_call(
        functools.partial(_sample_attn_kernel, past, ppb),
        grid_spec=grid_spec,
        out_shape=jax.ShapeDtypeStruct((n, ATT_HEADS, HEAD_DIM), F32),
        compiler_params=_params("arbitrary"),
        name="sample_moba_attn",
    )(page_table, idx, q3, k_new3, v_new3, cache_k3, cache_v3)


def _sample_ssd_kernel(z_ref, xbc_ref, dt_ref, buf_ref, h0_ref, cw_ref, cb_ref, dtb_ref, alog_ref, dsk_ref, g_ref,
                       y_ref, buf_out_ref, h_out_ref):
    x = xbc_ref[0]
    buf = buf_ref[0]
    cw = cw_ref[...]
    conv = cb_ref[...] + cw[CONV_WIDTH - 1:CONV_WIDTH] * x
    for t in range(CONV_WIDTH - 1):
        conv = conv + cw[t:t + 1] * buf[t:t + 1]
    buf_out_ref[0, 0:CONV_WIDTH - 2, :] = buf[1:CONV_WIDTH - 1]
    buf_out_ref[0, CONV_WIDTH - 2:CONV_WIDTH - 1, :] = x
    xc = _silu(conv)
    xs = xc[:, :D_INNER]
    bm = xc[:, D_INNER:D_INNER + SSM_GROUPS * D_STATE]
    cm = xc[:, D_INNER + SSM_GROUPS * D_STATE:]

    lane1 = lax.broadcasted_iota(jnp.int32, (1, LANES), 1)
    rowi = lax.broadcasted_iota(jnp.int32, (LANES, LANES), 0)
    dt = _softplus(dt_ref[0] + dtb_ref[...])
    a = jnp.where(lane1 < SSM_HEADS, -jnp.exp(alog_ref[...]), 0.0)
    dec = jnp.exp(dt * a)
    dt_e = _expand_heads(dt, lane1 < SSM_HEAD_DIM)
    xdt = xs * dt_e
    xdt_rows = jnp.broadcast_to(xdt, (LANES, D_INNER))

    hpg = SSM_HEADS // SSM_GROUPS
    y_parts = []
    for pair in range(SSM_HEADS // 2):
        grp = (2 * pair) // hpg
        xcol = xdt_rows[:, pair * LANES:(pair + 1) * LANES].T
        dcol = jnp.where(rowi < SSM_HEAD_DIM,
                         jnp.broadcast_to(dec[:, 2 * pair:2 * pair + 1], (LANES, LANES)),
                         jnp.broadcast_to(dec[:, 2 * pair + 1:2 * pair + 2], (LANES, LANES)))
        h0 = h0_ref[0, 2 * pair:2 * pair + 2].reshape(LANES, D_STATE)
        hn = dcol * h0 + xcol * bm[:, grp * D_STATE:(grp + 1) * D_STATE]
        h_out_ref[0, 2 * pair:2 * pair + 2] = hn.reshape(2, SSM_HEAD_DIM, D_STATE)
        cgrow = jnp.broadcast_to(cm[:, grp * D_STATE:(grp + 1) * D_STATE], (8, D_STATE))
        ypair = _dot3(cgrow, hn, NT_DIMS)
        y_parts.append(ypair[0:1])
    y = jnp.concatenate(y_parts, axis=1) + dsk_ref[...] * xs
    y_ref[0] = _gated_group_norm(y, z_ref[0], g_ref[...]).astype(y_ref.dtype)


def _ssd_sample(proj, dtp, state_conv, state_ssm, conv_w, conv_b, dt_bias, a_log, d_skip, g_ssm):
    n = proj.shape[0]
    z0 = ATT_WIDTH + 2 * KV_WIDTH
    z3 = proj[:, z0:z0 + D_INNER].reshape(n, 1, D_INNER)
    x3 = proj[:, z0 + D_INNER:z0 + D_INNER + CONV_DIM].reshape(n, 1, CONV_DIM)
    dt3 = dtp.reshape(n, 1, LANES)
    const = lambda shape: pl.BlockSpec(shape, lambda s: tuple(0 for _ in shape))
    per_seq = lambda shape: pl.BlockSpec((1,) + shape, lambda s: (s,) + tuple(0 for _ in shape))
    y, buf, h = pl.pallas_call(
        _sample_ssd_kernel,
        grid=(n,),
        in_specs=[per_seq((1, D_INNER)), per_seq((1, CONV_DIM)), per_seq((1, LANES)),
                  per_seq((CONV_WIDTH - 1, CONV_DIM)), per_seq((SSM_HEADS, SSM_HEAD_DIM, D_STATE)),
                  const((CONV_WIDTH, CONV_DIM)), const((1, CONV_DIM)), const((1, LANES)), const((1, LANES)),
                  const((1, D_INNER)), const((1, D_INNER))],
        out_specs=[per_seq((1, D_INNER)), per_seq((CONV_WIDTH - 1, CONV_DIM)),
                   per_seq((SSM_HEADS, SSM_HEAD_DIM, D_STATE))],
        out_shape=[jax.ShapeDtypeStruct((n, 1, D_INNER), BF16),
                   jax.ShapeDtypeStruct((n, CONV_WIDTH - 1, CONV_DIM), F32),
                   jax.ShapeDtypeStruct((n, SSM_HEADS, SSM_HEAD_DIM, D_STATE), F32)],
        compiler_params=_params("arbitrary"),
        name="sample_ssd",
    )(z3, x3, dt3, state_conv, state_ssm, conv_w, conv_b.reshape(1, CONV_DIM), _pad_lanes(dt_bias),
      _pad_lanes(a_log), jnp.repeat(d_skip, SSM_HEAD_DIM).reshape(1, D_INNER), g_ssm.reshape(1, D_INNER))
    return y.reshape(n, D_INNER), buf, h


def _layer_tail(x, mod, o_att, y_ssm, g_att_out, w_out, g_ffn, w_gate, w_up, w_down):
    a = _rms_rows(o_att, g_att_out)
    x1 = _outproj(a, y_ssm, w_out, x, mod, 2)
    h2 = _norm_mod(x1, g_ffn, mod, 4, 3)
    hid = _gateup(h2, w_gate, w_up)
    return _down(hid, w_down, x1, mod, 5)


def kernel(x_prompt, x_sample, cache_k, cache_v, state_conv, state_ssm, page_table, c_prompt, c_sample, w_ada, b_ada, g_mix_norm, w_in, q_gain, k_gain, g_att_out, conv_w, conv_b, dt_bias, a_log, d_skip, g_ssm_out, w_out, g_ffn_norm, w_gate, w_up, w_down):
    n_p, seq, d = x_prompt.shape
    n_s, dec_seq, _ = x_sample.shape
    assert n_p == 1 and dec_seq == 1
    depth = w_ada.shape[0]
    main_w = ATT_WIDTH + 2 * KV_WIDTH + D_INNER + CONV_DIM
    pool, page = cache_k.shape[1], cache_k.shape[2]

    yp = x_prompt.reshape(seq, d)
    ys = x_sample.reshape(n_s, d)
    c_rows = n_p + n_s
    c_pad = -(-c_rows // 8) * 8
    c_all = jnp.pad(jnp.concatenate([c_prompt, c_sample], axis=0), ((0, c_pad - c_rows), (0, 0)))
    outs = [[] for _ in range(8)]
    for l in range(depth):
        mod = _ada(c_all, w_ada[l], b_ada[l])
        mod_p, mod_s = mod[0:1], mod[1:1 + n_s]
        w_dt = jnp.pad(w_in[l][:, main_w:], ((0, 0), (0, LANES - SSM_HEADS)))

        hn = _norm_mod(yp, g_mix_norm[l], mod_p, 1, 0)
        proj = _mm(hn, w_in[l], main_w, tn=512, name="in_proj")
        dtp = _mm(hn, w_dt, LANES, tn=LANES, name="in_proj_dt")
        k_out, v_out, kp, vp, qp = _prompt_prep(proj, q_gain[l], k_gain[l])
        o_att = _prompt_attn(qp, kp, vp)
        y_ssm, tail, st = _ssd_prompt(proj, dtp, conv_w[l], conv_b[l], dt_bias[l], a_log[l], d_skip[l], g_ssm_out[l])
        yp = _layer_tail(yp, mod_p, o_att, y_ssm, g_att_out[l], w_out[l], g_ffn_norm[l], w_gate[l], w_up[l], w_down[l])
        hpg = SSM_HEADS // SSM_GROUPS
        ssm_p = st.reshape(SSM_GROUPS, D_STATE, hpg, SSM_HEAD_DIM).transpose(0, 2, 3, 1).reshape(
            1, SSM_HEADS, SSM_HEAD_DIM, D_STATE)
        outs[0].append(k_out.reshape(1, seq, N_KV_HEADS, HEAD_DIM))
        outs[1].append(v_out.reshape(1, seq, N_KV_HEADS, HEAD_DIM))
        outs[2].append(tail[8 - (CONV_WIDTH - 1):].reshape(1, CONV_WIDTH - 1, CONV_DIM))
        outs[3].append(ssm_p)

        hs = _norm_mod(ys, g_mix_norm[l], mod_s, 1, 0)
        proj_s = _mm(hs, w_in[l], main_w, tn=512, name="in_proj")
        dts = _mm(hs, w_dt, LANES, tn=LANES, name="in_proj_dt")
        q_s, k_s, v_s = _sample_prep(proj_s, q_gain[l], k_gain[l])
        q3 = q_s.reshape(n_s, ATT_HEADS, HEAD_DIM)
        ck3 = cache_k[l].reshape(pool, page, KV_WIDTH)
        cv3 = cache_v[l].reshape(pool, page, KV_WIDTH)
        idx = _sample_gate(q3, ck3, page_table)
        idx_flat = idx[:, :, :MOBA_TOPK].reshape(n_s, ATT_HEADS * MOBA_TOPK)
        o_s = _sample_attn(q3, k_s.reshape(n_s, N_KV_HEADS, HEAD_DIM), v_s.reshape(n_s, N_KV_HEADS, HEAD_DIM),
                           ck3, cv3, page_table, idx_flat)
        y_s, buf_s, h_s = _ssd_sample(proj_s, dts, state_conv[l], state_ssm[l], conv_w[l], conv_b[l], dt_bias[l],
                                      a_log[l], d_skip[l], g_ssm_out[l])
        ys = _layer_tail(ys, mod_s, o_s.reshape(n_s, ATT_WIDTH), y_s, g_att_out[l], w_out[l], g_ffn_norm[l],
                         w_gate[l], w_up[l], w_down[l])
        outs[4].append(k_s.reshape(n_s, 1, N_KV_HEADS, HEAD_DIM))
        outs[5].append(v_s.reshape(n_s, 1, N_KV_HEADS, HEAD_DIM))
        outs[6].append(buf_s)
        outs[7].append(h_s)
    stacked = [jnp.stack(o) for o in outs]
    return (yp.reshape(1, seq, d), ys.reshape(n_s, 1, d), *stacked)
```

```python
import functools

import jax
import jax.numpy as jnp
from jax import lax
from jax.experimental import pallas as pl
from jax.experimental.pallas import tpu as pltpu

F32 = jnp.float32
BF16 = jnp.bfloat16

HEAD_DIM = 128
ATT_HEADS = 8
N_KV_HEADS = 4
KV_GROUP = ATT_HEADS // N_KV_HEADS
ATT_WIDTH = ATT_HEADS * HEAD_DIM
KV_WIDTH = N_KV_HEADS * HEAD_DIM
MOBA_BLOCK = 256
MOBA_TOPK = 3
D_INNER = 1024
SSM_HEAD_DIM = 64
SSM_HEADS = D_INNER // SSM_HEAD_DIM
SSM_GROUPS = 2
D_STATE = 128
CONV_WIDTH = 4
CONV_DIM = D_INNER + 2 * SSM_GROUPS * D_STATE
SSD_CHUNK = 128
EPS = 1e-6
ATT_SCALE = HEAD_DIM ** -0.5

LANES = 128
FEAT_KBLK = 96
FEAT_KOFF = 97
FEAT_ONE_A = 98
FEAT_ONE_B = 99
NEG_BIG = -1e30

NT_DIMS = (((1,), (1,)), ((), ()))
VMEM_LIMIT = 56 * 1024 * 1024


def _params(*sem):
    return pltpu.CompilerParams(dimension_semantics=sem, vmem_limit_bytes=VMEM_LIMIT)


def _silu(x):
    return x / (1.0 + jnp.exp(-x))


def _softplus(x):
    return jnp.maximum(x, 0.0) + jnp.log1p(jnp.exp(-jnp.abs(x)))


def _split_bf16(x):
    hi = x.astype(BF16)
    lo = (x - hi.astype(F32)).astype(BF16)
    return hi, lo


def _dot3(a, b, dims):
    ah, al = _split_bf16(a)
    bh, bl = _split_bf16(b)
    d = lambda x, y: lax.dot_general(x, y, dims, preferred_element_type=F32)
    return d(ah, bh) + d(al, bh) + d(ah, bl)


def _alibi_slope(h):
    return 2.0 ** (-8.0 * (h + 1) / ATT_HEADS)


def _ada_kernel(c_ref, w_ref, b_ref, o_ref):
    a = _silu(c_ref[...]).astype(BF16)
    o_ref[...] = jnp.dot(a, w_ref[...].astype(BF16), preferred_element_type=F32) + b_ref[...]


def _ada(c_all, w, b):
    rows, d = c_all.shape
    n = w.shape[1]
    tn = 1024
    return pl.pallas_call(
        _ada_kernel,
        grid=(n // tn,),
        in_specs=[pl.BlockSpec((rows, d), lambda j: (0, 0)),
                  pl.BlockSpec((d, tn), lambda j: (0, j)),
                  pl.BlockSpec((1, tn), lambda j: (0, j))],
        out_specs=pl.BlockSpec((rows, tn), lambda j: (0, j)),
        out_shape=jax.ShapeDtypeStruct((rows, n), F32),
        compiler_params=_params("arbitrary"),
        name="ada_mod",
    )(c_all, w, b.reshape(1, n))


def _norm_mod_kernel(x_ref, g_ref, sc_ref, sh_ref, o_ref):
    x = x_ref[...]
    y = x * lax.rsqrt(jnp.mean(x * x, axis=-1, keepdims=True) + EPS)
    y = y * g_ref[...]
    o_ref[...] = (y * (1.0 + sc_ref[...]) + sh_ref[...]).astype(o_ref.dtype)


def _norm_mod(x, g, mod, sc_idx, sh_idx):
    m, d = x.shape
    mr = mod.shape[0]
    tm = min(256, m)
    mod_map = (lambda idx: (lambda i: (0, idx))) if mr == 1 else (lambda idx: (lambda i: (i, idx)))
    return pl.pallas_call(
        _norm_mod_kernel,
        grid=(m // tm,),
        in_specs=[pl.BlockSpec((tm, d), lambda i: (i, 0)),
                  pl.BlockSpec((1, d), lambda i: (0, 0)),
                  pl.BlockSpec((1 if mr == 1 else tm, d), mod_map(sc_idx)),
                  pl.BlockSpec((1 if mr == 1 else tm, d), mod_map(sh_idx))],
        out_specs=pl.BlockSpec((tm, d), lambda i: (i, 0)),
        out_shape=jax.ShapeDtypeStruct((m, d), BF16),
        compiler_params=_params("arbitrary"),
        name="norm_mod",
    )(x, g.reshape(1, d), mod, mod)


def _rms_rows_kernel(x_ref, g_ref, o_ref):
    x = x_ref[...]
    y = x * lax.rsqrt(jnp.mean(x * x, axis=-1, keepdims=True) + EPS)
    o_ref[...] = (y * g_ref[...]).astype(o_ref.dtype)


def _rms_rows(x, g):
    m, d = x.shape
    tm = min(512, m)
    return pl.pallas_call(
        _rms_rows_kernel,
        grid=(m // tm,),
        in_specs=[pl.BlockSpec((tm, d), lambda i: (i, 0)), pl.BlockSpec((1, d), lambda i: (0, 0))],
        out_specs=pl.BlockSpec((tm, d), lambda i: (i, 0)),
        out_shape=jax.ShapeDtypeStruct((m, d), BF16),
        compiler_params=_params("arbitrary"),
        name="att_out_norm",
    )(x, g.reshape(1, d))


def _mm_kernel(x_ref, w_ref, o_ref, wbf_ref):
    @pl.when(pl.program_id(1) == 0)
    def _():
        wbf_ref[...] = w_ref[...].astype(BF16)

    o_ref[...] = jnp.dot(x_ref[...], wbf_ref[...], preferred_element_type=F32).astype(o_ref.dtype)


def _mm(x, w, n_cols, *, tn, tm=512, name):
    m, k = x.shape
    tm = min(tm, m)
    return pl.pallas_call(
        _mm_kernel,
        grid=(n_cols // tn, m // tm),
        in_specs=[pl.BlockSpec((tm, k), lambda n, i: (i, 0)),
                  pl.BlockSpec((k, tn), lambda n, i: (0, n))],
        out_specs=pl.BlockSpec((tm, tn), lambda n, i: (i, n)),
        out_shape=jax.ShapeDtypeStruct((m, n_cols), F32),
        scratch_shapes=[pltpu.VMEM((k, tn), BF16)],
        compiler_params=_params("arbitrary", "arbitrary"),
        name=name,
    )(x, w)


def _outproj_kernel(a_ref, b_ref, w_ref, x_ref, gt_ref, o_ref, wbf_ref):
    @pl.when(pl.program_id(1) == 0)
    def _():
        wbf_ref[...] = w_ref[...].astype(BF16)

    ka = a_ref.shape[1]
    acc = jnp.dot(a_ref[...], wbf_ref[:ka, :], preferred_element_type=F32)
    acc = acc + jnp.dot(b_ref[...], wbf_ref[ka:, :], preferred_element_type=F32)
    o_ref[...] = x_ref[...] + gt_ref[...] * acc


def _outproj(a, b, w, x, mod, gt_idx):
    m, ka = a.shape
    kb = b.shape[1]
    d = w.shape[1]
    tn, tm = 512, min(512, m)
    mr = mod.shape[0]
    nb = d // tn
    gt_map = (lambda n, i: (0, gt_idx * nb + n)) if mr == 1 else (lambda n, i: (i, gt_idx * nb + n))
    return pl.pallas_call(
        _outproj_kernel,
        grid=(d // tn, m // tm),
        in_specs=[pl.BlockSpec((tm, ka), lambda n, i: (i, 0)),
                  pl.BlockSpec((tm, kb), lambda n, i: (i, 0)),
                  pl.BlockSpec((ka + kb, tn), lambda n, i: (0, n)),
                  pl.BlockSpec((tm, tn), lambda n, i: (i, n)),
                  pl.BlockSpec((1 if mr == 1 else tm, tn), gt_map)],
        out_specs=pl.BlockSpec((tm, tn), lambda n, i: (i, n)),
        out_shape=jax.ShapeDtypeStruct((m, d), F32),
        scratch_shapes=[pltpu.VMEM((ka + kb, tn), BF16)],
        compiler_params=_params("arbitrary", "arbitrary"),
        name="out_proj",
    )(a, b, w, x, mod)


def _gateup_kernel(x_ref, wg_ref, wu_ref, o_ref, wgb_ref, wub_ref):
    @pl.when(pl.program_id(1) == 0)
    def _():
        wgb_ref[...] = wg_ref[...].astype(BF16)
        wub_ref[...] = wu_ref[...].astype(BF16)

    x = x_ref[...]
    g = jnp.dot(x, wgb_ref[...], preferred_element_type=F32)
    u = jnp.dot(x, wub_ref[...], preferred_element_type=F32)
    o_ref[...] = (_silu(g) * u).astype(o_ref.dtype)


def _gateup(x, wg, wu):
    m, k = x.shape
    f = wg.shape[1]
    tn, tm = 512, min(512, m)
    return pl.pallas_call(
        _gateup_kernel,
        grid=(f // tn, m // tm),
        in_specs=[pl.BlockSpec((tm, k), lambda n, i: (i, 0)),
                  pl.BlockSpec((k, tn), lambda n, i: (0, n)),
                  pl.BlockSpec((k, tn), lambda n, i: (0, n))],
        out_specs=pl.BlockSpec((tm, tn), lambda n, i: (i, n)),
        out_shape=jax.ShapeDtypeStruct((m, f), BF16),
        scratch_shapes=[pltpu.VMEM((k, tn), BF16), pltpu.VMEM((k, tn), BF16)],
        compiler_params=_params("arbitrary", "arbitrary"),
        name="ffn_gate_up",
    )(x, wg, wu)


def _down_kernel(h_ref, w_ref, x_ref, gt_ref, o_ref, wbf_ref):
    @pl.when(pl.program_id(1) == 0)
    def _():
        wbf_ref[...] = w_ref[...].astype(BF16)

    acc = jnp.dot(h_ref[...], wbf_ref[...], preferred_element_type=F32)
    o_ref[...] = x_ref[...] + gt_ref[...] * acc


def _down(h, w, x, mod, gt_idx):
    m, f = h.shape
    d = w.shape[1]
    tn, tm = 512, min(256, m)
    mr = mod.shape[0]
    nb = d // tn
    gt_map = (lambda n, i: (0, gt_idx * nb + n)) if mr == 1 else (lambda n, i: (i, gt_idx * nb + n))
    return pl.pallas_call(
        _down_kernel,
        grid=(d // tn, m // tm),
        in_specs=[pl.BlockSpec((tm, f), lambda n, i: (i, 0)),
                  pl.BlockSpec((f, tn), lambda n, i: (0, n)),
                  pl.BlockSpec((tm, tn), lambda n, i: (i, n)),
                  pl.BlockSpec((1 if mr == 1 else tm, tn), gt_map)],
        out_specs=pl.BlockSpec((tm, tn), lambda n, i: (i, n)),
        out_shape=jax.ShapeDtypeStruct((m, d), F32),
        scratch_shapes=[pltpu.VMEM((f, tn), BF16)],
        compiler_params=_params("arbitrary", "arbitrary"),
        name="ffn_down",
    )(h, w, x, mod)


def _head_norm(x, gain):
    return x * lax.rsqrt(jnp.mean(x * x, axis=-1, keepdims=True) + EPS) * gain


def _top3(g, lane):
    sel = jnp.zeros(g.shape, jnp.int32)
    lane_f = lane.astype(F32)
    firsts = []
    for _ in range(MOBA_TOPK):
        mx = jnp.max(g, axis=1, keepdims=True)
        ismax = jnp.logical_and(g == mx, mx > -jnp.inf)
        first = jnp.min(jnp.where(ismax, lane_f, float(LANES)), axis=1, keepdims=True)
        pick = lane_f == first
        sel = jnp.where(pick, 1, sel)
        g = jnp.where(pick, -jnp.inf, g)
        firsts.append(first.astype(jnp.int32))
    return sel, firsts


def _prompt_prep_kernel(proj_ref, qg_ref, kg_ref, kout_ref, vout_ref, kp_ref, vp_ref, qp_ref, km_ref):
    i = pl.program_id(0)
    blk = MOBA_BLOCK

    @pl.when(i == 0)
    def _():
        km_ref[...] = jnp.zeros_like(km_ref)

    lane = lax.broadcasted_iota(jnp.int32, (blk, LANES), 1)
    row = lax.broadcasted_iota(jnp.int32, (blk, LANES), 0)
    sq_row = lax.broadcasted_iota(jnp.int32, (LANES, LANES), 0)
    i_f = i.astype(F32)
    row_f = row.astype(F32)

    kfeat = jnp.where(lane == i, 1.0, 0.0)
    kfeat = jnp.where(lane == FEAT_KBLK, i_f, kfeat)
    kfeat = jnp.where(lane == FEAT_KOFF, row_f, kfeat)
    kfeat = jnp.where(jnp.logical_or(lane == FEAT_ONE_A, lane == FEAT_ONE_B), 1.0, kfeat).astype(BF16)
    vfeat = jnp.where(lane == 0, 1.0, 0.0).astype(BF16)

    kg = kg_ref[...]
    for kv in range(N_KV_HEADS):
        k = proj_ref[:, ATT_WIDTH + kv * HEAD_DIM:ATT_WIDTH + (kv + 1) * HEAD_DIM]
        kn = _head_norm(k, kg)
        kout_ref[:, kv * HEAD_DIM:(kv + 1) * HEAD_DIM] = kn
        kp_ref[kv, :, :HEAD_DIM] = kn.astype(BF16)
        kp_ref[kv, :, HEAD_DIM:] = kfeat
        ksum = jnp.sum(kn, axis=0, keepdims=True) * (1.0 / blk)
        km_ref[kv] = jnp.where(sq_row == i, jnp.broadcast_to(ksum, (LANES, LANES)), km_ref[kv])
        v = proj_ref[:, ATT_WIDTH + KV_WIDTH + kv * HEAD_DIM:ATT_WIDTH + KV_WIDTH + (kv + 1) * HEAD_DIM]
        vout_ref[:, kv * HEAD_DIM:(kv + 1) * HEAD_DIM] = v
        vp_ref[kv, :, :HEAD_DIM] = v.astype(BF16)
        vp_ref[kv, :, HEAD_DIM:] = vfeat

    qg = qg_ref[...]
    valid = lane < i
    for h in range(ATT_HEADS):
        q = proj_ref[:, h * HEAD_DIM:(h + 1) * HEAD_DIM]
        qn = _head_norm(q, qg)
        gate = _dot3(qn, km_ref[h // KV_GROUP], NT_DIMS)
        sel, _ = _top3(jnp.where(valid, gate, -jnp.inf), lane)
        slope = _alibi_slope(h)
        qfeat = jnp.where(jnp.logical_and(lane < FEAT_KBLK, sel == 0), NEG_BIG, 0.0)
        qfeat = jnp.where(lane == FEAT_KBLK, slope * blk, qfeat)
        qfeat = jnp.where(lane == FEAT_KOFF, slope, qfeat)
        qfeat = jnp.where(lane == FEAT_ONE_A, -(slope * blk) * i_f, qfeat)
        qfeat = jnp.where(lane == FEAT_ONE_B, -slope * row_f, qfeat)
        qp_ref[h, :, :HEAD_DIM] = (qn * ATT_SCALE).astype(BF16)
        qp_ref[h, :, HEAD_DIM:] = qfeat.astype(BF16)


def _prompt_prep(proj, q_gain, k_gain):
    m = proj.shape[0]
    nb = m // MOBA_BLOCK
    assert m % MOBA_BLOCK == 0 and nb <= FEAT_KBLK
    blk = MOBA_BLOCK
    qkv_w = ATT_WIDTH + 2 * KV_WIDTH
    return pl.pallas_call(
        _prompt_prep_kernel,
        grid=(nb,),
        in_specs=[pl.BlockSpec((blk, qkv_w), lambda i: (i, 0)),
                  pl.BlockSpec((1, HEAD_DIM), lambda i: (0, 0)),
                  pl.BlockSpec((1, HEAD_DIM), lambda i: (0, 0))],
        out_specs=[pl.BlockSpec((blk, KV_WIDTH), lambda i: (i, 0)),
                   pl.BlockSpec((blk, KV_WIDTH), lambda i: (i, 0)),
                   pl.BlockSpec((N_KV_HEADS, blk, 2 * HEAD_DIM), lambda i: (0, i, 0)),
                   pl.BlockSpec((N_KV_HEADS, blk, 2 * HEAD_DIM), lambda i: (0, i, 0)),
                   pl.BlockSpec((ATT_HEADS, blk, 2 * HEAD_DIM), lambda i: (0, i, 0))],
        out_shape=[jax.ShapeDtypeStruct((m, KV_WIDTH), F32),
                   jax.ShapeDtypeStruct((m, KV_WIDTH), F32),
                   jax.ShapeDtypeStruct((N_KV_HEADS, m, 2 * HEAD_DIM), BF16),
                   jax.ShapeDtypeStruct((N_KV_HEADS, m, 2 * HEAD_DIM), BF16),
                   jax.ShapeDtypeStruct((ATT_HEADS, m, 2 * HEAD_DIM), BF16)],
        scratch_shapes=[pltpu.VMEM((N_KV_HEADS, LANES, LANES), F32)],
        compiler_params=_params("arbitrary"),
        name="prompt_qk_prep",
    )(proj, q_gain.reshape(1, HEAD_DIM), k_gain.reshape(1, HEAD_DIM))


ATTN_UNROLL = 4
LOG2E = 1.4426950408889634


def _prompt_attn_kernel(q_ref, k_ref, v_ref, o_ref, s_ref, acc_ref, m_ref):
    i = pl.program_id(1)
    blk = MOBA_BLOCK
    rows = KV_GROUP * blk
    span = ATTN_UNROLL * blk
    qs = q_ref[...].reshape(rows, 2 * HEAD_DIM)

    def lane_fold(s):
        out = s[:, :LANES]
        for t in range(1, s.shape[1] // LANES):
            out = jnp.maximum(out, s[:, t * LANES:(t + 1) * LANES])
        return out

    def probs(s):
        mb = m_ref[...]
        return jnp.concatenate([jnp.exp2(s[:, t * LANES:(t + 1) * LANES] - mb) for t in range(s.shape[1] // LANES)],
                               axis=1).astype(BF16)

    own = pl.ds(pl.multiple_of(i * blk, blk), blk)
    r = lax.broadcasted_iota(jnp.int32, (rows, blk), 0)
    c = lax.broadcasted_iota(jnp.int32, (rows, blk), 1)
    dist = jnp.bitwise_and(r, blk - 1) - c
    slope = qs[:, HEAD_DIM + FEAT_KOFF:HEAD_DIM + FEAT_KOFF + 1].astype(F32)
    s_own = lax.dot_general(qs[:, :HEAD_DIM], k_ref[own, :HEAD_DIM], NT_DIMS, preferred_element_type=F32)
    s_own = jnp.where(dist >= 0, (s_own - slope * dist.astype(F32)) * LOG2E, -jnp.inf)
    m_ref[...] = lane_fold(s_own)

    trips = (i + ATTN_UNROLL - 1) // ATTN_UNROLL

    def pass1(t, carry):
        ks = k_ref[pl.ds(pl.multiple_of(t * span, span), span), :]
        s = lax.dot_general(qs, ks, NT_DIMS, preferred_element_type=F32) * LOG2E
        s_ref[t] = s
        m_ref[...] = jnp.maximum(m_ref[...], lane_fold(s))
        return carry

    lax.fori_loop(0, trips, pass1, 0)
    m_ref[...] = jnp.broadcast_to(jnp.max(m_ref[...], axis=1, keepdims=True), (rows, LANES))

    acc_ref[...] = jnp.dot(probs(s_own), v_ref[own, :], preferred_element_type=F32)

    def pass2(t, carry):
        vs = v_ref[pl.ds(pl.multiple_of(t * span, span), span), :]
        acc_ref[...] += jnp.dot(probs(s_ref[t]), vs, preferred_element_type=F32)
        return carry

    lax.fori_loop(0, trips, pass2, 0)
    acc = acc_ref[...]
    o = acc[:, :HEAD_DIM] / acc[:, HEAD_DIM:HEAD_DIM + 1]
    for g in range(KV_GROUP):
        o_ref[:, g * HEAD_DIM:(g + 1) * HEAD_DIM] = o[g * blk:(g + 1) * blk]


def _prompt_attn(qp, kp, vp):
    m = kp.shape[1]
    nb = m // MOBA_BLOCK
    assert nb % ATTN_UNROLL == 0
    blk = MOBA_BLOCK
    rows = KV_GROUP * blk
    return pl.pallas_call(
        _prompt_attn_kernel,
        grid=(N_KV_HEADS, nb),
        in_specs=[pl.BlockSpec((KV_GROUP, blk, 2 * HEAD_DIM), lambda kv, i: (kv, i, 0)),
                  pl.BlockSpec((None, m, 2 * HEAD_DIM), lambda kv, i: (kv, 0, 0)),
                  pl.BlockSpec((None, m, 2 * HEAD_DIM), lambda kv, i: (kv, 0, 0))],
        out_specs=pl.BlockSpec((blk, KV_GROUP * HEAD_DIM), lambda kv, i: (i, kv)),
        out_shape=jax.ShapeDtypeStruct((m, ATT_WIDTH), F32),
        scratch_shapes=[pltpu.VMEM((nb // ATTN_UNROLL, rows, ATTN_UNROLL * blk), F32),
                        pltpu.VMEM((rows, 2 * HEAD_DIM), F32),
                        pltpu.VMEM((rows, LANES), F32)],
        compiler_params=_params("arbitrary", "arbitrary"),
        name="prompt_moba_attn",
    )(qp, kp, vp)


def _expand_heads(v, lane_lo):
    r = v.shape[0]
    parts = []
    for k in range(SSM_HEADS // 2):
        a0 = jnp.broadcast_to(v[:, 2 * k:2 * k + 1], (r, LANES))
        a1 = jnp.broadcast_to(v[:, 2 * k + 1:2 * k + 2], (r, LANES))
        parts.append(jnp.where(lane_lo, a0, a1))
    return jnp.concatenate(parts, axis=1)


def _gated_group_norm(y, z, g):
    yz = y * _silu(z)
    gw = D_INNER // SSM_GROUPS
    outs = []
    for grp in range(SSM_GROUPS):
        t = yz[:, grp * gw:(grp + 1) * gw]
        t = t * lax.rsqrt(jnp.mean(t * t, axis=-1, keepdims=True) + EPS)
        outs.append(t * g[:, grp * gw:(grp + 1) * gw])
    return jnp.concatenate(outs, axis=1)


def _ssd_kernel(z_ref, xbc_ref, dt_ref, cw_ref, cb_ref, dtb_ref, alog_ref, dsk_ref, g_ref,
                y_ref, tail_out_ref, st_out_ref, tail_ref, st_ref):
    c = pl.program_id(0)
    cs = SSD_CHUNK
    gw = D_INNER // SSM_GROUPS

    @pl.when(c == 0)
    def _():
        tail_ref[...] = jnp.zeros_like(tail_ref)
        st_ref[...] = jnp.zeros_like(st_ref)

    xr = xbc_ref[...]
    xp = jnp.concatenate([tail_ref[...], xr], axis=0)
    cw = cw_ref[...]
    conv = cb_ref[...] + cw[3:4] * xr
    for t in range(CONV_WIDTH - 1):
        conv = conv + cw[t:t + 1] * xp[8 - (CONV_WIDTH - 1) + t:8 - (CONV_WIDTH - 1) + t + cs]
    tail_ref[...] = xr[cs - 8:]
    tail_out_ref[...] = xr[cs - 8:]
    xc = _silu(conv)
    xs = xc[:, :D_INNER]
    bm = xc[:, D_INNER:D_INNER + SSM_GROUPS * D_STATE]
    cm = xc[:, D_INNER + SSM_GROUPS * D_STATE:]

    lane = lax.broadcasted_iota(jnp.int32, (cs, LANES), 1)
    rowi = lax.broadcasted_iota(jnp.int32, (cs, LANES), 0)
    lane_lo = lane < SSM_HEAD_DIM
    tri = rowi >= lane

    dt = _softplus(dt_ref[...] + dtb_ref[...])
    a = jnp.where(lane[:1] < SSM_HEADS, -jnp.exp(alog_ref[...]), 0.0)
    da = dt * a
    tril = jnp.where(tri, 1.0, 0.0).astype(BF16)
    p1 = da.astype(BF16)
    r1 = da - p1.astype(F32)
    p2 = r1.astype(BF16)
    p3 = (r1 - p2.astype(F32)).astype(BF16)
    acum = (jnp.dot(tril, p1, preferred_element_type=F32) + jnp.dot(tril, p2, preferred_element_type=F32)
            + jnp.dot(tril, p3, preferred_element_type=F32))
    acum_t = acum.T

    dt_e = _expand_heads(dt, lane_lo)
    ac_e = _expand_heads(acum, lane_lo)
    xdt = xs * dt_e
    ea_e = jnp.exp(ac_e)
    dend_e = jnp.exp(ac_e[cs - 1:cs, :] - ac_e)
    cdec = ea_e[cs - 1:cs, :]
    xdt_bf = xdt.astype(BF16)
    xdec_bf = (xdt * dend_e).astype(BF16)

    y_parts = []
    for grp in range(SSM_GROUPS):
        bg = bm[:, grp * D_STATE:(grp + 1) * D_STATE]
        cg = cm[:, grp * D_STATE:(grp + 1) * D_STATE].astype(BF16)
        cb = lax.dot_general(cg, bg.astype(BF16), NT_DIMS, preferred_element_type=F32)
        hpg = SSM_HEADS // SSM_GROUPS
        intra = []
        for k in range(hpg // 2):
            pair = grp * (hpg // 2) + k
            xpair = xdt_bf[:, pair * LANES:(pair + 1) * LANES]
            acc = None
            for hh in range(2):
                h = 2 * pair + hh
                seg = jnp.broadcast_to(acum[:, h:h + 1], (cs, cs)) - acum_t[h:h + 1, :]
                lmat = jnp.exp(jnp.where(tri, seg, -jnp.inf))
                mh = (cb * lmat).astype(BF16)
                xh = jnp.where(lane_lo if hh == 0 else jnp.logical_not(lane_lo), xpair, jnp.zeros_like(xpair))
                part = jnp.dot(mh, xh, preferred_element_type=F32)
                acc = part if acc is None else acc + part
            intra.append(acc)
        y_intra = jnp.concatenate(intra, axis=1)
        st = st_ref[grp]
        y_inter = jnp.dot(cg, st.astype(BF16), preferred_element_type=F32) * ea_e[:, grp * gw:(grp + 1) * gw]
        new_st = cdec[:, grp * gw:(grp + 1) * gw] * st + jnp.dot(
            bg.T.astype(BF16), xdec_bf[:, grp * gw:(grp + 1) * gw], preferred_element_type=F32)
        st_ref[grp] = new_st
        st_out_ref[grp] = new_st
        y_parts.append(y_intra + y_inter)
    y = jnp.concatenate(y_parts, axis=1) + dsk_ref[...] * xs
    y_ref[...] = _gated_group_norm(y, z_ref[...], g_ref[...]).astype(y_ref.dtype)


def _pad_lanes(v):
    return jnp.pad(v.reshape(1, -1), ((0, 0), (0, LANES - v.size)))


def _ssd_prompt(proj, dtp, conv_w, conv_b, dt_bias, a_log, d_skip, g_ssm):
    m = proj.shape[0]
    cs = SSD_CHUNK
    assert m % cs == 0
    gw = D_INNER // SSM_GROUPS
    z_blk = (ATT_WIDTH + 2 * KV_WIDTH) // D_INNER
    x_blk = (ATT_WIDTH + 2 * KV_WIDTH + D_INNER) // CONV_DIM
    assert z_blk * D_INNER == ATT_WIDTH + 2 * KV_WIDTH and x_blk * CONV_DIM == ATT_WIDTH + 2 * KV_WIDTH + D_INNER
    const = lambda shape: pl.BlockSpec(shape, lambda c: tuple(0 for _ in shape))
    return pl.pallas_call(
        _ssd_kernel,
        grid=(m // cs,),
        in_specs=[pl.BlockSpec((cs, D_INNER), lambda c: (c, z_blk)),
                  pl.BlockSpec((cs, CONV_DIM), lambda c: (c, x_blk)),
                  pl.BlockSpec((cs, LANES), lambda c: (c, 0)),
                  const((CONV_WIDTH, CONV_DIM)), const((1, CONV_DIM)), const((1, LANES)), const((1, LANES)),
                  const((1, D_INNER)), const((1, D_INNER))],
        out_specs=[pl.BlockSpec((cs, D_INNER), lambda c: (c, 0)),
                   const((8, CONV_DIM)), const((SSM_GROUPS, D_STATE, gw))],
        out_shape=[jax.ShapeDtypeStruct((m, D_INNER), BF16),
                   jax.ShapeDtypeStruct((8, CONV_DIM), F32),
                   jax.ShapeDtypeStruct((SSM_GROUPS, D_STATE, gw), F32)],
        scratch_shapes=[pltpu.VMEM((8, CONV_DIM), F32), pltpu.VMEM((SSM_GROUPS, D_STATE, gw), F32)],
        compiler_params=_params("arbitrary"),
        name="prompt_ssd",
    )(proj, proj, dtp, conv_w, conv_b.reshape(1, CONV_DIM), _pad_lanes(dt_bias), _pad_lanes(a_log),
      jnp.repeat(d_skip, SSM_HEAD_DIM).reshape(1, D_INNER), g_ssm.reshape(1, D_INNER))


def _sample_prep_kernel(proj_ref, qg_ref, kg_ref, q_ref, k_ref, v_ref):
    for h in range(ATT_HEADS):
        q_ref[:, h * HEAD_DIM:(h + 1) * HEAD_DIM] = _head_norm(proj_ref[:, h * HEAD_DIM:(h + 1) * HEAD_DIM], qg_ref[...])
    for kv in range(N_KV_HEADS):
        lo = ATT_WIDTH + kv * HEAD_DIM
        k_ref[:, kv * HEAD_DIM:(kv + 1) * HEAD_DIM] = _head_norm(proj_ref[:, lo:lo + HEAD_DIM], kg_ref[...])
    v_ref[...] = proj_ref[:, ATT_WIDTH + KV_WIDTH:ATT_WIDTH + 2 * KV_WIDTH]


def _sample_prep(proj, q_gain, k_gain):
    n = proj.shape[0]
    qkv_w = ATT_WIDTH + 2 * KV_WIDTH
    return pl.pallas_call(
        _sample_prep_kernel,
        grid=(1,),
        in_specs=[pl.BlockSpec((n, qkv_w), lambda i: (0, 0)),
                  pl.BlockSpec((1, HEAD_DIM), lambda i: (0, 0)),
                  pl.BlockSpec((1, HEAD_DIM), lambda i: (0, 0))],
        out_specs=[pl.BlockSpec((n, ATT_WIDTH), lambda i: (0, 0)),
                   pl.BlockSpec((n, KV_WIDTH), lambda i: (0, 0)),
                   pl.BlockSpec((n, KV_WIDTH), lambda i: (0, 0))],
        out_shape=[jax.ShapeDtypeStruct((n, ATT_WIDTH), F32),
                   jax.ShapeDtypeStruct((n, KV_WIDTH), F32),
                   jax.ShapeDtypeStruct((n, KV_WIDTH), F32)],
        compiler_params=_params("arbitrary"),
        name="sample_qk_prep",
    )(proj, q_gain.reshape(1, HEAD_DIM), k_gain.reshape(1, HEAD_DIM))


def _sample_gate_kernel(pps, pt_ref, q_ref, *rest):
    page_refs = rest[:pps]
    idx_ref = rest[pps]
    ksum_ref = rest[pps + 1]
    j = pl.program_id(1)
    ppb = MOBA_BLOCK // page_refs[0].shape[0]
    bps = pps // ppb
    for b in range(bps):
        s = jnp.sum(page_refs[b * ppb][...], axis=0)
        for t in range(1, ppb):
            s = s + jnp.sum(page_refs[b * ppb + t][...], axis=0)
        s = s * (1.0 / MOBA_BLOCK)
        for kv in range(N_KV_HEADS):
            ksum_ref[kv, j, b:b + 1, :] = s[kv:kv + 1, :]

    @pl.when(j == pl.num_programs(1) - 1)
    def _():
        nblk = ksum_ref.shape[1] * bps
        q = q_ref[0]
        hrow = lax.broadcasted_iota(jnp.int32, (ATT_HEADS, nblk), 0)
        gate = jnp.zeros((ATT_HEADS, nblk), F32)
        for kv in range(N_KV_HEADS):
            gk = _dot3(q, ksum_ref[kv].reshape(nblk, HEAD_DIM), NT_DIMS)
            gate = jnp.where(hrow // KV_GROUP == kv, gk, gate)
        lane = lax.broadcasted_iota(jnp.int32, (ATT_HEADS, LANES), 1)
        if nblk < LANES:
            gate = jnp.concatenate([gate, jnp.full((ATT_HEADS, LANES - nblk), -jnp.inf, F32)], axis=1)
        _, firsts = _top3(gate, lane)
        out = jnp.zeros((ATT_HEADS, LANES), jnp.int32)
        for t, first in enumerate(firsts):
            out = jnp.where(lane == t, first, out)
        idx_ref[0] = out


def _sample_gate(q3, cache_k, layer, page_table):
    n, n_pages = page_table.shape
    page = cache_k.shape[2]
    ppb = MOBA_BLOCK // page
    nblk = n_pages // ppb
    assert n_pages % ppb == 0 and MOBA_TOPK <= nblk <= LANES
    pps = min(16, n_pages)
    assert n_pages % pps == 0 and pps % ppb == 0
    steps = n_pages // pps

    def page_spec(t):
        return pl.BlockSpec((None, None, page, N_KV_HEADS, HEAD_DIM),
                            lambda s, j, pt: (layer, pt[s, j * pps + t], 0, 0, 0))

    grid_spec = pltpu.PrefetchScalarGridSpec(
        num_scalar_prefetch=1,
        grid=(n, steps),
        in_specs=[pl.BlockSpec((1, ATT_HEADS, HEAD_DIM), lambda s, j, pt: (s, 0, 0))]
        + [page_spec(t) for t in range(pps)],
        out_specs=pl.BlockSpec((1, ATT_HEADS, LANES), lambda s, j, pt: (s, 0, 0)),
        scratch_shapes=[pltpu.VMEM((N_KV_HEADS, steps, pps // ppb, HEAD_DIM), F32)],
    )
    return pl.pallas_call(
        functools.partial(_sample_gate_kernel, pps),
        grid_spec=grid_spec,
        out_shape=jax.ShapeDtypeStruct((n, ATT_HEADS, LANES), jnp.int32),
        compiler_params=_params("arbitrary", "arbitrary"),
        name="sample_gate_topk",
    )(page_table, q3, *([cache_k] * pps))


def _sample_attn_kernel(past, ppb, layer, pt_ref, idx_ref, q_ref, kn_ref, vn_ref, ck_ref, cv_ref, o_ref,
                        kbuf, vbuf, sems):
    s = pl.program_id(0)
    page = MOBA_BLOCK // ppb

    def copies(h, t, p):
        kv = h // KV_GROUP
        blk = idx_ref[s, h * MOBA_TOPK + t]
        phys = pt_ref[s, blk * ppb + p]
        dst = pl.ds((t * ppb + p) * page, page)
        return (pltpu.make_async_copy(ck_ref.at[layer, phys, :, kv, :], kbuf.at[h, dst, :], sems.at[0, h, t * ppb + p]),
                pltpu.make_async_copy(cv_ref.at[layer, phys, :, kv, :], vbuf.at[h, dst, :], sems.at[1, h, t * ppb + p]))

    triples = [(h, t, p) for h in range(ATT_HEADS) for t in range(MOBA_TOPK) for p in range(ppb)]
    for h, t, p in triples:
        ck, cv = copies(h, t, p)
        ck.start()
        cv.start()
    for h, t, p in triples:
        ck, cv = copies(h, t, p)
        ck.wait()
        cv.wait()

    nsel = MOBA_TOPK * MOBA_BLOCK
    rowi = lax.broadcasted_iota(jnp.int32, (nsel, 1), 0)
    off = jnp.bitwise_and(rowi, MOBA_BLOCK - 1)
    for h in range(ATT_HEADS):
        kv = h // KV_GROUP
        slope = _alibi_slope(h)
        q = q_ref[0, h:h + 1, :]
        sc = jnp.sum(kbuf[h] * q, axis=1, keepdims=True) * ATT_SCALE
        pos = jnp.zeros((nsel, 1), jnp.int32)
        for t in range(MOBA_TOPK):
            pos = jnp.where(rowi // MOBA_BLOCK == t, idx_ref[s, h * MOBA_TOPK + t] * MOBA_BLOCK, pos)
        dist = (past - (pos + off)).astype(F32)
        sc = sc - slope * dist
        s_own = jnp.sum(kn_ref[0, kv:kv + 1, :] * q, axis=1, keepdims=True) * ATT_SCALE
        mx = jnp.maximum(jnp.max(sc, axis=0, keepdims=True), s_own)
        p = jnp.exp(sc - mx)
        p_own = jnp.exp(s_own - mx)
        denom = jnp.sum(p, axis=0, keepdims=True) + p_own
        num = jnp.sum(p * vbuf[h], axis=0, keepdims=True) + p_own * vn_ref[0, kv:kv + 1, :]
        o_ref[0, h:h + 1, :] = num / denom


def _sample_attn(q3, k_new3, v_new3, cache_k, cache_v, layer, page_table, idx):
    n, n_pages = page_table.shape
    page = cache_k.shape[2]
    ppb = MOBA_BLOCK // page
    past = n_pages * page
    assert past % MOBA_BLOCK == 0
    nsel = MOBA_TOPK * MOBA_BLOCK
    grid_spec = pltpu.PrefetchScalarGridSpec(
        num_scalar_prefetch=2,
        grid=(n,),
        in_specs=[pl.BlockSpec((1, ATT_HEADS, HEAD_DIM), lambda s, pt, ix: (s, 0, 0)),
                  pl.BlockSpec((1, N_KV_HEADS, HEAD_DIM), lambda s, pt, ix: (s, 0, 0)),
                  pl.BlockSpec((1, N_KV_HEADS, HEAD_DIM), lambda s, pt, ix: (s, 0, 0)),
                  pl.BlockSpec(memory_space=pl.ANY),
                  pl.BlockSpec(memory_space=pl.ANY)],
        out_specs=pl.BlockSpec((1, ATT_HEADS, HEAD_DIM), lambda s, pt, ix: (s, 0, 0)),
        scratch_shapes=[pltpu.VMEM((ATT_HEADS, nsel, HEAD_DIM), F32),
                        pltpu.VMEM((ATT_HEADS, nsel, HEAD_DIM), F32),
                        pltpu.SemaphoreType.DMA((2, ATT_HEADS, MOBA_TOPK * ppb))],
    )
    return pl.pallas_call(
        functools.partial(_sample_attn_kernel, past, ppb, layer),
        grid_spec=grid_spec,
        out_shape=jax.ShapeDtypeStruct((n, ATT_HEADS, HEAD_DIM), F32),
        compiler_params=_params("arbitrary"),
        name="sample_moba_attn",
    )(page_table, idx, q3, k_new3, v_new3, cache_k, cache_v)


def _sample_ssd_kernel(z_ref, xbc_ref, dt_ref, buf_ref, h0_ref, cw_ref, cb_ref, dtb_ref, alog_ref, dsk_ref, g_ref,
                       y_ref, buf_out_ref, h_out_ref):
    x = xbc_ref[0]
    buf = buf_ref[0]
    cw = cw_ref[...]
    conv = cb_ref[...] + cw[CONV_WIDTH - 1:CONV_WIDTH] * x
    for t in range(CONV_WIDTH - 1):
        conv = conv + cw[t:t + 1] * buf[t:t + 1]
    buf_out_ref[0, 0:CONV_WIDTH - 2, :] = buf[1:CONV_WIDTH - 1]
    buf_out_ref[0, CONV_WIDTH - 2:CONV_WIDTH - 1, :] = x
    xc = _silu(conv)
    xs = xc[:, :D_INNER]
    bm = xc[:, D_INNER:D_INNER + SSM_GROUPS * D_STATE]
    cm = xc[:, D_INNER + SSM_GROUPS * D_STATE:]

    lane1 = lax.broadcasted_iota(jnp.int32, (1, LANES), 1)
    rowi = lax.broadcasted_iota(jnp.int32, (LANES, LANES), 0)
    dt = _softplus(dt_ref[0] + dtb_ref[...])
    a = jnp.where(lane1 < SSM_HEADS, -jnp.exp(alog_ref[...]), 0.0)
    dec = jnp.exp(dt * a)
    dt_e = _expand_heads(dt, lane1 < SSM_HEAD_DIM)
    xdt = xs * dt_e
    xdt_rows = jnp.broadcast_to(xdt, (LANES, D_INNER))

    hpg = SSM_HEADS // SSM_GROUPS
    y_parts = []
    for pair in range(SSM_HEADS // 2):
        grp = (2 * pair) // hpg
        xcol = xdt_rows[:, pair * LANES:(pair + 1) * LANES].T
        dcol = jnp.where(rowi < SSM_HEAD_DIM,
                         jnp.broadcast_to(dec[:, 2 * pair:2 * pair + 1], (LANES, LANES)),
                         jnp.broadcast_to(dec[:, 2 * pair + 1:2 * pair + 2], (LANES, LANES)))
        h0 = h0_ref[0, 2 * pair:2 * pair + 2].reshape(LANES, D_STATE)
        hn = dcol * h0 + xcol * bm[:, grp * D_STATE:(grp + 1) * D_STATE]
        h_out_ref[0, 2 * pair:2 * pair + 2] = hn.reshape(2, SSM_HEAD_DIM, D_STATE)
        cgrow = jnp.broadcast_to(cm[:, grp * D_STATE:(grp + 1) * D_STATE], (8, D_STATE))
        ypair = _dot3(cgrow, hn, NT_DIMS)
        y_parts.append(ypair[0:1])
    y = jnp.concatenate(y_parts, axis=1) + dsk_ref[...] * xs
    y_ref[0] = _gated_group_norm(y, z_ref[0], g_ref[...]).astype(y_ref.dtype)


def _ssd_sample(proj, dtp, state_conv, state_ssm, conv_w, conv_b, dt_bias, a_log, d_skip, g_ssm):
    n = proj.shape[0]
    z0 = ATT_WIDTH + 2 * KV_WIDTH
    z3 = proj[:, z0:z0 + D_INNER].reshape(n, 1, D_INNER)
    x3 = proj[:, z0 + D_INNER:z0 + D_INNER + CONV_DIM].reshape(n, 1, CONV_DIM)
    dt3 = dtp.reshape(n, 1, LANES)
    const = lambda shape: pl.BlockSpec(shape, lambda s: tuple(0 for _ in shape))
    per_seq = lambda shape: pl.BlockSpec((1,) + shape, lambda s: (s,) + tuple(0 for _ in shape))
    y, buf, h = pl.pallas_call(
        _sample_ssd_kernel,
        grid=(n,),
        in_specs=[per_seq((1, D_INNER)), per_seq((1, CONV_DIM)), per_seq((1, LANES)),
                  per_seq((CONV_WIDTH - 1, CONV_DIM)), per_seq((SSM_HEADS, SSM_HEAD_DIM, D_STATE)),
                  const((CONV_WIDTH, CONV_DIM)), const((1, CONV_DIM)), const((1, LANES)), const((1, LANES)),
                  const((1, D_INNER)), const((1, D_INNER))],
        out_specs=[per_seq((1, D_INNER)), per_seq((CONV_WIDTH - 1, CONV_DIM)),
                   per_seq((SSM_HEADS, SSM_HEAD_DIM, D_STATE))],
        out_shape=[jax.ShapeDtypeStruct((n, 1, D_INNER), BF16),
                   jax.ShapeDtypeStruct((n, CONV_WIDTH - 1, CONV_DIM), F32),
                   jax.ShapeDtypeStruct((n, SSM_HEADS, SSM_HEAD_DIM, D_STATE), F32)],
        compiler_params=_params("arbitrary"),
        name="sample_ssd",
    )(z3, x3, dt3, state_conv, state_ssm, conv_w, conv_b.reshape(1, CONV_DIM), _pad_lanes(dt_bias),
      _pad_lanes(a_log), jnp.repeat(d_skip, SSM_HEAD_DIM).reshape(1, D_INNER), g_ssm.reshape(1, D_INNER))
    return y.reshape(n, D_INNER), buf, h


def _layer_tail(x, mod, o_att, y_ssm, g_att_out, w_out, g_ffn, w_gate, w_up, w_down):
    a = _rms_rows(o_att, g_att_out)
    x1 = _outproj(a, y_ssm, w_out, x, mod, 2)
    h2 = _norm_mod(x1, g_ffn, mod, 4, 3)
    hid = _gateup(h2, w_gate, w_up)
    return _down(hid, w_down, x1, mod, 5)


def kernel(x_prompt, x_sample, cache_k, cache_v, state_conv, state_ssm, page_table, c_prompt, c_sample, w_ada, b_ada, g_mix_norm, w_in, q_gain, k_gain, g_att_out, conv_w, conv_b, dt_bias, a_log, d_skip, g_ssm_out, w_out, g_ffn_norm, w_gate, w_up, w_down):
    n_p, seq, d = x_prompt.shape
    n_s, dec_seq, _ = x_sample.shape
    assert n_p == 1 and dec_seq == 1
    depth = w_ada.shape[0]
    main_w = ATT_WIDTH + 2 * KV_WIDTH + D_INNER + CONV_DIM
    pool, page = cache_k.shape[1], cache_k.shape[2]

    yp = x_prompt.reshape(seq, d)
    ys = x_sample.reshape(n_s, d)
    c_rows = n_p + n_s
    c_pad = -(-c_rows // 8) * 8
    c_all = jnp.pad(jnp.concatenate([c_prompt, c_sample], axis=0), ((0, c_pad - c_rows), (0, 0)))
    outs = [[] for _ in range(8)]
    for l in range(depth):
        mod = _ada(c_all, w_ada[l], b_ada[l])
        mod_p, mod_s = mod[0:1], mod[1:1 + n_s]
        w_dt = jnp.pad(w_in[l][:, main_w:], ((0, 0), (0, LANES - SSM_HEADS)))

        hn = _norm_mod(yp, g_mix_norm[l], mod_p, 1, 0)
        proj = _mm(hn, w_in[l], main_w, tn=512, name="in_proj")
        dtp = _mm(hn, w_dt, LANES, tn=LANES, name="in_proj_dt")
        k_out, v_out, kp, vp, qp = _prompt_prep(proj, q_gain[l], k_gain[l])
        o_att = _prompt_attn(qp, kp, vp)
        y_ssm, tail, st = _ssd_prompt(proj, dtp, conv_w[l], conv_b[l], dt_bias[l], a_log[l], d_skip[l], g_ssm_out[l])
        yp = _layer_tail(yp, mod_p, o_att, y_ssm, g_att_out[l], w_out[l], g_ffn_norm[l], w_gate[l], w_up[l], w_down[l])
        hpg = SSM_HEADS // SSM_GROUPS
        ssm_p = st.reshape(SSM_GROUPS, D_STATE, hpg, SSM_HEAD_DIM).transpose(0, 2, 3, 1).reshape(
            1, SSM_HEADS, SSM_HEAD_DIM, D_STATE)
        outs[0].append(k_out.reshape(1, seq, N_KV_HEADS, HEAD_DIM))
        outs[1].append(v_out.reshape(1, seq, N_KV_HEADS, HEAD_DIM))
        outs[2].append(tail[8 - (CONV_WIDTH - 1):].reshape(1, CONV_WIDTH - 1, CONV_DIM))
        outs[3].append(ssm_p)

        hs = _norm_mod(ys, g_mix_norm[l], mod_s, 1, 0)
        proj_s = _mm(hs, w_in[l], main_w, tn=512, name="in_proj")
        dts = _mm(hs, w_dt, LANES, tn=LANES, name="in_proj_dt")
        q_s, k_s, v_s = _sample_prep(proj_s, q_gain[l], k_gain[l])
        q3 = q_s.reshape(n_s, ATT_HEADS, HEAD_DIM)
        idx = _sample_gate(q3, cache_k, l, page_table)
        idx_flat = idx[:, :, :MOBA_TOPK].reshape(n_s, ATT_HEADS * MOBA_TOPK)
        o_s = _sample_attn(q3, k_s.reshape(n_s, N_KV_HEADS, HEAD_DIM), v_s.reshape(n_s, N_KV_HEADS, HEAD_DIM),
                           cache_k, cache_v, l, page_table, idx_flat)
        y_s, buf_s, h_s = _ssd_sample(proj_s, dts, state_conv[l], state_ssm[l], conv_w[l], conv_b[l], dt_bias[l],
                                      a_log[l], d_skip[l], g_ssm_out[l])
        ys = _layer_tail(ys, mod_s, o_s.reshape(n_s, ATT_WIDTH), y_s, g_att_out[l], w_out[l], g_ffn_norm[l],
                         w_gate[l], w_up[l], w_down[l])
        outs[4].append(k_s.reshape(n_s, 1, N_KV_HEADS, HEAD_DIM))
        outs[5].append(v_s.reshape(n_s, 1, N_KV_HEADS, HEAD_DIM))
        outs[6].append(buf_s)
        outs[7].append(h_s)
    stacked = [jnp.stack(o) for o in outs]
    return (yp.reshape(1, seq, d), ys.reshape(n_s, 1, d), *stacked)
```

```python
import functools

import jax
import jax.numpy as jnp
from jax import lax
from jax.experimental import pallas as pl
from jax.experimental.pallas import tpu as pltpu

F32 = jnp.float32
BF16 = jnp.bfloat16

HEAD_DIM = 128
ATT_HEADS = 8
N_KV_HEADS = 4
KV_GROUP = ATT_HEADS // N_KV_HEADS
ATT_WIDTH = ATT_HEADS * HEAD_DIM
KV_WIDTH = N_KV_HEADS * HEAD_DIM
MOBA_BLOCK = 256
MOBA_TOPK = 3
D_INNER = 1024
SSM_HEAD_DIM = 64
SSM_HEADS = D_INNER // SSM_HEAD_DIM
SSM_GROUPS = 2
D_STATE = 128
CONV_WIDTH = 4
CONV_DIM = D_INNER + 2 * SSM_GROUPS * D_STATE
SSD_CHUNK = 128
EPS = 1e-6
ATT_SCALE = HEAD_DIM ** -0.5

LANES = 128
FEAT_KBLK = 96
FEAT_KOFF = 97
FEAT_ONE_A = 98
FEAT_ONE_B = 99
NEG_BIG = -1e30

NT_DIMS = (((1,), (1,)), ((), ()))
VMEM_LIMIT = 56 * 1024 * 1024


def _params(*sem):
    return pltpu.CompilerParams(dimension_semantics=sem, vmem_limit_bytes=VMEM_LIMIT)


def _silu(x):
    return x / (1.0 + jnp.exp(-x))


def _softplus(x):
    return jnp.maximum(x, 0.0) + jnp.log1p(jnp.exp(-jnp.abs(x)))


def _split_bf16(x):
    hi = x.astype(BF16)
    lo = (x - hi.astype(F32)).astype(BF16)
    return hi, lo


def _dot3(a, b, dims):
    ah, al = _split_bf16(a)
    bh, bl = _split_bf16(b)
    d = lambda x, y: lax.dot_general(x, y, dims, preferred_element_type=F32)
    return d(ah, bh) + d(al, bh) + d(ah, bl)


def _alibi_slope(h):
    return 2.0 ** (-8.0 * (h + 1) / ATT_HEADS)


def _ada_kernel(c_ref, w_ref, b_ref, o_ref):
    a = _silu(c_ref[...]).astype(BF16)
    o_ref[...] = jnp.dot(a, w_ref[...].astype(BF16), preferred_element_type=F32) + b_ref[...]


def _ada(c_all, w, b):
    rows, d = c_all.shape
    n = w.shape[1]
    tn = 1024
    return pl.pallas_call(
        _ada_kernel,
        grid=(n // tn,),
        in_specs=[pl.BlockSpec((rows, d), lambda j: (0, 0)),
                  pl.BlockSpec((d, tn), lambda j: (0, j)),
                  pl.BlockSpec((1, tn), lambda j: (0, j))],
        out_specs=pl.BlockSpec((rows, tn), lambda j: (0, j)),
        out_shape=jax.ShapeDtypeStruct((rows, n), F32),
        compiler_params=_params("arbitrary"),
        name="ada_mod",
    )(c_all, w, b.reshape(1, n))


def _mod_spec(mod_rows, tm, width, col_of):
    if mod_rows == 1:
        return pl.BlockSpec((1, width), lambda n, i: (0, col_of(n)))
    return pl.BlockSpec((tm, width), lambda n, i: (i, col_of(n)))


def _rms(x):
    return x * lax.rsqrt(jnp.mean(x * x, axis=-1, keepdims=True) + EPS)


def _cast_weight_once(w_ref, wbf_ref):
    @pl.when(pl.program_id(1) == 0)
    def _():
        wbf_ref[...] = w_ref[...].astype(BF16)


def _inproj_kernel(x_ref, g_ref, sc_ref, sh_ref, w_ref, o_ref, wbf_ref):
    _cast_weight_once(w_ref, wbf_ref)
    hn = ((_rms(x_ref[...]) * g_ref[...]) * (1.0 + sc_ref[...]) + sh_ref[...]).astype(BF16)
    o_ref[...] = jnp.dot(hn, wbf_ref[...], preferred_element_type=F32)


def _inproj(x, g, mod, w, n_cols, *, tn, tm, name):
    m, d = x.shape
    tm = min(tm, m)
    mr = mod.shape[0]
    return pl.pallas_call(
        _inproj_kernel,
        grid=(n_cols // tn, m // tm),
        in_specs=[pl.BlockSpec((tm, d), lambda n, i: (i, 0)),
                  pl.BlockSpec((1, d), lambda n, i: (0, 0)),
                  _mod_spec(mr, tm, d, lambda n: 1),
                  _mod_spec(mr, tm, d, lambda n: 0),
                  pl.BlockSpec((d, tn), lambda n, i: (0, n))],
        out_specs=pl.BlockSpec((tm, tn), lambda n, i: (i, n)),
        out_shape=jax.ShapeDtypeStruct((m, n_cols), F32),
        scratch_shapes=[pltpu.VMEM((d, tn), BF16)],
        compiler_params=_params("arbitrary", "arbitrary"),
        name=name,
    )(x, g.reshape(1, d), mod, mod, w)


def _outproj_kernel(a_ref, ga_ref, b_ref, w_ref, x_ref, gt_ref, o_ref, wbf_ref):
    _cast_weight_once(w_ref, wbf_ref)
    ka = a_ref.shape[1]
    a = (_rms(a_ref[...]) * ga_ref[...]).astype(BF16)
    acc = jnp.dot(a, wbf_ref[:ka, :], preferred_element_type=F32)
    acc = acc + jnp.dot(b_ref[...], wbf_ref[ka:, :], preferred_element_type=F32)
    o_ref[...] = x_ref[...] + gt_ref[...] * acc


def _outproj(a, g_a, b, w, x, mod):
    m, ka = a.shape
    kb = b.shape[1]
    d = w.shape[1]
    tn, tm = 1024, min(512, m)
    mr = mod.shape[0]
    nb = d // tn
    return pl.pallas_call(
        _outproj_kernel,
        grid=(nb, m // tm),
        in_specs=[pl.BlockSpec((tm, ka), lambda n, i: (i, 0)),
                  pl.BlockSpec((1, ka), lambda n, i: (0, 0)),
                  pl.BlockSpec((tm, kb), lambda n, i: (i, 0)),
                  pl.BlockSpec((ka + kb, tn), lambda n, i: (0, n)),
                  pl.BlockSpec((tm, tn), lambda n, i: (i, n)),
                  _mod_spec(mr, tm, tn, lambda n: 2 * nb + n)],
        out_specs=pl.BlockSpec((tm, tn), lambda n, i: (i, n)),
        out_shape=jax.ShapeDtypeStruct((m, d), F32),
        scratch_shapes=[pltpu.VMEM((ka + kb, tn), BF16)],
        compiler_params=_params("arbitrary", "arbitrary"),
        name="out_proj",
    )(a, g_a.reshape(1, ka), b, w, x, mod)


def _gateup_kernel(x_ref, g_ref, sc_ref, sh_ref, wg_ref, wu_ref, o_ref, wgb_ref, wub_ref):
    _cast_weight_once(wg_ref, wgb_ref)
    _cast_weight_once(wu_ref, wub_ref)
    h2 = ((_rms(x_ref[...]) * g_ref[...]) * (1.0 + sc_ref[...]) + sh_ref[...]).astype(BF16)
    g = jnp.dot(h2, wgb_ref[...], preferred_element_type=F32)
    u = jnp.dot(h2, wub_ref[...], preferred_element_type=F32)
    o_ref[...] = (_silu(g) * u).astype(o_ref.dtype)


def _gateup(x, g, mod, wg, wu):
    m, d = x.shape
    f = wg.shape[1]
    tn, tm = 512, min(512, m)
    mr = mod.shape[0]
    return pl.pallas_call(
        _gateup_kernel,
        grid=(f // tn, m // tm),
        in_specs=[pl.BlockSpec((tm, d), lambda n, i: (i, 0)),
                  pl.BlockSpec((1, d), lambda n, i: (0, 0)),
                  _mod_spec(mr, tm, d, lambda n: 4),
                  _mod_spec(mr, tm, d, lambda n: 3),
                  pl.BlockSpec((d, tn), lambda n, i: (0, n)),
                  pl.BlockSpec((d, tn), lambda n, i: (0, n))],
        out_specs=pl.BlockSpec((tm, tn), lambda n, i: (i, n)),
        out_shape=jax.ShapeDtypeStruct((m, f), BF16),
        scratch_shapes=[pltpu.VMEM((d, tn), BF16), pltpu.VMEM((d, tn), BF16)],
        compiler_params=_params("arbitrary", "arbitrary"),
        name="ffn_gate_up",
    )(x, g.reshape(1, d), mod, mod, wg, wu)


def _down_kernel(h_ref, w_ref, x_ref, gt_ref, o_ref, wbf_ref):
    _cast_weight_once(w_ref, wbf_ref)
    acc = jnp.dot(h_ref[...], wbf_ref[...], preferred_element_type=F32)
    o_ref[...] = x_ref[...] + gt_ref[...] * acc


def _down(h, w, x, mod):
    m, f = h.shape
    d = w.shape[1]
    tn, tm = 512, min(512, m)
    mr = mod.shape[0]
    nb = d // tn
    return pl.pallas_call(
        _down_kernel,
        grid=(nb, m // tm),
        in_specs=[pl.BlockSpec((tm, f), lambda n, i: (i, 0)),
                  pl.BlockSpec((f, tn), lambda n, i: (0, n)),
                  pl.BlockSpec((tm, tn), lambda n, i: (i, n)),
                  _mod_spec(mr, tm, tn, lambda n: 5 * nb + n)],
        out_specs=pl.BlockSpec((tm, tn), lambda n, i: (i, n)),
        out_shape=jax.ShapeDtypeStruct((m, d), F32),
        scratch_shapes=[pltpu.VMEM((f, tn), BF16)],
        compiler_params=_params("arbitrary", "arbitrary"),
        name="ffn_down",
    )(h, w, x, mod)


def _head_norm(x, gain):
    return x * lax.rsqrt(jnp.mean(x * x, axis=-1, keepdims=True) + EPS) * gain


def _top3(g, idx_f, axis):
    sel = jnp.zeros(g.shape, F32)
    firsts = []
    for _ in range(MOBA_TOPK):
        mx = jnp.max(g, axis=axis, keepdims=True)
        ismax = jnp.logical_and(g == mx, mx > -jnp.inf)
        first = jnp.min(jnp.where(ismax, idx_f, float(LANES)), axis=axis, keepdims=True)
        pick = idx_f == first
        sel = jnp.where(pick, 1.0, sel)
        g = jnp.where(pick, -jnp.inf, g)
        firsts.append(first)
    return sel, firsts


def _prompt_prep_kernel(proj_ref, qg_ref, kg_ref, kout_ref, vout_ref, kp_ref, vp_ref, qp_ref, km_ref):
    i = pl.program_id(0)
    blk = MOBA_BLOCK

    @pl.when(i == 0)
    def _():
        km_ref[...] = jnp.zeros_like(km_ref)

    lane = lax.broadcasted_iota(jnp.int32, (blk, LANES), 1)
    row = lax.broadcasted_iota(jnp.int32, (blk, LANES), 0)
    sq_row = lax.broadcasted_iota(jnp.int32, (LANES, LANES), 0)
    i_f = i.astype(F32)
    row_f = row.astype(F32)

    kfeat = jnp.where(lane == i, 1.0, 0.0)
    kfeat = jnp.where(lane == FEAT_KBLK, i_f, kfeat)
    kfeat = jnp.where(lane == FEAT_KOFF, row_f, kfeat)
    kfeat = jnp.where(jnp.logical_or(lane == FEAT_ONE_A, lane == FEAT_ONE_B), 1.0, kfeat).astype(BF16)
    vfeat = jnp.where(lane == 0, 1.0, 0.0).astype(BF16)

    kg = kg_ref[...]
    for kv in range(N_KV_HEADS):
        k = proj_ref[:, ATT_WIDTH + kv * HEAD_DIM:ATT_WIDTH + (kv + 1) * HEAD_DIM]
        kn = _head_norm(k, kg)
        kout_ref[:, kv * HEAD_DIM:(kv + 1) * HEAD_DIM] = kn
        kp_ref[kv, :, :HEAD_DIM] = kn.astype(BF16)
        kp_ref[kv, :, HEAD_DIM:] = kfeat
        ksum = jnp.sum(kn, axis=0, keepdims=True) * (1.0 / blk)
        km_ref[kv] = jnp.where(sq_row == i, jnp.broadcast_to(ksum, (LANES, LANES)), km_ref[kv])
        v = proj_ref[:, ATT_WIDTH + KV_WIDTH + kv * HEAD_DIM:ATT_WIDTH + KV_WIDTH + (kv + 1) * HEAD_DIM]
        vout_ref[:, kv * HEAD_DIM:(kv + 1) * HEAD_DIM] = v
        vp_ref[kv, :, :HEAD_DIM] = v.astype(BF16)
        vp_ref[kv, :, HEAD_DIM:] = vfeat

    qg = qg_ref[...]
    blk_id = lax.broadcasted_iota(jnp.int32, (LANES, blk), 0)
    blk_id_f = blk_id.astype(F32)
    valid = blk_id < i
    for h in range(ATT_HEADS):
        q = proj_ref[:, h * HEAD_DIM:(h + 1) * HEAD_DIM]
        qn = _head_norm(q, qg)
        gate = _dot3(km_ref[h // KV_GROUP], qn, NT_DIMS)
        sel_t, _ = _top3(jnp.where(valid, gate, -jnp.inf), blk_id_f, 0)
        sel = sel_t.T
        slope = _alibi_slope(h)
        qfeat = jnp.where(jnp.logical_and(lane < FEAT_KBLK, sel == 0.0), NEG_BIG, 0.0)
        qfeat = jnp.where(lane == FEAT_KBLK, slope * blk, qfeat)
        qfeat = jnp.where(lane == FEAT_KOFF, slope, qfeat)
        qfeat = jnp.where(lane == FEAT_ONE_A, -(slope * blk) * i_f, qfeat)
        qfeat = jnp.where(lane == FEAT_ONE_B, -slope * row_f, qfeat)
        qp_ref[h, :, :HEAD_DIM] = (qn * ATT_SCALE).astype(BF16)
        qp_ref[h, :, HEAD_DIM:] = qfeat.astype(BF16)


def _prompt_prep(proj, q_gain, k_gain):
    m = proj.shape[0]
    nb = m // MOBA_BLOCK
    assert m % MOBA_BLOCK == 0 and nb <= FEAT_KBLK
    blk = MOBA_BLOCK
    qkv_w = ATT_WIDTH + 2 * KV_WIDTH
    return pl.pallas_call(
        _prompt_prep_kernel,
        grid=(nb,),
        in_specs=[pl.BlockSpec((blk, qkv_w), lambda i: (i, 0)),
                  pl.BlockSpec((1, HEAD_DIM), lambda i: (0, 0)),
                  pl.BlockSpec((1, HEAD_DIM), lambda i: (0, 0))],
        out_specs=[pl.BlockSpec((blk, KV_WIDTH), lambda i: (i, 0)),
                   pl.BlockSpec((blk, KV_WIDTH), lambda i: (i, 0)),
                   pl.BlockSpec((N_KV_HEADS, blk, 2 * HEAD_DIM), lambda i: (0, i, 0)),
                   pl.BlockSpec((N_KV_HEADS, blk, 2 * HEAD_DIM), lambda i: (0, i, 0)),
                   pl.BlockSpec((ATT_HEADS, blk, 2 * HEAD_DIM), lambda i: (0, i, 0))],
        out_shape=[jax.ShapeDtypeStruct((m, KV_WIDTH), F32),
                   jax.ShapeDtypeStruct((m, KV_WIDTH), F32),
                   jax.ShapeDtypeStruct((N_KV_HEADS, m, 2 * HEAD_DIM), BF16),
                   jax.ShapeDtypeStruct((N_KV_HEADS, m, 2 * HEAD_DIM), BF16),
                   jax.ShapeDtypeStruct((ATT_HEADS, m, 2 * HEAD_DIM), BF16)],
        scratch_shapes=[pltpu.VMEM((N_KV_HEADS, LANES, LANES), F32)],
        compiler_params=_params("arbitrary"),
        name="prompt_qk_prep",
    )(proj, q_gain.reshape(1, HEAD_DIM), k_gain.reshape(1, HEAD_DIM))


ATTN_UNROLL = 4
LOG2E = 1.4426950408889634


def _prompt_attn_kernel(q_ref, k_ref, v_ref, o_ref, s_ref, acc_ref, m_ref):
    i = pl.program_id(1)
    blk = MOBA_BLOCK
    rows = KV_GROUP * blk
    span = ATTN_UNROLL * blk
    qs = q_ref[...].reshape(rows, 2 * HEAD_DIM)

    def lane_fold(s):
        out = s[:, :LANES]
        for t in range(1, s.shape[1] // LANES):
            out = jnp.maximum(out, s[:, t * LANES:(t + 1) * LANES])
        return out

    def probs(s):
        mb = m_ref[...]
        return jnp.concatenate([jnp.exp2(s[:, t * LANES:(t + 1) * LANES] - mb) for t in range(s.shape[1] // LANES)],
                               axis=1).astype(BF16)

    own = pl.ds(pl.multiple_of(i * blk, blk), blk)
    r = lax.broadcasted_iota(jnp.int32, (rows, blk), 0)
    c = lax.broadcasted_iota(jnp.int32, (rows, blk), 1)
    dist = jnp.bitwise_and(r, blk - 1) - c
    slope = qs[:, HEAD_DIM + FEAT_KOFF:HEAD_DIM + FEAT_KOFF + 1].astype(F32)
    s_own = lax.dot_general(qs[:, :HEAD_DIM], k_ref[own, :HEAD_DIM], NT_DIMS, preferred_element_type=F32)
    s_own = jnp.where(dist >= 0, (s_own - slope * dist.astype(F32)) * LOG2E, -jnp.inf)
    m_ref[...] = lane_fold(s_own)

    trips = (i + ATTN_UNROLL - 1) // ATTN_UNROLL

    def pass1(t, carry):
        ks = k_ref[pl.ds(pl.multiple_of(t * span, span), span), :]
        s = lax.dot_general(qs, ks, NT_DIMS, preferred_element_type=F32) * LOG2E
        s_ref[t] = s
        m_ref[...] = jnp.maximum(m_ref[...], lane_fold(s))
        return carry

    lax.fori_loop(0, trips, pass1, 0)
    m_ref[...] = jnp.broadcast_to(jnp.max(m_ref[...], axis=1, keepdims=True), (rows, LANES))

    acc_ref[...] = jnp.dot(probs(s_own), v_ref[own, :], preferred_element_type=F32)

    def pass2(t, carry):
        vs = v_ref[pl.ds(pl.multiple_of(t * span, span), span), :]
        acc_ref[...] += jnp.dot(probs(s_ref[t]), vs, preferred_element_type=F32)
        return carry

    lax.fori_loop(0, trips, pass2, 0)
    acc = acc_ref[...]
    o = acc[:, :HEAD_DIM] / acc[:, HEAD_DIM:HEAD_DIM + 1]
    for g in range(KV_GROUP):
        o_ref[:, g * HEAD_DIM:(g + 1) * HEAD_DIM] = o[g * blk:(g + 1) * blk]


def _prompt_attn(qp, kp, vp):
    m = kp.shape[1]
    nb = m // MOBA_BLOCK
    assert nb % ATTN_UNROLL == 0
    blk = MOBA_BLOCK
    rows = KV_GROUP * blk
    return pl.pallas_call(
        _prompt_attn_kernel,
        grid=(N_KV_HEADS, nb),
        in_specs=[pl.BlockSpec((KV_GROUP, blk, 2 * HEAD_DIM), lambda kv, i: (kv, i, 0)),
                  pl.BlockSpec((None, m, 2 * HEAD_DIM), lambda kv, i: (kv, 0, 0)),
                  pl.BlockSpec((None, m, 2 * HEAD_DIM), lambda kv, i: (kv, 0, 0))],
        out_specs=pl.BlockSpec((blk, KV_GROUP * HEAD_DIM), lambda kv, i: (i, kv)),
        out_shape=jax.ShapeDtypeStruct((m, ATT_WIDTH), F32),
        scratch_shapes=[pltpu.VMEM((nb // ATTN_UNROLL, rows, ATTN_UNROLL * blk), F32),
                        pltpu.VMEM((rows, 2 * HEAD_DIM), F32),
                        pltpu.VMEM((rows, LANES), F32)],
        compiler_params=_params("arbitrary", "arbitrary"),
        name="prompt_moba_attn",
    )(qp, kp, vp)


def _expand_heads(v, lane_lo):
    r = v.shape[0]
    parts = []
    for k in range(SSM_HEADS // 2):
        a0 = jnp.broadcast_to(v[:, 2 * k:2 * k + 1], (r, LANES))
        a1 = jnp.broadcast_to(v[:, 2 * k + 1:2 * k + 2], (r, LANES))
        parts.append(jnp.where(lane_lo, a0, a1))
    return jnp.concatenate(parts, axis=1)


def _gated_group_norm(y, z, g):
    yz = y * _silu(z)
    gw = D_INNER // SSM_GROUPS
    outs = []
    for grp in range(SSM_GROUPS):
        t = yz[:, grp * gw:(grp + 1) * gw]
        t = t * lax.rsqrt(jnp.mean(t * t, axis=-1, keepdims=True) + EPS)
        outs.append(t * g[:, grp * gw:(grp + 1) * gw])
    return jnp.concatenate(outs, axis=1)


def _ssd_kernel(z_ref, xbc_ref, dt_ref, cw_ref, cb_ref, dtb_ref, alog_ref, dsk_ref, g_ref,
                y_ref, tail_out_ref, st_out_ref, tail_ref, st_ref):
    c = pl.program_id(0)
    cs = SSD_CHUNK
    gw = D_INNER // SSM_GROUPS

    @pl.when(c == 0)
    def _():
        tail_ref[...] = jnp.zeros_like(tail_ref)
        st_ref[...] = jnp.zeros_like(st_ref)

    xr = xbc_ref[...]
    xp = jnp.concatenate([tail_ref[...], xr], axis=0)
    cw = cw_ref[...]
    conv = cb_ref[...] + cw[3:4] * xr
    for t in range(CONV_WIDTH - 1):
        conv = conv + cw[t:t + 1] * xp[8 - (CONV_WIDTH - 1) + t:8 - (CONV_WIDTH - 1) + t + cs]
    tail_ref[...] = xr[cs - 8:]
    tail_out_ref[...] = xr[cs - 8:]
    xc = _silu(conv)
    xs = xc[:, :D_INNER]
    bm = xc[:, D_INNER:D_INNER + SSM_GROUPS * D_STATE]
    cm = xc[:, D_INNER + SSM_GROUPS * D_STATE:]

    lane = lax.broadcasted_iota(jnp.int32, (cs, LANES), 1)
    rowi = lax.broadcasted_iota(jnp.int32, (cs, LANES), 0)
    lane_lo = lane < SSM_HEAD_DIM
    tri = rowi >= lane

    dt = _softplus(dt_ref[...] + dtb_ref[...])
    a = jnp.where(lane[:1] < SSM_HEADS, -jnp.exp(alog_ref[...]), 0.0)
    da = dt * a
    tril = jnp.where(tri, 1.0, 0.0).astype(BF16)
    p1 = da.astype(BF16)
    r1 = da - p1.astype(F32)
    p2 = r1.astype(BF16)
    p3 = (r1 - p2.astype(F32)).astype(BF16)
    acum = (jnp.dot(tril, p1, preferred_element_type=F32) + jnp.dot(tril, p2, preferred_element_type=F32)
            + jnp.dot(tril, p3, preferred_element_type=F32))
    acum_t = acum.T

    dt_e = _expand_heads(dt, lane_lo)
    ac_e = _expand_heads(acum, lane_lo)
    xdt = xs * dt_e
    ea_e = jnp.exp(ac_e)
    dend_e = jnp.exp(ac_e[cs - 1:cs, :] - ac_e)
    cdec = ea_e[cs - 1:cs, :]
    xdt_bf = xdt.astype(BF16)
    xdec_bf = (xdt * dend_e).astype(BF16)

    y_parts = []
    for grp in range(SSM_GROUPS):
        bg = bm[:, grp * D_STATE:(grp + 1) * D_STATE]
        cg = cm[:, grp * D_STATE:(grp + 1) * D_STATE].astype(BF16)
        cb = lax.dot_general(cg, bg.astype(BF16), NT_DIMS, preferred_element_type=F32)
        hpg = SSM_HEADS // SSM_GROUPS
        intra = []
        for k in range(hpg // 2):
            pair = grp * (hpg // 2) + k
            xpair = xdt_bf[:, pair * LANES:(pair + 1) * LANES]
            acc = None
            for hh in range(2):
                h = 2 * pair + hh
                seg = jnp.broadcast_to(acum[:, h:h + 1], (cs, cs)) - acum_t[h:h + 1, :]
                lmat = jnp.exp(jnp.where(tri, seg, -jnp.inf))
                mh = (cb * lmat).astype(BF16)
                xh = jnp.where(lane_lo if hh == 0 else jnp.logical_not(lane_lo), xpair, jnp.zeros_like(xpair))
                part = jnp.dot(mh, xh, preferred_element_type=F32)
                acc = part if acc is None else acc + part
            intra.append(acc)
        y_intra = jnp.concatenate(intra, axis=1)
        st = st_ref[grp]
        y_inter = jnp.dot(cg, st.astype(BF16), preferred_element_type=F32) * ea_e[:, grp * gw:(grp + 1) * gw]
        new_st = cdec[:, grp * gw:(grp + 1) * gw] * st + jnp.dot(
            bg.T.astype(BF16), xdec_bf[:, grp * gw:(grp + 1) * gw], preferred_element_type=F32)
        st_ref[grp] = new_st
        st_out_ref[grp] = new_st
        y_parts.append(y_intra + y_inter)
    y = jnp.concatenate(y_parts, axis=1) + dsk_ref[...] * xs
    y_ref[...] = _gated_group_norm(y, z_ref[...], g_ref[...]).astype(y_ref.dtype)


def _pad_lanes(v):
    return jnp.pad(v.reshape(1, -1), ((0, 0), (0, LANES - v.size)))


def _ssd_prompt(proj, dtp, conv_w, conv_b, dt_bias, a_log, d_skip, g_ssm):
    m = proj.shape[0]
    cs = SSD_CHUNK
    assert m % cs == 0
    gw = D_INNER // SSM_GROUPS
    z_blk = (ATT_WIDTH + 2 * KV_WIDTH) // D_INNER
    x_blk = (ATT_WIDTH + 2 * KV_WIDTH + D_INNER) // CONV_DIM
    assert z_blk * D_INNER == ATT_WIDTH + 2 * KV_WIDTH and x_blk * CONV_DIM == ATT_WIDTH + 2 * KV_WIDTH + D_INNER
    const = lambda shape: pl.BlockSpec(shape, lambda c: tuple(0 for _ in shape))
    return pl.pallas_call(
        _ssd_kernel,
        grid=(m // cs,),
        in_specs=[pl.BlockSpec((cs, D_INNER), lambda c: (c, z_blk)),
                  pl.BlockSpec((cs, CONV_DIM), lambda c: (c, x_blk)),
                  pl.BlockSpec((cs, LANES), lambda c: (c, 0)),
                  const((CONV_WIDTH, CONV_DIM)), const((1, CONV_DIM)), const((1, LANES)), const((1, LANES)),
                  const((1, D_INNER)), const((1, D_INNER))],
        out_specs=[pl.BlockSpec((cs, D_INNER), lambda c: (c, 0)),
                   const((8, CONV_DIM)), const((SSM_GROUPS, D_STATE, gw))],
        out_shape=[jax.ShapeDtypeStruct((m, D_INNER), BF16),
                   jax.ShapeDtypeStruct((8, CONV_DIM), F32),
                   jax.ShapeDtypeStruct((SSM_GROUPS, D_STATE, gw), F32)],
        scratch_shapes=[pltpu.VMEM((8, CONV_DIM), F32), pltpu.VMEM((SSM_GROUPS, D_STATE, gw), F32)],
        compiler_params=_params("arbitrary"),
        name="prompt_ssd",
    )(proj, proj, dtp, conv_w, conv_b.reshape(1, CONV_DIM), _pad_lanes(dt_bias), _pad_lanes(a_log),
      jnp.repeat(d_skip, SSM_HEAD_DIM).reshape(1, D_INNER), g_ssm.reshape(1, D_INNER))


def _sample_prep_kernel(proj_ref, qg_ref, kg_ref, q_ref, k_ref, v_ref):
    for h in range(ATT_HEADS):
        q_ref[:, h * HEAD_DIM:(h + 1) * HEAD_DIM] = _head_norm(proj_ref[:, h * HEAD_DIM:(h + 1) * HEAD_DIM], qg_ref[...])
    for kv in range(N_KV_HEADS):
        lo = ATT_WIDTH + kv * HEAD_DIM
        k_ref[:, kv * HEAD_DIM:(kv + 1) * HEAD_DIM] = _head_norm(proj_ref[:, lo:lo + HEAD_DIM], kg_ref[...])
    v_ref[...] = proj_ref[:, ATT_WIDTH + KV_WIDTH:ATT_WIDTH + 2 * KV_WIDTH]


def _sample_prep(proj, q_gain, k_gain):
    n = proj.shape[0]
    qkv_w = ATT_WIDTH + 2 * KV_WIDTH
    return pl.pallas_call(
        _sample_prep_kernel,
        grid=(1,),
        in_specs=[pl.BlockSpec((n, qkv_w), lambda i: (0, 0)),
                  pl.BlockSpec((1, HEAD_DIM), lambda i: (0, 0)),
                  pl.BlockSpec((1, HEAD_DIM), lambda i: (0, 0))],
        out_specs=[pl.BlockSpec((n, ATT_WIDTH), lambda i: (0, 0)),
                   pl.BlockSpec((n, KV_WIDTH), lambda i: (0, 0)),
                   pl.BlockSpec((n, KV_WIDTH), lambda i: (0, 0))],
        out_shape=[jax.ShapeDtypeStruct((n, ATT_WIDTH), F32),
                   jax.ShapeDtypeStruct((n, KV_WIDTH), F32),
                   jax.ShapeDtypeStruct((n, KV_WIDTH), F32)],
        compiler_params=_params("arbitrary"),
        name="sample_qk_prep",
    )(proj, q_gain.reshape(1, HEAD_DIM), k_gain.reshape(1, HEAD_DIM))


def _sample_gate_kernel(pps, pt_ref, q_ref, *rest):
    page_refs = rest[:pps]
    idx_ref = rest[pps]
    ksum_ref = rest[pps + 1]
    j = pl.program_id(1)
    page_rows = page_refs[0].shape[0]
    ppb = MOBA_BLOCK * N_KV_HEADS // page_rows
    bps = pps // ppb
    fold = 8 // N_KV_HEADS
    for b in range(bps):
        s8 = jnp.sum(page_refs[b * ppb][...].reshape(page_rows // 8, 8, HEAD_DIM), axis=0)
        for t in range(1, ppb):
            s8 = s8 + jnp.sum(page_refs[b * ppb + t][...].reshape(page_rows // 8, 8, HEAD_DIM), axis=0)
        s = s8[0:N_KV_HEADS]
        for t in range(1, fold):
            s = s + s8[t * N_KV_HEADS:(t + 1) * N_KV_HEADS]
        s = s * (1.0 / MOBA_BLOCK)
        for kv in range(N_KV_HEADS):
            ksum_ref[kv, j, b:b + 1, :] = s[kv:kv + 1, :]

    @pl.when(j == pl.num_programs(1) - 1)
    def _():
        nblk = ksum_ref.shape[1] * bps
        q = q_ref[0]
        hrow = lax.broadcasted_iota(jnp.int32, (ATT_HEADS, nblk), 0)
        gate = jnp.zeros((ATT_HEADS, nblk), F32)
        for kv in range(N_KV_HEADS):
            gk = _dot3(q, ksum_ref[kv].reshape(nblk, HEAD_DIM), NT_DIMS)
            gate = jnp.where(hrow // KV_GROUP == kv, gk, gate)
        lane = lax.broadcasted_iota(jnp.int32, (ATT_HEADS, LANES), 1)
        if nblk < LANES:
            gate = jnp.concatenate([gate, jnp.full((ATT_HEADS, LANES - nblk), -jnp.inf, F32)], axis=1)
        _, firsts = _top3(gate, lane.astype(F32), 1)
        out = jnp.zeros((ATT_HEADS, LANES), jnp.int32)
        for t, first in enumerate(firsts):
            out = jnp.where(lane == t, first.astype(jnp.int32), out)
        idx_ref[0] = out


def _sample_gate(q3, cache_k, layer, page_table):
    n, n_pages = page_table.shape
    depth, pool, page = cache_k.shape[:3]
    ppb = MOBA_BLOCK // page
    nblk = n_pages // ppb
    assert n_pages % ppb == 0 and MOBA_TOPK <= nblk <= LANES and 8 % N_KV_HEADS == 0
    pps = min(16, n_pages)
    assert n_pages % pps == 0 and pps % ppb == 0
    steps = n_pages // pps
    cache_k = cache_k.reshape(depth, pool, page * N_KV_HEADS, HEAD_DIM)

    def page_spec(t):
        return pl.BlockSpec((None, None, page * N_KV_HEADS, HEAD_DIM),
                            lambda s, j, pt: (layer, pt[s, j * pps + t], 0, 0))

    grid_spec = pltpu.PrefetchScalarGridSpec(
        num_scalar_prefetch=1,
        grid=(n, steps),
        in_specs=[pl.BlockSpec((1, ATT_HEADS, HEAD_DIM), lambda s, j, pt: (s, 0, 0))]
        + [page_spec(t) for t in range(pps)],
        out_specs=pl.BlockSpec((1, ATT_HEADS, LANES), lambda s, j, pt: (s, 0, 0)),
        scratch_shapes=[pltpu.VMEM((N_KV_HEADS, steps, pps // ppb, HEAD_DIM), F32)],
    )
    return pl.pallas_call(
        functools.partial(_sample_gate_kernel, pps),
        grid_spec=grid_spec,
        out_shape=jax.ShapeDtypeStruct((n, ATT_HEADS, LANES), jnp.int32),
        compiler_params=_params("arbitrary", "arbitrary"),
        name="sample_gate_topk",
    )(page_table, q3, *([cache_k] * pps))


def _sample_attn_kernel(past, ppb, layer, pt_ref, idx_ref, q_ref, kn_ref, vn_ref, ck_ref, cv_ref, o_ref,
                        kbuf, vbuf, sems):
    s = pl.program_id(0)
    page = MOBA_BLOCK // ppb

    def copies(h, t, p):
        kv = h // KV_GROUP
        blk = idx_ref[s, h * MOBA_TOPK + t]
        phys = pt_ref[s, blk * ppb + p]
        dst = pl.ds((t * ppb + p) * page, page)
        return (pltpu.make_async_copy(ck_ref.at[layer, phys, :, kv, :], kbuf.at[h, dst, :], sems.at[0, h, t * ppb + p]),
                pltpu.make_async_copy(cv_ref.at[layer, phys, :, kv, :], vbuf.at[h, dst, :], sems.at[1, h, t * ppb + p]))

    triples = [(h, t, p) for h in range(ATT_HEADS) for t in range(MOBA_TOPK) for p in range(ppb)]
    for h, t, p in triples:
        ck, cv = copies(h, t, p)
        ck.start()
        cv.start()
    for h, t, p in triples:
        ck, cv = copies(h, t, p)
        ck.wait()
        cv.wait()

    nsel = MOBA_TOPK * MOBA_BLOCK
    rowi = lax.broadcasted_iota(jnp.int32, (nsel, 1), 0)
    off = jnp.bitwise_and(rowi, MOBA_BLOCK - 1)
    for h in range(ATT_HEADS):
        kv = h // KV_GROUP
        slope = _alibi_slope(h)
        q = q_ref[0, h:h + 1, :]
        sc = jnp.sum(kbuf[h] * q, axis=1, keepdims=True) * ATT_SCALE
        pos = jnp.zeros((nsel, 1), jnp.int32)
        for t in range(MOBA_TOPK):
            pos = jnp.where(rowi // MOBA_BLOCK == t, idx_ref[s, h * MOBA_TOPK + t] * MOBA_BLOCK, pos)
        dist = (past - (pos + off)).astype(F32)
        sc = sc - slope * dist
        s_own = jnp.sum(kn_ref[0, kv:kv + 1, :] * q, axis=1, keepdims=True) * ATT_SCALE
        mx = jnp.maximum(jnp.max(sc, axis=0, keepdims=True), s_own)
        p = jnp.exp(sc - mx)
        p_own = jnp.exp(s_own - mx)
        denom = jnp.sum(p, axis=0, keepdims=True) + p_own
        num = jnp.sum(p * vbuf[h], axis=0, keepdims=True) + p_own * vn_ref[0, kv:kv + 1, :]
        o_ref[0, h:h + 1, :] = num / denom


def _sample_attn(q3, k_new3, v_new3, cache_k, cache_v, layer, page_table, idx):
    n, n_pages = page_table.shape
    page = cache_k.shape[2]
    ppb = MOBA_BLOCK // page
    past = n_pages * page
    assert past % MOBA_BLOCK == 0
    nsel = MOBA_TOPK * MOBA_BLOCK
    grid_spec = pltpu.PrefetchScalarGridSpec(
        num_scalar_prefetch=2,
        grid=(n,),
        in_specs=[pl.BlockSpec((1, ATT_HEADS, HEAD_DIM), lambda s, pt, ix: (s, 0, 0)),
                  pl.BlockSpec((1, N_KV_HEADS, HEAD_DIM), lambda s, pt, ix: (s, 0, 0)),
                  pl.BlockSpec((1, N_KV_HEADS, HEAD_DIM), lambda s, pt, ix: (s, 0, 0)),
                  pl.BlockSpec(memory_space=pl.ANY),
                  pl.BlockSpec(memory_space=pl.ANY)],
        out_specs=pl.BlockSpec((1, ATT_HEADS, HEAD_DIM), lambda s, pt, ix: (s, 0, 0)),
        scratch_shapes=[pltpu.VMEM((ATT_HEADS, nsel, HEAD_DIM), F32),
                        pltpu.VMEM((ATT_HEADS, nsel, HEAD_DIM), F32),
                        pltpu.SemaphoreType.DMA((2, ATT_HEADS, MOBA_TOPK * ppb))],
    )
    return pl.pallas_call(
        functools.partial(_sample_attn_kernel, past, ppb, layer),
        grid_spec=grid_spec,
        out_shape=jax.ShapeDtypeStruct((n, ATT_HEADS, HEAD_DIM), F32),
        compiler_params=_params("arbitrary"),
        name="sample_moba_attn",
    )(page_table, idx, q3, k_new3, v_new3, cache_k, cache_v)


def _sample_ssd_kernel(z_ref, xbc_ref, dt_ref, buf_ref, h0_ref, cw_ref, cb_ref, dtb_ref, alog_ref, dsk_ref, g_ref,
                       y_ref, buf_out_ref, h_out_ref):
    x = xbc_ref[0]
    buf = buf_ref[0]
    cw = cw_ref[...]
    conv = cb_ref[...] + cw[CONV_WIDTH - 1:CONV_WIDTH] * x
    for t in range(CONV_WIDTH - 1):
        conv = conv + cw[t:t + 1] * buf[t:t + 1]
    buf_out_ref[0, 0:CONV_WIDTH - 2, :] = buf[1:CONV_WIDTH - 1]
    buf_out_ref[0, CONV_WIDTH - 2:CONV_WIDTH - 1, :] = x
    xc = _silu(conv)
    xs = xc[:, :D_INNER]
    bm = xc[:, D_INNER:D_INNER + SSM_GROUPS * D_STATE]
    cm = xc[:, D_INNER + SSM_GROUPS * D_STATE:]

    lane1 = lax.broadcasted_iota(jnp.int32, (1, LANES), 1)
    rowi = lax.broadcasted_iota(jnp.int32, (LANES, LANES), 0)
    dt = _softplus(dt_ref[0] + dtb_ref[...])
    a = jnp.where(lane1 < SSM_HEADS, -jnp.exp(alog_ref[...]), 0.0)
    dec = jnp.exp(dt * a)
    dt_e = _expand_heads(dt, lane1 < SSM_HEAD_DIM)
    xdt = xs * dt_e
    xdt_rows = jnp.broadcast_to(xdt, (LANES, D_INNER))

    hpg = SSM_HEADS // SSM_GROUPS
    y_parts = []
    for pair in range(SSM_HEADS // 2):
        grp = (2 * pair) // hpg
        xcol = xdt_rows[:, pair * LANES:(pair + 1) * LANES].T
        dcol = jnp.where(rowi < SSM_HEAD_DIM,
                         jnp.broadcast_to(dec[:, 2 * pair:2 * pair + 1], (LANES, LANES)),
                         jnp.broadcast_to(dec[:, 2 * pair + 1:2 * pair + 2], (LANES, LANES)))
        h0 = h0_ref[0, 2 * pair:2 * pair + 2].reshape(LANES, D_STATE)
        hn = dcol * h0 + xcol * bm[:, grp * D_STATE:(grp + 1) * D_STATE]
        h_out_ref[0, 2 * pair:2 * pair + 2] = hn.reshape(2, SSM_HEAD_DIM, D_STATE)
        cgrow = jnp.broadcast_to(cm[:, grp * D_STATE:(grp + 1) * D_STATE], (8, D_STATE))
        ypair = _dot3(cgrow, hn, NT_DIMS)
        y_parts.append(ypair[0:1])
    y = jnp.concatenate(y_parts, axis=1) + dsk_ref[...] * xs
    y_ref[0] = _gated_group_norm(y, z_ref[0], g_ref[...]).astype(y_ref.dtype)


def _ssd_sample(proj, dtp, state_conv, state_ssm, conv_w, conv_b, dt_bias, a_log, d_skip, g_ssm):
    n = proj.shape[0]
    z0 = ATT_WIDTH + 2 * KV_WIDTH
    z3 = proj[:, z0:z0 + D_INNER].reshape(n, 1, D_INNER)
    x3 = proj[:, z0 + D_INNER:z0 + D_INNER + CONV_DIM].reshape(n, 1, CONV_DIM)
    dt3 = dtp.reshape(n, 1, LANES)
    const = lambda shape: pl.BlockSpec(shape, lambda s: tuple(0 for _ in shape))
    per_seq = lambda shape: pl.BlockSpec((1,) + shape, lambda s: (s,) + tuple(0 for _ in shape))
    y, buf, h = pl.pallas_call(
        _sample_ssd_kernel,
        grid=(n,),
        in_specs=[per_seq((1, D_INNER)), per_seq((1, CONV_DIM)), per_seq((1, LANES)),
                  per_seq((CONV_WIDTH - 1, CONV_DIM)), per_seq((SSM_HEADS, SSM_HEAD_DIM, D_STATE)),
                  const((CONV_WIDTH, CONV_DIM)), const((1, CONV_DIM)), const((1, LANES)), const((1, LANES)),
                  const((1, D_INNER)), const((1, D_INNER))],
        out_specs=[per_seq((1, D_INNER)), per_seq((CONV_WIDTH - 1, CONV_DIM)),
                   per_seq((SSM_HEADS, SSM_HEAD_DIM, D_STATE))],
        out_shape=[jax.ShapeDtypeStruct((n, 1, D_INNER), BF16),
                   jax.ShapeDtypeStruct((n, CONV_WIDTH - 1, CONV_DIM), F32),
                   jax.ShapeDtypeStruct((n, SSM_HEADS, SSM_HEAD_DIM, D_STATE), F32)],
        compiler_params=_params("arbitrary"),
        name="sample_ssd",
    )(z3, x3, dt3, state_conv, state_ssm, conv_w, conv_b.reshape(1, CONV_DIM), _pad_lanes(dt_bias),
      _pad_lanes(a_log), jnp.repeat(d_skip, SSM_HEAD_DIM).reshape(1, D_INNER), g_ssm.reshape(1, D_INNER))
    return y.reshape(n, D_INNER), buf, h


def _layer_tail(x, mod, o_att, y_ssm, g_att_out, w_out, g_ffn, w_gate, w_up, w_down):
    x1 = _outproj(o_att, g_att_out, y_ssm, w_out, x, mod)
    hid = _gateup(x1, g_ffn, mod, w_gate, w_up)
    return _down(hid, w_down, x1, mod)


def kernel(x_prompt, x_sample, cache_k, cache_v, state_conv, state_ssm, page_table, c_prompt, c_sample, w_ada, b_ada, g_mix_norm, w_in, q_gain, k_gain, g_att_out, conv_w, conv_b, dt_bias, a_log, d_skip, g_ssm_out, w_out, g_ffn_norm, w_gate, w_up, w_down):
    n_p, seq, d = x_prompt.shape
    n_s, dec_seq, _ = x_sample.shape
    assert n_p == 1 and dec_seq == 1
    depth = w_ada.shape[0]
    main_w = ATT_WIDTH + 2 * KV_WIDTH + D_INNER + CONV_DIM

    yp = x_prompt.reshape(seq, d)
    ys = x_sample.reshape(n_s, d)
    c_rows = n_p + n_s
    c_pad = -(-c_rows // 8) * 8
    c_all = jnp.pad(jnp.concatenate([c_prompt, c_sample], axis=0), ((0, c_pad - c_rows), (0, 0)))
    outs = [[] for _ in range(8)]
    for l in range(depth):
        mod = _ada(c_all, w_ada[l], b_ada[l])
        mod_p, mod_s = mod[0:1], mod[1:1 + n_s]
        w_dt = jnp.pad(w_in[l][:, main_w:], ((0, 0), (0, LANES - SSM_HEADS)))

        proj = _inproj(yp, g_mix_norm[l], mod_p, w_in[l], main_w, tn=1536, tm=512, name="in_proj")
        dtp = _inproj(yp, g_mix_norm[l], mod_p, w_dt, LANES, tn=LANES, tm=512, name="in_proj_dt")
        k_out, v_out, kp, vp, qp = _prompt_prep(proj, q_gain[l], k_gain[l])
        o_att = _prompt_attn(qp, kp, vp)
        y_ssm, tail, st = _ssd_prompt(proj, dtp, conv_w[l], conv_b[l], dt_bias[l], a_log[l], d_skip[l], g_ssm_out[l])
        yp = _layer_tail(yp, mod_p, o_att, y_ssm, g_att_out[l], w_out[l], g_ffn_norm[l], w_gate[l], w_up[l], w_down[l])
        hpg = SSM_HEADS // SSM_GROUPS
        ssm_p = st.reshape(SSM_GROUPS, D_STATE, hpg, SSM_HEAD_DIM).transpose(0, 2, 3, 1).reshape(
            1, SSM_HEADS, SSM_HEAD_DIM, D_STATE)
        outs[0].append(k_out.reshape(1, seq, N_KV_HEADS, HEAD_DIM))
        outs[1].append(v_out.reshape(1, seq, N_KV_HEADS, HEAD_DIM))
        outs[2].append(tail[8 - (CONV_WIDTH - 1):].reshape(1, CONV_WIDTH - 1, CONV_DIM))
        outs[3].append(ssm_p)

        proj_s = _inproj(ys, g_mix_norm[l], mod_s, w_in[l], main_w, tn=1536, tm=512, name="in_proj")
        dts = _inproj(ys, g_mix_norm[l], mod_s, w_dt, LANES, tn=LANES, tm=512, name="in_proj_dt")
        q_s, k_s, v_s = _sample_prep(proj_s, q_gain[l], k_gain[l])
        q3 = q_s.reshape(n_s, ATT_HEADS, HEAD_DIM)
        idx = _sample_gate(q3, cache_k, l, page_table)
        idx_flat = idx[:, :, :MOBA_TOPK].reshape(n_s, ATT_HEADS * MOBA_TOPK)
        o_s = _sample_attn(q3, k_s.reshape(n_s, N_KV_HEADS, HEAD_DIM), v_s.reshape(n_s, N_KV_HEADS, HEAD_DIM),
                           cache_k, cache_v, l, page_table, idx_flat)
        y_s, buf_s, h_s = _ssd_sample(proj_s, dts, state_conv[l], state_ssm[l], conv_w[l], conv_b[l], dt_bias[l],
                                      a_log[l], d_skip[l], g_ssm_out[l])
        ys = _layer_tail(ys, mod_s, o_s.reshape(n_s, ATT_WIDTH), y_s, g_att_out[l], w_out[l], g_ffn_norm[l],
                         w_gate[l], w_up[l], w_down[l])
        outs[4].append(k_s.reshape(n_s, 1, N_KV_HEADS, HEAD_DIM))
        outs[5].append(v_s.reshape(n_s, 1, N_KV_HEADS, HEAD_DIM))
        outs[6].append(buf_s)
        outs[7].append(h_s)
    stacked = [jnp.stack(o) for o in outs]
    return (yp.reshape(1, seq, d), ys.reshape(n_s, 1, d), *stacked)
```

```python
import functools

import jax
import jax.numpy as jnp
from jax import lax
from jax.experimental import pallas as pl
from jax.experimental.pallas import tpu as pltpu

F32 = jnp.float32
BF16 = jnp.bfloat16

HEAD_DIM = 128
ATT_HEADS = 8
N_KV_HEADS = 4
KV_GROUP = ATT_HEADS // N_KV_HEADS
ATT_WIDTH = ATT_HEADS * HEAD_DIM
KV_WIDTH = N_KV_HEADS * HEAD_DIM
MOBA_BLOCK = 256
MOBA_TOPK = 3
D_INNER = 1024
SSM_HEAD_DIM = 64
SSM_HEADS = D_INNER // SSM_HEAD_DIM
SSM_GROUPS = 2
D_STATE = 128
CONV_WIDTH = 4
CONV_DIM = D_INNER + 2 * SSM_GROUPS * D_STATE
SSD_CHUNK = 128
EPS = 1e-6
ATT_SCALE = HEAD_DIM ** -0.5

LANES = 128
FEAT_KBLK = 96
FEAT_KOFF = 97
FEAT_ONE_A = 98
FEAT_ONE_B = 99
NEG_BIG = -1e30

NT_DIMS = (((1,), (1,)), ((), ()))
VMEM_LIMIT = 56 * 1024 * 1024


def _params(*sem):
    return pltpu.CompilerParams(dimension_semantics=sem, vmem_limit_bytes=VMEM_LIMIT)


def _silu(x):
    return x / (1.0 + jnp.exp(-x))


def _softplus(x):
    return jnp.maximum(x, 0.0) + jnp.log1p(jnp.exp(-jnp.abs(x)))


def _split_bf16(x):
    hi = x.astype(BF16)
    lo = (x - hi.astype(F32)).astype(BF16)
    return hi, lo


def _dot3(a, b, dims):
    ah, al = _split_bf16(a)
    bh, bl = _split_bf16(b)
    d = lambda x, y: lax.dot_general(x, y, dims, preferred_element_type=F32)
    return d(ah, bh) + d(al, bh) + d(ah, bl)


def _alibi_slope(h):
    return 2.0 ** (-8.0 * (h + 1) / ATT_HEADS)


NN_DIMS = (((1,), (0,)), ((), ()))


def _ada_kernel(c_ref, w_ref, b_ref, o_ref):
    o_ref[...] = _dot3(_silu(c_ref[...]), w_ref[...], NN_DIMS) + b_ref[...]


def _ada(c_all, w, b):
    rows, d = c_all.shape
    n = w.shape[1]
    tn = 512
    return pl.pallas_call(
        _ada_kernel,
        grid=(n // tn,),
        in_specs=[pl.BlockSpec((rows, d), lambda j: (0, 0)),
                  pl.BlockSpec((d, tn), lambda j: (0, j)),
                  pl.BlockSpec((1, tn), lambda j: (0, j))],
        out_specs=pl.BlockSpec((rows, tn), lambda j: (0, j)),
        out_shape=jax.ShapeDtypeStruct((rows, n), F32),
        compiler_params=_params("arbitrary"),
        name="ada_mod",
    )(c_all, w, b.reshape(1, n))


def _mod_spec(mod_rows, tm, width, col_of):
    if mod_rows == 1:
        return pl.BlockSpec((1, width), lambda n, i: (0, col_of(n)))
    return pl.BlockSpec((tm, width), lambda n, i: (i, col_of(n)))


def _rms(x):
    return x * lax.rsqrt(jnp.mean(x * x, axis=-1, keepdims=True) + EPS)


def _cast_weight_once(w_ref, wbf_ref):
    @pl.when(pl.program_id(1) == 0)
    def _():
        wbf_ref[...] = w_ref[...].astype(BF16)


def _inproj_kernel(x_ref, g_ref, sc_ref, sh_ref, w_ref, o_ref, wbf_ref):
    _cast_weight_once(w_ref, wbf_ref)
    hn = ((_rms(x_ref[...]) * g_ref[...]) * (1.0 + sc_ref[...]) + sh_ref[...]).astype(BF16)
    o_ref[...] = jnp.dot(hn, wbf_ref[...], preferred_element_type=F32)


def _inproj_split_kernel(x_ref, g_ref, sc_ref, sh_ref, w_ref, o_ref, whi_ref, wlo_ref):
    @pl.when(pl.program_id(1) == 0)
    def _():
        whi, wlo = _split_bf16(w_ref[...])
        whi_ref[...] = whi
        wlo_ref[...] = wlo

    hn = (_rms(x_ref[...]) * g_ref[...]) * (1.0 + sc_ref[...]) + sh_ref[...]
    hhi, hlo = _split_bf16(hn)
    d = lambda a, b: jnp.dot(a, b, preferred_element_type=F32)
    o_ref[...] = d(hhi, whi_ref[...]) + d(hlo, whi_ref[...]) + d(hhi, wlo_ref[...])


def _inproj(x, g, mod, w, n_cols, *, tn, tm, name, split=False):
    m, d = x.shape
    tm = min(tm, m)
    mr = mod.shape[0]
    return pl.pallas_call(
        _inproj_split_kernel if split else _inproj_kernel,
        grid=(n_cols // tn, m // tm),
        in_specs=[pl.BlockSpec((tm, d), lambda n, i: (i, 0)),
                  pl.BlockSpec((1, d), lambda n, i: (0, 0)),
                  _mod_spec(mr, tm, d, lambda n: 1),
                  _mod_spec(mr, tm, d, lambda n: 0),
                  pl.BlockSpec((d, tn), lambda n, i: (0, n))],
        out_specs=pl.BlockSpec((tm, tn), lambda n, i: (i, n)),
        out_shape=jax.ShapeDtypeStruct((m, n_cols), F32),
        scratch_shapes=[pltpu.VMEM((d, tn), BF16)] * (2 if split else 1),
        compiler_params=_params("arbitrary", "arbitrary"),
        name=name,
    )(x, g.reshape(1, d), mod, mod, w)


def _outproj_kernel(a_ref, ga_ref, b_ref, w_ref, x_ref, gt_ref, gf_ref, sc_ref, sh_ref, x1_ref, h2_ref, wbf_ref):
    _cast_weight_once(w_ref, wbf_ref)
    ka = a_ref.shape[1]
    a = (_rms(a_ref[...]) * ga_ref[...]).astype(BF16)
    acc = jnp.dot(a, wbf_ref[:ka, :], preferred_element_type=F32)
    acc = acc + jnp.dot(b_ref[...], wbf_ref[ka:, :], preferred_element_type=F32)
    x1 = x_ref[...] + gt_ref[...] * acc
    x1_ref[...] = x1
    h2_ref[...] = ((_rms(x1) * gf_ref[...]) * (1.0 + sc_ref[...]) + sh_ref[...]).astype(BF16)


def _outproj(a, g_a, b, w, x, mod, g_ffn):
    m, ka = a.shape
    kb = b.shape[1]
    d = w.shape[1]
    tm = min(256, m)
    mr = mod.shape[0]
    return pl.pallas_call(
        _outproj_kernel,
        grid=(1, m // tm),
        in_specs=[pl.BlockSpec((tm, ka), lambda n, i: (i, 0)),
                  pl.BlockSpec((1, ka), lambda n, i: (0, 0)),
                  pl.BlockSpec((tm, kb), lambda n, i: (i, 0)),
                  pl.BlockSpec((ka + kb, d), lambda n, i: (0, 0), pipeline_mode=pl.Buffered(1)),
                  pl.BlockSpec((tm, d), lambda n, i: (i, 0)),
                  _mod_spec(mr, tm, d, lambda n: 2),
                  pl.BlockSpec((1, d), lambda n, i: (0, 0)),
                  _mod_spec(mr, tm, d, lambda n: 4),
                  _mod_spec(mr, tm, d, lambda n: 3)],
        out_specs=[pl.BlockSpec((tm, d), lambda n, i: (i, 0)),
                   pl.BlockSpec((tm, d), lambda n, i: (i, 0))],
        out_shape=[jax.ShapeDtypeStruct((m, d), F32), jax.ShapeDtypeStruct((m, d), BF16)],
        scratch_shapes=[pltpu.VMEM((ka + kb, d), BF16)],
        compiler_params=_params("arbitrary", "arbitrary"),
        name="out_proj",
    )(a, g_a.reshape(1, ka), b, w, x, mod, g_ffn.reshape(1, d), mod, mod)


def _gateup_kernel(tiles_p, xp_ref, xs_ref, wg_ref, wu_ref, op_ref, os_ref, wgb_ref, wub_ref):
    _cast_weight_once(wg_ref, wgb_ref)
    _cast_weight_once(wu_ref, wub_ref)
    i = pl.program_id(1)

    def swiglu(h2):
        g = jnp.dot(h2, wgb_ref[...], preferred_element_type=F32)
        u = jnp.dot(h2, wub_ref[...], preferred_element_type=F32)
        return (_silu(g) * u).astype(BF16)

    @pl.when(i < tiles_p)
    def _():
        op_ref[...] = swiglu(xp_ref[...])

    @pl.when(i == tiles_p)
    def _():
        os_ref[...] = swiglu(xs_ref[...])


def _gateup(h2_p, h2_s, wg, wu):
    m, d = h2_p.shape
    ms = h2_s.shape[0]
    f = wg.shape[1]
    tn, tm = 512, 512
    tiles_p = m // tm
    prow = lambda i: jnp.minimum(i, tiles_p - 1)
    return pl.pallas_call(
        functools.partial(_gateup_kernel, tiles_p),
        grid=(f // tn, tiles_p + 1),
        in_specs=[pl.BlockSpec((tm, d), lambda n, i: (prow(i), 0)),
                  pl.BlockSpec((ms, d), lambda n, i: (0, 0)),
                  pl.BlockSpec((d, tn), lambda n, i: (0, n)),
                  pl.BlockSpec((d, tn), lambda n, i: (0, n))],
        out_specs=[pl.BlockSpec((tm, tn), lambda n, i: (prow(i), n)),
                   pl.BlockSpec((ms, tn), lambda n, i: (0, n))],
        out_shape=[jax.ShapeDtypeStruct((m, f), BF16), jax.ShapeDtypeStruct((ms, f), BF16)],
        scratch_shapes=[pltpu.VMEM((d, tn), BF16), pltpu.VMEM((d, tn), BF16)],
        compiler_params=_params("arbitrary", "arbitrary"),
        name="ffn_gate_up",
    )(h2_p, h2_s, wg, wu)


def _down_kernel(tiles_p, hp_ref, hs_ref, w_ref, xp_ref, xs_ref, gtp_ref, gts_ref, op_ref, os_ref, wbf_ref):
    _cast_weight_once(w_ref, wbf_ref)
    i = pl.program_id(1)

    @pl.when(i < tiles_p)
    def _():
        acc = jnp.dot(hp_ref[...], wbf_ref[...], preferred_element_type=F32)
        op_ref[...] = xp_ref[...] + gtp_ref[...] * acc

    @pl.when(i == tiles_p)
    def _():
        acc = jnp.dot(hs_ref[...], wbf_ref[...], preferred_element_type=F32)
        os_ref[...] = xs_ref[...] + gts_ref[...] * acc


def _down(h_p, h_s, w, x_p, x_s, mod_p, mod_s):
    m, f = h_p.shape
    ms = h_s.shape[0]
    d = w.shape[1]
    tn, tm = 512, 512
    tiles_p = m // tm
    nb = d // tn
    assert mod_p.shape[0] == 1 and mod_s.shape[0] == ms
    prow = lambda i: jnp.minimum(i, tiles_p - 1)
    return pl.pallas_call(
        functools.partial(_down_kernel, tiles_p),
        grid=(nb, tiles_p + 1),
        in_specs=[pl.BlockSpec((tm, f), lambda n, i: (prow(i), 0)),
                  pl.BlockSpec((ms, f), lambda n, i: (0, 0)),
                  pl.BlockSpec((f, tn), lambda n, i: (0, n)),
                  pl.BlockSpec((tm, tn), lambda n, i: (prow(i), n)),
                  pl.BlockSpec((ms, tn), lambda n, i: (0, n)),
                  pl.BlockSpec((1, tn), lambda n, i: (0, 5 * nb + n)),
                  pl.BlockSpec((ms, tn), lambda n, i: (0, 5 * nb + n))],
        out_specs=[pl.BlockSpec((tm, tn), lambda n, i: (prow(i), n)),
                   pl.BlockSpec((ms, tn), lambda n, i: (0, n))],
        out_shape=[jax.ShapeDtypeStruct((m, d), F32), jax.ShapeDtypeStruct((ms, d), F32)],
        scratch_shapes=[pltpu.VMEM((f, tn), BF16)],
        compiler_params=_params("arbitrary", "arbitrary"),
        name="ffn_down",
    )(h_p, h_s, w, x_p, x_s, mod_p, mod_s)


def _head_norm(x, gain):
    return x * lax.rsqrt(jnp.mean(x * x, axis=-1, keepdims=True) + EPS) * gain


def _top3(g, idx_f, axis):
    sel = jnp.zeros(g.shape, F32)
    firsts = []
    for _ in range(MOBA_TOPK):
        mx = jnp.max(g, axis=axis, keepdims=True)
        ismax = jnp.logical_and(g == mx, mx > -jnp.inf)
        first = jnp.min(jnp.where(ismax, idx_f, float(LANES)), axis=axis, keepdims=True)
        pick = idx_f == first
        sel = jnp.where(pick, 1.0, sel)
        g = jnp.where(pick, -jnp.inf, g)
        firsts.append(first)
    return sel, firsts


def _prompt_prep_kernel(proj_ref, qg_ref, kg_ref, kout_ref, vout_ref, kp_ref, vp_ref, qp_ref, km_ref):
    i = pl.program_id(0)
    blk = MOBA_BLOCK

    @pl.when(i == 0)
    def _():
        km_ref[...] = jnp.zeros_like(km_ref)

    lane = lax.broadcasted_iota(jnp.int32, (blk, LANES), 1)
    row = lax.broadcasted_iota(jnp.int32, (blk, LANES), 0)
    sq_row = lax.broadcasted_iota(jnp.int32, (LANES, LANES), 0)
    i_f = i.astype(F32)
    row_f = row.astype(F32)

    kfeat = jnp.where(lane == i, 1.0, 0.0)
    kfeat = jnp.where(lane == FEAT_KBLK, i_f, kfeat)
    kfeat = jnp.where(lane == FEAT_KOFF, row_f, kfeat)
    kfeat = jnp.where(jnp.logical_or(lane == FEAT_ONE_A, lane == FEAT_ONE_B), 1.0, kfeat).astype(BF16)
    vfeat = jnp.where(lane == 0, 1.0, 0.0).astype(BF16)

    kg = kg_ref[...]
    for kv in range(N_KV_HEADS):
        k = proj_ref[:, ATT_WIDTH + kv * HEAD_DIM:ATT_WIDTH + (kv + 1) * HEAD_DIM]
        kn = _head_norm(k, kg)
        kout_ref[:, kv * HEAD_DIM:(kv + 1) * HEAD_DIM] = kn
        kp_ref[kv, :, :HEAD_DIM] = kn.astype(BF16)
        kp_ref[kv, :, HEAD_DIM:] = kfeat
        ksum = jnp.sum(kn, axis=0, keepdims=True) * (1.0 / blk)
        km_ref[kv] = jnp.where(sq_row == i, jnp.broadcast_to(ksum, (LANES, LANES)), km_ref[kv])
        v = proj_ref[:, ATT_WIDTH + KV_WIDTH + kv * HEAD_DIM:ATT_WIDTH + KV_WIDTH + (kv + 1) * HEAD_DIM]
        vout_ref[:, kv * HEAD_DIM:(kv + 1) * HEAD_DIM] = v
        vp_ref[kv, :, :HEAD_DIM] = v.astype(BF16)
        vp_ref[kv, :, HEAD_DIM:] = vfeat

    qg = qg_ref[...]
    blk_id = lax.broadcasted_iota(jnp.int32, (LANES, blk), 0)
    blk_id_f = blk_id.astype(F32)
    valid = blk_id < i
    for h in range(ATT_HEADS):
        q = proj_ref[:, h * HEAD_DIM:(h + 1) * HEAD_DIM]
        qn = _head_norm(q, qg)
        gate = _dot3(km_ref[h // KV_GROUP], qn, NT_DIMS)
        sel_t, _ = _top3(jnp.where(valid, gate, -jnp.inf), blk_id_f, 0)
        sel = sel_t.T
        slope = _alibi_slope(h)
        qfeat = jnp.where(jnp.logical_and(lane < FEAT_KBLK, sel == 0.0), NEG_BIG, 0.0)
        qfeat = jnp.where(lane == FEAT_KBLK, slope * blk, qfeat)
        qfeat = jnp.where(lane == FEAT_KOFF, slope, qfeat)
        qfeat = jnp.where(lane == FEAT_ONE_A, -(slope * blk) * i_f, qfeat)
        qfeat = jnp.where(lane == FEAT_ONE_B, -slope * row_f, qfeat)
        qp_ref[h, :, :HEAD_DIM] = (qn * ATT_SCALE).astype(BF16)
        qp_ref[h, :, HEAD_DIM:] = qfeat.astype(BF16)


def _prompt_prep(proj, q_gain, k_gain):
    m = proj.shape[0]
    nb = m // MOBA_BLOCK
    assert m % MOBA_BLOCK == 0 and nb <= FEAT_KBLK
    blk = MOBA_BLOCK
    qkv_w = ATT_WIDTH + 2 * KV_WIDTH
    return pl.pallas_call(
        _prompt_prep_kernel,
        grid=(nb,),
        in_specs=[pl.BlockSpec((blk, qkv_w), lambda i: (i, 0)),
                  pl.BlockSpec((1, HEAD_DIM), lambda i: (0, 0)),
                  pl.BlockSpec((1, HEAD_DIM), lambda i: (0, 0))],
        out_specs=[pl.BlockSpec((blk, KV_WIDTH), lambda i: (i, 0)),
                   pl.BlockSpec((blk, KV_WIDTH), lambda i: (i, 0)),
                   pl.BlockSpec((N_KV_HEADS, blk, 2 * HEAD_DIM), lambda i: (0, i, 0)),
                   pl.BlockSpec((N_KV_HEADS, blk, 2 * HEAD_DIM), lambda i: (0, i, 0)),
                   pl.BlockSpec((ATT_HEADS, blk, 2 * HEAD_DIM), lambda i: (0, i, 0))],
        out_shape=[jax.ShapeDtypeStruct((m, KV_WIDTH), F32),
                   jax.ShapeDtypeStruct((m, KV_WIDTH), F32),
                   jax.ShapeDtypeStruct((N_KV_HEADS, m, 2 * HEAD_DIM), BF16),
                   jax.ShapeDtypeStruct((N_KV_HEADS, m, 2 * HEAD_DIM), BF16),
                   jax.ShapeDtypeStruct((ATT_HEADS, m, 2 * HEAD_DIM), BF16)],
        scratch_shapes=[pltpu.VMEM((N_KV_HEADS, LANES, LANES), F32)],
        compiler_params=_params("arbitrary"),
        name="prompt_qk_prep",
    )(proj, q_gain.reshape(1, HEAD_DIM), k_gain.reshape(1, HEAD_DIM))


ATTN_UNROLL = 4
LOG2E = 1.4426950408889634


def _prompt_attn_kernel(q_ref, k_ref, v_ref, o_ref, s_ref, acc_ref, m_ref):
    i = pl.program_id(1)
    blk = MOBA_BLOCK
    rows = KV_GROUP * blk
    span = ATTN_UNROLL * blk
    qs = q_ref[...].reshape(rows, 2 * HEAD_DIM)

    def lane_fold(s):
        out = s[:, :LANES]
        for t in range(1, s.shape[1] // LANES):
            out = jnp.maximum(out, s[:, t * LANES:(t + 1) * LANES])
        return out

    def probs(s):
        mb = m_ref[...]
        return jnp.concatenate([jnp.exp2(s[:, t * LANES:(t + 1) * LANES] - mb) for t in range(s.shape[1] // LANES)],
                               axis=1).astype(BF16)

    own = pl.ds(pl.multiple_of(i * blk, blk), blk)
    r = lax.broadcasted_iota(jnp.int32, (rows, blk), 0)
    c = lax.broadcasted_iota(jnp.int32, (rows, blk), 1)
    dist = jnp.bitwise_and(r, blk - 1) - c
    slope = qs[:, HEAD_DIM + FEAT_KOFF:HEAD_DIM + FEAT_KOFF + 1].astype(F32)
    s_own = lax.dot_general(qs[:, :HEAD_DIM], k_ref[own, :HEAD_DIM], NT_DIMS, preferred_element_type=F32)
    s_own = jnp.where(dist >= 0, (s_own - slope * dist.astype(F32)) * LOG2E, -jnp.inf)
    m_ref[...] = lane_fold(s_own)

    trips = (i + ATTN_UNROLL - 1) // ATTN_UNROLL

    def two_trips_per_iteration(trip):
        def pair(u, carry):
            trip(2 * u)
            trip(2 * u + 1)
            return carry

        lax.fori_loop(0, trips // 2, pair, 0)

        @pl.when(trips % 2 == 1)
        def _():
            trip(trips - 1)

    def pass1(t):
        ks = k_ref[pl.ds(pl.multiple_of(t * span, span), span), :]
        s = lax.dot_general(qs, ks, NT_DIMS, preferred_element_type=F32) * LOG2E
        s_ref[t] = s
        m_ref[...] = jnp.maximum(m_ref[...], lane_fold(s))

    two_trips_per_iteration(pass1)
    m_ref[...] = jnp.broadcast_to(jnp.max(m_ref[...], axis=1, keepdims=True), (rows, LANES))

    acc_ref[...] = jnp.dot(probs(s_own), v_ref[own, :], preferred_element_type=F32)

    def pass2(t):
        vs = v_ref[pl.ds(pl.multiple_of(t * span, span), span), :]
        acc_ref[...] += jnp.dot(probs(s_ref[t]), vs, preferred_element_type=F32)

    two_trips_per_iteration(pass2)
    acc = acc_ref[...]
    o = acc[:, :HEAD_DIM] / acc[:, HEAD_DIM:HEAD_DIM + 1]
    for g in range(KV_GROUP):
        o_ref[:, g * HEAD_DIM:(g + 1) * HEAD_DIM] = o[g * blk:(g + 1) * blk]


def _prompt_attn(qp, kp, vp):
    m = kp.shape[1]
    nb = m // MOBA_BLOCK
    assert nb % ATTN_UNROLL == 0
    blk = MOBA_BLOCK
    rows = KV_GROUP * blk
    return pl.pallas_call(
        _prompt_attn_kernel,
        grid=(N_KV_HEADS, nb),
        in_specs=[pl.BlockSpec((KV_GROUP, blk, 2 * HEAD_DIM), lambda kv, i: (kv, i, 0)),
                  pl.BlockSpec((None, m, 2 * HEAD_DIM), lambda kv, i: (kv, 0, 0)),
                  pl.BlockSpec((None, m, 2 * HEAD_DIM), lambda kv, i: (kv, 0, 0))],
        out_specs=pl.BlockSpec((blk, KV_GROUP * HEAD_DIM), lambda kv, i: (i, kv)),
        out_shape=jax.ShapeDtypeStruct((m, ATT_WIDTH), F32),
        scratch_shapes=[pltpu.VMEM((nb // ATTN_UNROLL, rows, ATTN_UNROLL * blk), F32),
                        pltpu.VMEM((rows, 2 * HEAD_DIM), F32),
                        pltpu.VMEM((rows, LANES), F32)],
        compiler_params=_params("arbitrary", "arbitrary"),
        name="prompt_moba_attn",
    )(qp, kp, vp)


def _expand_heads(v, lane_lo):
    r = v.shape[0]
    parts = []
    for k in range(SSM_HEADS // 2):
        a0 = jnp.broadcast_to(v[:, 2 * k:2 * k + 1], (r, LANES))
        a1 = jnp.broadcast_to(v[:, 2 * k + 1:2 * k + 2], (r, LANES))
        parts.append(jnp.where(lane_lo, a0, a1))
    return jnp.concatenate(parts, axis=1)


def _gated_group_norm(y, z, g):
    yz = y * _silu(z)
    gw = D_INNER // SSM_GROUPS
    outs = []
    for grp in range(SSM_GROUPS):
        t = yz[:, grp * gw:(grp + 1) * gw]
        t = t * lax.rsqrt(jnp.mean(t * t, axis=-1, keepdims=True) + EPS)
        outs.append(t * g[:, grp * gw:(grp + 1) * gw])
    return jnp.concatenate(outs, axis=1)


def _ssd_kernel(z_ref, xbc_ref, dt_ref, cw_ref, cb_ref, dtb_ref, alog_ref, dsk_ref, g_ref,
                y_ref, tail_out_ref, st_out_ref, tail_ref, st_ref):
    c = pl.program_id(0)
    cs = SSD_CHUNK
    gw = D_INNER // SSM_GROUPS

    @pl.when(c == 0)
    def _():
        tail_ref[...] = jnp.zeros_like(tail_ref)
        st_ref[...] = jnp.zeros_like(st_ref)

    xr = xbc_ref[...]
    xp = jnp.concatenate([tail_ref[...], xr], axis=0)
    cw = cw_ref[...]
    conv = cb_ref[...] + cw[3:4] * xr
    for t in range(CONV_WIDTH - 1):
        conv = conv + cw[t:t + 1] * xp[8 - (CONV_WIDTH - 1) + t:8 - (CONV_WIDTH - 1) + t + cs]
    tail_ref[...] = xr[cs - 8:]
    tail_out_ref[...] = xr[cs - 8:]
    xc = _silu(conv)
    xs = xc[:, :D_INNER]
    bm = xc[:, D_INNER:D_INNER + SSM_GROUPS * D_STATE]
    cm = xc[:, D_INNER + SSM_GROUPS * D_STATE:]

    lane = lax.broadcasted_iota(jnp.int32, (cs, LANES), 1)
    rowi = lax.broadcasted_iota(jnp.int32, (cs, LANES), 0)
    lane_lo = lane < SSM_HEAD_DIM
    tri = rowi >= lane

    dt = _softplus(dt_ref[...] + dtb_ref[...])
    a = jnp.where(lane[:1] < SSM_HEADS, -jnp.exp(alog_ref[...]), 0.0)
    da = dt * a
    tril = jnp.where(tri, 1.0, 0.0).astype(BF16)
    p1 = da.astype(BF16)
    r1 = da - p1.astype(F32)
    p2 = r1.astype(BF16)
    p3 = (r1 - p2.astype(F32)).astype(BF16)
    acum = (jnp.dot(tril, p1, preferred_element_type=F32) + jnp.dot(tril, p2, preferred_element_type=F32)
            + jnp.dot(tril, p3, preferred_element_type=F32))
    acum_t = acum.T

    dt_e = _expand_heads(dt, lane_lo)
    ac_e = _expand_heads(acum, lane_lo)
    xdt = xs * dt_e
    ea_e = jnp.exp(ac_e)
    dend_e = jnp.exp(ac_e[cs - 1:cs, :] - ac_e)
    cdec = ea_e[cs - 1:cs, :]
    xdt_bf = xdt.astype(BF16)
    xdec_bf = (xdt * dend_e).astype(BF16)

    y_parts = []
    for grp in range(SSM_GROUPS):
        bg = bm[:, grp * D_STATE:(grp + 1) * D_STATE]
        cg = cm[:, grp * D_STATE:(grp + 1) * D_STATE].astype(BF16)
        cb = lax.dot_general(cg, bg.astype(BF16), NT_DIMS, preferred_element_type=F32)
        hpg = SSM_HEADS // SSM_GROUPS
        intra = []
        for k in range(hpg // 2):
            pair = grp * (hpg // 2) + k
            xpair = xdt_bf[:, pair * LANES:(pair + 1) * LANES]
            acc = None
            for hh in range(2):
                h = 2 * pair + hh
                seg = jnp.broadcast_to(acum[:, h:h + 1], (cs, cs)) - acum_t[h:h + 1, :]
                lmat = jnp.exp(jnp.where(tri, seg, -jnp.inf))
                mh = (cb * lmat).astype(BF16)
                xh = jnp.where(lane_lo if hh == 0 else jnp.logical_not(lane_lo), xpair, jnp.zeros_like(xpair))
                part = jnp.dot(mh, xh, preferred_element_type=F32)
                acc = part if acc is None else acc + part
            intra.append(acc)
        y_intra = jnp.concatenate(intra, axis=1)
        st = st_ref[grp]
        y_inter = jnp.dot(cg, st.astype(BF16), preferred_element_type=F32) * ea_e[:, grp * gw:(grp + 1) * gw]
        new_st = cdec[:, grp * gw:(grp + 1) * gw] * st + jnp.dot(
            bg.T.astype(BF16), xdec_bf[:, grp * gw:(grp + 1) * gw], preferred_element_type=F32)
        st_ref[grp] = new_st
        st_out_ref[grp] = new_st
        y_parts.append(y_intra + y_inter)
    y = jnp.concatenate(y_parts, axis=1) + dsk_ref[...] * xs
    y_ref[...] = _gated_group_norm(y, z_ref[...], g_ref[...]).astype(y_ref.dtype)


def _pad_lanes(v):
    return jnp.pad(v.reshape(1, -1), ((0, 0), (0, LANES - v.size)))


def _ssd_prompt(proj, dtp, conv_w, conv_b, dt_bias, a_log, d_skip, g_ssm):
    m = proj.shape[0]
    cs = SSD_CHUNK
    assert m % cs == 0
    gw = D_INNER // SSM_GROUPS
    z_blk = (ATT_WIDTH + 2 * KV_WIDTH) // D_INNER
    x_blk = (ATT_WIDTH + 2 * KV_WIDTH + D_INNER) // CONV_DIM
    assert z_blk * D_INNER == ATT_WIDTH + 2 * KV_WIDTH and x_blk * CONV_DIM == ATT_WIDTH + 2 * KV_WIDTH + D_INNER
    const = lambda shape: pl.BlockSpec(shape, lambda c: tuple(0 for _ in shape))
    return pl.pallas_call(
        _ssd_kernel,
        grid=(m // cs,),
        in_specs=[pl.BlockSpec((cs, D_INNER), lambda c: (c, z_blk)),
                  pl.BlockSpec((cs, CONV_DIM), lambda c: (c, x_blk)),
                  pl.BlockSpec((cs, LANES), lambda c: (c, 0)),
                  const((CONV_WIDTH, CONV_DIM)), const((1, CONV_DIM)), const((1, LANES)), const((1, LANES)),
                  const((1, D_INNER)), const((1, D_INNER))],
        out_specs=[pl.BlockSpec((cs, D_INNER), lambda c: (c, 0)),
                   const((8, CONV_DIM)), const((SSM_GROUPS, D_STATE, gw))],
        out_shape=[jax.ShapeDtypeStruct((m, D_INNER), BF16),
                   jax.ShapeDtypeStruct((8, CONV_DIM), F32),
                   jax.ShapeDtypeStruct((SSM_GROUPS, D_STATE, gw), F32)],
        scratch_shapes=[pltpu.VMEM((8, CONV_DIM), F32), pltpu.VMEM((SSM_GROUPS, D_STATE, gw), F32)],
        compiler_params=_params("arbitrary"),
        name="prompt_ssd",
    )(proj, proj, dtp, conv_w, conv_b.reshape(1, CONV_DIM), _pad_lanes(dt_bias), _pad_lanes(a_log),
      jnp.repeat(d_skip, SSM_HEAD_DIM).reshape(1, D_INNER), g_ssm.reshape(1, D_INNER))


def _sample_prep_kernel(proj_ref, qg_ref, kg_ref, q_ref, k_ref, v_ref):
    for h in range(ATT_HEADS):
        q_ref[:, h * HEAD_DIM:(h + 1) * HEAD_DIM] = _head_norm(proj_ref[:, h * HEAD_DIM:(h + 1) * HEAD_DIM], qg_ref[...])
    for kv in range(N_KV_HEADS):
        lo = ATT_WIDTH + kv * HEAD_DIM
        k_ref[:, kv * HEAD_DIM:(kv + 1) * HEAD_DIM] = _head_norm(proj_ref[:, lo:lo + HEAD_DIM], kg_ref[...])
    v_ref[...] = proj_ref[:, ATT_WIDTH + KV_WIDTH:ATT_WIDTH + 2 * KV_WIDTH]


def _sample_prep(proj, q_gain, k_gain):
    n = proj.shape[0]
    qkv_w = ATT_WIDTH + 2 * KV_WIDTH
    return pl.pallas_call(
        _sample_prep_kernel,
        grid=(1,),
        in_specs=[pl.BlockSpec((n, qkv_w), lambda i: (0, 0)),
                  pl.BlockSpec((1, HEAD_DIM), lambda i: (0, 0)),
                  pl.BlockSpec((1, HEAD_DIM), lambda i: (0, 0))],
        out_specs=[pl.BlockSpec((n, ATT_WIDTH), lambda i: (0, 0)),
                   pl.BlockSpec((n, KV_WIDTH), lambda i: (0, 0)),
                   pl.BlockSpec((n, KV_WIDTH), lambda i: (0, 0))],
        out_shape=[jax.ShapeDtypeStruct((n, ATT_WIDTH), F32),
                   jax.ShapeDtypeStruct((n, KV_WIDTH), F32),
                   jax.ShapeDtypeStruct((n, KV_WIDTH), F32)],
        compiler_params=_params("arbitrary"),
        name="sample_qk_prep",
    )(proj, q_gain.reshape(1, HEAD_DIM), k_gain.reshape(1, HEAD_DIM))


def _sample_gate_kernel(pps, pt_ref, q_ref, *rest):
    page_refs = rest[:pps]
    idx_ref = rest[pps]
    ksum_ref = rest[pps + 1]
    j = pl.program_id(1)
    page_rows = page_refs[0].shape[0]
    ppb = MOBA_BLOCK * N_KV_HEADS // page_rows
    bps = pps // ppb
    fold = 8 // N_KV_HEADS
    for b in range(bps):
        s8 = jnp.sum(page_refs[b * ppb][...].reshape(page_rows // 8, 8, HEAD_DIM), axis=0)
        for t in range(1, ppb):
            s8 = s8 + jnp.sum(page_refs[b * ppb + t][...].reshape(page_rows // 8, 8, HEAD_DIM), axis=0)
        s = s8[0:N_KV_HEADS]
        for t in range(1, fold):
            s = s + s8[t * N_KV_HEADS:(t + 1) * N_KV_HEADS]
        s = s * (1.0 / MOBA_BLOCK)
        for kv in range(N_KV_HEADS):
            ksum_ref[kv, j, b:b + 1, :] = s[kv:kv + 1, :]

    @pl.when(j == pl.num_programs(1) - 1)
    def _():
        nblk = ksum_ref.shape[1] * bps
        q = q_ref[0]
        hrow = lax.broadcasted_iota(jnp.int32, (ATT_HEADS, nblk), 0)
        gate = jnp.zeros((ATT_HEADS, nblk), F32)
        for kv in range(N_KV_HEADS):
            gk = _dot3(q, ksum_ref[kv].reshape(nblk, HEAD_DIM), NT_DIMS)
            gate = jnp.where(hrow // KV_GROUP == kv, gk, gate)
        lane = lax.broadcasted_iota(jnp.int32, (ATT_HEADS, LANES), 1)
        if nblk < LANES:
            gate = jnp.concatenate([gate, jnp.full((ATT_HEADS, LANES - nblk), -jnp.inf, F32)], axis=1)
        _, firsts = _top3(gate, lane.astype(F32), 1)
        out = jnp.zeros((ATT_HEADS, LANES), jnp.int32)
        for t, first in enumerate(firsts):
            out = jnp.where(lane == t, first.astype(jnp.int32), out)
        idx_ref[0] = out


def _sample_gate(q3, cache_k, layer, page_table):
    n, n_pages = page_table.shape
    depth, pool, page = cache_k.shape[:3]
    ppb = MOBA_BLOCK // page
    nblk = n_pages // ppb
    assert n_pages % ppb == 0 and MOBA_TOPK <= nblk <= LANES and 8 % N_KV_HEADS == 0
    pps = min(16, n_pages)
    assert n_pages % pps == 0 and pps % ppb == 0
    steps = n_pages // pps
    cache_k = cache_k.reshape(depth, pool, page * N_KV_HEADS, HEAD_DIM)

    def page_spec(t):
        return pl.BlockSpec((None, None, page * N_KV_HEADS, HEAD_DIM),
                            lambda s, j, pt: (layer, pt[s, j * pps + t], 0, 0))

    grid_spec = pltpu.PrefetchScalarGridSpec(
        num_scalar_prefetch=1,
        grid=(n, steps),
        in_specs=[pl.BlockSpec((1, ATT_HEADS, HEAD_DIM), lambda s, j, pt: (s, 0, 0))]
        + [page_spec(t) for t in range(pps)],
        out_specs=pl.BlockSpec((1, ATT_HEADS, LANES), lambda s, j, pt: (s, 0, 0)),
        scratch_shapes=[pltpu.VMEM((N_KV_HEADS, steps, pps // ppb, HEAD_DIM), F32)],
    )
    return pl.pallas_call(
        functools.partial(_sample_gate_kernel, pps),
        grid_spec=grid_spec,
        out_shape=jax.ShapeDtypeStruct((n, ATT_HEADS, LANES), jnp.int32),
        compiler_params=_params("arbitrary", "arbitrary"),
        name="sample_gate_topk",
    )(page_table, q3, *([cache_k] * pps))


def _sample_attn_kernel(past, ppb, layer, pt_ref, idx_ref, q_ref, kn_ref, vn_ref, ck_ref, cv_ref, o_ref,
                        kbuf, vbuf, sems):
    s = pl.program_id(0)
    page = MOBA_BLOCK // ppb
    slot = s % 2

    def copies(seq, buf, h, t, p):
        kv = h // KV_GROUP
        blk = idx_ref[seq, h * MOBA_TOPK + t]
        phys = pt_ref[seq, blk * ppb + p]
        dst = pl.ds((t * ppb + p) * page, page)
        return (pltpu.make_async_copy(ck_ref.at[layer, phys, :, kv, :], kbuf.at[buf, h, dst, :],
                                      sems.at[buf, 0, h, t * ppb + p]),
                pltpu.make_async_copy(cv_ref.at[layer, phys, :, kv, :], vbuf.at[buf, h, dst, :],
                                      sems.at[buf, 1, h, t * ppb + p]))

    triples = [(h, t, p) for h in range(ATT_HEADS) for t in range(MOBA_TOPK) for p in range(ppb)]

    def start_all(seq, buf):
        for h, t, p in triples:
            ck, cv = copies(seq, buf, h, t, p)
            ck.start()
            cv.start()

    @pl.when(s == 0)
    def _():
        start_all(0, 0)

    @pl.when(s + 1 < pl.num_programs(0))
    def _():
        start_all(s + 1, 1 - slot)

    for h, t, p in triples:
        ck, cv = copies(s, slot, h, t, p)
        ck.wait()
        cv.wait()

    nsel = MOBA_TOPK * MOBA_BLOCK
    rowi = lax.broadcasted_iota(jnp.int32, (nsel, 1), 0)
    off = jnp.bitwise_and(rowi, MOBA_BLOCK - 1)
    for h in range(ATT_HEADS):
        kv = h // KV_GROUP
        slope = _alibi_slope(h)
        q = q_ref[0, h:h + 1, :]
        sc = jnp.sum(kbuf[slot, h] * q, axis=1, keepdims=True) * ATT_SCALE
        pos = jnp.zeros((nsel, 1), jnp.int32)
        for t in range(MOBA_TOPK):
            pos = jnp.where(rowi // MOBA_BLOCK == t, idx_ref[s, h * MOBA_TOPK + t] * MOBA_BLOCK, pos)
        dist = (past - (pos + off)).astype(F32)
        sc = sc - slope * dist
        s_own = jnp.sum(kn_ref[0, kv:kv + 1, :] * q, axis=1, keepdims=True) * ATT_SCALE
        mx = jnp.maximum(jnp.max(sc, axis=0, keepdims=True), s_own)
        p = jnp.exp(sc - mx)
        p_own = jnp.exp(s_own - mx)
        denom = jnp.sum(p, axis=0, keepdims=True) + p_own
        num = jnp.sum(p * vbuf[slot, h], axis=0, keepdims=True) + p_own * vn_ref[0, kv:kv + 1, :]
        o_ref[0, h:h + 1, :] = num / denom


def _sample_attn(q3, k_new3, v_new3, cache_k, cache_v, layer, page_table, idx):
    n, n_pages = page_table.shape
    page = cache_k.shape[2]
    ppb = MOBA_BLOCK // page
    past = n_pages * page
    assert past % MOBA_BLOCK == 0
    nsel = MOBA_TOPK * MOBA_BLOCK
    grid_spec = pltpu.PrefetchScalarGridSpec(
        num_scalar_prefetch=2,
        grid=(n,),
        in_specs=[pl.BlockSpec((1, ATT_HEADS, HEAD_DIM), lambda s, pt, ix: (s, 0, 0)),
                  pl.BlockSpec((1, N_KV_HEADS, HEAD_DIM), lambda s, pt, ix: (s, 0, 0)),
                  pl.BlockSpec((1, N_KV_HEADS, HEAD_DIM), lambda s, pt, ix: (s, 0, 0)),
                  pl.BlockSpec(memory_space=pl.ANY),
                  pl.BlockSpec(memory_space=pl.ANY)],
        out_specs=pl.BlockSpec((1, ATT_HEADS, HEAD_DIM), lambda s, pt, ix: (s, 0, 0)),
        scratch_shapes=[pltpu.VMEM((2, ATT_HEADS, nsel, HEAD_DIM), F32),
                        pltpu.VMEM((2, ATT_HEADS, nsel, HEAD_DIM), F32),
                        pltpu.SemaphoreType.DMA((2, 2, ATT_HEADS, MOBA_TOPK * ppb))],
    )
    return pl.pallas_call(
        functools.partial(_sample_attn_kernel, past, ppb, layer),
        grid_spec=grid_spec,
        out_shape=jax.ShapeDtypeStruct((n, ATT_HEADS, HEAD_DIM), F32),
        compiler_params=_params("arbitrary"),
        name="sample_moba_attn",
    )(page_table, idx, q3, k_new3, v_new3, cache_k, cache_v)


def _sample_ssd_kernel(z_ref, xbc_ref, dt_ref, buf_ref, h0_ref, cw_ref, cb_ref, dtb_ref, alog_ref, dsk_ref, g_ref,
                       y_ref, buf_out_ref, h_out_ref):
    x = xbc_ref[0]
    buf = buf_ref[0]
    cw = cw_ref[...]
    conv = cb_ref[...] + cw[CONV_WIDTH - 1:CONV_WIDTH] * x
    for t in range(CONV_WIDTH - 1):
        conv = conv + cw[t:t + 1] * buf[t:t + 1]
    buf_out_ref[0, 0:CONV_WIDTH - 2, :] = buf[1:CONV_WIDTH - 1]
    buf_out_ref[0, CONV_WIDTH - 2:CONV_WIDTH - 1, :] = x
    xc = _silu(conv)
    xs = xc[:, :D_INNER]
    bm = xc[:, D_INNER:D_INNER + SSM_GROUPS * D_STATE]
    cm = xc[:, D_INNER + SSM_GROUPS * D_STATE:]

    lane1 = lax.broadcasted_iota(jnp.int32, (1, LANES), 1)
    rowi = lax.broadcasted_iota(jnp.int32, (LANES, LANES), 0)
    dt = _softplus(dt_ref[0] + dtb_ref[...])
    a = jnp.where(lane1 < SSM_HEADS, -jnp.exp(alog_ref[...]), 0.0)
    dec = jnp.exp(dt * a)
    dt_e = _expand_heads(dt, lane1 < SSM_HEAD_DIM)
    xdt = xs * dt_e
    xdt_rows = jnp.broadcast_to(xdt, (LANES, D_INNER))

    hpg = SSM_HEADS // SSM_GROUPS
    y_parts = []
    for pair in range(SSM_HEADS // 2):
        grp = (2 * pair) // hpg
        xcol = xdt_rows[:, pair * LANES:(pair + 1) * LANES].T
        dcol = jnp.where(rowi < SSM_HEAD_DIM,
                         jnp.broadcast_to(dec[:, 2 * pair:2 * pair + 1], (LANES, LANES)),
                         jnp.broadcast_to(dec[:, 2 * pair + 1:2 * pair + 2], (LANES, LANES)))
        h0 = h0_ref[0, 2 * pair:2 * pair + 2].reshape(LANES, D_STATE)
        hn = dcol * h0 + xcol * bm[:, grp * D_STATE:(grp + 1) * D_STATE]
        h_out_ref[0, 2 * pair:2 * pair + 2] = hn.reshape(2, SSM_HEAD_DIM, D_STATE)
        cgrow = jnp.broadcast_to(cm[:, grp * D_STATE:(grp + 1) * D_STATE], (8, D_STATE))
        ypair = _dot3(cgrow, hn, NT_DIMS)
        y_parts.append(ypair[0:1])
    y = jnp.concatenate(y_parts, axis=1) + dsk_ref[...] * xs
    y_ref[0] = _gated_group_norm(y, z_ref[0], g_ref[...]).astype(y_ref.dtype)


def _ssd_sample(proj, dtp, state_conv, state_ssm, conv_w, conv_b, dt_bias, a_log, d_skip, g_ssm):
    n = proj.shape[0]
    z0 = ATT_WIDTH + 2 * KV_WIDTH
    z3 = proj[:, z0:z0 + D_INNER].reshape(n, 1, D_INNER)
    x3 = proj[:, z0 + D_INNER:z0 + D_INNER + CONV_DIM].reshape(n, 1, CONV_DIM)
    dt3 = dtp.reshape(n, 1, LANES)
    const = lambda shape: pl.BlockSpec(shape, lambda s: tuple(0 for _ in shape))
    per_seq = lambda shape: pl.BlockSpec((1,) + shape, lambda s: (s,) + tuple(0 for _ in shape))
    y, buf, h = pl.pallas_call(
        _sample_ssd_kernel,
        grid=(n,),
        in_specs=[per_seq((1, D_INNER)), per_seq((1, CONV_DIM)), per_seq((1, LANES)),
                  per_seq((CONV_WIDTH - 1, CONV_DIM)), per_seq((SSM_HEADS, SSM_HEAD_DIM, D_STATE)),
                  const((CONV_WIDTH, CONV_DIM)), const((1, CONV_DIM)), const((1, LANES)), const((1, LANES)),
                  const((1, D_INNER)), const((1, D_INNER))],
        out_specs=[per_seq((1, D_INNER)), per_seq((CONV_WIDTH - 1, CONV_DIM)),
                   per_seq((SSM_HEADS, SSM_HEAD_DIM, D_STATE))],
        out_shape=[jax.ShapeDtypeStruct((n, 1, D_INNER), BF16),
                   jax.ShapeDtypeStruct((n, CONV_WIDTH - 1, CONV_DIM), F32),
                   jax.ShapeDtypeStruct((n, SSM_HEADS, SSM_HEAD_DIM, D_STATE), F32)],
        compiler_params=_params("arbitrary"),
        name="sample_ssd",
    )(z3, x3, dt3, state_conv, state_ssm, conv_w, conv_b.reshape(1, CONV_DIM), _pad_lanes(dt_bias),
      _pad_lanes(a_log), jnp.repeat(d_skip, SSM_HEAD_DIM).reshape(1, D_INNER), g_ssm.reshape(1, D_INNER))
    return y.reshape(n, D_INNER), buf, h


def kernel(x_prompt, x_sample, cache_k, cache_v, state_conv, state_ssm, page_table, c_prompt, c_sample, w_ada, b_ada, g_mix_norm, w_in, q_gain, k_gain, g_att_out, conv_w, conv_b, dt_bias, a_log, d_skip, g_ssm_out, w_out, g_ffn_norm, w_gate, w_up, w_down):
    n_p, seq, d = x_prompt.shape
    n_s, dec_seq, _ = x_sample.shape
    assert n_p == 1 and dec_seq == 1
    depth = w_ada.shape[0]
    main_w = ATT_WIDTH + 2 * KV_WIDTH + D_INNER + CONV_DIM

    yp = x_prompt.reshape(seq, d)
    ys = x_sample.reshape(n_s, d)
    c_rows = n_p + n_s
    c_pad = -(-c_rows // 8) * 8
    c_all = jnp.pad(jnp.concatenate([c_prompt, c_sample], axis=0), ((0, c_pad - c_rows), (0, 0)))
    outs = [[] for _ in range(8)]
    for l in range(depth):
        mod = _ada(c_all, w_ada[l], b_ada[l])
        mod_p, mod_s = mod[0:1], mod[1:1 + n_s]
        w_dt = jnp.pad(w_in[l][:, main_w:], ((0, 0), (0, LANES - SSM_HEADS)))

        proj = _inproj(yp, g_mix_norm[l], mod_p, w_in[l], main_w, tn=1536, tm=512, name="in_proj")
        dtp = _inproj(yp, g_mix_norm[l], mod_p, w_dt, LANES, tn=LANES, tm=512, name="in_proj_dt")
        k_out, v_out, kp, vp, qp = _prompt_prep(proj, q_gain[l], k_gain[l])
        o_att = _prompt_attn(qp, kp, vp)
        y_ssm, tail, st = _ssd_prompt(proj, dtp, conv_w[l], conv_b[l], dt_bias[l], a_log[l], d_skip[l], g_ssm_out[l])
        x1_p, h2_p = _outproj(o_att, g_att_out[l], y_ssm, w_out[l], yp, mod_p, g_ffn_norm[l])
        hpg = SSM_HEADS // SSM_GROUPS
        ssm_p = st.reshape(SSM_GROUPS, D_STATE, hpg, SSM_HEAD_DIM).transpose(0, 2, 3, 1).reshape(
            1, SSM_HEADS, SSM_HEAD_DIM, D_STATE)
        outs[0].append(k_out.reshape(1, seq, N_KV_HEADS, HEAD_DIM))
        outs[1].append(v_out.reshape(1, seq, N_KV_HEADS, HEAD_DIM))
        outs[2].append(tail[8 - (CONV_WIDTH - 1):].reshape(1, CONV_WIDTH - 1, CONV_DIM))
        outs[3].append(ssm_p)

        proj_s = _inproj(ys, g_mix_norm[l], mod_s, w_in[l], main_w, tn=768, tm=512, name="in_proj_sample", split=True)
        dts = _inproj(ys, g_mix_norm[l], mod_s, w_dt, LANES, tn=LANES, tm=512, name="in_proj_dt_sample", split=True)
        q_s, k_s, v_s = _sample_prep(proj_s, q_gain[l], k_gain[l])
        q3 = q_s.reshape(n_s, ATT_HEADS, HEAD_DIM)
        idx = _sample_gate(q3, cache_k, l, page_table)
        idx_flat = idx[:, :, :MOBA_TOPK].reshape(n_s, ATT_HEADS * MOBA_TOPK)
        o_s = _sample_attn(q3, k_s.reshape(n_s, N_KV_HEADS, HEAD_DIM), v_s.reshape(n_s, N_KV_HEADS, HEAD_DIM),
                           cache_k, cache_v, l, page_table, idx_flat)
        y_s, buf_s, h_s = _ssd_sample(proj_s, dts, state_conv[l], state_ssm[l], conv_w[l], conv_b[l], dt_bias[l],
                                      a_log[l], d_skip[l], g_ssm_out[l])
        x1_s, h2_s = _outproj(o_s.reshape(n_s, ATT_WIDTH), g_att_out[l], y_s, w_out[l], ys, mod_s, g_ffn_norm[l])

        hid_p, hid_s = _gateup(h2_p, h2_s, w_gate[l], w_up[l])
        yp, ys = _down(hid_p, hid_s, w_down[l], x1_p, x1_s, mod_p, mod_s)
        outs[4].append(k_s.reshape(n_s, 1, N_KV_HEADS, HEAD_DIM))
        outs[5].append(v_s.reshape(n_s, 1, N_KV_HEADS, HEAD_DIM))
        outs[6].append(buf_s)
        outs[7].append(h_s)
    stacked = [jnp.stack(o) for o in outs]
    return (yp.reshape(1, seq, d), ys.reshape(n_s, 1, d), *stacked)
```

```python
import functools

import jax
import jax.numpy as jnp
from jax import lax
from jax.experimental import pallas as pl
from jax.experimental.pallas import tpu as pltpu

F32 = jnp.float32
BF16 = jnp.bfloat16

HEAD_DIM = 128
ATT_HEADS = 8
N_KV_HEADS = 4
KV_GROUP = ATT_HEADS // N_KV_HEADS
ATT_WIDTH = ATT_HEADS * HEAD_DIM
KV_WIDTH = N_KV_HEADS * HEAD_DIM
MOBA_BLOCK = 256
MOBA_TOPK = 3
D_INNER = 1024
SSM_HEAD_DIM = 64
SSM_HEADS = D_INNER // SSM_HEAD_DIM
SSM_GROUPS = 2
D_STATE = 128
CONV_WIDTH = 4
CONV_DIM = D_INNER + 2 * SSM_GROUPS * D_STATE
SSD_CHUNK = 128
EPS = 1e-6
ATT_SCALE = HEAD_DIM ** -0.5

LANES = 128
FEAT_KBLK = 96
FEAT_KOFF = 97
FEAT_ONE_A = 98
FEAT_ONE_B = 99
NEG_BIG = -1e30

NT_DIMS = (((1,), (1,)), ((), ()))
VMEM_LIMIT = 56 * 1024 * 1024


def _params(*sem):
    return pltpu.CompilerParams(dimension_semantics=sem, vmem_limit_bytes=VMEM_LIMIT)


def _silu(x):
    return x / (1.0 + jnp.exp(-x))


def _softplus(x):
    return jnp.maximum(x, 0.0) + jnp.log1p(jnp.exp(-jnp.abs(x)))


def _split_bf16(x):
    hi = x.astype(BF16)
    lo = (x - hi.astype(F32)).astype(BF16)
    return hi, lo


def _dot3(a, b, dims):
    ah, al = _split_bf16(a)
    bh, bl = _split_bf16(b)
    d = lambda x, y: lax.dot_general(x, y, dims, preferred_element_type=F32)
    return d(ah, bh) + d(al, bh) + d(ah, bl)


def _alibi_slope(h):
    return 2.0 ** (-8.0 * (h + 1) / ATT_HEADS)


NN_DIMS = (((1,), (0,)), ((), ()))


def _ada_kernel(c_ref, w_ref, b_ref, o_ref):
    o_ref[...] = _dot3(_silu(c_ref[...]), w_ref[...], NN_DIMS) + b_ref[...]


def _ada(c_all, w, b):
    rows, d = c_all.shape
    n = w.shape[1]
    tn = 512
    return pl.pallas_call(
        _ada_kernel,
        grid=(n // tn,),
        in_specs=[pl.BlockSpec((rows, d), lambda j: (0, 0)),
                  pl.BlockSpec((d, tn), lambda j: (0, j)),
                  pl.BlockSpec((1, tn), lambda j: (0, j))],
        out_specs=pl.BlockSpec((rows, tn), lambda j: (0, j)),
        out_shape=jax.ShapeDtypeStruct((rows, n), F32),
        compiler_params=_params("arbitrary"),
        name="ada_mod",
    )(c_all, w, b.reshape(1, n))


def _mod_spec(mod_rows, tm, width, col_of):
    if mod_rows == 1:
        return pl.BlockSpec((1, width), lambda n, i: (0, col_of(n)))
    return pl.BlockSpec((tm, width), lambda n, i: (i, col_of(n)))


def _rms(x):
    return x * lax.rsqrt(jnp.mean(x * x, axis=-1, keepdims=True) + EPS)


def _cast_weight_once(w_ref, wbf_ref):
    @pl.when(pl.program_id(1) == 0)
    def _():
        wbf_ref[...] = w_ref[...].astype(BF16)


def _inproj_kernel(n_main, split, x_ref, g_ref, sc_ref, sh_ref, w_ref, wdt_ref, o_ref, odt_ref, *w_scratch):
    n = pl.program_id(0)
    first_row_tile = pl.program_id(1) == 0
    main_refs, dt_refs = w_scratch[:len(w_scratch) // 2], w_scratch[len(w_scratch) // 2:]

    def stage(w, refs, pad_rows):
        for part, ref in zip(_split_bf16(w) if split else (w.astype(BF16),), refs):
            if pad_rows:
                part = jnp.concatenate([part, jnp.zeros((pad_rows, part.shape[1]), BF16)], axis=0)
            ref[...] = part

    def product(hn, refs):
        nt = lambda a, b: lax.dot_general(a, b, NT_DIMS, preferred_element_type=F32)
        if not split:
            return nt(hn.astype(BF16), refs[0][...])
        hhi, hlo = _split_bf16(hn)
        return nt(hhi, refs[0][...]) + nt(hlo, refs[0][...]) + nt(hhi, refs[1][...])

    @pl.when(jnp.logical_and(n < n_main, first_row_tile))
    def _():
        stage(w_ref[...], main_refs, 0)

    @pl.when(jnp.logical_and(n == n_main, first_row_tile))
    def _():
        stage(wdt_ref[...], dt_refs, LANES - wdt_ref.shape[0])

    hn = (_rms(x_ref[...]) * g_ref[...]) * (1.0 + sc_ref[...]) + sh_ref[...]

    @pl.when(n < n_main)
    def _():
        o_ref[...] = product(hn, main_refs)

    @pl.when(n == n_main)
    def _():
        odt_ref[...] = product(hn, dt_refs)


def _inproj(x, g, mod, w_t, main_w, *, tn, tm, name, split=False):
    m, d = x.shape
    tm = min(tm, m)
    mr = mod.shape[0]
    n_main = main_w // tn
    n_dt = w_t.shape[0] - main_w
    assert n_main * tn == main_w and main_w % n_dt == 0 and n_dt % 16 == 0
    last_i = m // tm - 1
    ncopies = 2 if split else 1
    main_col = lambda n: jnp.minimum(n, n_main - 1)
    proj, dtp = pl.pallas_call(
        functools.partial(_inproj_kernel, n_main, split),
        grid=(n_main + 1, m // tm),
        in_specs=[pl.BlockSpec((tm, d), lambda n, i: (i, 0)),
                  pl.BlockSpec((1, d), lambda n, i: (0, 0)),
                  _mod_spec(mr, tm, d, lambda n: 1),
                  _mod_spec(mr, tm, d, lambda n: 0),
                  pl.BlockSpec((tn, d), lambda n, i: (main_col(n), 0)),
                  pl.BlockSpec((n_dt, d), lambda n, i: (main_w // n_dt, 0))],
        out_specs=[pl.BlockSpec((tm, tn), lambda n, i: (jnp.where(n == n_main, last_i, i), main_col(n))),
                   pl.BlockSpec((tm, LANES), lambda n, i: (jnp.where(n == n_main, i, 0), 0))],
        out_shape=[jax.ShapeDtypeStruct((m, main_w), F32), jax.ShapeDtypeStruct((m, LANES), F32)],
        scratch_shapes=[pltpu.VMEM((tn, d), BF16)] * ncopies + [pltpu.VMEM((LANES, d), BF16)] * ncopies,
        compiler_params=_params("arbitrary", "arbitrary"),
        name=name,
    )(x, g.reshape(1, d), mod, mod, w_t, w_t)
    return proj, dtp


def _outproj_kernel(a_ref, ga_ref, b_ref, w_ref, x_ref, gt_ref, gf_ref, sc_ref, sh_ref, x1_ref, h2_ref, wbf_ref):
    _cast_weight_once(w_ref, wbf_ref)
    ka = a_ref.shape[1]
    a = (_rms(a_ref[...]) * ga_ref[...]).astype(BF16)
    acc = jnp.dot(a, wbf_ref[:ka, :], preferred_element_type=F32)
    acc = acc + jnp.dot(b_ref[...], wbf_ref[ka:, :], preferred_element_type=F32)
    x1 = x_ref[...] + gt_ref[...] * acc
    x1_ref[...] = x1
    h2_ref[...] = ((_rms(x1) * gf_ref[...]) * (1.0 + sc_ref[...]) + sh_ref[...]).astype(BF16)


def _outproj(a, g_a, b, w, x, mod, g_ffn):
    m, ka = a.shape
    kb = b.shape[1]
    d = w.shape[1]
    tm = min(256, m)
    mr = mod.shape[0]
    return pl.pallas_call(
        _outproj_kernel,
        grid=(1, m // tm),
        in_specs=[pl.BlockSpec((tm, ka), lambda n, i: (i, 0)),
                  pl.BlockSpec((1, ka), lambda n, i: (0, 0)),
                  pl.BlockSpec((tm, kb), lambda n, i: (i, 0)),
                  pl.BlockSpec((ka + kb, d), lambda n, i: (0, 0), pipeline_mode=pl.Buffered(1)),
                  pl.BlockSpec((tm, d), lambda n, i: (i, 0)),
                  _mod_spec(mr, tm, d, lambda n: 2),
                  pl.BlockSpec((1, d), lambda n, i: (0, 0)),
                  _mod_spec(mr, tm, d, lambda n: 4),
                  _mod_spec(mr, tm, d, lambda n: 3)],
        out_specs=[pl.BlockSpec((tm, d), lambda n, i: (i, 0)),
                   pl.BlockSpec((tm, d), lambda n, i: (i, 0))],
        out_shape=[jax.ShapeDtypeStruct((m, d), F32), jax.ShapeDtypeStruct((m, d), BF16)],
        scratch_shapes=[pltpu.VMEM((ka + kb, d), BF16)],
        compiler_params=_params("arbitrary", "arbitrary"),
        name="out_proj",
    )(a, g_a.reshape(1, ka), b, w, x, mod, g_ffn.reshape(1, d), mod, mod)


def _gateup_kernel(tiles_p, xp_ref, xs_ref, wg_ref, wu_ref, op_ref, os_ref, wgb_ref, wub_ref):
    _cast_weight_once(wg_ref, wgb_ref)
    _cast_weight_once(wu_ref, wub_ref)
    i = pl.program_id(1)

    def swiglu(h2):
        g = jnp.dot(h2, wgb_ref[...], preferred_element_type=F32)
        u = jnp.dot(h2, wub_ref[...], preferred_element_type=F32)
        return (_silu(g) * u).astype(BF16)

    @pl.when(i < tiles_p)
    def _():
        op_ref[...] = swiglu(xp_ref[...])

    @pl.when(i == tiles_p)
    def _():
        os_ref[...] = swiglu(xs_ref[...])


def _gateup(h2_p, h2_s, wg, wu):
    m, d = h2_p.shape
    ms = h2_s.shape[0]
    f = wg.shape[1]
    tn, tm = 512, 1024
    tiles_p = m // tm
    prow = lambda i: jnp.minimum(i, tiles_p - 1)
    return pl.pallas_call(
        functools.partial(_gateup_kernel, tiles_p),
        grid=(f // tn, tiles_p + 1),
        in_specs=[pl.BlockSpec((tm, d), lambda n, i: (prow(i), 0)),
                  pl.BlockSpec((ms, d), lambda n, i: (0, 0)),
                  pl.BlockSpec((d, tn), lambda n, i: (0, n)),
                  pl.BlockSpec((d, tn), lambda n, i: (0, n))],
        out_specs=[pl.BlockSpec((tm, tn), lambda n, i: (prow(i), n)),
                   pl.BlockSpec((ms, tn), lambda n, i: (0, n))],
        out_shape=[jax.ShapeDtypeStruct((m, f), BF16), jax.ShapeDtypeStruct((ms, f), BF16)],
        scratch_shapes=[pltpu.VMEM((d, tn), BF16), pltpu.VMEM((d, tn), BF16)],
        compiler_params=_params("arbitrary", "arbitrary"),
        name="ffn_gate_up",
    )(h2_p, h2_s, wg, wu)


def _down_kernel(tiles_p, hp_ref, hs_ref, w_ref, xp_ref, xs_ref, gtp_ref, gts_ref, op_ref, os_ref, wbf_ref):
    _cast_weight_once(w_ref, wbf_ref)
    i = pl.program_id(1)

    @pl.when(i < tiles_p)
    def _():
        acc = jnp.dot(hp_ref[...], wbf_ref[...], preferred_element_type=F32)
        op_ref[...] = xp_ref[...] + gtp_ref[...] * acc

    @pl.when(i == tiles_p)
    def _():
        acc = jnp.dot(hs_ref[...], wbf_ref[...], preferred_element_type=F32)
        os_ref[...] = xs_ref[...] + gts_ref[...] * acc


def _down(h_p, h_s, w, x_p, x_s, mod_p, mod_s):
    m, f = h_p.shape
    ms = h_s.shape[0]
    d = w.shape[1]
    tn, tm = 512, 512
    tiles_p = m // tm
    nb = d // tn
    assert mod_p.shape[0] == 1 and mod_s.shape[0] == ms
    prow = lambda i: jnp.minimum(i, tiles_p - 1)
    return pl.pallas_call(
        functools.partial(_down_kernel, tiles_p),
        grid=(nb, tiles_p + 1),
        in_specs=[pl.BlockSpec((tm, f), lambda n, i: (prow(i), 0)),
                  pl.BlockSpec((ms, f), lambda n, i: (0, 0)),
                  pl.BlockSpec((f, tn), lambda n, i: (0, n)),
                  pl.BlockSpec((tm, tn), lambda n, i: (prow(i), n)),
                  pl.BlockSpec((ms, tn), lambda n, i: (0, n)),
                  pl.BlockSpec((1, tn), lambda n, i: (0, 5 * nb + n)),
                  pl.BlockSpec((ms, tn), lambda n, i: (0, 5 * nb + n))],
        out_specs=[pl.BlockSpec((tm, tn), lambda n, i: (prow(i), n)),
                   pl.BlockSpec((ms, tn), lambda n, i: (0, n))],
        out_shape=[jax.ShapeDtypeStruct((m, d), F32), jax.ShapeDtypeStruct((ms, d), F32)],
        scratch_shapes=[pltpu.VMEM((f, tn), BF16)],
        compiler_params=_params("arbitrary", "arbitrary"),
        name="ffn_down",
    )(h_p, h_s, w, x_p, x_s, mod_p, mod_s)


def _head_norm(x, gain):
    return x * lax.rsqrt(jnp.mean(x * x, axis=-1, keepdims=True) + EPS) * gain


def _top3(g, idx_f, axis):
    sel = jnp.zeros(g.shape, F32)
    firsts = []
    for _ in range(MOBA_TOPK):
        mx = jnp.max(g, axis=axis, keepdims=True)
        ismax = jnp.logical_and(g == mx, mx > -jnp.inf)
        first = jnp.min(jnp.where(ismax, idx_f, float(LANES)), axis=axis, keepdims=True)
        pick = idx_f == first
        sel = jnp.where(pick, 1.0, sel)
        g = jnp.where(pick, -jnp.inf, g)
        firsts.append(first)
    return sel, firsts


def _prompt_prep_kernel(proj_ref, qg_ref, kg_ref, kout_ref, vout_ref, kp_ref, vp_ref, qp_ref, km_ref):
    i = pl.program_id(0)
    blk = MOBA_BLOCK

    @pl.when(i == 0)
    def _():
        km_ref[...] = jnp.zeros_like(km_ref)

    lane = lax.broadcasted_iota(jnp.int32, (blk, LANES), 1)
    row = lax.broadcasted_iota(jnp.int32, (blk, LANES), 0)
    sq_row = lax.broadcasted_iota(jnp.int32, (LANES, LANES), 0)
    i_f = i.astype(F32)
    row_f = row.astype(F32)

    kfeat = jnp.where(lane == i, 1.0, 0.0)
    kfeat = jnp.where(lane == FEAT_KBLK, i_f, kfeat)
    kfeat = jnp.where(lane == FEAT_KOFF, row_f, kfeat)
    kfeat = jnp.where(jnp.logical_or(lane == FEAT_ONE_A, lane == FEAT_ONE_B), 1.0, kfeat).astype(BF16)
    vfeat = jnp.where(lane == 0, 1.0, 0.0).astype(BF16)

    kg = kg_ref[...]
    for kv in range(N_KV_HEADS):
        k = proj_ref[:, ATT_WIDTH + kv * HEAD_DIM:ATT_WIDTH + (kv + 1) * HEAD_DIM]
        kn = _head_norm(k, kg)
        kout_ref[:, kv * HEAD_DIM:(kv + 1) * HEAD_DIM] = kn
        kp_ref[kv, :, :HEAD_DIM] = kn.astype(BF16)
        kp_ref[kv, :, HEAD_DIM:] = kfeat
        ksum = jnp.sum(kn, axis=0, keepdims=True) * (1.0 / blk)
        km_ref[kv] = jnp.where(sq_row == i, jnp.broadcast_to(ksum, (LANES, LANES)), km_ref[kv])
        v = proj_ref[:, ATT_WIDTH + KV_WIDTH + kv * HEAD_DIM:ATT_WIDTH + KV_WIDTH + (kv + 1) * HEAD_DIM]
        vout_ref[:, kv * HEAD_DIM:(kv + 1) * HEAD_DIM] = v
        vp_ref[kv, :, :HEAD_DIM] = v.astype(BF16)
        vp_ref[kv, :, HEAD_DIM:] = vfeat

    qg = qg_ref[...]
    blk_id = lax.broadcasted_iota(jnp.int32, (LANES, blk), 0)
    blk_id_f = blk_id.astype(F32)
    valid = blk_id < i
    for h in range(ATT_HEADS):
        q = proj_ref[:, h * HEAD_DIM:(h + 1) * HEAD_DIM]
        qn = _head_norm(q, qg)
        gate = _dot3(km_ref[h // KV_GROUP], qn, NT_DIMS)
        sel_t, _ = _top3(jnp.where(valid, gate, -jnp.inf), blk_id_f, 0)
        sel = sel_t.T
        slope = _alibi_slope(h)
        qfeat = jnp.where(jnp.logical_and(lane < FEAT_KBLK, sel == 0.0), NEG_BIG, 0.0)
        qfeat = jnp.where(lane == FEAT_KBLK, slope * blk, qfeat)
        qfeat = jnp.where(lane == FEAT_KOFF, slope, qfeat)
        qfeat = jnp.where(lane == FEAT_ONE_A, -(slope * blk) * i_f, qfeat)
        qfeat = jnp.where(lane == FEAT_ONE_B, -slope * row_f, qfeat)
        qp_ref[h, :, :HEAD_DIM] = (qn * ATT_SCALE).astype(BF16)
        qp_ref[h, :, HEAD_DIM:] = qfeat.astype(BF16)


def _prompt_prep(proj, q_gain, k_gain):
    m = proj.shape[0]
    nb = m // MOBA_BLOCK
    assert m % MOBA_BLOCK == 0 and nb <= FEAT_KBLK
    blk = MOBA_BLOCK
    qkv_w = ATT_WIDTH + 2 * KV_WIDTH
    return pl.pallas_call(
        _prompt_prep_kernel,
        grid=(nb,),
        in_specs=[pl.BlockSpec((blk, qkv_w), lambda i: (i, 0)),
                  pl.BlockSpec((1, HEAD_DIM), lambda i: (0, 0)),
                  pl.BlockSpec((1, HEAD_DIM), lambda i: (0, 0))],
        out_specs=[pl.BlockSpec((blk, KV_WIDTH), lambda i: (i, 0)),
                   pl.BlockSpec((blk, KV_WIDTH), lambda i: (i, 0)),
                   pl.BlockSpec((N_KV_HEADS, blk, 2 * HEAD_DIM), lambda i: (0, i, 0)),
                   pl.BlockSpec((N_KV_HEADS, blk, 2 * HEAD_DIM), lambda i: (0, i, 0)),
                   pl.BlockSpec((ATT_HEADS, blk, 2 * HEAD_DIM), lambda i: (0, i, 0))],
        out_shape=[jax.ShapeDtypeStruct((m, KV_WIDTH), F32),
                   jax.ShapeDtypeStruct((m, KV_WIDTH), F32),
                   jax.ShapeDtypeStruct((N_KV_HEADS, m, 2 * HEAD_DIM), BF16),
                   jax.ShapeDtypeStruct((N_KV_HEADS, m, 2 * HEAD_DIM), BF16),
                   jax.ShapeDtypeStruct((ATT_HEADS, m, 2 * HEAD_DIM), BF16)],
        scratch_shapes=[pltpu.VMEM((N_KV_HEADS, LANES, LANES), F32)],
        compiler_params=_params("arbitrary"),
        name="prompt_qk_prep",
    )(proj, q_gain.reshape(1, HEAD_DIM), k_gain.reshape(1, HEAD_DIM))


ATTN_UNROLL = 4
LOG2E = 1.4426950408889634


def _prompt_attn_kernel(q_ref, k_ref, v_ref, o_ref, s_ref, acc_ref, m_ref):
    i = pl.program_id(1)
    blk = MOBA_BLOCK
    rows = KV_GROUP * blk
    span = ATTN_UNROLL * blk
    qs = q_ref[...].reshape(rows, 2 * HEAD_DIM)

    def lane_fold(s):
        out = s[:, :LANES]
        for t in range(1, s.shape[1] // LANES):
            out = jnp.maximum(out, s[:, t * LANES:(t + 1) * LANES])
        return out

    def probs(s):
        mb = m_ref[...]
        return jnp.concatenate([jnp.exp2(s[:, t * LANES:(t + 1) * LANES] - mb) for t in range(s.shape[1] // LANES)],
                               axis=1).astype(BF16)

    own = pl.ds(pl.multiple_of(i * blk, blk), blk)
    r = lax.broadcasted_iota(jnp.int32, (rows, blk), 0)
    c = lax.broadcasted_iota(jnp.int32, (rows, blk), 1)
    dist = jnp.bitwise_and(r, blk - 1) - c
    slope = qs[:, HEAD_DIM + FEAT_KOFF:HEAD_DIM + FEAT_KOFF + 1].astype(F32)
    s_own = lax.dot_general(qs[:, :HEAD_DIM], k_ref[own, :HEAD_DIM], NT_DIMS, preferred_element_type=F32)
    s_own = jnp.where(dist >= 0, (s_own - slope * dist.astype(F32)) * LOG2E, -jnp.inf)
    m_ref[...] = lane_fold(s_own)

    trips = (i + ATTN_UNROLL - 1) // ATTN_UNROLL

    def two_trips_per_iteration(trip):
        def pair(u, carry):
            trip(2 * u)
            trip(2 * u + 1)
            return carry

        lax.fori_loop(0, trips // 2, pair, 0)

        @pl.when(trips % 2 == 1)
        def _():
            trip(trips - 1)

    def pass1(t):
        ks = k_ref[pl.ds(pl.multiple_of(t * span, span), span), :]
        s = lax.dot_general(qs, ks, NT_DIMS, preferred_element_type=F32) * LOG2E
        s_ref[t] = s
        m_ref[...] = jnp.maximum(m_ref[...], lane_fold(s))

    two_trips_per_iteration(pass1)
    m_ref[...] = jnp.broadcast_to(jnp.max(m_ref[...], axis=1, keepdims=True), (rows, LANES))

    acc_ref[...] = jnp.dot(probs(s_own), v_ref[own, :], preferred_element_type=F32)

    def pass2(t):
        vs = v_ref[pl.ds(pl.multiple_of(t * span, span), span), :]
        acc_ref[...] += jnp.dot(probs(s_ref[t]), vs, preferred_element_type=F32)

    two_trips_per_iteration(pass2)
    acc = acc_ref[...]
    o = acc[:, :HEAD_DIM] / acc[:, HEAD_DIM:HEAD_DIM + 1]
    for g in range(KV_GROUP):
        o_ref[:, g * HEAD_DIM:(g + 1) * HEAD_DIM] = o[g * blk:(g + 1) * blk]


def _prompt_attn(qp, kp, vp):
    m = kp.shape[1]
    nb = m // MOBA_BLOCK
    assert nb % ATTN_UNROLL == 0
    blk = MOBA_BLOCK
    rows = KV_GROUP * blk
    return pl.pallas_call(
        _prompt_attn_kernel,
        grid=(N_KV_HEADS, nb),
        in_specs=[pl.BlockSpec((KV_GROUP, blk, 2 * HEAD_DIM), lambda kv, i: (kv, i, 0)),
                  pl.BlockSpec((None, m, 2 * HEAD_DIM), lambda kv, i: (kv, 0, 0)),
                  pl.BlockSpec((None, m, 2 * HEAD_DIM), lambda kv, i: (kv, 0, 0))],
        out_specs=pl.BlockSpec((blk, KV_GROUP * HEAD_DIM), lambda kv, i: (i, kv)),
        out_shape=jax.ShapeDtypeStruct((m, ATT_WIDTH), F32),
        scratch_shapes=[pltpu.VMEM((nb // ATTN_UNROLL, rows, ATTN_UNROLL * blk), F32),
                        pltpu.VMEM((rows, 2 * HEAD_DIM), F32),
                        pltpu.VMEM((rows, LANES), F32)],
        compiler_params=_params("arbitrary", "arbitrary"),
        name="prompt_moba_attn",
    )(qp, kp, vp)


def _expand_heads(v, lane_lo):
    r = v.shape[0]
    parts = []
    for k in range(SSM_HEADS // 2):
        a0 = jnp.broadcast_to(v[:, 2 * k:2 * k + 1], (r, LANES))
        a1 = jnp.broadcast_to(v[:, 2 * k + 1:2 * k + 2], (r, LANES))
        parts.append(jnp.where(lane_lo, a0, a1))
    return jnp.concatenate(parts, axis=1)


def _gated_group_norm(y, z, g):
    yz = y * _silu(z)
    gw = D_INNER // SSM_GROUPS
    outs = []
    for grp in range(SSM_GROUPS):
        t = yz[:, grp * gw:(grp + 1) * gw]
        t = t * lax.rsqrt(jnp.mean(t * t, axis=-1, keepdims=True) + EPS)
        outs.append(t * g[:, grp * gw:(grp + 1) * gw])
    return jnp.concatenate(outs, axis=1)


def _ssd_kernel(z_ref, xbc_ref, dt_ref, cw_ref, cb_ref, dtb_ref, alog_ref, dsk_ref, g_ref,
                y_ref, tail_out_ref, st_out_ref, tail_ref, st_ref):
    c = pl.program_id(0)
    cs = SSD_CHUNK
    gw = D_INNER // SSM_GROUPS

    @pl.when(c == 0)
    def _():
        tail_ref[...] = jnp.zeros_like(tail_ref)
        st_ref[...] = jnp.zeros_like(st_ref)

    xr = xbc_ref[...]
    xp = jnp.concatenate([tail_ref[...], xr], axis=0)
    cw = cw_ref[...]
    conv = cb_ref[...] + cw[3:4] * xr
    for t in range(CONV_WIDTH - 1):
        conv = conv + cw[t:t + 1] * xp[8 - (CONV_WIDTH - 1) + t:8 - (CONV_WIDTH - 1) + t + cs]
    tail_ref[...] = xr[cs - 8:]
    tail_out_ref[...] = xr[cs - 8:]
    xc = _silu(conv)
    xs = xc[:, :D_INNER]
    bm = xc[:, D_INNER:D_INNER + SSM_GROUPS * D_STATE]
    cm = xc[:, D_INNER + SSM_GROUPS * D_STATE:]

    lane = lax.broadcasted_iota(jnp.int32, (cs, LANES), 1)
    rowi = lax.broadcasted_iota(jnp.int32, (cs, LANES), 0)
    lane_lo = lane < SSM_HEAD_DIM
    tri = rowi >= lane

    dt = _softplus(dt_ref[...] + dtb_ref[...])
    a = jnp.where(lane[:1] < SSM_HEADS, -jnp.exp(alog_ref[...]), 0.0)
    da = dt * a
    tril = jnp.where(tri, 1.0, 0.0).astype(BF16)
    p1 = da.astype(BF16)
    r1 = da - p1.astype(F32)
    p2 = r1.astype(BF16)
    p3 = (r1 - p2.astype(F32)).astype(BF16)
    acum = (jnp.dot(tril, p1, preferred_element_type=F32) + jnp.dot(tril, p2, preferred_element_type=F32)
            + jnp.dot(tril, p3, preferred_element_type=F32))
    acum_t = acum.T

    dt_e = _expand_heads(dt, lane_lo)
    ac_e = _expand_heads(acum, lane_lo)
    xdt = xs * dt_e
    ea_e = jnp.exp(ac_e)
    dend_e = jnp.exp(ac_e[cs - 1:cs, :] - ac_e)
    cdec = ea_e[cs - 1:cs, :]
    xdt_bf = xdt.astype(BF16)
    xdec_bf = (xdt * dend_e).astype(BF16)

    y_parts = []
    for grp in range(SSM_GROUPS):
        bg = bm[:, grp * D_STATE:(grp + 1) * D_STATE]
        cg = cm[:, grp * D_STATE:(grp + 1) * D_STATE].astype(BF16)
        cb = lax.dot_general(cg, bg.astype(BF16), NT_DIMS, preferred_element_type=F32)
        hpg = SSM_HEADS // SSM_GROUPS
        intra = []
        for k in range(hpg // 2):
            pair = grp * (hpg // 2) + k
            xpair = xdt_bf[:, pair * LANES:(pair + 1) * LANES]
            acc = None
            for hh in range(2):
                h = 2 * pair + hh
                seg = jnp.broadcast_to(acum[:, h:h + 1], (cs, cs)) - acum_t[h:h + 1, :]
                lmat = jnp.exp(jnp.where(tri, seg, -jnp.inf))
                mh = (cb * lmat).astype(BF16)
                xh = jnp.where(lane_lo if hh == 0 else jnp.logical_not(lane_lo), xpair, jnp.zeros_like(xpair))
                part = jnp.dot(mh, xh, preferred_element_type=F32)
                acc = part if acc is None else acc + part
            intra.append(acc)
        y_intra = jnp.concatenate(intra, axis=1)
        st = st_ref[grp]
        y_inter = jnp.dot(cg, st.astype(BF16), preferred_element_type=F32) * ea_e[:, grp * gw:(grp + 1) * gw]
        new_st = cdec[:, grp * gw:(grp + 1) * gw] * st + jnp.dot(
            bg.T.astype(BF16), xdec_bf[:, grp * gw:(grp + 1) * gw], preferred_element_type=F32)
        st_ref[grp] = new_st
        st_out_ref[grp] = new_st
        y_parts.append(y_intra + y_inter)
    y = jnp.concatenate(y_parts, axis=1) + dsk_ref[...] * xs
    y_ref[...] = _gated_group_norm(y, z_ref[...], g_ref[...]).astype(y_ref.dtype)


def _pad_lanes(v):
    return jnp.pad(v.reshape(1, -1), ((0, 0), (0, LANES - v.size)))


def _ssd_prompt(proj, dtp, conv_w, conv_b, dt_bias, a_log, d_skip, g_ssm):
    m = proj.shape[0]
    cs = SSD_CHUNK
    assert m % cs == 0
    gw = D_INNER // SSM_GROUPS
    z_blk = (ATT_WIDTH + 2 * KV_WIDTH) // D_INNER
    x_blk = (ATT_WIDTH + 2 * KV_WIDTH + D_INNER) // CONV_DIM
    assert z_blk * D_INNER == ATT_WIDTH + 2 * KV_WIDTH and x_blk * CONV_DIM == ATT_WIDTH + 2 * KV_WIDTH + D_INNER
    const = lambda shape: pl.BlockSpec(shape, lambda c: tuple(0 for _ in shape))
    return pl.pallas_call(
        _ssd_kernel,
        grid=(m // cs,),
        in_specs=[pl.BlockSpec((cs, D_INNER), lambda c: (c, z_blk)),
                  pl.BlockSpec((cs, CONV_DIM), lambda c: (c, x_blk)),
                  pl.BlockSpec((cs, LANES), lambda c: (c, 0)),
                  const((CONV_WIDTH, CONV_DIM)), const((1, CONV_DIM)), const((1, LANES)), const((1, LANES)),
                  const((1, D_INNER)), const((1, D_INNER))],
        out_specs=[pl.BlockSpec((cs, D_INNER), lambda c: (c, 0)),
                   const((8, CONV_DIM)), const((SSM_GROUPS, D_STATE, gw))],
        out_shape=[jax.ShapeDtypeStruct((m, D_INNER), BF16),
                   jax.ShapeDtypeStruct((8, CONV_DIM), F32),
                   jax.ShapeDtypeStruct((SSM_GROUPS, D_STATE, gw), F32)],
        scratch_shapes=[pltpu.VMEM((8, CONV_DIM), F32), pltpu.VMEM((SSM_GROUPS, D_STATE, gw), F32)],
        compiler_params=_params("arbitrary"),
        name="prompt_ssd",
    )(proj, proj, dtp, conv_w, conv_b.reshape(1, CONV_DIM), _pad_lanes(dt_bias), _pad_lanes(a_log),
      jnp.repeat(d_skip, SSM_HEAD_DIM).reshape(1, D_INNER), g_ssm.reshape(1, D_INNER))


def _sample_prep_kernel(proj_ref, qg_ref, kg_ref, q_ref, k_ref, v_ref):
    for h in range(ATT_HEADS):
        q_ref[:, h * HEAD_DIM:(h + 1) * HEAD_DIM] = _head_norm(proj_ref[:, h * HEAD_DIM:(h + 1) * HEAD_DIM], qg_ref[...])
    for kv in range(N_KV_HEADS):
        lo = ATT_WIDTH + kv * HEAD_DIM
        k_ref[:, kv * HEAD_DIM:(kv + 1) * HEAD_DIM] = _head_norm(proj_ref[:, lo:lo + HEAD_DIM], kg_ref[...])
    v_ref[...] = proj_ref[:, ATT_WIDTH + KV_WIDTH:ATT_WIDTH + 2 * KV_WIDTH]


def _sample_prep(proj, q_gain, k_gain):
    n = proj.shape[0]
    qkv_w = ATT_WIDTH + 2 * KV_WIDTH
    return pl.pallas_call(
        _sample_prep_kernel,
        grid=(1,),
        in_specs=[pl.BlockSpec((n, qkv_w), lambda i: (0, 0)),
                  pl.BlockSpec((1, HEAD_DIM), lambda i: (0, 0)),
                  pl.BlockSpec((1, HEAD_DIM), lambda i: (0, 0))],
        out_specs=[pl.BlockSpec((n, ATT_WIDTH), lambda i: (0, 0)),
                   pl.BlockSpec((n, KV_WIDTH), lambda i: (0, 0)),
                   pl.BlockSpec((n, KV_WIDTH), lambda i: (0, 0))],
        out_shape=[jax.ShapeDtypeStruct((n, ATT_WIDTH), F32),
                   jax.ShapeDtypeStruct((n, KV_WIDTH), F32),
                   jax.ShapeDtypeStruct((n, KV_WIDTH), F32)],
        compiler_params=_params("arbitrary"),
        name="sample_qk_prep",
    )(proj, q_gain.reshape(1, HEAD_DIM), k_gain.reshape(1, HEAD_DIM))


def _sample_gate_kernel(pps, pt_ref, q_ref, *rest):
    page_refs = rest[:pps]
    idx_ref = rest[pps]
    ksum_ref = rest[pps + 1]
    j = pl.program_id(1)
    page_rows = page_refs[0].shape[0]
    ppb = MOBA_BLOCK * N_KV_HEADS // page_rows
    bps = pps // ppb
    fold = 8 // N_KV_HEADS
    def page_sum(ref):
        x = ref[...].reshape(page_rows // 8, 8, HEAD_DIM)
        q = x.shape[0] // 4
        parts = [jnp.sum(x[t * q:(t + 1) * q], axis=0) for t in range(4)]
        return (parts[0] + parts[1]) + (parts[2] + parts[3])

    for b in range(bps):
        s8 = page_sum(page_refs[b * ppb])
        for t in range(1, ppb):
            s8 = s8 + page_sum(page_refs[b * ppb + t])
        s = s8[0:N_KV_HEADS]
        for t in range(1, fold):
            s = s + s8[t * N_KV_HEADS:(t + 1) * N_KV_HEADS]
        s = s * (1.0 / MOBA_BLOCK)
        for kv in range(N_KV_HEADS):
            ksum_ref[kv, j, b:b + 1, :] = s[kv:kv + 1, :]

    @pl.when(j == pl.num_programs(1) - 1)
    def _():
        nblk = ksum_ref.shape[1] * bps
        q = q_ref[0]
        hrow = lax.broadcasted_iota(jnp.int32, (ATT_HEADS, nblk), 0)
        gate = jnp.zeros((ATT_HEADS, nblk), F32)
        for kv in range(N_KV_HEADS):
            gk = _dot3(q, ksum_ref[kv].reshape(nblk, HEAD_DIM), NT_DIMS)
            gate = jnp.where(hrow // KV_GROUP == kv, gk, gate)
        lane = lax.broadcasted_iota(jnp.int32, (ATT_HEADS, LANES), 1)
        if nblk < LANES:
            gate = jnp.concatenate([gate, jnp.full((ATT_HEADS, LANES - nblk), -jnp.inf, F32)], axis=1)
        _, firsts = _top3(gate, lane.astype(F32), 1)
        out = jnp.zeros((ATT_HEADS, LANES), jnp.int32)
        for t, first in enumerate(firsts):
            out = jnp.where(lane == t, first.astype(jnp.int32), out)
        idx_ref[0] = out


def _sample_gate(q3, cache_k, layer, page_table):
    n, n_pages = page_table.shape
    depth, pool, page = cache_k.shape[:3]
    ppb = MOBA_BLOCK // page
    nblk = n_pages // ppb
    assert n_pages % ppb == 0 and MOBA_TOPK <= nblk <= LANES and 8 % N_KV_HEADS == 0
    pps = min(32, n_pages)
    assert n_pages % pps == 0 and pps % ppb == 0
    steps = n_pages // pps
    cache_k = cache_k.reshape(depth, pool, page * N_KV_HEADS, HEAD_DIM)

    def page_spec(t):
        return pl.BlockSpec((None, None, page * N_KV_HEADS, HEAD_DIM),
                            lambda s, j, pt: (layer, pt[s, j * pps + t], 0, 0))

    grid_spec = pltpu.PrefetchScalarGridSpec(
        num_scalar_prefetch=1,
        grid=(n, steps),
        in_specs=[pl.BlockSpec((1, ATT_HEADS, HEAD_DIM), lambda s, j, pt: (s, 0, 0))]
        + [page_spec(t) for t in range(pps)],
        out_specs=pl.BlockSpec((1, ATT_HEADS, LANES), lambda s, j, pt: (s, 0, 0)),
        scratch_shapes=[pltpu.VMEM((N_KV_HEADS, steps, pps // ppb, HEAD_DIM), F32)],
    )
    return pl.pallas_call(
        functools.partial(_sample_gate_kernel, pps),
        grid_spec=grid_spec,
        out_shape=jax.ShapeDtypeStruct((n, ATT_HEADS, LANES), jnp.int32),
        compiler_params=_params("arbitrary", "arbitrary"),
        name="sample_gate_topk",
    )(page_table, q3, *([cache_k] * pps))


def _sample_attn_kernel(past, ppb, layer, pt_ref, idx_ref, q_ref, kn_ref, vn_ref, ck_ref, cv_ref, o_ref,
                        kbuf, vbuf, sems):
    s = pl.program_id(0)
    page = MOBA_BLOCK // ppb
    slot = s % 2

    def copies(seq, buf, h, t, p):
        kv = h // KV_GROUP
        blk = idx_ref[seq, h * MOBA_TOPK + t]
        phys = pt_ref[seq, blk * ppb + p]
        dst = pl.ds((t * ppb + p) * page, page)
        return (pltpu.make_async_copy(ck_ref.at[layer, phys, :, kv, :], kbuf.at[buf, h, dst, :],
                                      sems.at[buf, 0, h, t * ppb + p]),
                pltpu.make_async_copy(cv_ref.at[layer, phys, :, kv, :], vbuf.at[buf, h, dst, :],
                                      sems.at[buf, 1, h, t * ppb + p]))

    triples = [(h, t, p) for h in range(ATT_HEADS) for t in range(MOBA_TOPK) for p in range(ppb)]

    def start_all(seq, buf):
        for h, t, p in triples:
            ck, cv = copies(seq, buf, h, t, p)
            ck.start()
            cv.start()

    @pl.when(s == 0)
    def _():
        start_all(0, 0)

    @pl.when(s + 1 < pl.num_programs(0))
    def _():
        start_all(s + 1, 1 - slot)

    for h, t, p in triples:
        ck, cv = copies(s, slot, h, t, p)
        ck.wait()
        cv.wait()

    nsel = MOBA_TOPK * MOBA_BLOCK
    rowi = lax.broadcasted_iota(jnp.int32, (nsel, 1), 0)
    off = jnp.bitwise_and(rowi, MOBA_BLOCK - 1)
    for h in range(ATT_HEADS):
        kv = h // KV_GROUP
        slope = _alibi_slope(h)
        q = q_ref[0, h:h + 1, :]
        sc = jnp.sum(kbuf[slot, h] * q, axis=1, keepdims=True) * ATT_SCALE
        pos = jnp.zeros((nsel, 1), jnp.int32)
        for t in range(MOBA_TOPK):
            pos = jnp.where(rowi // MOBA_BLOCK == t, idx_ref[s, h * MOBA_TOPK + t] * MOBA_BLOCK, pos)
        dist = (past - (pos + off)).astype(F32)
        sc = sc - slope * dist
        s_own = jnp.sum(kn_ref[0, kv:kv + 1, :] * q, axis=1, keepdims=True) * ATT_SCALE
        mx = jnp.maximum(jnp.max(sc, axis=0, keepdims=True), s_own)
        p = jnp.exp(sc - mx)
        p_own = jnp.exp(s_own - mx)
        denom = jnp.sum(p, axis=0, keepdims=True) + p_own
        num = jnp.sum(p * vbuf[slot, h], axis=0, keepdims=True) + p_own * vn_ref[0, kv:kv + 1, :]
        o_ref[0, h:h + 1, :] = num / denom


def _sample_attn(q3, k_new3, v_new3, cache_k, cache_v, layer, page_table, idx):
    n, n_pages = page_table.shape
    page = cache_k.shape[2]
    ppb = MOBA_BLOCK // page
    past = n_pages * page
    assert past % MOBA_BLOCK == 0
    nsel = MOBA_TOPK * MOBA_BLOCK
    grid_spec = pltpu.PrefetchScalarGridSpec(
        num_scalar_prefetch=2,
        grid=(n,),
        in_specs=[pl.BlockSpec((1, ATT_HEADS, HEAD_DIM), lambda s, pt, ix: (s, 0, 0)),
                  pl.BlockSpec((1, N_KV_HEADS, HEAD_DIM), lambda s, pt, ix: (s, 0, 0)),
                  pl.BlockSpec((1, N_KV_HEADS, HEAD_DIM), lambda s, pt, ix: (s, 0, 0)),
                  pl.BlockSpec(memory_space=pl.ANY),
                  pl.BlockSpec(memory_space=pl.ANY)],
        out_specs=pl.BlockSpec((1, ATT_HEADS, HEAD_DIM), lambda s, pt, ix: (s, 0, 0)),
        scratch_shapes=[pltpu.VMEM((2, ATT_HEADS, nsel, HEAD_DIM), F32),
                        pltpu.VMEM((2, ATT_HEADS, nsel, HEAD_DIM), F32),
                        pltpu.SemaphoreType.DMA((2, 2, ATT_HEADS, MOBA_TOPK * ppb))],
    )
    return pl.pallas_call(
        functools.partial(_sample_attn_kernel, past, ppb, layer),
        grid_spec=grid_spec,
        out_shape=jax.ShapeDtypeStruct((n, ATT_HEADS, HEAD_DIM), F32),
        compiler_params=_params("arbitrary"),
        name="sample_moba_attn",
    )(page_table, idx, q3, k_new3, v_new3, cache_k, cache_v)


def _sample_ssd_kernel(z_ref, xbc_ref, dt_ref, buf_ref, h0_ref, cw_ref, cb_ref, dtb_ref, alog_ref, dsk_ref, g_ref,
                       y_ref, buf_out_ref, h_out_ref):
    x = xbc_ref[0]
    buf = buf_ref[0]
    cw = cw_ref[...]
    conv = cb_ref[...] + cw[CONV_WIDTH - 1:CONV_WIDTH] * x
    for t in range(CONV_WIDTH - 1):
        conv = conv + cw[t:t + 1] * buf[t:t + 1]
    buf_out_ref[0, 0:CONV_WIDTH - 2, :] = buf[1:CONV_WIDTH - 1]
    buf_out_ref[0, CONV_WIDTH - 2:CONV_WIDTH - 1, :] = x
    xc = _silu(conv)
    xs = xc[:, :D_INNER]
    bm = xc[:, D_INNER:D_INNER + SSM_GROUPS * D_STATE]
    cm = xc[:, D_INNER + SSM_GROUPS * D_STATE:]

    lane1 = lax.broadcasted_iota(jnp.int32, (1, LANES), 1)
    rowi = lax.broadcasted_iota(jnp.int32, (LANES, LANES), 0)
    dt = _softplus(dt_ref[0] + dtb_ref[...])
    a = jnp.where(lane1 < SSM_HEADS, -jnp.exp(alog_ref[...]), 0.0)
    dec = jnp.exp(dt * a)
    dt_e = _expand_heads(dt, lane1 < SSM_HEAD_DIM)
    xdt = xs * dt_e
    xdt_rows = jnp.broadcast_to(xdt, (LANES, D_INNER))

    hpg = SSM_HEADS // SSM_GROUPS
    y_parts = []
    for pair in range(SSM_HEADS // 2):
        grp = (2 * pair) // hpg
        xcol = xdt_rows[:, pair * LANES:(pair + 1) * LANES].T
        dcol = jnp.where(rowi < SSM_HEAD_DIM,
                         jnp.broadcast_to(dec[:, 2 * pair:2 * pair + 1], (LANES, LANES)),
                         jnp.broadcast_to(dec[:, 2 * pair + 1:2 * pair + 2], (LANES, LANES)))
        h0 = h0_ref[0, 2 * pair:2 * pair + 2].reshape(LANES, D_STATE)
        hn = dcol * h0 + xcol * bm[:, grp * D_STATE:(grp + 1) * D_STATE]
        h_out_ref[0, 2 * pair:2 * pair + 2] = hn.reshape(2, SSM_HEAD_DIM, D_STATE)
        cgrow = jnp.broadcast_to(cm[:, grp * D_STATE:(grp + 1) * D_STATE], (8, D_STATE))
        ypair = _dot3(cgrow, hn, NT_DIMS)
        y_parts.append(ypair[0:1])
    y = jnp.concatenate(y_parts, axis=1) + dsk_ref[...] * xs
    y_ref[0] = _gated_group_norm(y, z_ref[0], g_ref[...]).astype(y_ref.dtype)


def _ssd_sample(proj, dtp, state_conv, state_ssm, conv_w, conv_b, dt_bias, a_log, d_skip, g_ssm):
    n = proj.shape[0]
    z0 = ATT_WIDTH + 2 * KV_WIDTH
    z3 = proj[:, z0:z0 + D_INNER].reshape(n, 1, D_INNER)
    x3 = proj[:, z0 + D_INNER:z0 + D_INNER + CONV_DIM].reshape(n, 1, CONV_DIM)
    dt3 = dtp.reshape(n, 1, LANES)
    const = lambda shape: pl.BlockSpec(shape, lambda s: tuple(0 for _ in shape))
    per_seq = lambda shape: pl.BlockSpec((1,) + shape, lambda s: (s,) + tuple(0 for _ in shape))
    y, buf, h = pl.pallas_call(
        _sample_ssd_kernel,
        grid=(n,),
        in_specs=[per_seq((1, D_INNER)), per_seq((1, CONV_DIM)), per_seq((1, LANES)),
                  per_seq((CONV_WIDTH - 1, CONV_DIM)), per_seq((SSM_HEADS, SSM_HEAD_DIM, D_STATE)),
                  const((CONV_WIDTH, CONV_DIM)), const((1, CONV_DIM)), const((1, LANES)), const((1, LANES)),
                  const((1, D_INNER)), const((1, D_INNER))],
        out_specs=[per_seq((1, D_INNER)), per_seq((CONV_WIDTH - 1, CONV_DIM)),
                   per_seq((SSM_HEADS, SSM_HEAD_DIM, D_STATE))],
        out_shape=[jax.ShapeDtypeStruct((n, 1, D_INNER), BF16),
                   jax.ShapeDtypeStruct((n, CONV_WIDTH - 1, CONV_DIM), F32),
                   jax.ShapeDtypeStruct((n, SSM_HEADS, SSM_HEAD_DIM, D_STATE), F32)],
        compiler_params=_params("arbitrary"),
        name="sample_ssd",
    )(z3, x3, dt3, state_conv, state_ssm, conv_w, conv_b.reshape(1, CONV_DIM), _pad_lanes(dt_bias),
      _pad_lanes(a_log), jnp.repeat(d_skip, SSM_HEAD_DIM).reshape(1, D_INNER), g_ssm.reshape(1, D_INNER))
    return y.reshape(n, D_INNER), buf, h


def kernel(x_prompt, x_sample, cache_k, cache_v, state_conv, state_ssm, page_table, c_prompt, c_sample, w_ada, b_ada, g_mix_norm, w_in, q_gain, k_gain, g_att_out, conv_w, conv_b, dt_bias, a_log, d_skip, g_ssm_out, w_out, g_ffn_norm, w_gate, w_up, w_down):
    n_p, seq, d = x_prompt.shape
    n_s, dec_seq, _ = x_sample.shape
    assert n_p == 1 and dec_seq == 1
    depth = w_ada.shape[0]
    main_w = ATT_WIDTH + 2 * KV_WIDTH + D_INNER + CONV_DIM

    yp = x_prompt.reshape(seq, d)
    ys = x_sample.reshape(n_s, d)
    c_rows = n_p + n_s
    c_pad = -(-c_rows // 8) * 8
    c_all = jnp.pad(jnp.concatenate([c_prompt, c_sample], axis=0), ((0, c_pad - c_rows), (0, 0)))
    outs = [[] for _ in range(8)]
    for l in range(depth):
        mod = _ada(c_all, w_ada[l], b_ada[l])
        mod_p, mod_s = mod[0:1], mod[1:1 + n_s]
        w_in_t = jnp.swapaxes(w_in[l], 0, 1)

        proj, dtp = _inproj(yp, g_mix_norm[l], mod_p, w_in_t, main_w, tn=1536, tm=512, name="in_proj")
        k_out, v_out, kp, vp, qp = _prompt_prep(proj, q_gain[l], k_gain[l])
        o_att = _prompt_attn(qp, kp, vp)
        y_ssm, tail, st = _ssd_prompt(proj, dtp, conv_w[l], conv_b[l], dt_bias[l], a_log[l], d_skip[l], g_ssm_out[l])
        x1_p, h2_p = _outproj(o_att, g_att_out[l], y_ssm, w_out[l], yp, mod_p, g_ffn_norm[l])
        hpg = SSM_HEADS // SSM_GROUPS
        ssm_p = st.reshape(SSM_GROUPS, D_STATE, hpg, SSM_HEAD_DIM).transpose(0, 2, 3, 1).reshape(
            1, SSM_HEADS, SSM_HEAD_DIM, D_STATE)
        outs[0].append(k_out.reshape(1, seq, N_KV_HEADS, HEAD_DIM))
        outs[1].append(v_out.reshape(1, seq, N_KV_HEADS, HEAD_DIM))
        outs[2].append(tail[8 - (CONV_WIDTH - 1):].reshape(1, CONV_WIDTH - 1, CONV_DIM))
        outs[3].append(ssm_p)

        proj_s, dts = _inproj(ys, g_mix_norm[l], mod_s, w_in_t, main_w, tn=768, tm=512, name="in_proj_sample",
                              split=True)
        q_s, k_s, v_s = _sample_prep(proj_s, q_gain[l], k_gain[l])
        q3 = q_s.reshape(n_s, ATT_HEADS, HEAD_DIM)
        idx = _sample_gate(q3, cache_k, l, page_table)
        idx_flat = idx[:, :, :MOBA_TOPK].reshape(n_s, ATT_HEADS * MOBA_TOPK)
        o_s = _sample_attn(q3, k_s.reshape(n_s, N_KV_HEADS, HEAD_DIM), v_s.reshape(n_s, N_KV_HEADS, HEAD_DIM),
                           cache_k, cache_v, l, page_table, idx_flat)
        y_s, buf_s, h_s = _ssd_sample(proj_s, dts, state_conv[l], state_ssm[l], conv_w[l], conv_b[l], dt_bias[l],
                                      a_log[l], d_skip[l], g_ssm_out[l])
        x1_s, h2_s = _outproj(o_s.reshape(n_s, ATT_WIDTH), g_att_out[l], y_s, w_out[l], ys, mod_s, g_ffn_norm[l])

        hid_p, hid_s = _gateup(h2_p, h2_s, w_gate[l], w_up[l])
        yp, ys = _down(hid_p, hid_s, w_down[l], x1_p, x1_s, mod_p, mod_s)
        outs[4].append(k_s.reshape(n_s, 1, N_KV_HEADS, HEAD_DIM))
        outs[5].append(v_s.reshape(n_s, 1, N_KV_HEADS, HEAD_DIM))
        outs[6].append(buf_s)
        outs[7].append(h_s)
    stacked = [jnp.stack(o) for o in outs]
    return (yp.reshape(1, seq, d), ys.reshape(n_s, 1, d), *stacked)
```

```python
import functools

import jax
import jax.numpy as jnp
from jax import lax
from jax.experimental import pallas as pl
from jax.experimental.pallas import tpu as pltpu

F32 = jnp.float32
BF16 = jnp.bfloat16

HEAD_DIM = 128
ATT_HEADS = 8
N_KV_HEADS = 4
KV_GROUP = ATT_HEADS // N_KV_HEADS
ATT_WIDTH = ATT_HEADS * HEAD_DIM
KV_WIDTH = N_KV_HEADS * HEAD_DIM
MOBA_BLOCK = 256
MOBA_TOPK = 3
D_INNER = 1024
SSM_HEAD_DIM = 64
SSM_HEADS = D_INNER // SSM_HEAD_DIM
SSM_GROUPS = 2
D_STATE = 128
CONV_WIDTH = 4
CONV_DIM = D_INNER + 2 * SSM_GROUPS * D_STATE
SSD_CHUNK = 128
EPS = 1e-6
ATT_SCALE = HEAD_DIM ** -0.5

LANES = 128
FEAT_KBLK = 96
FEAT_KOFF = 97
FEAT_ONE_A = 98
FEAT_ONE_B = 99
NEG_BIG = -1e30

NT_DIMS = (((1,), (1,)), ((), ()))
VMEM_LIMIT = 56 * 1024 * 1024


def _params(*sem):
    return pltpu.CompilerParams(dimension_semantics=sem, vmem_limit_bytes=VMEM_LIMIT)


def _silu(x):
    return x / (1.0 + jnp.exp(-x))


def _softplus(x):
    return jnp.maximum(x, 0.0) + jnp.log1p(jnp.exp(-jnp.abs(x)))


def _split_bf16(x):
    hi = x.astype(BF16)
    lo = (x - hi.astype(F32)).astype(BF16)
    return hi, lo


def _dot3(a, b, dims):
    ah, al = _split_bf16(a)
    bh, bl = _split_bf16(b)
    d = lambda x, y: lax.dot_general(x, y, dims, preferred_element_type=F32)
    return d(ah, bh) + d(al, bh) + d(ah, bl)


def _alibi_slope(h):
    return 2.0 ** (-8.0 * (h + 1) / ATT_HEADS)


NN_DIMS = (((1,), (0,)), ((), ()))


def _ada_kernel(c_ref, w_ref, b_ref, o_ref):
    rows = c_ref.shape[0]
    ahi, alo = _split_bf16(_silu(c_ref[...]))
    whi, wlo = _split_bf16(w_ref[...])
    both = jnp.dot(jnp.concatenate([ahi, alo], axis=0), whi, preferred_element_type=F32)
    o_ref[...] = both[:rows] + both[rows:] + jnp.dot(ahi, wlo, preferred_element_type=F32) + b_ref[...]


def _ada(c_all, w, b):
    rows, d = c_all.shape
    n = w.shape[1]
    tn = 512
    return pl.pallas_call(
        _ada_kernel,
        grid=(n // tn,),
        in_specs=[pl.BlockSpec((rows, d), lambda j: (0, 0)),
                  pl.BlockSpec((d, tn), lambda j: (0, j)),
                  pl.BlockSpec((1, tn), lambda j: (0, j))],
        out_specs=pl.BlockSpec((rows, tn), lambda j: (0, j)),
        out_shape=jax.ShapeDtypeStruct((rows, n), F32),
        compiler_params=_params("arbitrary"),
        name="ada_mod",
    )(c_all, w, b.reshape(1, n))


def _mod_spec(mod_rows, tm, width, col_of):
    if mod_rows == 1:
        return pl.BlockSpec((1, width), lambda n, i: (0, col_of(n)))
    return pl.BlockSpec((tm, width), lambda n, i: (i, col_of(n)))


def _rms(x):
    return x * lax.rsqrt(jnp.mean(x * x, axis=-1, keepdims=True) + EPS)


def _cast_weight_once(w_ref, wbf_ref):
    @pl.when(pl.program_id(1) == 0)
    def _():
        wbf_ref[...] = w_ref[...].astype(BF16)


STAGE_CHUNK = 384


def _inproj_kernel(n_main, split, x_ref, g_ref, sc_ref, sh_ref, w_ref, wdt_ref, o_ref, odt_ref, *w_scratch):
    n = pl.program_id(0)
    first_row_tile = pl.program_id(1) == 0
    main_refs, dt_refs = w_scratch[:len(w_scratch) // 2], w_scratch[len(w_scratch) // 2:]

    def stage(src_ref, refs, pad_rows):
        cols = src_ref.shape[0]
        chunk = STAGE_CHUNK if cols % STAGE_CHUNK == 0 else cols
        for c0 in range(0, cols, chunk):
            w = src_ref[c0:c0 + chunk, :]
            if pad_rows:
                w = jnp.concatenate([w, jnp.zeros((pad_rows, w.shape[1]), F32)], axis=0)
            w = w.T
            for part, ref in zip(_split_bf16(w) if split else (w.astype(BF16),), refs):
                ref[:, c0:c0 + w.shape[1]] = part

    def product(hn, refs):
        mm = lambda a, b: jnp.dot(a, b, preferred_element_type=F32)
        if not split:
            return mm(hn.astype(BF16), refs[0][...])
        hhi, hlo = _split_bf16(hn)
        return mm(hhi, refs[0][...]) + mm(hlo, refs[0][...]) + mm(hhi, refs[1][...])

    @pl.when(first_row_tile)
    def _():
        stage(w_ref, main_refs, 0)

    @pl.when(jnp.logical_and(n == 0, first_row_tile))
    def _():
        stage(wdt_ref, dt_refs, LANES - wdt_ref.shape[0])

    hn = (_rms(x_ref[...]) * g_ref[...]) * (1.0 + sc_ref[...]) + sh_ref[...]
    o_ref[...] = product(hn, main_refs)

    @pl.when(n == 0)
    def _():
        odt_ref[...] = product(hn, dt_refs)


def _inproj(x, g, mod, w_t, main_w, *, tn, tm, name, split=False):
    m, d = x.shape
    tm = min(tm, m)
    mr = mod.shape[0]
    n_main = main_w // tn
    n_dt = w_t.shape[0] - main_w
    assert n_main * tn == main_w and main_w % n_dt == 0 and n_dt % 16 == 0
    last_i = m // tm - 1
    ncopies = 2 if split else 1
    proj, dtp = pl.pallas_call(
        functools.partial(_inproj_kernel, n_main, split),
        grid=(n_main, m // tm),
        in_specs=[pl.BlockSpec((tm, d), lambda n, i: (i, 0)),
                  pl.BlockSpec((1, d), lambda n, i: (0, 0)),
                  _mod_spec(mr, tm, d, lambda n: 1),
                  _mod_spec(mr, tm, d, lambda n: 0),
                  pl.BlockSpec((tn, d), lambda n, i: (n, 0)),
                  pl.BlockSpec((n_dt, d), lambda n, i: (main_w // n_dt, 0))],
        out_specs=[pl.BlockSpec((tm, tn), lambda n, i: (i, n)),
                   pl.BlockSpec((tm, LANES), lambda n, i: (jnp.where(n == 0, i, last_i), 0))],
        out_shape=[jax.ShapeDtypeStruct((m, main_w), F32), jax.ShapeDtypeStruct((m, LANES), F32)],
        scratch_shapes=[pltpu.VMEM((d, tn), BF16)] * ncopies + [pltpu.VMEM((d, LANES), BF16)] * ncopies,
        compiler_params=_params("arbitrary", "arbitrary"),
        name=name,
    )(x, g.reshape(1, d), mod, mod, w_t, w_t)
    return proj, dtp


def _outproj_kernel(a_ref, ga_ref, b_ref, w_ref, x_ref, gt_ref, gf_ref, sc_ref, sh_ref, x1_ref, h2_ref, wbf_ref):
    _cast_weight_once(w_ref, wbf_ref)
    ka = a_ref.shape[1]
    a = (_rms(a_ref[...]) * ga_ref[...]).astype(BF16)
    acc = jnp.dot(a, wbf_ref[:ka, :], preferred_element_type=F32)
    acc = acc + jnp.dot(b_ref[...], wbf_ref[ka:, :], preferred_element_type=F32)
    x1 = x_ref[...] + gt_ref[...] * acc
    x1_ref[...] = x1
    h2_ref[...] = ((_rms(x1) * gf_ref[...]) * (1.0 + sc_ref[...]) + sh_ref[...]).astype(BF16)


def _outproj(a, g_a, b, w, x, mod, g_ffn):
    m, ka = a.shape
    kb = b.shape[1]
    d = w.shape[1]
    tm = min(256, m)
    mr = mod.shape[0]
    return pl.pallas_call(
        _outproj_kernel,
        grid=(1, m // tm),
        in_specs=[pl.BlockSpec((tm, ka), lambda n, i: (i, 0)),
                  pl.BlockSpec((1, ka), lambda n, i: (0, 0)),
                  pl.BlockSpec((tm, kb), lambda n, i: (i, 0)),
                  pl.BlockSpec((ka + kb, d), lambda n, i: (0, 0), pipeline_mode=pl.Buffered(1)),
                  pl.BlockSpec((tm, d), lambda n, i: (i, 0)),
                  _mod_spec(mr, tm, d, lambda n: 2),
                  pl.BlockSpec((1, d), lambda n, i: (0, 0)),
                  _mod_spec(mr, tm, d, lambda n: 4),
                  _mod_spec(mr, tm, d, lambda n: 3)],
        out_specs=[pl.BlockSpec((tm, d), lambda n, i: (i, 0)),
                   pl.BlockSpec((tm, d), lambda n, i: (i, 0))],
        out_shape=[jax.ShapeDtypeStruct((m, d), F32), jax.ShapeDtypeStruct((m, d), BF16)],
        scratch_shapes=[pltpu.VMEM((ka + kb, d), BF16)],
        compiler_params=_params("arbitrary", "arbitrary"),
        name="out_proj",
    )(a, g_a.reshape(1, ka), b, w, x, mod, g_ffn.reshape(1, d), mod, mod)


def _gateup_kernel(tiles_p, xp_ref, xs_ref, wg_ref, wu_ref, op_ref, os_ref, wgb_ref, wub_ref):
    _cast_weight_once(wg_ref, wgb_ref)
    _cast_weight_once(wu_ref, wub_ref)
    i = pl.program_id(1)

    def swiglu(h2):
        g = jnp.dot(h2, wgb_ref[...], preferred_element_type=F32)
        u = jnp.dot(h2, wub_ref[...], preferred_element_type=F32)
        return (_silu(g) * u).astype(BF16)

    @pl.when(i < tiles_p)
    def _():
        op_ref[...] = swiglu(xp_ref[...])

    @pl.when(i == tiles_p)
    def _():
        os_ref[...] = swiglu(xs_ref[...])


def _gateup(h2_p, h2_s, wg, wu):
    m, d = h2_p.shape
    ms = h2_s.shape[0]
    f = wg.shape[1]
    tn, tm = 512, 1024
    tiles_p = m // tm
    prow = lambda i: jnp.minimum(i, tiles_p - 1)
    return pl.pallas_call(
        functools.partial(_gateup_kernel, tiles_p),
        grid=(f // tn, tiles_p + 1),
        in_specs=[pl.BlockSpec((tm, d), lambda n, i: (prow(i), 0)),
                  pl.BlockSpec((ms, d), lambda n, i: (0, 0)),
                  pl.BlockSpec((d, tn), lambda n, i: (0, n)),
                  pl.BlockSpec((d, tn), lambda n, i: (0, n))],
        out_specs=[pl.BlockSpec((tm, tn), lambda n, i: (prow(i), n)),
                   pl.BlockSpec((ms, tn), lambda n, i: (0, n))],
        out_shape=[jax.ShapeDtypeStruct((m, f), BF16), jax.ShapeDtypeStruct((ms, f), BF16)],
        scratch_shapes=[pltpu.VMEM((d, tn), BF16), pltpu.VMEM((d, tn), BF16)],
        compiler_params=_params("arbitrary", "arbitrary"),
        name="ffn_gate_up",
    )(h2_p, h2_s, wg, wu)


def _down_kernel(tiles_p, hp_ref, hs_ref, w_ref, xp_ref, xs_ref, gtp_ref, gts_ref, op_ref, os_ref, wbf_ref):
    _cast_weight_once(w_ref, wbf_ref)
    i = pl.program_id(1)

    @pl.when(i < tiles_p)
    def _():
        acc = jnp.dot(hp_ref[...], wbf_ref[...], preferred_element_type=F32)
        op_ref[...] = xp_ref[...] + gtp_ref[...] * acc

    @pl.when(i == tiles_p)
    def _():
        acc = jnp.dot(hs_ref[...], wbf_ref[...], preferred_element_type=F32)
        os_ref[...] = xs_ref[...] + gts_ref[...] * acc


def _down(h_p, h_s, w, x_p, x_s, mod_p, mod_s):
    m, f = h_p.shape
    ms = h_s.shape[0]
    d = w.shape[1]
    tn, tm = 512, 512
    tiles_p = m // tm
    nb = d // tn
    assert mod_p.shape[0] == 1 and mod_s.shape[0] == ms
    prow = lambda i: jnp.minimum(i, tiles_p - 1)
    return pl.pallas_call(
        functools.partial(_down_kernel, tiles_p),
        grid=(nb, tiles_p + 1),
        in_specs=[pl.BlockSpec((tm, f), lambda n, i: (prow(i), 0)),
                  pl.BlockSpec((ms, f), lambda n, i: (0, 0)),
                  pl.BlockSpec((f, tn), lambda n, i: (0, n)),
                  pl.BlockSpec((tm, tn), lambda n, i: (prow(i), n)),
                  pl.BlockSpec((ms, tn), lambda n, i: (0, n)),
                  pl.BlockSpec((1, tn), lambda n, i: (0, 5 * nb + n)),
                  pl.BlockSpec((ms, tn), lambda n, i: (0, 5 * nb + n))],
        out_specs=[pl.BlockSpec((tm, tn), lambda n, i: (prow(i), n)),
                   pl.BlockSpec((ms, tn), lambda n, i: (0, n))],
        out_shape=[jax.ShapeDtypeStruct((m, d), F32), jax.ShapeDtypeStruct((ms, d), F32)],
        scratch_shapes=[pltpu.VMEM((f, tn), BF16)],
        compiler_params=_params("arbitrary", "arbitrary"),
        name="ffn_down",
    )(h_p, h_s, w, x_p, x_s, mod_p, mod_s)


def _head_norm(x, gain):
    return x * lax.rsqrt(jnp.mean(x * x, axis=-1, keepdims=True) + EPS) * gain


def _top3(g, idx_f, axis):
    sel = jnp.zeros(g.shape, F32)
    firsts = []
    for _ in range(MOBA_TOPK):
        mx = jnp.max(g, axis=axis, keepdims=True)
        ismax = jnp.logical_and(g == mx, mx > -jnp.inf)
        first = jnp.min(jnp.where(ismax, idx_f, float(LANES)), axis=axis, keepdims=True)
        pick = idx_f == first
        sel = jnp.where(pick, 1.0, sel)
        g = jnp.where(pick, -jnp.inf, g)
        firsts.append(first)
    return sel, firsts


def _prompt_prep_kernel(proj_ref, qg_ref, kg_ref, kout_ref, vout_ref, kp_ref, vp_ref, qp_ref, km_ref):
    i = pl.program_id(0)
    blk = MOBA_BLOCK

    @pl.when(i == 0)
    def _():
        km_ref[...] = jnp.zeros_like(km_ref)

    lane = lax.broadcasted_iota(jnp.int32, (blk, LANES), 1)
    row = lax.broadcasted_iota(jnp.int32, (blk, LANES), 0)
    sq_row = lax.broadcasted_iota(jnp.int32, (LANES, LANES), 0)
    i_f = i.astype(F32)
    row_f = row.astype(F32)

    kfeat = jnp.where(lane == i, 1.0, 0.0)
    kfeat = jnp.where(lane == FEAT_KBLK, i_f, kfeat)
    kfeat = jnp.where(lane == FEAT_KOFF, row_f, kfeat)
    kfeat = jnp.where(jnp.logical_or(lane == FEAT_ONE_A, lane == FEAT_ONE_B), 1.0, kfeat).astype(BF16)
    vfeat = jnp.ones((blk, LANES), BF16)

    kg = kg_ref[...]
    for kv in range(N_KV_HEADS):
        k = proj_ref[:, ATT_WIDTH + kv * HEAD_DIM:ATT_WIDTH + (kv + 1) * HEAD_DIM]
        kn = _head_norm(k, kg)
        kout_ref[:, kv * HEAD_DIM:(kv + 1) * HEAD_DIM] = kn
        kp_ref[kv, :, :HEAD_DIM] = kn.astype(BF16)
        kp_ref[kv, :, HEAD_DIM:] = kfeat
        ksum = jnp.sum(kn, axis=0, keepdims=True) * (1.0 / blk)
        km_ref[kv] = jnp.where(sq_row == i, jnp.broadcast_to(ksum, (LANES, LANES)), km_ref[kv])
        v = proj_ref[:, ATT_WIDTH + KV_WIDTH + kv * HEAD_DIM:ATT_WIDTH + KV_WIDTH + (kv + 1) * HEAD_DIM]
        vout_ref[:, kv * HEAD_DIM:(kv + 1) * HEAD_DIM] = v
        vp_ref[kv, :, :HEAD_DIM] = v.astype(BF16)
        vp_ref[kv, :, HEAD_DIM:] = vfeat

    qg = qg_ref[...]
    blk_id = lax.broadcasted_iota(jnp.int32, (LANES, blk), 0)
    blk_id_f = blk_id.astype(F32)
    valid = blk_id < i
    for h in range(ATT_HEADS):
        q = proj_ref[:, h * HEAD_DIM:(h + 1) * HEAD_DIM]
        qn = _head_norm(q, qg)
        gate = _dot3(km_ref[h // KV_GROUP], qn, NT_DIMS)
        sel_t, _ = _top3(jnp.where(valid, gate, -jnp.inf), blk_id_f, 0)
        sel = sel_t.T
        slope = _alibi_slope(h)
        qfeat = jnp.where(jnp.logical_and(lane < FEAT_KBLK, sel == 0.0), NEG_BIG, 0.0)
        qfeat = jnp.where(lane == FEAT_KBLK, slope * blk, qfeat)
        qfeat = jnp.where(lane == FEAT_KOFF, slope, qfeat)
        qfeat = jnp.where(lane == FEAT_ONE_A, -(slope * blk) * i_f, qfeat)
        qfeat = jnp.where(lane == FEAT_ONE_B, -slope * row_f, qfeat)
        qp_ref[h, :, :HEAD_DIM] = (qn * ATT_SCALE).astype(BF16)
        qp_ref[h, :, HEAD_DIM:] = qfeat.astype(BF16)


def _prompt_prep(proj, q_gain, k_gain):
    m = proj.shape[0]
    nb = m // MOBA_BLOCK
    assert m % MOBA_BLOCK == 0 and nb <= FEAT_KBLK
    blk = MOBA_BLOCK
    qkv_w = ATT_WIDTH + 2 * KV_WIDTH
    return pl.pallas_call(
        _prompt_prep_kernel,
        grid=(nb,),
        in_specs=[pl.BlockSpec((blk, qkv_w), lambda i: (i, 0)),
                  pl.BlockSpec((1, HEAD_DIM), lambda i: (0, 0)),
                  pl.BlockSpec((1, HEAD_DIM), lambda i: (0, 0))],
        out_specs=[pl.BlockSpec((blk, KV_WIDTH), lambda i: (i, 0)),
                   pl.BlockSpec((blk, KV_WIDTH), lambda i: (i, 0)),
                   pl.BlockSpec((N_KV_HEADS, blk, 2 * HEAD_DIM), lambda i: (0, i, 0)),
                   pl.BlockSpec((N_KV_HEADS, blk, 2 * HEAD_DIM), lambda i: (0, i, 0)),
                   pl.BlockSpec((ATT_HEADS, blk, 2 * HEAD_DIM), lambda i: (0, i, 0))],
        out_shape=[jax.ShapeDtypeStruct((m, KV_WIDTH), F32),
                   jax.ShapeDtypeStruct((m, KV_WIDTH), F32),
                   jax.ShapeDtypeStruct((N_KV_HEADS, m, 2 * HEAD_DIM), BF16),
                   jax.ShapeDtypeStruct((N_KV_HEADS, m, 2 * HEAD_DIM), BF16),
                   jax.ShapeDtypeStruct((ATT_HEADS, m, 2 * HEAD_DIM), BF16)],
        scratch_shapes=[pltpu.VMEM((N_KV_HEADS, LANES, LANES), F32)],
        compiler_params=_params("arbitrary"),
        name="prompt_qk_prep",
    )(proj, q_gain.reshape(1, HEAD_DIM), k_gain.reshape(1, HEAD_DIM))


ATTN_UNROLL = 4
LOG2E = 1.4426950408889634


def _prompt_attn_kernel(q_ref, k_ref, v_ref, o_ref, s_ref, acc_ref, m_ref):
    i = pl.program_id(1)
    blk = MOBA_BLOCK
    rows = KV_GROUP * blk
    span = ATTN_UNROLL * blk
    qs = q_ref[...].reshape(rows, 2 * HEAD_DIM)

    def lane_fold(s):
        out = s[:, :LANES]
        for t in range(1, s.shape[1] // LANES):
            out = jnp.maximum(out, s[:, t * LANES:(t + 1) * LANES])
        return out

    def probs(s):
        mb = m_ref[...]
        return jnp.concatenate([jnp.exp2(s[:, t * LANES:(t + 1) * LANES] - mb) for t in range(s.shape[1] // LANES)],
                               axis=1).astype(BF16)

    own = pl.ds(pl.multiple_of(i * blk, blk), blk)
    r = lax.broadcasted_iota(jnp.int32, (rows, blk), 0)
    c = lax.broadcasted_iota(jnp.int32, (rows, blk), 1)
    dist = jnp.bitwise_and(r, blk - 1) - c
    slope = qs[:, HEAD_DIM + FEAT_KOFF:HEAD_DIM + FEAT_KOFF + 1].astype(F32)
    s_own = lax.dot_general(qs[:, :HEAD_DIM], k_ref[own, :HEAD_DIM], NT_DIMS, preferred_element_type=F32)
    s_own = jnp.where(dist >= 0, (s_own - slope * dist.astype(F32)) * LOG2E, -jnp.inf)
    m_ref[...] = lane_fold(s_own)

    trips = (i + ATTN_UNROLL - 1) // ATTN_UNROLL

    def two_trips_per_iteration(trip):
        def pair(u, carry):
            trip(2 * u)
            trip(2 * u + 1)
            return carry

        lax.fori_loop(0, trips // 2, pair, 0)

        @pl.when(trips % 2 == 1)
        def _():
            trip(trips - 1)

    def pass1(t):
        ks = k_ref[pl.ds(pl.multiple_of(t * span, span), span), :]
        s = lax.dot_general(qs, ks, NT_DIMS, preferred_element_type=F32) * LOG2E
        s_ref[t] = s
        m_ref[...] = jnp.maximum(m_ref[...], lane_fold(s))

    two_trips_per_iteration(pass1)
    m_ref[...] = jnp.broadcast_to(jnp.max(m_ref[...], axis=1, keepdims=True), (rows, LANES))

    acc_ref[...] = jnp.dot(probs(s_own), v_ref[own, :], preferred_element_type=F32)

    def pass2(t):
        vs = v_ref[pl.ds(pl.multiple_of(t * span, span), span), :]
        acc_ref[...] += jnp.dot(probs(s_ref[t]), vs, preferred_element_type=F32)

    two_trips_per_iteration(pass2)
    acc = acc_ref[...]
    o = acc[:, :HEAD_DIM] / acc[:, HEAD_DIM:]
    for g in range(KV_GROUP):
        o_ref[:, g * HEAD_DIM:(g + 1) * HEAD_DIM] = o[g * blk:(g + 1) * blk]


def _prompt_attn(qp, kp, vp):
    m = kp.shape[1]
    nb = m // MOBA_BLOCK
    assert nb % ATTN_UNROLL == 0
    blk = MOBA_BLOCK
    rows = KV_GROUP * blk
    return pl.pallas_call(
        _prompt_attn_kernel,
        grid=(N_KV_HEADS, nb),
        in_specs=[pl.BlockSpec((KV_GROUP, blk, 2 * HEAD_DIM), lambda kv, i: (kv, i, 0)),
                  pl.BlockSpec((None, m, 2 * HEAD_DIM), lambda kv, i: (kv, 0, 0)),
                  pl.BlockSpec((None, m, 2 * HEAD_DIM), lambda kv, i: (kv, 0, 0))],
        out_specs=pl.BlockSpec((blk, KV_GROUP * HEAD_DIM), lambda kv, i: (i, kv)),
        out_shape=jax.ShapeDtypeStruct((m, ATT_WIDTH), F32),
        scratch_shapes=[pltpu.VMEM((nb // ATTN_UNROLL, rows, ATTN_UNROLL * blk), F32),
                        pltpu.VMEM((rows, 2 * HEAD_DIM), F32),
                        pltpu.VMEM((rows, LANES), F32)],
        compiler_params=_params("arbitrary", "arbitrary"),
        name="prompt_moba_attn",
    )(qp, kp, vp)


def _expand_heads(v, lane_lo):
    r = v.shape[0]
    parts = []
    for k in range(SSM_HEADS // 2):
        a0 = jnp.broadcast_to(v[:, 2 * k:2 * k + 1], (r, LANES))
        a1 = jnp.broadcast_to(v[:, 2 * k + 1:2 * k + 2], (r, LANES))
        parts.append(jnp.where(lane_lo, a0, a1))
    return jnp.concatenate(parts, axis=1)


def _gated_group_norm(y, z, g):
    yz = y * _silu(z)
    gw = D_INNER // SSM_GROUPS
    outs = []
    for grp in range(SSM_GROUPS):
        t = yz[:, grp * gw:(grp + 1) * gw]
        t = t * lax.rsqrt(jnp.mean(t * t, axis=-1, keepdims=True) + EPS)
        outs.append(t * g[:, grp * gw:(grp + 1) * gw])
    return jnp.concatenate(outs, axis=1)


def _ssd_kernel(z_ref, xbc_ref, dt_ref, cw_ref, cb_ref, dtb_ref, alog_ref, dsk_ref, g_ref,
                y_ref, tail_out_ref, st_out_ref, tail_ref, st_ref):
    c = pl.program_id(0)
    cs = SSD_CHUNK
    gw = D_INNER // SSM_GROUPS

    @pl.when(c == 0)
    def _():
        tail_ref[...] = jnp.zeros_like(tail_ref)
        st_ref[...] = jnp.zeros_like(st_ref)

    xr = xbc_ref[...]
    xp = jnp.concatenate([tail_ref[...], xr], axis=0)
    cw = cw_ref[...]
    conv = cb_ref[...] + cw[3:4] * xr
    for t in range(CONV_WIDTH - 1):
        conv = conv + cw[t:t + 1] * xp[8 - (CONV_WIDTH - 1) + t:8 - (CONV_WIDTH - 1) + t + cs]
    tail_ref[...] = xr[cs - 8:]
    tail_out_ref[...] = xr[cs - 8:]
    xc = _silu(conv)
    xs = xc[:, :D_INNER]
    bm = xc[:, D_INNER:D_INNER + SSM_GROUPS * D_STATE]
    cm = xc[:, D_INNER + SSM_GROUPS * D_STATE:]

    lane = lax.broadcasted_iota(jnp.int32, (cs, LANES), 1)
    rowi = lax.broadcasted_iota(jnp.int32, (cs, LANES), 0)
    lane_lo = lane < SSM_HEAD_DIM
    tri = rowi >= lane

    dt = _softplus(dt_ref[...] + dtb_ref[...])
    a = jnp.where(lane[:1] < SSM_HEADS, -jnp.exp(alog_ref[...]), 0.0)
    da = dt * a
    tril = jnp.where(tri, 1.0, 0.0).astype(BF16)
    p1 = da.astype(BF16)
    r1 = da - p1.astype(F32)
    p2 = r1.astype(BF16)
    p3 = (r1 - p2.astype(F32)).astype(BF16)
    acum = (jnp.dot(tril, p1, preferred_element_type=F32) + jnp.dot(tril, p2, preferred_element_type=F32)
            + jnp.dot(tril, p3, preferred_element_type=F32))
    acum_t = acum.T

    dt_e = _expand_heads(dt, lane_lo)
    ac_e = _expand_heads(acum, lane_lo)
    xdt = xs * dt_e
    ea_e = jnp.exp(ac_e)
    dend_e = jnp.exp(ac_e[cs - 1:cs, :] - ac_e)
    cdec = ea_e[cs - 1:cs, :]
    xdt_bf = xdt.astype(BF16)
    xdec_bf = (xdt * dend_e).astype(BF16)

    y_parts = []
    for grp in range(SSM_GROUPS):
        bg = bm[:, grp * D_STATE:(grp + 1) * D_STATE]
        cg = cm[:, grp * D_STATE:(grp + 1) * D_STATE].astype(BF16)
        cb = lax.dot_general(cg, bg.astype(BF16), NT_DIMS, preferred_element_type=F32)
        hpg = SSM_HEADS // SSM_GROUPS
        intra = []
        for k in range(hpg // 2):
            pair = grp * (hpg // 2) + k
            xpair = xdt_bf[:, pair * LANES:(pair + 1) * LANES]
            acc = None
            for hh in range(2):
                h = 2 * pair + hh
                seg = jnp.broadcast_to(acum[:, h:h + 1], (cs, cs)) - acum_t[h:h + 1, :]
                lmat = jnp.exp(jnp.where(tri, seg, -jnp.inf))
                mh = (cb * lmat).astype(BF16)
                xh = jnp.where(lane_lo if hh == 0 else jnp.logical_not(lane_lo), xpair, jnp.zeros_like(xpair))
                part = jnp.dot(mh, xh, preferred_element_type=F32)
                acc = part if acc is None else acc + part
            intra.append(acc)
        y_intra = jnp.concatenate(intra, axis=1)
        st = st_ref[grp]
        y_inter = jnp.dot(cg, st.astype(BF16), preferred_element_type=F32) * ea_e[:, grp * gw:(grp + 1) * gw]
        new_st = cdec[:, grp * gw:(grp + 1) * gw] * st + jnp.dot(
            bg.T.astype(BF16), xdec_bf[:, grp * gw:(grp + 1) * gw], preferred_element_type=F32)
        st_ref[grp] = new_st
        st_out_ref[grp] = new_st
        y_parts.append(y_intra + y_inter)
    y = jnp.concatenate(y_parts, axis=1) + dsk_ref[...] * xs
    y_ref[...] = _gated_group_norm(y, z_ref[...], g_ref[...]).astype(y_ref.dtype)


def _pad_lanes(v):
    return jnp.pad(v.reshape(1, -1), ((0, 0), (0, LANES - v.size)))


def _ssd_prompt(proj, dtp, conv_w, conv_b, dt_bias, a_log, d_skip, g_ssm):
    m = proj.shape[0]
    cs = SSD_CHUNK
    assert m % cs == 0
    gw = D_INNER // SSM_GROUPS
    z_blk = (ATT_WIDTH + 2 * KV_WIDTH) // D_INNER
    x_blk = (ATT_WIDTH + 2 * KV_WIDTH + D_INNER) // CONV_DIM
    assert z_blk * D_INNER == ATT_WIDTH + 2 * KV_WIDTH and x_blk * CONV_DIM == ATT_WIDTH + 2 * KV_WIDTH + D_INNER
    const = lambda shape: pl.BlockSpec(shape, lambda c: tuple(0 for _ in shape))
    return pl.pallas_call(
        _ssd_kernel,
        grid=(m // cs,),
        in_specs=[pl.BlockSpec((cs, D_INNER), lambda c: (c, z_blk)),
                  pl.BlockSpec((cs, CONV_DIM), lambda c: (c, x_blk)),
                  pl.BlockSpec((cs, LANES), lambda c: (c, 0)),
                  const((CONV_WIDTH, CONV_DIM)), const((1, CONV_DIM)), const((1, LANES)), const((1, LANES)),
                  const((1, D_INNER)), const((1, D_INNER))],
        out_specs=[pl.BlockSpec((cs, D_INNER), lambda c: (c, 0)),
                   const((8, CONV_DIM)), const((SSM_GROUPS, D_STATE, gw))],
        out_shape=[jax.ShapeDtypeStruct((m, D_INNER), BF16),
                   jax.ShapeDtypeStruct((8, CONV_DIM), F32),
                   jax.ShapeDtypeStruct((SSM_GROUPS, D_STATE, gw), F32)],
        scratch_shapes=[pltpu.VMEM((8, CONV_DIM), F32), pltpu.VMEM((SSM_GROUPS, D_STATE, gw), F32)],
        compiler_params=_params("arbitrary"),
        name="prompt_ssd",
    )(proj, proj, dtp, conv_w, conv_b.reshape(1, CONV_DIM), _pad_lanes(dt_bias), _pad_lanes(a_log),
      jnp.repeat(d_skip, SSM_HEAD_DIM).reshape(1, D_INNER), g_ssm.reshape(1, D_INNER))


def _sample_prep_kernel(proj_ref, qg_ref, kg_ref, q_ref, k_ref, v_ref):
    for h in range(ATT_HEADS):
        q_ref[:, h * HEAD_DIM:(h + 1) * HEAD_DIM] = _head_norm(proj_ref[:, h * HEAD_DIM:(h + 1) * HEAD_DIM], qg_ref[...])
    for kv in range(N_KV_HEADS):
        lo = ATT_WIDTH + kv * HEAD_DIM
        k_ref[:, kv * HEAD_DIM:(kv + 1) * HEAD_DIM] = _head_norm(proj_ref[:, lo:lo + HEAD_DIM], kg_ref[...])
    v_ref[...] = proj_ref[:, ATT_WIDTH + KV_WIDTH:ATT_WIDTH + 2 * KV_WIDTH]


def _sample_prep(proj, q_gain, k_gain):
    n = proj.shape[0]
    qkv_w = ATT_WIDTH + 2 * KV_WIDTH
    return pl.pallas_call(
        _sample_prep_kernel,
        grid=(1,),
        in_specs=[pl.BlockSpec((n, qkv_w), lambda i: (0, 0)),
                  pl.BlockSpec((1, HEAD_DIM), lambda i: (0, 0)),
                  pl.BlockSpec((1, HEAD_DIM), lambda i: (0, 0))],
        out_specs=[pl.BlockSpec((n, ATT_WIDTH), lambda i: (0, 0)),
                   pl.BlockSpec((n, KV_WIDTH), lambda i: (0, 0)),
                   pl.BlockSpec((n, KV_WIDTH), lambda i: (0, 0))],
        out_shape=[jax.ShapeDtypeStruct((n, ATT_WIDTH), F32),
                   jax.ShapeDtypeStruct((n, KV_WIDTH), F32),
                   jax.ShapeDtypeStruct((n, KV_WIDTH), F32)],
        compiler_params=_params("arbitrary"),
        name="sample_qk_prep",
    )(proj, q_gain.reshape(1, HEAD_DIM), k_gain.reshape(1, HEAD_DIM))


def _sample_gate_kernel(pps, pt_ref, q_ref, *rest):
    page_refs = rest[:pps]
    idx_ref = rest[pps]
    ksum_ref = rest[pps + 1]
    j = pl.program_id(1)
    page_rows = page_refs[0].shape[0]
    ppb = MOBA_BLOCK * N_KV_HEADS // page_rows
    bps = pps // ppb
    fold = 8 // N_KV_HEADS
    def page_sum(ref):
        x = ref[...].reshape(page_rows // 8, 8, HEAD_DIM)
        q = x.shape[0] // 4
        parts = [jnp.sum(x[t * q:(t + 1) * q], axis=0) for t in range(4)]
        return (parts[0] + parts[1]) + (parts[2] + parts[3])

    for b in range(bps):
        s8 = page_sum(page_refs[b * ppb])
        for t in range(1, ppb):
            s8 = s8 + page_sum(page_refs[b * ppb + t])
        s = s8[0:N_KV_HEADS]
        for t in range(1, fold):
            s = s + s8[t * N_KV_HEADS:(t + 1) * N_KV_HEADS]
        s = s * (1.0 / MOBA_BLOCK)
        for kv in range(N_KV_HEADS):
            ksum_ref[kv, j, b:b + 1, :] = s[kv:kv + 1, :]

    @pl.when(j == pl.num_programs(1) - 1)
    def _():
        nblk = ksum_ref.shape[1] * bps
        q = q_ref[0]
        hrow = lax.broadcasted_iota(jnp.int32, (ATT_HEADS, nblk), 0)
        gate = jnp.zeros((ATT_HEADS, nblk), F32)
        for kv in range(N_KV_HEADS):
            gk = _dot3(q, ksum_ref[kv].reshape(nblk, HEAD_DIM), NT_DIMS)
            gate = jnp.where(hrow // KV_GROUP == kv, gk, gate)
        lane = lax.broadcasted_iota(jnp.int32, (ATT_HEADS, LANES), 1)
        if nblk < LANES:
            gate = jnp.concatenate([gate, jnp.full((ATT_HEADS, LANES - nblk), -jnp.inf, F32)], axis=1)
        _, firsts = _top3(gate, lane.astype(F32), 1)
        out = jnp.zeros((ATT_HEADS, LANES), jnp.int32)
        for t, first in enumerate(firsts):
            out = jnp.where(lane == t, first.astype(jnp.int32), out)
        idx_ref[0] = out


def _sample_gate(q3, cache_k, layer, page_table):
    n, n_pages = page_table.shape
    depth, pool, page = cache_k.shape[:3]
    ppb = MOBA_BLOCK // page
    nblk = n_pages // ppb
    assert n_pages % ppb == 0 and MOBA_TOPK <= nblk <= LANES and 8 % N_KV_HEADS == 0
    pps = min(32, n_pages)
    assert n_pages % pps == 0 and pps % ppb == 0
    steps = n_pages // pps
    cache_k = cache_k.reshape(depth, pool, page * N_KV_HEADS, HEAD_DIM)

    def page_spec(t):
        return pl.BlockSpec((None, None, page * N_KV_HEADS, HEAD_DIM),
                            lambda s, j, pt: (layer, pt[s, j * pps + t], 0, 0))

    grid_spec = pltpu.PrefetchScalarGridSpec(
        num_scalar_prefetch=1,
        grid=(n, steps),
        in_specs=[pl.BlockSpec((1, ATT_HEADS, HEAD_DIM), lambda s, j, pt: (s, 0, 0))]
        + [page_spec(t) for t in range(pps)],
        out_specs=pl.BlockSpec((1, ATT_HEADS, LANES), lambda s, j, pt: (s, 0, 0)),
        scratch_shapes=[pltpu.VMEM((N_KV_HEADS, steps, pps // ppb, HEAD_DIM), F32)],
    )
    return pl.pallas_call(
        functools.partial(_sample_gate_kernel, pps),
        grid_spec=grid_spec,
        out_shape=jax.ShapeDtypeStruct((n, ATT_HEADS, LANES), jnp.int32),
        compiler_params=_params("arbitrary", "arbitrary"),
        name="sample_gate_topk",
    )(page_table, q3, *([cache_k] * pps))


def _sample_attn_kernel(past, ppb, layer, pt_ref, idx_ref, q_ref, kn_ref, vn_ref, ck_ref, cv_ref, o_ref,
                        kbuf, vbuf, sems):
    s = pl.program_id(0)
    page = MOBA_BLOCK // ppb
    slot = s % 2

    def copies(seq, buf, h, t, p):
        kv = h // KV_GROUP
        blk = idx_ref[seq, h * MOBA_TOPK + t]
        phys = pt_ref[seq, blk * ppb + p]
        dst = pl.ds((t * ppb + p) * page, page)
        return (pltpu.make_async_copy(ck_ref.at[layer, phys, :, kv, :], kbuf.at[buf, h, dst, :],
                                      sems.at[buf, 0, h, t * ppb + p]),
                pltpu.make_async_copy(cv_ref.at[layer, phys, :, kv, :], vbuf.at[buf, h, dst, :],
                                      sems.at[buf, 1, h, t * ppb + p]))

    triples = [(h, t, p) for h in range(ATT_HEADS) for t in range(MOBA_TOPK) for p in range(ppb)]

    def start_all(seq, buf):
        for h, t, p in triples:
            ck, cv = copies(seq, buf, h, t, p)
            ck.start()
            cv.start()

    @pl.when(s == 0)
    def _():
        start_all(0, 0)

    @pl.when(s + 1 < pl.num_programs(0))
    def _():
        start_all(s + 1, 1 - slot)

    for h, t, p in triples:
        ck, cv = copies(s, slot, h, t, p)
        ck.wait()
        cv.wait()

    nsel = MOBA_TOPK * MOBA_BLOCK
    rowi = lax.broadcasted_iota(jnp.int32, (nsel, 1), 0)
    off = jnp.bitwise_and(rowi, MOBA_BLOCK - 1)
    for h in range(ATT_HEADS):
        kv = h // KV_GROUP
        slope = _alibi_slope(h)
        q = q_ref[0, h:h + 1, :]
        sc = jnp.sum(kbuf[slot, h] * q, axis=1, keepdims=True) * ATT_SCALE
        pos = jnp.zeros((nsel, 1), jnp.int32)
        for t in range(MOBA_TOPK):
            pos = jnp.where(rowi // MOBA_BLOCK == t, idx_ref[s, h * MOBA_TOPK + t] * MOBA_BLOCK, pos)
        dist = (past - (pos + off)).astype(F32)
        sc = sc - slope * dist
        s_own = jnp.sum(kn_ref[0, kv:kv + 1, :] * q, axis=1, keepdims=True) * ATT_SCALE
        mx = jnp.maximum(jnp.max(sc, axis=0, keepdims=True), s_own)
        p = jnp.exp(sc - mx)
        p_own = jnp.exp(s_own - mx)
        denom = jnp.sum(p, axis=0, keepdims=True) + p_own
        num = jnp.sum(p * vbuf[slot, h], axis=0, keepdims=True) + p_own * vn_ref[0, kv:kv + 1, :]
        o_ref[0, h:h + 1, :] = num / denom


def _sample_attn(q3, k_new3, v_new3, cache_k, cache_v, layer, page_table, idx):
    n, n_pages = page_table.shape
    page = cache_k.shape[2]
    ppb = MOBA_BLOCK // page
    past = n_pages * page
    assert past % MOBA_BLOCK == 0
    nsel = MOBA_TOPK * MOBA_BLOCK
    grid_spec = pltpu.PrefetchScalarGridSpec(
        num_scalar_prefetch=2,
        grid=(n,),
        in_specs=[pl.BlockSpec((1, ATT_HEADS, HEAD_DIM), lambda s, pt, ix: (s, 0, 0)),
                  pl.BlockSpec((1, N_KV_HEADS, HEAD_DIM), lambda s, pt, ix: (s, 0, 0)),
                  pl.BlockSpec((1, N_KV_HEADS, HEAD_DIM), lambda s, pt, ix: (s, 0, 0)),
                  pl.BlockSpec(memory_space=pl.ANY),
                  pl.BlockSpec(memory_space=pl.ANY)],
        out_specs=pl.BlockSpec((1, ATT_HEADS, HEAD_DIM), lambda s, pt, ix: (s, 0, 0)),
        scratch_shapes=[pltpu.VMEM((2, ATT_HEADS, nsel, HEAD_DIM), F32),
                        pltpu.VMEM((2, ATT_HEADS, nsel, HEAD_DIM), F32),
                        pltpu.SemaphoreType.DMA((2, 2, ATT_HEADS, MOBA_TOPK * ppb))],
    )
    return pl.pallas_call(
        functools.partial(_sample_attn_kernel, past, ppb, layer),
        grid_spec=grid_spec,
        out_shape=jax.ShapeDtypeStruct((n, ATT_HEADS, HEAD_DIM), F32),
        compiler_params=_params("arbitrary"),
        name="sample_moba_attn",
    )(page_table, idx, q3, k_new3, v_new3, cache_k, cache_v)


def _sample_ssd_kernel(z_ref, xbc_ref, dt_ref, buf_ref, h0_ref, cw_ref, cb_ref, dtb_ref, alog_ref, dsk_ref, g_ref,
                       y_ref, buf_out_ref, h_out_ref):
    x = xbc_ref[0]
    buf = buf_ref[0]
    cw = cw_ref[...]
    conv = cb_ref[...] + cw[CONV_WIDTH - 1:CONV_WIDTH] * x
    for t in range(CONV_WIDTH - 1):
        conv = conv + cw[t:t + 1] * buf[t:t + 1]
    buf_out_ref[0, 0:CONV_WIDTH - 2, :] = buf[1:CONV_WIDTH - 1]
    buf_out_ref[0, CONV_WIDTH - 2:CONV_WIDTH - 1, :] = x
    xc = _silu(conv)
    xs = xc[:, :D_INNER]
    bm = xc[:, D_INNER:D_INNER + SSM_GROUPS * D_STATE]
    cm = xc[:, D_INNER + SSM_GROUPS * D_STATE:]

    lane1 = lax.broadcasted_iota(jnp.int32, (1, LANES), 1)
    rowi = lax.broadcasted_iota(jnp.int32, (LANES, LANES), 0)
    dt = _softplus(dt_ref[0] + dtb_ref[...])
    a = jnp.where(lane1 < SSM_HEADS, -jnp.exp(alog_ref[...]), 0.0)
    dec = jnp.exp(dt * a)
    dt_e = _expand_heads(dt, lane1 < SSM_HEAD_DIM)
    xdt = xs * dt_e
    xdt_rows = jnp.broadcast_to(xdt, (LANES, D_INNER))

    hpg = SSM_HEADS // SSM_GROUPS
    y_parts = []
    for pair in range(SSM_HEADS // 2):
        grp = (2 * pair) // hpg
        xcol = xdt_rows[:, pair * LANES:(pair + 1) * LANES].T
        dcol = jnp.where(rowi < SSM_HEAD_DIM,
                         jnp.broadcast_to(dec[:, 2 * pair:2 * pair + 1], (LANES, LANES)),
                         jnp.broadcast_to(dec[:, 2 * pair + 1:2 * pair + 2], (LANES, LANES)))
        h0 = h0_ref[0, 2 * pair:2 * pair + 2].reshape(LANES, D_STATE)
        hn = dcol * h0 + xcol * bm[:, grp * D_STATE:(grp + 1) * D_STATE]
        h_out_ref[0, 2 * pair:2 * pair + 2] = hn.reshape(2, SSM_HEAD_DIM, D_STATE)
        cgrow = jnp.broadcast_to(cm[:, grp * D_STATE:(grp + 1) * D_STATE], (8, D_STATE))
        ypair = _dot3(cgrow, hn, NT_DIMS)
        y_parts.append(ypair[0:1])
    y = jnp.concatenate(y_parts, axis=1) + dsk_ref[...] * xs
    y_ref[0] = _gated_group_norm(y, z_ref[0], g_ref[...]).astype(y_ref.dtype)


def _ssd_sample(proj, dtp, state_conv, state_ssm, conv_w, conv_b, dt_bias, a_log, d_skip, g_ssm):
    n = proj.shape[0]
    z0 = ATT_WIDTH + 2 * KV_WIDTH
    z3 = proj[:, z0:z0 + D_INNER].reshape(n, 1, D_INNER)
    x3 = proj[:, z0 + D_INNER:z0 + D_INNER + CONV_DIM].reshape(n, 1, CONV_DIM)
    dt3 = dtp.reshape(n, 1, LANES)
    const = lambda shape: pl.BlockSpec(shape, lambda s: tuple(0 for _ in shape))
    per_seq = lambda shape: pl.BlockSpec((1,) + shape, lambda s: (s,) + tuple(0 for _ in shape))
    y, buf, h = pl.pallas_call(
        _sample_ssd_kernel,
        grid=(n,),
        in_specs=[per_seq((1, D_INNER)), per_seq((1, CONV_DIM)), per_seq((1, LANES)),
                  per_seq((CONV_WIDTH - 1, CONV_DIM)), per_seq((SSM_HEADS, SSM_HEAD_DIM, D_STATE)),
                  const((CONV_WIDTH, CONV_DIM)), const((1, CONV_DIM)), const((1, LANES)), const((1, LANES)),
                  const((1, D_INNER)), const((1, D_INNER))],
        out_specs=[per_seq((1, D_INNER)), per_seq((CONV_WIDTH - 1, CONV_DIM)),
                   per_seq((SSM_HEADS, SSM_HEAD_DIM, D_STATE))],
        out_shape=[jax.ShapeDtypeStruct((n, 1, D_INNER), BF16),
                   jax.ShapeDtypeStruct((n, CONV_WIDTH - 1, CONV_DIM), F32),
                   jax.ShapeDtypeStruct((n, SSM_HEADS, SSM_HEAD_DIM, D_STATE), F32)],
        compiler_params=_params("arbitrary"),
        name="sample_ssd",
    )(z3, x3, dt3, state_conv, state_ssm, conv_w, conv_b.reshape(1, CONV_DIM), _pad_lanes(dt_bias),
      _pad_lanes(a_log), jnp.repeat(d_skip, SSM_HEAD_DIM).reshape(1, D_INNER), g_ssm.reshape(1, D_INNER))
    return y.reshape(n, D_INNER), buf, h


def kernel(x_prompt, x_sample, cache_k, cache_v, state_conv, state_ssm, page_table, c_prompt, c_sample, w_ada, b_ada, g_mix_norm, w_in, q_gain, k_gain, g_att_out, conv_w, conv_b, dt_bias, a_log, d_skip, g_ssm_out, w_out, g_ffn_norm, w_gate, w_up, w_down):
    n_p, seq, d = x_prompt.shape
    n_s, dec_seq, _ = x_sample.shape
    assert n_p == 1 and dec_seq == 1
    depth = w_ada.shape[0]
    main_w = ATT_WIDTH + 2 * KV_WIDTH + D_INNER + CONV_DIM

    yp = x_prompt.reshape(seq, d)
    ys = x_sample.reshape(n_s, d)
    c_rows = n_p + n_s
    c_pad = -(-c_rows // 16) * 16
    c_all = jnp.pad(jnp.concatenate([c_prompt, c_sample], axis=0), ((0, c_pad - c_rows), (0, 0)))
    outs = [[] for _ in range(8)]
    for l in range(depth):
        mod = _ada(c_all, w_ada[l], b_ada[l])
        mod_p, mod_s = mod[0:1], mod[1:1 + n_s]
        w_in_t = jnp.swapaxes(w_in[l], 0, 1)

        proj, dtp = _inproj(yp, g_mix_norm[l], mod_p, w_in_t, main_w, tn=1536, tm=512, name="in_proj")
        k_out, v_out, kp, vp, qp = _prompt_prep(proj, q_gain[l], k_gain[l])
        o_att = _prompt_attn(qp, kp, vp)
        y_ssm, tail, st = _ssd_prompt(proj, dtp, conv_w[l], conv_b[l], dt_bias[l], a_log[l], d_skip[l], g_ssm_out[l])
        x1_p, h2_p = _outproj(o_att, g_att_out[l], y_ssm, w_out[l], yp, mod_p, g_ffn_norm[l])
        hpg = SSM_HEADS // SSM_GROUPS
        ssm_p = st.reshape(SSM_GROUPS, D_STATE, hpg, SSM_HEAD_DIM).transpose(0, 2, 3, 1).reshape(
            1, SSM_HEADS, SSM_HEAD_DIM, D_STATE)
        outs[0].append(k_out.reshape(1, seq, N_KV_HEADS, HEAD_DIM))
        outs[1].append(v_out.reshape(1, seq, N_KV_HEADS, HEAD_DIM))
        outs[2].append(tail[8 - (CONV_WIDTH - 1):].reshape(1, CONV_WIDTH - 1, CONV_DIM))
        outs[3].append(ssm_p)

        proj_s, dts = _inproj(ys, g_mix_norm[l], mod_s, w_in_t, main_w, tn=768, tm=512, name="in_proj_sample",
                              split=True)
        q_s, k_s, v_s = _sample_prep(proj_s, q_gain[l], k_gain[l])
        q3 = q_s.reshape(n_s, ATT_HEADS, HEAD_DIM)
        idx = _sample_gate(q3, cache_k, l, page_table)
        idx_flat = idx[:, :, :MOBA_TOPK].reshape(n_s, ATT_HEADS * MOBA_TOPK)
        o_s = _sample_attn(q3, k_s.reshape(n_s, N_KV_HEADS, HEAD_DIM), v_s.reshape(n_s, N_KV_HEADS, HEAD_DIM),
                           cache_k, cache_v, l, page_table, idx_flat)
        y_s, buf_s, h_s = _ssd_sample(proj_s, dts, state_conv[l], state_ssm[l], conv_w[l], conv_b[l], dt_bias[l],
                                      a_log[l], d_skip[l], g_ssm_out[l])
        x1_s, h2_s = _outproj(o_s.reshape(n_s, ATT_WIDTH), g_att_out[l], y_s, w_out[l], ys, mod_s, g_ffn_norm[l])

        hid_p, hid_s = _gateup(h2_p, h2_s, w_gate[l], w_up[l])
        yp, ys = _down(hid_p, hid_s, w_down[l], x1_p, x1_s, mod_p, mod_s)
        outs[4].append(k_s.reshape(n_s, 1, N_KV_HEADS, HEAD_DIM))
        outs[5].append(v_s.reshape(n_s, 1, N_KV_HEADS, HEAD_DIM))
        outs[6].append(buf_s)
        outs[7].append(h_s)
    stacked = [jnp.stack(o) for o in outs]
    return (yp.reshape(1, seq, d), ys.reshape(n_s, 1, d), *stacked)
```

```python
import functools

import jax
import jax.numpy as jnp
from jax import lax
from jax.experimental import pallas as pl
from jax.experimental.pallas import tpu as pltpu

F32 = jnp.float32
BF16 = jnp.bfloat16

HEAD_DIM = 128
ATT_HEADS = 8
N_KV_HEADS = 4
KV_GROUP = ATT_HEADS // N_KV_HEADS
ATT_WIDTH = ATT_HEADS * HEAD_DIM
KV_WIDTH = N_KV_HEADS * HEAD_DIM
MOBA_BLOCK = 256
MOBA_TOPK = 3
D_INNER = 1024
SSM_HEAD_DIM = 64
SSM_HEADS = D_INNER // SSM_HEAD_DIM
SSM_GROUPS = 2
D_STATE = 128
CONV_WIDTH = 4
CONV_DIM = D_INNER + 2 * SSM_GROUPS * D_STATE
SSD_CHUNK = 128
EPS = 1e-6
ATT_SCALE = HEAD_DIM ** -0.5

LANES = 128
FEAT_KBLK = 96
FEAT_KOFF = 97
FEAT_ONE_A = 98
FEAT_ONE_B = 99
NEG_BIG = -1e30

NT_DIMS = (((1,), (1,)), ((), ()))
VMEM_LIMIT = 56 * 1024 * 1024


def _params(*sem):
    return pltpu.CompilerParams(dimension_semantics=sem, vmem_limit_bytes=VMEM_LIMIT)


def _silu(x):
    return x / (1.0 + jnp.exp(-x))


def _softplus(x):
    return jnp.maximum(x, 0.0) + jnp.log1p(jnp.exp(-jnp.abs(x)))


def _split_bf16(x):
    hi = x.astype(BF16)
    lo = (x - hi.astype(F32)).astype(BF16)
    return hi, lo


def _dot3(a, b, dims):
    ah, al = _split_bf16(a)
    bh, bl = _split_bf16(b)
    d = lambda x, y: lax.dot_general(x, y, dims, preferred_element_type=F32)
    return d(ah, bh) + d(al, bh) + d(ah, bl)


def _alibi_slope(h):
    return 2.0 ** (-8.0 * (h + 1) / ATT_HEADS)


NN_DIMS = (((1,), (0,)), ((), ()))


def _ada_kernel(c_ref, w_ref, b_ref, o_ref):
    rows = c_ref.shape[0]
    ahi, alo = _split_bf16(_silu(c_ref[...]))
    whi, wlo = _split_bf16(w_ref[...])
    both = jnp.dot(jnp.concatenate([ahi, alo], axis=0), whi, preferred_element_type=F32)
    o_ref[...] = both[:rows] + both[rows:] + jnp.dot(ahi, wlo, preferred_element_type=F32) + b_ref[...]


def _ada(c_all, w, b):
    rows, d = c_all.shape
    n = w.shape[1]
    tn = 512
    return pl.pallas_call(
        _ada_kernel,
        grid=(n // tn,),
        in_specs=[pl.BlockSpec((rows, d), lambda j: (0, 0)),
                  pl.BlockSpec((d, tn), lambda j: (0, j)),
                  pl.BlockSpec((1, tn), lambda j: (0, j))],
        out_specs=pl.BlockSpec((rows, tn), lambda j: (0, j)),
        out_shape=jax.ShapeDtypeStruct((rows, n), F32),
        compiler_params=_params("arbitrary"),
        name="ada_mod",
    )(c_all, w, b.reshape(1, n))


def _mod_spec(mod_rows, tm, width, col_of):
    if mod_rows == 1:
        return pl.BlockSpec((1, width), lambda n, i: (0, col_of(n)))
    return pl.BlockSpec((tm, width), lambda n, i: (i, col_of(n)))


def _rms(x):
    return x * lax.rsqrt(jnp.mean(x * x, axis=-1, keepdims=True) + EPS)


def _cast_weight_once(w_ref, wbf_ref):
    @pl.when(pl.program_id(1) == 0)
    def _():
        wbf_ref[...] = w_ref[...].astype(BF16)


STAGE_CHUNK = 384


def _inproj_kernel(n_main, split, x_ref, g_ref, sc_ref, sh_ref, w_ref, wdt_ref, o_ref, odt_ref, *w_scratch):
    n = pl.program_id(0)
    first_row_tile = pl.program_id(1) == 0
    main_refs, dt_refs = w_scratch[:len(w_scratch) // 2], w_scratch[len(w_scratch) // 2:]

    def stage(src_ref, refs, pad_rows):
        cols = src_ref.shape[0]
        chunk = STAGE_CHUNK if cols % STAGE_CHUNK == 0 else cols
        for c0 in range(0, cols, chunk):
            w = src_ref[c0:c0 + chunk, :]
            if pad_rows:
                w = jnp.concatenate([w, jnp.zeros((pad_rows, w.shape[1]), F32)], axis=0)
            w = w.T
            for part, ref in zip(_split_bf16(w) if split else (w.astype(BF16),), refs):
                ref[:, c0:c0 + w.shape[1]] = part

    def product(hn, refs):
        mm = lambda a, b: jnp.dot(a, b, preferred_element_type=F32)
        if not split:
            return mm(hn.astype(BF16), refs[0][...])
        hhi, hlo = _split_bf16(hn)
        return mm(hhi, refs[0][...]) + mm(hlo, refs[0][...]) + mm(hhi, refs[1][...])

    @pl.when(first_row_tile)
    def _():
        stage(w_ref, main_refs, 0)

    @pl.when(jnp.logical_and(n == 0, first_row_tile))
    def _():
        stage(wdt_ref, dt_refs, LANES - wdt_ref.shape[0])

    hn = (_rms(x_ref[...]) * g_ref[...]) * (1.0 + sc_ref[...]) + sh_ref[...]
    o_ref[...] = product(hn, main_refs)

    @pl.when(n == 0)
    def _():
        odt_ref[...] = product(hn, dt_refs)


def _inproj(x, g, mod, w_t, main_w, *, tn, tm, name, split=False):
    m, d = x.shape
    tm = min(tm, m)
    mr = mod.shape[0]
    n_main = main_w // tn
    n_dt = w_t.shape[0] - main_w
    assert n_main * tn == main_w and main_w % n_dt == 0 and n_dt % 16 == 0
    last_i = m // tm - 1
    ncopies = 2 if split else 1
    proj, dtp = pl.pallas_call(
        functools.partial(_inproj_kernel, n_main, split),
        grid=(n_main, m // tm),
        in_specs=[pl.BlockSpec((tm, d), lambda n, i: (i, 0)),
                  pl.BlockSpec((1, d), lambda n, i: (0, 0)),
                  _mod_spec(mr, tm, d, lambda n: 1),
                  _mod_spec(mr, tm, d, lambda n: 0),
                  pl.BlockSpec((tn, d), lambda n, i: (n, 0)),
                  pl.BlockSpec((n_dt, d), lambda n, i: (main_w // n_dt, 0))],
        out_specs=[pl.BlockSpec((tm, tn), lambda n, i: (i, n)),
                   pl.BlockSpec((tm, LANES), lambda n, i: (jnp.where(n == 0, i, last_i), 0))],
        out_shape=[jax.ShapeDtypeStruct((m, main_w), F32), jax.ShapeDtypeStruct((m, LANES), F32)],
        scratch_shapes=[pltpu.VMEM((d, tn), BF16)] * ncopies + [pltpu.VMEM((d, LANES), BF16)] * ncopies,
        compiler_params=_params("arbitrary", "arbitrary"),
        name=name,
    )(x, g.reshape(1, d), mod, mod, w_t, w_t)
    return proj, dtp


def _outproj_kernel(a_ref, ga_ref, b_ref, w_ref, x_ref, gt_ref, gf_ref, sc_ref, sh_ref, x1_ref, h2_ref, wbf_ref):
    _cast_weight_once(w_ref, wbf_ref)
    ka = a_ref.shape[1]
    a = (_rms(a_ref[...]) * ga_ref[...]).astype(BF16)
    acc = jnp.dot(a, wbf_ref[:ka, :], preferred_element_type=F32)
    acc = acc + jnp.dot(b_ref[...], wbf_ref[ka:, :], preferred_element_type=F32)
    x1 = x_ref[...] + gt_ref[...] * acc
    x1_ref[...] = x1
    h2_ref[...] = ((_rms(x1) * gf_ref[...]) * (1.0 + sc_ref[...]) + sh_ref[...]).astype(BF16)


def _outproj(a, g_a, b, w, x, mod, g_ffn):
    m, ka = a.shape
    kb = b.shape[1]
    d = w.shape[1]
    tm = min(256, m)
    mr = mod.shape[0]
    return pl.pallas_call(
        _outproj_kernel,
        grid=(1, m // tm),
        in_specs=[pl.BlockSpec((tm, ka), lambda n, i: (i, 0)),
                  pl.BlockSpec((1, ka), lambda n, i: (0, 0)),
                  pl.BlockSpec((tm, kb), lambda n, i: (i, 0)),
                  pl.BlockSpec((ka + kb, d), lambda n, i: (0, 0), pipeline_mode=pl.Buffered(1)),
                  pl.BlockSpec((tm, d), lambda n, i: (i, 0)),
                  _mod_spec(mr, tm, d, lambda n: 2),
                  pl.BlockSpec((1, d), lambda n, i: (0, 0)),
                  _mod_spec(mr, tm, d, lambda n: 4),
                  _mod_spec(mr, tm, d, lambda n: 3)],
        out_specs=[pl.BlockSpec((tm, d), lambda n, i: (i, 0)),
                   pl.BlockSpec((tm, d), lambda n, i: (i, 0))],
        out_shape=[jax.ShapeDtypeStruct((m, d), F32), jax.ShapeDtypeStruct((m, d), BF16)],
        scratch_shapes=[pltpu.VMEM((ka + kb, d), BF16)],
        compiler_params=_params("arbitrary", "arbitrary"),
        name="out_proj",
    )(a, g_a.reshape(1, ka), b, w, x, mod, g_ffn.reshape(1, d), mod, mod)


def _swiglu_tile(h2, wgb_ref, wub_ref):
    g = jnp.dot(h2, wgb_ref[...], preferred_element_type=F32)
    u = jnp.dot(h2, wub_ref[...], preferred_element_type=F32)
    return (_silu(g) * u).astype(BF16)


def _gateup_kernel(x_ref, wg_ref, wu_ref, o_ref, wgb_ref, wub_ref):
    _cast_weight_once(wg_ref, wgb_ref)
    _cast_weight_once(wu_ref, wub_ref)
    o_ref[...] = _swiglu_tile(x_ref[...], wgb_ref, wub_ref)


def _gateup(h2, wg, wu):
    m, d = h2.shape
    f = wg.shape[1]
    tn, tm = 512, min(512, m)
    return pl.pallas_call(
        _gateup_kernel,
        grid=(f // tn, m // tm),
        in_specs=[pl.BlockSpec((tm, d), lambda n, i: (i, 0)),
                  pl.BlockSpec((d, tn), lambda n, i: (0, n)),
                  pl.BlockSpec((d, tn), lambda n, i: (0, n))],
        out_specs=pl.BlockSpec((tm, tn), lambda n, i: (i, n)),
        out_shape=jax.ShapeDtypeStruct((m, f), BF16),
        scratch_shapes=[pltpu.VMEM((d, tn), BF16), pltpu.VMEM((d, tn), BF16)],
        compiler_params=_params("arbitrary", "arbitrary"),
        name="ffn_gate_up_sample",
    )(h2, wg, wu)


def _gateup_stream_kernel(pps, stream_steps, steps_per_seq, pt_ref, x_ref, wg_ref, wu_ref, *rest):
    page_refs = rest[:pps]
    o_ref, km_ref, wgb_ref, wub_ref = rest[pps:]
    _cast_weight_once(wg_ref, wgb_ref)
    _cast_weight_once(wu_ref, wub_ref)
    o_ref[...] = _swiglu_tile(x_ref[...], wgb_ref, wub_ref)

    t = pl.program_id(0) * pl.num_programs(1) + pl.program_id(1)

    @pl.when(t < stream_steps)
    def _():
        page_rows = page_refs[0].shape[0]
        ppb = MOBA_BLOCK * N_KV_HEADS // page_rows
        fold = 8 // N_KV_HEADS
        j = t % steps_per_seq
        for b in range(pps // ppb):
            s8 = jnp.sum(page_refs[b * ppb][...].reshape(page_rows // 8, 8, HEAD_DIM), axis=0)
            for p in range(1, ppb):
                s8 = s8 + jnp.sum(page_refs[b * ppb + p][...].reshape(page_rows // 8, 8, HEAD_DIM), axis=0)
            s = s8[0:N_KV_HEADS]
            for p in range(1, fold):
                s = s + s8[p * N_KV_HEADS:(p + 1) * N_KV_HEADS]
            s = s * (1.0 / MOBA_BLOCK)
            for kv in range(N_KV_HEADS):
                km_ref[0, kv, j, b:b + 1, :] = s[kv:kv + 1, :]


def _gateup_stream(h2, wg, wu, cache_k, layer, page_table):
    m, d = h2.shape
    f = wg.shape[1]
    tn, tm = 512, 512
    n_seq, n_pages = page_table.shape
    depth, pool, page = cache_k.shape[:3]
    ppb = MOBA_BLOCK // page
    pps = min(32, n_pages)
    steps_per_seq = n_pages // pps
    stream_steps = n_seq * steps_per_seq
    row_tiles = m // tm
    assert n_pages % pps == 0 and pps % ppb == 0 and 8 % N_KV_HEADS == 0
    assert stream_steps <= (f // tn) * row_tiles, "not enough grid steps to stream the cache"
    cache_k = cache_k.reshape(depth, pool, page * N_KV_HEADS, HEAD_DIM)

    def stream_pos(n, i):
        t = jnp.minimum(n * row_tiles + i, stream_steps - 1)
        return t // steps_per_seq, t % steps_per_seq

    def page_spec(k):
        def index(n, i, pt):
            seq, j = stream_pos(n, i)
            return layer, pt[seq, j * pps + k], 0, 0
        return pl.BlockSpec((None, None, page * N_KV_HEADS, HEAD_DIM), index)

    bps = pps // ppb
    grid_spec = pltpu.PrefetchScalarGridSpec(
        num_scalar_prefetch=1,
        grid=(f // tn, row_tiles),
        in_specs=[pl.BlockSpec((tm, d), lambda n, i, pt: (i, 0)),
                  pl.BlockSpec((d, tn), lambda n, i, pt: (0, n)),
                  pl.BlockSpec((d, tn), lambda n, i, pt: (0, n))] + [page_spec(k) for k in range(pps)],
        out_specs=[pl.BlockSpec((tm, tn), lambda n, i, pt: (i, n)),
                   pl.BlockSpec((1, N_KV_HEADS, steps_per_seq, bps, HEAD_DIM),
                                lambda n, i, pt: (stream_pos(n, i)[0], 0, 0, 0, 0))],
        scratch_shapes=[pltpu.VMEM((d, tn), BF16), pltpu.VMEM((d, tn), BF16)],
    )
    hid, km = pl.pallas_call(
        functools.partial(_gateup_stream_kernel, pps, stream_steps, steps_per_seq),
        grid_spec=grid_spec,
        out_shape=[jax.ShapeDtypeStruct((m, f), BF16),
                   jax.ShapeDtypeStruct((n_seq, N_KV_HEADS, steps_per_seq, bps, HEAD_DIM), F32)],
        compiler_params=_params("arbitrary", "arbitrary"),
        name="ffn_gate_up_stream",
    )(page_table, h2, wg, wu, *([cache_k] * pps))
    return hid, km.reshape(n_seq, N_KV_HEADS, steps_per_seq * bps, HEAD_DIM)


def _down_kernel(tiles_p, hp_ref, hs_ref, w_ref, xp_ref, xs_ref, gtp_ref, gts_ref, op_ref, os_ref, wbf_ref):
    _cast_weight_once(w_ref, wbf_ref)
    i = pl.program_id(1)

    @pl.when(i < tiles_p)
    def _():
        acc = jnp.dot(hp_ref[...], wbf_ref[...], preferred_element_type=F32)
        op_ref[...] = xp_ref[...] + gtp_ref[...] * acc

    @pl.when(i == tiles_p)
    def _():
        acc = jnp.dot(hs_ref[...], wbf_ref[...], preferred_element_type=F32)
        os_ref[...] = xs_ref[...] + gts_ref[...] * acc


def _down(h_p, h_s, w, x_p, x_s, mod_p, mod_s):
    m, f = h_p.shape
    ms = h_s.shape[0]
    d = w.shape[1]
    tn, tm = 512, 512
    tiles_p = m // tm
    nb = d // tn
    assert mod_p.shape[0] == 1 and mod_s.shape[0] == ms
    prow = lambda i: jnp.minimum(i, tiles_p - 1)
    return pl.pallas_call(
        functools.partial(_down_kernel, tiles_p),
        grid=(nb, tiles_p + 1),
        in_specs=[pl.BlockSpec((tm, f), lambda n, i: (prow(i), 0)),
                  pl.BlockSpec((ms, f), lambda n, i: (0, 0)),
                  pl.BlockSpec((f, tn), lambda n, i: (0, n)),
                  pl.BlockSpec((tm, tn), lambda n, i: (prow(i), n)),
                  pl.BlockSpec((ms, tn), lambda n, i: (0, n)),
                  pl.BlockSpec((1, tn), lambda n, i: (0, 5 * nb + n)),
                  pl.BlockSpec((ms, tn), lambda n, i: (0, 5 * nb + n))],
        out_specs=[pl.BlockSpec((tm, tn), lambda n, i: (prow(i), n)),
                   pl.BlockSpec((ms, tn), lambda n, i: (0, n))],
        out_shape=[jax.ShapeDtypeStruct((m, d), F32), jax.ShapeDtypeStruct((ms, d), F32)],
        scratch_shapes=[pltpu.VMEM((f, tn), BF16)],
        compiler_params=_params("arbitrary", "arbitrary"),
        name="ffn_down",
    )(h_p, h_s, w, x_p, x_s, mod_p, mod_s)


def _head_norm(x, gain):
    return x * lax.rsqrt(jnp.mean(x * x, axis=-1, keepdims=True) + EPS) * gain


def _top3(g, idx_f, axis):
    sel = jnp.zeros(g.shape, F32)
    firsts = []
    for _ in range(MOBA_TOPK):
        mx = jnp.max(g, axis=axis, keepdims=True)
        ismax = jnp.logical_and(g == mx, mx > -jnp.inf)
        first = jnp.min(jnp.where(ismax, idx_f, float(LANES)), axis=axis, keepdims=True)
        pick = idx_f == first
        sel = jnp.where(pick, 1.0, sel)
        g = jnp.where(pick, -jnp.inf, g)
        firsts.append(first)
    return sel, firsts


def _prompt_prep_kernel(proj_ref, qg_ref, kg_ref, kout_ref, vout_ref, kp_ref, vp_ref, qp_ref, km_ref):
    i = pl.program_id(0)
    blk = MOBA_BLOCK

    @pl.when(i == 0)
    def _():
        km_ref[...] = jnp.zeros_like(km_ref)

    lane = lax.broadcasted_iota(jnp.int32, (blk, LANES), 1)
    row = lax.broadcasted_iota(jnp.int32, (blk, LANES), 0)
    sq_row = lax.broadcasted_iota(jnp.int32, (LANES, LANES), 0)
    i_f = i.astype(F32)
    row_f = row.astype(F32)

    kfeat = jnp.where(lane == i, 1.0, 0.0)
    kfeat = jnp.where(lane == FEAT_KBLK, i_f, kfeat)
    kfeat = jnp.where(lane == FEAT_KOFF, row_f, kfeat)
    kfeat = jnp.where(jnp.logical_or(lane == FEAT_ONE_A, lane == FEAT_ONE_B), 1.0, kfeat).astype(BF16)
    vfeat = jnp.ones((blk, LANES), BF16)

    kg = kg_ref[...]
    for kv in range(N_KV_HEADS):
        k = proj_ref[:, ATT_WIDTH + kv * HEAD_DIM:ATT_WIDTH + (kv + 1) * HEAD_DIM]
        kn = _head_norm(k, kg)
        kout_ref[:, kv * HEAD_DIM:(kv + 1) * HEAD_DIM] = kn
        kp_ref[kv, :, :HEAD_DIM] = kn.astype(BF16)
        kp_ref[kv, :, HEAD_DIM:] = kfeat
        ksum = jnp.sum(kn, axis=0, keepdims=True) * (1.0 / blk)
        km_ref[kv] = jnp.where(sq_row == i, jnp.broadcast_to(ksum, (LANES, LANES)), km_ref[kv])
        v = proj_ref[:, ATT_WIDTH + KV_WIDTH + kv * HEAD_DIM:ATT_WIDTH + KV_WIDTH + (kv + 1) * HEAD_DIM]
        vout_ref[:, kv * HEAD_DIM:(kv + 1) * HEAD_DIM] = v
        vp_ref[kv, :, :HEAD_DIM] = v.astype(BF16)
        vp_ref[kv, :, HEAD_DIM:] = vfeat

    qg = qg_ref[...]
    blk_id = lax.broadcasted_iota(jnp.int32, (LANES, blk), 0)
    blk_id_f = blk_id.astype(F32)
    valid = blk_id < i
    for h in range(ATT_HEADS):
        q = proj_ref[:, h * HEAD_DIM:(h + 1) * HEAD_DIM]
        qn = _head_norm(q, qg)
        gate = _dot3(km_ref[h // KV_GROUP], qn, NT_DIMS)
        sel_t, _ = _top3(jnp.where(valid, gate, -jnp.inf), blk_id_f, 0)
        sel = sel_t.T
        slope = _alibi_slope(h)
        qfeat = jnp.where(jnp.logical_and(lane < FEAT_KBLK, sel == 0.0), NEG_BIG, 0.0)
        qfeat = jnp.where(lane == FEAT_KBLK, slope * blk, qfeat)
        qfeat = jnp.where(lane == FEAT_KOFF, slope, qfeat)
        qfeat = jnp.where(lane == FEAT_ONE_A, -(slope * blk) * i_f, qfeat)
        qfeat = jnp.where(lane == FEAT_ONE_B, -slope * row_f, qfeat)
        qp_ref[h, :, :HEAD_DIM] = (qn * ATT_SCALE).astype(BF16)
        qp_ref[h, :, HEAD_DIM:] = qfeat.astype(BF16)


def _prompt_prep(proj, q_gain, k_gain):
    m = proj.shape[0]
    nb = m // MOBA_BLOCK
    assert m % MOBA_BLOCK == 0 and nb <= FEAT_KBLK
    blk = MOBA_BLOCK
    qkv_w = ATT_WIDTH + 2 * KV_WIDTH
    return pl.pallas_call(
        _prompt_prep_kernel,
        grid=(nb,),
        in_specs=[pl.BlockSpec((blk, qkv_w), lambda i: (i, 0)),
                  pl.BlockSpec((1, HEAD_DIM), lambda i: (0, 0)),
                  pl.BlockSpec((1, HEAD_DIM), lambda i: (0, 0))],
        out_specs=[pl.BlockSpec((blk, KV_WIDTH), lambda i: (i, 0)),
                   pl.BlockSpec((blk, KV_WIDTH), lambda i: (i, 0)),
                   pl.BlockSpec((N_KV_HEADS, blk, 2 * HEAD_DIM), lambda i: (0, i, 0)),
                   pl.BlockSpec((N_KV_HEADS, blk, 2 * HEAD_DIM), lambda i: (0, i, 0)),
                   pl.BlockSpec((ATT_HEADS, blk, 2 * HEAD_DIM), lambda i: (0, i, 0))],
        out_shape=[jax.ShapeDtypeStruct((m, KV_WIDTH), F32),
                   jax.ShapeDtypeStruct((m, KV_WIDTH), F32),
                   jax.ShapeDtypeStruct((N_KV_HEADS, m, 2 * HEAD_DIM), BF16),
                   jax.ShapeDtypeStruct((N_KV_HEADS, m, 2 * HEAD_DIM), BF16),
                   jax.ShapeDtypeStruct((ATT_HEADS, m, 2 * HEAD_DIM), BF16)],
        scratch_shapes=[pltpu.VMEM((N_KV_HEADS, LANES, LANES), F32)],
        compiler_params=_params("arbitrary"),
        name="prompt_qk_prep",
    )(proj, q_gain.reshape(1, HEAD_DIM), k_gain.reshape(1, HEAD_DIM))


ATTN_UNROLL = 4
LOG2E = 1.4426950408889634


def _prompt_attn_kernel(q_ref, k_ref, v_ref, o_ref, s_ref, acc_ref, m_ref):
    i = pl.program_id(1)
    blk = MOBA_BLOCK
    rows = KV_GROUP * blk
    span = ATTN_UNROLL * blk
    qs = q_ref[...].reshape(rows, 2 * HEAD_DIM)

    def lane_fold(s):
        out = s[:, :LANES]
        for t in range(1, s.shape[1] // LANES):
            out = jnp.maximum(out, s[:, t * LANES:(t + 1) * LANES])
        return out

    def probs(s):
        mb = m_ref[...]
        return jnp.concatenate([jnp.exp2(s[:, t * LANES:(t + 1) * LANES] - mb) for t in range(s.shape[1] // LANES)],
                               axis=1).astype(BF16)

    own = pl.ds(pl.multiple_of(i * blk, blk), blk)
    r = lax.broadcasted_iota(jnp.int32, (rows, blk), 0)
    c = lax.broadcasted_iota(jnp.int32, (rows, blk), 1)
    dist = jnp.bitwise_and(r, blk - 1) - c
    slope = qs[:, HEAD_DIM + FEAT_KOFF:HEAD_DIM + FEAT_KOFF + 1].astype(F32)
    s_own = lax.dot_general(qs[:, :HEAD_DIM], k_ref[own, :HEAD_DIM], NT_DIMS, preferred_element_type=F32)
    s_own = jnp.where(dist >= 0, (s_own - slope * dist.astype(F32)) * LOG2E, -jnp.inf)
    m_ref[...] = lane_fold(s_own)

    trips = (i + ATTN_UNROLL - 1) // ATTN_UNROLL

    def two_trips_per_iteration(trip):
        def pair(u, carry):
            trip(2 * u)
            trip(2 * u + 1)
            return carry

        lax.fori_loop(0, trips // 2, pair, 0)

        @pl.when(trips % 2 == 1)
        def _():
            trip(trips - 1)

    def pass1(t):
        ks = k_ref[pl.ds(pl.multiple_of(t * span, span), span), :]
        s = lax.dot_general(qs, ks, NT_DIMS, preferred_element_type=F32) * LOG2E
        s_ref[t] = s
        m_ref[...] = jnp.maximum(m_ref[...], lane_fold(s))

    two_trips_per_iteration(pass1)
    m_ref[...] = jnp.broadcast_to(jnp.max(m_ref[...], axis=1, keepdims=True), (rows, LANES))

    acc_ref[...] = jnp.dot(probs(s_own), v_ref[own, :], preferred_element_type=F32)

    def pass2(t):
        vs = v_ref[pl.ds(pl.multiple_of(t * span, span), span), :]
        acc_ref[...] += jnp.dot(probs(s_ref[t]), vs, preferred_element_type=F32)

    two_trips_per_iteration(pass2)
    acc = acc_ref[...]
    o = acc[:, :HEAD_DIM] / acc[:, HEAD_DIM:]
    for g in range(KV_GROUP):
        o_ref[:, g * HEAD_DIM:(g + 1) * HEAD_DIM] = o[g * blk:(g + 1) * blk]


def _prompt_attn(qp, kp, vp):
    m = kp.shape[1]
    nb = m // MOBA_BLOCK
    assert nb % ATTN_UNROLL == 0
    blk = MOBA_BLOCK
    rows = KV_GROUP * blk
    return pl.pallas_call(
        _prompt_attn_kernel,
        grid=(N_KV_HEADS, nb),
        in_specs=[pl.BlockSpec((KV_GROUP, blk, 2 * HEAD_DIM), lambda kv, i: (kv, i, 0)),
                  pl.BlockSpec((None, m, 2 * HEAD_DIM), lambda kv, i: (kv, 0, 0)),
                  pl.BlockSpec((None, m, 2 * HEAD_DIM), lambda kv, i: (kv, 0, 0))],
        out_specs=pl.BlockSpec((blk, KV_GROUP * HEAD_DIM), lambda kv, i: (i, kv)),
        out_shape=jax.ShapeDtypeStruct((m, ATT_WIDTH), F32),
        scratch_shapes=[pltpu.VMEM((nb // ATTN_UNROLL, rows, ATTN_UNROLL * blk), F32),
                        pltpu.VMEM((rows, 2 * HEAD_DIM), F32),
                        pltpu.VMEM((rows, LANES), F32)],
        compiler_params=_params("arbitrary", "arbitrary"),
        name="prompt_moba_attn",
    )(qp, kp, vp)


def _expand_heads(v, lane_lo):
    r = v.shape[0]
    parts = []
    for k in range(SSM_HEADS // 2):
        a0 = jnp.broadcast_to(v[:, 2 * k:2 * k + 1], (r, LANES))
        a1 = jnp.broadcast_to(v[:, 2 * k + 1:2 * k + 2], (r, LANES))
        parts.append(jnp.where(lane_lo, a0, a1))
    return jnp.concatenate(parts, axis=1)


def _gated_group_norm(y, z, g):
    yz = y * _silu(z)
    gw = D_INNER // SSM_GROUPS
    outs = []
    for grp in range(SSM_GROUPS):
        t = yz[:, grp * gw:(grp + 1) * gw]
        t = t * lax.rsqrt(jnp.mean(t * t, axis=-1, keepdims=True) + EPS)
        outs.append(t * g[:, grp * gw:(grp + 1) * gw])
    return jnp.concatenate(outs, axis=1)


def _ssd_kernel(z_ref, xbc_ref, dt_ref, cw_ref, cb_ref, dtb_ref, alog_ref, dsk_ref, g_ref,
                y_ref, tail_out_ref, st_out_ref, tail_ref, st_ref):
    c = pl.program_id(0)
    cs = SSD_CHUNK
    gw = D_INNER // SSM_GROUPS

    @pl.when(c == 0)
    def _():
        tail_ref[...] = jnp.zeros_like(tail_ref)
        st_ref[...] = jnp.zeros_like(st_ref)

    xr = xbc_ref[...]
    xp = jnp.concatenate([tail_ref[...], xr], axis=0)
    cw = cw_ref[...]
    conv = cb_ref[...] + cw[3:4] * xr
    for t in range(CONV_WIDTH - 1):
        conv = conv + cw[t:t + 1] * xp[8 - (CONV_WIDTH - 1) + t:8 - (CONV_WIDTH - 1) + t + cs]
    tail_ref[...] = xr[cs - 8:]
    tail_out_ref[...] = xr[cs - 8:]
    xc = _silu(conv)
    xs = xc[:, :D_INNER]
    bm = xc[:, D_INNER:D_INNER + SSM_GROUPS * D_STATE]
    cm = xc[:, D_INNER + SSM_GROUPS * D_STATE:]

    lane = lax.broadcasted_iota(jnp.int32, (cs, LANES), 1)
    rowi = lax.broadcasted_iota(jnp.int32, (cs, LANES), 0)
    lane_lo = lane < SSM_HEAD_DIM
    tri = rowi >= lane

    dt = _softplus(dt_ref[...] + dtb_ref[...])
    a = jnp.where(lane[:1] < SSM_HEADS, -jnp.exp(alog_ref[...]), 0.0)
    da = dt * a
    tril = jnp.where(tri, 1.0, 0.0).astype(BF16)
    p1 = da.astype(BF16)
    r1 = da - p1.astype(F32)
    p2 = r1.astype(BF16)
    p3 = (r1 - p2.astype(F32)).astype(BF16)
    acum = (jnp.dot(tril, p1, preferred_element_type=F32) + jnp.dot(tril, p2, preferred_element_type=F32)
            + jnp.dot(tril, p3, preferred_element_type=F32))
    acum_t = acum.T

    dt_e = _expand_heads(dt, lane_lo)
    ac_e = _expand_heads(acum, lane_lo)
    xdt = xs * dt_e
    ea_e = jnp.exp(ac_e)
    dend_e = jnp.exp(ac_e[cs - 1:cs, :] - ac_e)
    cdec = ea_e[cs - 1:cs, :]
    xdt_bf = xdt.astype(BF16)
    xdec_bf = (xdt * dend_e).astype(BF16)

    y_parts = []
    for grp in range(SSM_GROUPS):
        bg = bm[:, grp * D_STATE:(grp + 1) * D_STATE]
        cg = cm[:, grp * D_STATE:(grp + 1) * D_STATE].astype(BF16)
        cb = lax.dot_general(cg, bg.astype(BF16), NT_DIMS, preferred_element_type=F32)
        hpg = SSM_HEADS // SSM_GROUPS
        intra = []
        for k in range(hpg // 2):
            pair = grp * (hpg // 2) + k
            xpair = xdt_bf[:, pair * LANES:(pair + 1) * LANES]
            acc = None
            for hh in range(2):
                h = 2 * pair + hh
                seg = jnp.broadcast_to(acum[:, h:h + 1], (cs, cs)) - acum_t[h:h + 1, :]
                lmat = jnp.exp(jnp.where(tri, seg, -jnp.inf))
                mh = (cb * lmat).astype(BF16)
                xh = jnp.where(lane_lo if hh == 0 else jnp.logical_not(lane_lo), xpair, jnp.zeros_like(xpair))
                part = jnp.dot(mh, xh, preferred_element_type=F32)
                acc = part if acc is None else acc + part
            intra.append(acc)
        y_intra = jnp.concatenate(intra, axis=1)
        st = st_ref[grp]
        y_inter = jnp.dot(cg, st.astype(BF16), preferred_element_type=F32) * ea_e[:, grp * gw:(grp + 1) * gw]
        new_st = cdec[:, grp * gw:(grp + 1) * gw] * st + jnp.dot(
            bg.T.astype(BF16), xdec_bf[:, grp * gw:(grp + 1) * gw], preferred_element_type=F32)
        st_ref[grp] = new_st
        st_out_ref[grp] = new_st
        y_parts.append(y_intra + y_inter)
    y = jnp.concatenate(y_parts, axis=1) + dsk_ref[...] * xs
    y_ref[...] = _gated_group_norm(y, z_ref[...], g_ref[...]).astype(y_ref.dtype)


def _pad_lanes(v):
    return jnp.pad(v.reshape(1, -1), ((0, 0), (0, LANES - v.size)))


def _ssd_prompt(proj, dtp, conv_w, conv_b, dt_bias, a_log, d_skip, g_ssm):
    m = proj.shape[0]
    cs = SSD_CHUNK
    assert m % cs == 0
    gw = D_INNER // SSM_GROUPS
    z_blk = (ATT_WIDTH + 2 * KV_WIDTH) // D_INNER
    x_blk = (ATT_WIDTH + 2 * KV_WIDTH + D_INNER) // CONV_DIM
    assert z_blk * D_INNER == ATT_WIDTH + 2 * KV_WIDTH and x_blk * CONV_DIM == ATT_WIDTH + 2 * KV_WIDTH + D_INNER
    const = lambda shape: pl.BlockSpec(shape, lambda c: tuple(0 for _ in shape))
    return pl.pallas_call(
        _ssd_kernel,
        grid=(m // cs,),
        in_specs=[pl.BlockSpec((cs, D_INNER), lambda c: (c, z_blk)),
                  pl.BlockSpec((cs, CONV_DIM), lambda c: (c, x_blk)),
                  pl.BlockSpec((cs, LANES), lambda c: (c, 0)),
                  const((CONV_WIDTH, CONV_DIM)), const((1, CONV_DIM)), const((1, LANES)), const((1, LANES)),
                  const((1, D_INNER)), const((1, D_INNER))],
        out_specs=[pl.BlockSpec((cs, D_INNER), lambda c: (c, 0)),
                   const((8, CONV_DIM)), const((SSM_GROUPS, D_STATE, gw))],
        out_shape=[jax.ShapeDtypeStruct((m, D_INNER), BF16),
                   jax.ShapeDtypeStruct((8, CONV_DIM), F32),
                   jax.ShapeDtypeStruct((SSM_GROUPS, D_STATE, gw), F32)],
        scratch_shapes=[pltpu.VMEM((8, CONV_DIM), F32), pltpu.VMEM((SSM_GROUPS, D_STATE, gw), F32)],
        compiler_params=_params("arbitrary"),
        name="prompt_ssd",
    )(proj, proj, dtp, conv_w, conv_b.reshape(1, CONV_DIM), _pad_lanes(dt_bias), _pad_lanes(a_log),
      jnp.repeat(d_skip, SSM_HEAD_DIM).reshape(1, D_INNER), g_ssm.reshape(1, D_INNER))


def _sample_prep_kernel(proj_ref, qg_ref, kg_ref, q_ref, k_ref, v_ref):
    for h in range(ATT_HEADS):
        q_ref[:, h * HEAD_DIM:(h + 1) * HEAD_DIM] = _head_norm(proj_ref[:, h * HEAD_DIM:(h + 1) * HEAD_DIM], qg_ref[...])
    for kv in range(N_KV_HEADS):
        lo = ATT_WIDTH + kv * HEAD_DIM
        k_ref[:, kv * HEAD_DIM:(kv + 1) * HEAD_DIM] = _head_norm(proj_ref[:, lo:lo + HEAD_DIM], kg_ref[...])
    v_ref[...] = proj_ref[:, ATT_WIDTH + KV_WIDTH:ATT_WIDTH + 2 * KV_WIDTH]


def _sample_prep(proj, q_gain, k_gain):
    n = proj.shape[0]
    qkv_w = ATT_WIDTH + 2 * KV_WIDTH
    return pl.pallas_call(
        _sample_prep_kernel,
        grid=(1,),
        in_specs=[pl.BlockSpec((n, qkv_w), lambda i: (0, 0)),
                  pl.BlockSpec((1, HEAD_DIM), lambda i: (0, 0)),
                  pl.BlockSpec((1, HEAD_DIM), lambda i: (0, 0))],
        out_specs=[pl.BlockSpec((n, ATT_WIDTH), lambda i: (0, 0)),
                   pl.BlockSpec((n, KV_WIDTH), lambda i: (0, 0)),
                   pl.BlockSpec((n, KV_WIDTH), lambda i: (0, 0))],
        out_shape=[jax.ShapeDtypeStruct((n, ATT_WIDTH), F32),
                   jax.ShapeDtypeStruct((n, KV_WIDTH), F32),
                   jax.ShapeDtypeStruct((n, KV_WIDTH), F32)],
        compiler_params=_params("arbitrary"),
        name="sample_qk_prep",
    )(proj, q_gain.reshape(1, HEAD_DIM), k_gain.reshape(1, HEAD_DIM))


def _sample_topk_kernel(q_ref, km_ref, idx_ref):
    nblk = km_ref.shape[2]
    q = q_ref[0]
    hrow = lax.broadcasted_iota(jnp.int32, (ATT_HEADS, nblk), 0)
    gate = jnp.zeros((ATT_HEADS, nblk), F32)
    for kv in range(N_KV_HEADS):
        gk = _dot3(q, km_ref[0, kv], NT_DIMS)
        gate = jnp.where(hrow // KV_GROUP == kv, gk, gate)
    lane = lax.broadcasted_iota(jnp.int32, (ATT_HEADS, LANES), 1)
    if nblk < LANES:
        gate = jnp.concatenate([gate, jnp.full((ATT_HEADS, LANES - nblk), -jnp.inf, F32)], axis=1)
    _, firsts = _top3(gate, lane.astype(F32), 1)
    out = jnp.zeros((ATT_HEADS, LANES), jnp.int32)
    for t, first in enumerate(firsts):
        out = jnp.where(lane == t, first.astype(jnp.int32), out)
    idx_ref[0] = out


def _sample_topk(q3, kmean):
    n, _, nblk, _ = kmean.shape
    assert MOBA_TOPK <= nblk <= LANES
    return pl.pallas_call(
        _sample_topk_kernel,
        grid=(n,),
        in_specs=[pl.BlockSpec((1, ATT_HEADS, HEAD_DIM), lambda s: (s, 0, 0)),
                  pl.BlockSpec((1, N_KV_HEADS, nblk, HEAD_DIM), lambda s: (s, 0, 0, 0))],
        out_specs=pl.BlockSpec((1, ATT_HEADS, LANES), lambda s: (s, 0, 0)),
        out_shape=jax.ShapeDtypeStruct((n, ATT_HEADS, LANES), jnp.int32),
        compiler_params=_params("arbitrary"),
        name="sample_gate_topk",
    )(q3, kmean)


def _sample_attn_kernel(past, ppb, layer, pt_ref, idx_ref, q_ref, kn_ref, vn_ref, ck_ref, cv_ref, o_ref,
                        kbuf, vbuf, sems):
    s = pl.program_id(0)
    page = MOBA_BLOCK // ppb
    slot = s % 2

    def copies(seq, buf, h, t, p):
        kv = h // KV_GROUP
        blk = idx_ref[seq, h * MOBA_TOPK + t]
        phys = pt_ref[seq, blk * ppb + p]
        dst = pl.ds((t * ppb + p) * page, page)
        return (pltpu.make_async_copy(ck_ref.at[layer, phys, :, kv, :], kbuf.at[buf, h, dst, :],
                                      sems.at[buf, 0, h, t * ppb + p]),
                pltpu.make_async_copy(cv_ref.at[layer, phys, :, kv, :], vbuf.at[buf, h, dst, :],
                                      sems.at[buf, 1, h, t * ppb + p]))

    triples = [(h, t, p) for h in range(ATT_HEADS) for t in range(MOBA_TOPK) for p in range(ppb)]

    def start_all(seq, buf):
        for h, t, p in triples:
            ck, cv = copies(seq, buf, h, t, p)
            ck.start()
            cv.start()

    @pl.when(s == 0)
    def _():
        start_all(0, 0)

    @pl.when(s + 1 < pl.num_programs(0))
    def _():
        start_all(s + 1, 1 - slot)

    for h, t, p in triples:
        ck, cv = copies(s, slot, h, t, p)
        ck.wait()
        cv.wait()

    nsel = MOBA_TOPK * MOBA_BLOCK
    rowi = lax.broadcasted_iota(jnp.int32, (nsel, 1), 0)
    off = jnp.bitwise_and(rowi, MOBA_BLOCK - 1)
    for h in range(ATT_HEADS):
        kv = h // KV_GROUP
        slope = _alibi_slope(h)
        q = q_ref[0, h:h + 1, :]
        sc = jnp.sum(kbuf[slot, h] * q, axis=1, keepdims=True) * ATT_SCALE
        pos = jnp.zeros((nsel, 1), jnp.int32)
        for t in range(MOBA_TOPK):
            pos = jnp.where(rowi // MOBA_BLOCK == t, idx_ref[s, h * MOBA_TOPK + t] * MOBA_BLOCK, pos)
        dist = (past - (pos + off)).astype(F32)
        sc = sc - slope * dist
        s_own = jnp.sum(kn_ref[0, kv:kv + 1, :] * q, axis=1, keepdims=True) * ATT_SCALE
        mx = jnp.maximum(jnp.max(sc, axis=0, keepdims=True), s_own)
        p = jnp.exp(sc - mx)
        p_own = jnp.exp(s_own - mx)
        denom = jnp.sum(p, axis=0, keepdims=True) + p_own
        num = jnp.sum(p * vbuf[slot, h], axis=0, keepdims=True) + p_own * vn_ref[0, kv:kv + 1, :]
        o_ref[0, h:h + 1, :] = num / denom


def _sample_attn(q3, k_new3, v_new3, cache_k, cache_v, layer, page_table, idx):
    n, n_pages = page_table.shape
    page = cache_k.shape[2]
    ppb = MOBA_BLOCK // page
    past = n_pages * page
    assert past % MOBA_BLOCK == 0
    nsel = MOBA_TOPK * MOBA_BLOCK
    grid_spec = pltpu.PrefetchScalarGridSpec(
        num_scalar_prefetch=2,
        grid=(n,),
        in_specs=[pl.BlockSpec((1, ATT_HEADS, HEAD_DIM), lambda s, pt, ix: (s, 0, 0)),
                  pl.BlockSpec((1, N_KV_HEADS, HEAD_DIM), lambda s, pt, ix: (s, 0, 0)),
                  pl.BlockSpec((1, N_KV_HEADS, HEAD_DIM), lambda s, pt, ix: (s, 0, 0)),
                  pl.BlockSpec(memory_space=pl.ANY),
                  pl.BlockSpec(memory_space=pl.ANY)],
        out_specs=pl.BlockSpec((1, ATT_HEADS, HEAD_DIM), lambda s, pt, ix: (s, 0, 0)),
        scratch_shapes=[pltpu.VMEM((2, ATT_HEADS, nsel, HEAD_DIM), F32),
                        pltpu.VMEM((2, ATT_HEADS, nsel, HEAD_DIM), F32),
                        pltpu.SemaphoreType.DMA((2, 2, ATT_HEADS, MOBA_TOPK * ppb))],
    )
    return pl.pallas_call(
        functools.partial(_sample_attn_kernel, past, ppb, layer),
        grid_spec=grid_spec,
        out_shape=jax.ShapeDtypeStruct((n, ATT_HEADS, HEAD_DIM), F32),
        compiler_params=_params("arbitrary"),
        name="sample_moba_attn",
    )(page_table, idx, q3, k_new3, v_new3, cache_k, cache_v)


def _sample_ssd_kernel(z_ref, xbc_ref, dt_ref, buf_ref, h0_ref, cw_ref, cb_ref, dtb_ref, alog_ref, dsk_ref, g_ref,
                       y_ref, buf_out_ref, h_out_ref):
    x = xbc_ref[0]
    buf = buf_ref[0]
    cw = cw_ref[...]
    conv = cb_ref[...] + cw[CONV_WIDTH - 1:CONV_WIDTH] * x
    for t in range(CONV_WIDTH - 1):
        conv = conv + cw[t:t + 1] * buf[t:t + 1]
    buf_out_ref[0, 0:CONV_WIDTH - 2, :] = buf[1:CONV_WIDTH - 1]
    buf_out_ref[0, CONV_WIDTH - 2:CONV_WIDTH - 1, :] = x
    xc = _silu(conv)
    xs = xc[:, :D_INNER]
    bm = xc[:, D_INNER:D_INNER + SSM_GROUPS * D_STATE]
    cm = xc[:, D_INNER + SSM_GROUPS * D_STATE:]

    lane1 = lax.broadcasted_iota(jnp.int32, (1, LANES), 1)
    rowi = lax.broadcasted_iota(jnp.int32, (LANES, LANES), 0)
    dt = _softplus(dt_ref[0] + dtb_ref[...])
    a = jnp.where(lane1 < SSM_HEADS, -jnp.exp(alog_ref[...]), 0.0)
    dec = jnp.exp(dt * a)
    dt_e = _expand_heads(dt, lane1 < SSM_HEAD_DIM)
    xdt = xs * dt_e
    xdt_rows = jnp.broadcast_to(xdt, (LANES, D_INNER))

    hpg = SSM_HEADS // SSM_GROUPS
    y_parts = []
    for pair in range(SSM_HEADS // 2):
        grp = (2 * pair) // hpg
        xcol = xdt_rows[:, pair * LANES:(pair + 1) * LANES].T
        dcol = jnp.where(rowi < SSM_HEAD_DIM,
                         jnp.broadcast_to(dec[:, 2 * pair:2 * pair + 1], (LANES, LANES)),
                         jnp.broadcast_to(dec[:, 2 * pair + 1:2 * pair + 2], (LANES, LANES)))
        h0 = h0_ref[0, 2 * pair:2 * pair + 2].reshape(LANES, D_STATE)
        hn = dcol * h0 + xcol * bm[:, grp * D_STATE:(grp + 1) * D_STATE]
        h_out_ref[0, 2 * pair:2 * pair + 2] = hn.reshape(2, SSM_HEAD_DIM, D_STATE)
        cgrow = jnp.broadcast_to(cm[:, grp * D_STATE:(grp + 1) * D_STATE], (8, D_STATE))
        ypair = _dot3(cgrow, hn, NT_DIMS)
        y_parts.append(ypair[0:1])
    y = jnp.concatenate(y_parts, axis=1) + dsk_ref[...] * xs
    y_ref[0] = _gated_group_norm(y, z_ref[0], g_ref[...]).astype(y_ref.dtype)


def _ssd_sample(proj, dtp, state_conv, state_ssm, conv_w, conv_b, dt_bias, a_log, d_skip, g_ssm):
    n = proj.shape[0]
    z0 = ATT_WIDTH + 2 * KV_WIDTH
    z3 = proj[:, z0:z0 + D_INNER].reshape(n, 1, D_INNER)
    x3 = proj[:, z0 + D_INNER:z0 + D_INNER + CONV_DIM].reshape(n, 1, CONV_DIM)
    dt3 = dtp.reshape(n, 1, LANES)
    const = lambda shape: pl.BlockSpec(shape, lambda s: tuple(0 for _ in shape))
    per_seq = lambda shape: pl.BlockSpec((1,) + shape, lambda s: (s,) + tuple(0 for _ in shape))
    y, buf, h = pl.pallas_call(
        _sample_ssd_kernel,
        grid=(n,),
        in_specs=[per_seq((1, D_INNER)), per_seq((1, CONV_DIM)), per_seq((1, LANES)),
                  per_seq((CONV_WIDTH - 1, CONV_DIM)), per_seq((SSM_HEADS, SSM_HEAD_DIM, D_STATE)),
                  const((CONV_WIDTH, CONV_DIM)), const((1, CONV_DIM)), const((1, LANES)), const((1, LANES)),
                  const((1, D_INNER)), const((1, D_INNER))],
        out_specs=[per_seq((1, D_INNER)), per_seq((CONV_WIDTH - 1, CONV_DIM)),
                   per_seq((SSM_HEADS, SSM_HEAD_DIM, D_STATE))],
        out_shape=[jax.ShapeDtypeStruct((n, 1, D_INNER), BF16),
                   jax.ShapeDtypeStruct((n, CONV_WIDTH - 1, CONV_DIM), F32),
                   jax.ShapeDtypeStruct((n, SSM_HEADS, SSM_HEAD_DIM, D_STATE), F32)],
        compiler_params=_params("arbitrary"),
        name="sample_ssd",
    )(z3, x3, dt3, state_conv, state_ssm, conv_w, conv_b.reshape(1, CONV_DIM), _pad_lanes(dt_bias),
      _pad_lanes(a_log), jnp.repeat(d_skip, SSM_HEAD_DIM).reshape(1, D_INNER), g_ssm.reshape(1, D_INNER))
    return y.reshape(n, D_INNER), buf, h


def kernel(x_prompt, x_sample, cache_k, cache_v, state_conv, state_ssm, page_table, c_prompt, c_sample, w_ada, b_ada, g_mix_norm, w_in, q_gain, k_gain, g_att_out, conv_w, conv_b, dt_bias, a_log, d_skip, g_ssm_out, w_out, g_ffn_norm, w_gate, w_up, w_down):
    n_p, seq, d = x_prompt.shape
    n_s, dec_seq, _ = x_sample.shape
    assert n_p == 1 and dec_seq == 1
    depth = w_ada.shape[0]
    main_w = ATT_WIDTH + 2 * KV_WIDTH + D_INNER + CONV_DIM

    yp = x_prompt.reshape(seq, d)
    ys = x_sample.reshape(n_s, d)
    c_rows = n_p + n_s
    c_pad = -(-c_rows // 16) * 16
    c_all = jnp.pad(jnp.concatenate([c_prompt, c_sample], axis=0), ((0, c_pad - c_rows), (0, 0)))
    outs = [[] for _ in range(8)]
    for l in range(depth):
        mod = _ada(c_all, w_ada[l], b_ada[l])
        mod_p, mod_s = mod[0:1], mod[1:1 + n_s]
        w_in_t = jnp.swapaxes(w_in[l], 0, 1)

        proj, dtp = _inproj(yp, g_mix_norm[l], mod_p, w_in_t, main_w, tn=1536, tm=512, name="in_proj")
        k_out, v_out, kp, vp, qp = _prompt_prep(proj, q_gain[l], k_gain[l])
        o_att = _prompt_attn(qp, kp, vp)
        y_ssm, tail, st = _ssd_prompt(proj, dtp, conv_w[l], conv_b[l], dt_bias[l], a_log[l], d_skip[l], g_ssm_out[l])
        x1_p, h2_p = _outproj(o_att, g_att_out[l], y_ssm, w_out[l], yp, mod_p, g_ffn_norm[l])
        hid_p, kmean_s = _gateup_stream(h2_p, w_gate[l], w_up[l], cache_k, l, page_table)
        hpg = SSM_HEADS // SSM_GROUPS
        ssm_p = st.reshape(SSM_GROUPS, D_STATE, hpg, SSM_HEAD_DIM).transpose(0, 2, 3, 1).reshape(
            1, SSM_HEADS, SSM_HEAD_DIM, D_STATE)
        outs[0].append(k_out.reshape(1, seq, N_KV_HEADS, HEAD_DIM))
        outs[1].append(v_out.reshape(1, seq, N_KV_HEADS, HEAD_DIM))
        outs[2].append(tail[8 - (CONV_WIDTH - 1):].reshape(1, CONV_WIDTH - 1, CONV_DIM))
        outs[3].append(ssm_p)

        proj_s, dts = _inproj(ys, g_mix_norm[l], mod_s, w_in_t, main_w, tn=768, tm=512, name="in_proj_sample",
                              split=True)
        q_s, k_s, v_s = _sample_prep(proj_s, q_gain[l], k_gain[l])
        q3 = q_s.reshape(n_s, ATT_HEADS, HEAD_DIM)
        idx = _sample_topk(q3, kmean_s)
        idx_flat = idx[:, :, :MOBA_TOPK].reshape(n_s, ATT_HEADS * MOBA_TOPK)
        o_s = _sample_attn(q3, k_s.reshape(n_s, N_KV_HEADS, HEAD_DIM), v_s.reshape(n_s, N_KV_HEADS, HEAD_DIM),
                           cache_k, cache_v, l, page_table, idx_flat)
        y_s, buf_s, h_s = _ssd_sample(proj_s, dts, state_conv[l], state_ssm[l], conv_w[l], conv_b[l], dt_bias[l],
                                      a_log[l], d_skip[l], g_ssm_out[l])
        x1_s, h2_s = _outproj(o_s.reshape(n_s, ATT_WIDTH), g_att_out[l], y_s, w_out[l], ys, mod_s, g_ffn_norm[l])

        hid_s = _gateup(h2_s, w_gate[l], w_up[l])
        yp, ys = _down(hid_p, hid_s, w_down[l], x1_p, x1_s, mod_p, mod_s)
        outs[4].append(k_s.reshape(n_s, 1, N_KV_HEADS, HEAD_DIM))
        outs[5].append(v_s.reshape(n_s, 1, N_KV_HEADS, HEAD_DIM))
        outs[6].append(buf_s)
        outs[7].append(h_s)
    stacked = [jnp.stack(o) for o in outs]
    return (yp.reshape(1, seq, d), ys.reshape(n_s, 1, d), *stacked)
```

```python
import functools

import jax
import jax.numpy as jnp
from jax import lax
from jax.experimental import pallas as pl
from jax.experimental.pallas import tpu as pltpu

F32 = jnp.float32
BF16 = jnp.bfloat16

HEAD_DIM = 128
ATT_HEADS = 8
N_KV_HEADS = 4
KV_GROUP = ATT_HEADS // N_KV_HEADS
ATT_WIDTH = ATT_HEADS * HEAD_DIM
KV_WIDTH = N_KV_HEADS * HEAD_DIM
MOBA_BLOCK = 256
MOBA_TOPK = 3
D_INNER = 1024
SSM_HEAD_DIM = 64
SSM_HEADS = D_INNER // SSM_HEAD_DIM
SSM_GROUPS = 2
D_STATE = 128
CONV_WIDTH = 4
CONV_DIM = D_INNER + 2 * SSM_GROUPS * D_STATE
SSD_CHUNK = 128
EPS = 1e-6
ATT_SCALE = HEAD_DIM ** -0.5

LANES = 128
FEAT_KBLK = 96
FEAT_KOFF = 97
FEAT_ONE_A = 98
FEAT_ONE_B = 99
NEG_BIG = -1e30

NT_DIMS = (((1,), (1,)), ((), ()))
VMEM_LIMIT = 56 * 1024 * 1024


def _params(*sem):
    return pltpu.CompilerParams(dimension_semantics=sem, vmem_limit_bytes=VMEM_LIMIT)


def _silu(x):
    return x / (1.0 + jnp.exp(-x))


def _softplus(x):
    return jnp.maximum(x, 0.0) + jnp.log1p(jnp.exp(-jnp.abs(x)))


def _split_bf16(x):
    hi = x.astype(BF16)
    lo = (x - hi.astype(F32)).astype(BF16)
    return hi, lo


def _dot3(a, b, dims):
    ah, al = _split_bf16(a)
    bh, bl = _split_bf16(b)
    d = lambda x, y: lax.dot_general(x, y, dims, preferred_element_type=F32)
    return d(ah, bh) + d(al, bh) + d(ah, bl)


def _alibi_slope(h):
    return 2.0 ** (-8.0 * (h + 1) / ATT_HEADS)


NN_DIMS = (((1,), (0,)), ((), ()))


def _ada_kernel(c_ref, w_ref, b_ref, o_ref):
    rows = c_ref.shape[0]
    ahi, alo = _split_bf16(_silu(c_ref[...]))
    whi, wlo = _split_bf16(w_ref[...])
    both = jnp.dot(jnp.concatenate([ahi, alo], axis=0), whi, preferred_element_type=F32)
    o_ref[...] = both[:rows] + both[rows:] + jnp.dot(ahi, wlo, preferred_element_type=F32) + b_ref[...]


def _ada(c_all, w, b):
    rows, d = c_all.shape
    n = w.shape[1]
    tn = 512
    return pl.pallas_call(
        _ada_kernel,
        grid=(n // tn,),
        in_specs=[pl.BlockSpec((rows, d), lambda j: (0, 0)),
                  pl.BlockSpec((d, tn), lambda j: (0, j)),
                  pl.BlockSpec((1, tn), lambda j: (0, j))],
        out_specs=pl.BlockSpec((rows, tn), lambda j: (0, j)),
        out_shape=jax.ShapeDtypeStruct((rows, n), F32),
        compiler_params=_params("arbitrary"),
        name="ada_mod",
    )(c_all, w, b.reshape(1, n))


def _mod_spec(mod_rows, tm, width, col_of):
    if mod_rows == 1:
        return pl.BlockSpec((1, width), lambda n, i: (0, col_of(n)))
    return pl.BlockSpec((tm, width), lambda n, i: (i, col_of(n)))


def _rms(x):
    return x * lax.rsqrt(jnp.mean(x * x, axis=-1, keepdims=True) + EPS)


def _cast_weight_once(w_ref, wbf_ref):
    @pl.when(pl.program_id(1) == 0)
    def _():
        wbf_ref[...] = w_ref[...].astype(BF16)


STAGE_CHUNK = 384


def _inproj_kernel(n_main, split, x_ref, g_ref, sc_ref, sh_ref, w_ref, wdt_ref, o_ref, odt_ref, *w_scratch):
    n = pl.program_id(0)
    first_row_tile = pl.program_id(1) == 0
    main_refs, dt_refs = w_scratch[:len(w_scratch) // 2], w_scratch[len(w_scratch) // 2:]

    def stage(src_ref, refs, pad_rows):
        cols = src_ref.shape[0]
        chunk = STAGE_CHUNK if cols % STAGE_CHUNK == 0 else cols
        for c0 in range(0, cols, chunk):
            w = src_ref[c0:c0 + chunk, :]
            if pad_rows:
                w = jnp.concatenate([w, jnp.zeros((pad_rows, w.shape[1]), F32)], axis=0)
            w = w.T
            for part, ref in zip(_split_bf16(w) if split else (w.astype(BF16),), refs):
                ref[:, c0:c0 + w.shape[1]] = part

    def product(hn, refs):
        mm = lambda a, b: jnp.dot(a, b, preferred_element_type=F32)
        if not split:
            return mm(hn.astype(BF16), refs[0][...])
        hhi, hlo = _split_bf16(hn)
        return mm(hhi, refs[0][...]) + mm(hlo, refs[0][...]) + mm(hhi, refs[1][...])

    @pl.when(first_row_tile)
    def _():
        stage(w_ref, main_refs, 0)

    @pl.when(jnp.logical_and(n == 0, first_row_tile))
    def _():
        stage(wdt_ref, dt_refs, LANES - wdt_ref.shape[0])

    hn = (_rms(x_ref[...]) * g_ref[...]) * (1.0 + sc_ref[...]) + sh_ref[...]
    o_ref[...] = product(hn, main_refs)

    @pl.when(n == 0)
    def _():
        odt_ref[...] = product(hn, dt_refs)


def _inproj(x, g, mod, w_t, main_w, *, tn, tm, name, split=False):
    m, d = x.shape
    tm = min(tm, m)
    mr = mod.shape[0]
    n_main = main_w // tn
    n_dt = w_t.shape[0] - main_w
    assert n_main * tn == main_w and main_w % n_dt == 0 and n_dt % 16 == 0
    last_i = m // tm - 1
    ncopies = 2 if split else 1
    proj, dtp = pl.pallas_call(
        functools.partial(_inproj_kernel, n_main, split),
        grid=(n_main, m // tm),
        in_specs=[pl.BlockSpec((tm, d), lambda n, i: (i, 0)),
                  pl.BlockSpec((1, d), lambda n, i: (0, 0)),
                  _mod_spec(mr, tm, d, lambda n: 1),
                  _mod_spec(mr, tm, d, lambda n: 0),
                  pl.BlockSpec((tn, d), lambda n, i: (n, 0)),
                  pl.BlockSpec((n_dt, d), lambda n, i: (main_w // n_dt, 0))],
        out_specs=[pl.BlockSpec((tm, tn), lambda n, i: (i, n)),
                   pl.BlockSpec((tm, LANES), lambda n, i: (jnp.where(n == 0, i, last_i), 0))],
        out_shape=[jax.ShapeDtypeStruct((m, main_w), F32), jax.ShapeDtypeStruct((m, LANES), F32)],
        scratch_shapes=[pltpu.VMEM((d, tn), BF16)] * ncopies + [pltpu.VMEM((d, LANES), BF16)] * ncopies,
        compiler_params=_params("arbitrary", "arbitrary"),
        name=name,
    )(x, g.reshape(1, d), mod, mod, w_t, w_t)
    return proj, dtp


def _outproj_kernel(a_ref, ga_ref, b_ref, w_ref, x_ref, gt_ref, gf_ref, sc_ref, sh_ref, x1_ref, h2_ref, wbf_ref):
    _cast_weight_once(w_ref, wbf_ref)
    ka = a_ref.shape[1]
    a = (_rms(a_ref[...]) * ga_ref[...]).astype(BF16)
    acc = jnp.dot(a, wbf_ref[:ka, :], preferred_element_type=F32)
    acc = acc + jnp.dot(b_ref[...], wbf_ref[ka:, :], preferred_element_type=F32)
    x1 = x_ref[...] + gt_ref[...] * acc
    x1_ref[...] = x1
    h2_ref[...] = ((_rms(x1) * gf_ref[...]) * (1.0 + sc_ref[...]) + sh_ref[...]).astype(BF16)


def _outproj(a, g_a, b, w, x, mod, g_ffn):
    m, ka = a.shape
    kb = b.shape[1]
    d = w.shape[1]
    tm = min(256, m)
    mr = mod.shape[0]
    return pl.pallas_call(
        _outproj_kernel,
        grid=(1, m // tm),
        in_specs=[pl.BlockSpec((tm, ka), lambda n, i: (i, 0)),
                  pl.BlockSpec((1, ka), lambda n, i: (0, 0)),
                  pl.BlockSpec((tm, kb), lambda n, i: (i, 0)),
                  pl.BlockSpec((ka + kb, d), lambda n, i: (0, 0), pipeline_mode=pl.Buffered(1)),
                  pl.BlockSpec((tm, d), lambda n, i: (i, 0)),
                  _mod_spec(mr, tm, d, lambda n: 2),
                  pl.BlockSpec((1, d), lambda n, i: (0, 0)),
                  _mod_spec(mr, tm, d, lambda n: 4),
                  _mod_spec(mr, tm, d, lambda n: 3)],
        out_specs=[pl.BlockSpec((tm, d), lambda n, i: (i, 0)),
                   pl.BlockSpec((tm, d), lambda n, i: (i, 0))],
        out_shape=[jax.ShapeDtypeStruct((m, d), F32), jax.ShapeDtypeStruct((m, d), BF16)],
        scratch_shapes=[pltpu.VMEM((ka + kb, d), BF16)],
        compiler_params=_params("arbitrary", "arbitrary"),
        name="out_proj",
    )(a, g_a.reshape(1, ka), b, w, x, mod, g_ffn.reshape(1, d), mod, mod)


def _gateup_kernel(tiles_p, xp_ref, xs_ref, wg_ref, wu_ref, op_ref, os_ref, wgb_ref, wub_ref):
    _cast_weight_once(wg_ref, wgb_ref)
    _cast_weight_once(wu_ref, wub_ref)
    i = pl.program_id(1)

    def swiglu(h2):
        g = jnp.dot(h2, wgb_ref[...], preferred_element_type=F32)
        u = jnp.dot(h2, wub_ref[...], preferred_element_type=F32)
        return (_silu(g) * u).astype(BF16)

    @pl.when(i < tiles_p)
    def _():
        op_ref[...] = swiglu(xp_ref[...])

    @pl.when(i == tiles_p)
    def _():
        os_ref[...] = swiglu(xs_ref[...])


def _gateup(h2_p, h2_s, wg, wu):
    m, d = h2_p.shape
    ms = h2_s.shape[0]
    f = wg.shape[1]
    tn, tm = 512, 1024
    tiles_p = m // tm
    prow = lambda i: jnp.minimum(i, tiles_p - 1)
    return pl.pallas_call(
        functools.partial(_gateup_kernel, tiles_p),
        grid=(f // tn, tiles_p + 1),
        in_specs=[pl.BlockSpec((tm, d), lambda n, i: (prow(i), 0)),
                  pl.BlockSpec((ms, d), lambda n, i: (0, 0)),
                  pl.BlockSpec((d, tn), lambda n, i: (0, n)),
                  pl.BlockSpec((d, tn), lambda n, i: (0, n))],
        out_specs=[pl.BlockSpec((tm, tn), lambda n, i: (prow(i), n)),
                   pl.BlockSpec((ms, tn), lambda n, i: (0, n))],
        out_shape=[jax.ShapeDtypeStruct((m, f), BF16), jax.ShapeDtypeStruct((ms, f), BF16)],
        scratch_shapes=[pltpu.VMEM((d, tn), BF16), pltpu.VMEM((d, tn), BF16)],
        compiler_params=_params("arbitrary", "arbitrary"),
        name="ffn_gate_up",
    )(h2_p, h2_s, wg, wu)


def _stream_block_means(t, stream_steps, steps_per_seq, page_refs, km_ref):
    pps = len(page_refs)

    @pl.when(t < stream_steps)
    def _():
        page_rows = page_refs[0].shape[0]
        ppb = MOBA_BLOCK * N_KV_HEADS // page_rows
        fold = 8 // N_KV_HEADS
        j = t % steps_per_seq
        for b in range(pps // ppb):
            s8 = jnp.sum(page_refs[b * ppb][...].reshape(page_rows // 8, 8, HEAD_DIM), axis=0)
            for p in range(1, ppb):
                s8 = s8 + jnp.sum(page_refs[b * ppb + p][...].reshape(page_rows // 8, 8, HEAD_DIM), axis=0)
            s = s8[0:N_KV_HEADS]
            for p in range(1, fold):
                s = s + s8[p * N_KV_HEADS:(p + 1) * N_KV_HEADS]
            s = s * (1.0 / MOBA_BLOCK)
            for kv in range(N_KV_HEADS):
                km_ref[0, kv, j, b:b + 1, :] = s[kv:kv + 1, :]


def _page_stream(cache_k, layer, page_table, host_steps, step_of):
    n_seq, n_pages = page_table.shape
    depth, pool, page = cache_k.shape[:3]
    ppb = MOBA_BLOCK // page
    pps = min(32, n_pages)
    steps_per_seq = n_pages // pps
    stream_steps = n_seq * steps_per_seq
    assert n_pages % pps == 0 and pps % ppb == 0 and 8 % N_KV_HEADS == 0
    assert stream_steps <= host_steps, "not enough grid steps to stream the cache"
    view = cache_k.reshape(depth, pool, page * N_KV_HEADS, HEAD_DIM)

    def stream_pos(*grid_idx):
        t = jnp.minimum(step_of(*grid_idx), stream_steps - 1)
        return t // steps_per_seq, t % steps_per_seq

    def page_spec(k):
        def index(*args):
            seq, j = stream_pos(*args[:-1])
            return layer, args[-1][seq, j * pps + k], 0, 0
        return pl.BlockSpec((None, None, page * N_KV_HEADS, HEAD_DIM), index)

    bps = pps // ppb
    km_spec = pl.BlockSpec((1, N_KV_HEADS, steps_per_seq, bps, HEAD_DIM),
                           lambda *args: (stream_pos(*args[:-1])[0], 0, 0, 0, 0))
    km_shape = jax.ShapeDtypeStruct((n_seq, N_KV_HEADS, steps_per_seq, bps, HEAD_DIM), F32)
    return view, [page_spec(k) for k in range(pps)], km_spec, km_shape, stream_steps, steps_per_seq


def _down_kernel(tiles_p, hp_ref, hs_ref, w_ref, xp_ref, xs_ref, gtp_ref, gts_ref, op_ref, os_ref, wbf_ref):
    _cast_weight_once(w_ref, wbf_ref)
    i = pl.program_id(1)

    @pl.when(i < tiles_p)
    def _():
        acc = jnp.dot(hp_ref[...], wbf_ref[...], preferred_element_type=F32)
        op_ref[...] = xp_ref[...] + gtp_ref[...] * acc

    @pl.when(i == tiles_p)
    def _():
        acc = jnp.dot(hs_ref[...], wbf_ref[...], preferred_element_type=F32)
        os_ref[...] = xs_ref[...] + gts_ref[...] * acc


def _down(h_p, h_s, w, x_p, x_s, mod_p, mod_s):
    m, f = h_p.shape
    ms = h_s.shape[0]
    d = w.shape[1]
    tn, tm = 512, 512
    tiles_p = m // tm
    nb = d // tn
    assert mod_p.shape[0] == 1 and mod_s.shape[0] == ms
    prow = lambda i: jnp.minimum(i, tiles_p - 1)
    return pl.pallas_call(
        functools.partial(_down_kernel, tiles_p),
        grid=(nb, tiles_p + 1),
        in_specs=[pl.BlockSpec((tm, f), lambda n, i: (prow(i), 0)),
                  pl.BlockSpec((ms, f), lambda n, i: (0, 0)),
                  pl.BlockSpec((f, tn), lambda n, i: (0, n)),
                  pl.BlockSpec((tm, tn), lambda n, i: (prow(i), n)),
                  pl.BlockSpec((ms, tn), lambda n, i: (0, n)),
                  pl.BlockSpec((1, tn), lambda n, i: (0, 5 * nb + n)),
                  pl.BlockSpec((ms, tn), lambda n, i: (0, 5 * nb + n))],
        out_specs=[pl.BlockSpec((tm, tn), lambda n, i: (prow(i), n)),
                   pl.BlockSpec((ms, tn), lambda n, i: (0, n))],
        out_shape=[jax.ShapeDtypeStruct((m, d), F32), jax.ShapeDtypeStruct((ms, d), F32)],
        scratch_shapes=[pltpu.VMEM((f, tn), BF16)],
        compiler_params=_params("arbitrary", "arbitrary"),
        name="ffn_down",
    )(h_p, h_s, w, x_p, x_s, mod_p, mod_s)


def _head_norm(x, gain):
    return x * lax.rsqrt(jnp.mean(x * x, axis=-1, keepdims=True) + EPS) * gain


def _top3(g, idx_f, axis):
    sel = jnp.zeros(g.shape, F32)
    firsts = []
    for _ in range(MOBA_TOPK):
        mx = jnp.max(g, axis=axis, keepdims=True)
        ismax = jnp.logical_and(g == mx, mx > -jnp.inf)
        first = jnp.min(jnp.where(ismax, idx_f, float(LANES)), axis=axis, keepdims=True)
        pick = idx_f == first
        sel = jnp.where(pick, 1.0, sel)
        g = jnp.where(pick, -jnp.inf, g)
        firsts.append(first)
    return sel, firsts


def _prompt_prep_kernel(proj_ref, qg_ref, kg_ref, kout_ref, vout_ref, kp_ref, vp_ref, qp_ref, km_ref):
    i = pl.program_id(0)
    blk = MOBA_BLOCK

    @pl.when(i == 0)
    def _():
        km_ref[...] = jnp.zeros_like(km_ref)

    lane = lax.broadcasted_iota(jnp.int32, (blk, LANES), 1)
    row = lax.broadcasted_iota(jnp.int32, (blk, LANES), 0)
    sq_row = lax.broadcasted_iota(jnp.int32, (LANES, LANES), 0)
    i_f = i.astype(F32)
    row_f = row.astype(F32)

    kfeat = jnp.where(lane == i, 1.0, 0.0)
    kfeat = jnp.where(lane == FEAT_KBLK, i_f, kfeat)
    kfeat = jnp.where(lane == FEAT_KOFF, row_f, kfeat)
    kfeat = jnp.where(jnp.logical_or(lane == FEAT_ONE_A, lane == FEAT_ONE_B), 1.0, kfeat).astype(BF16)
    vfeat = jnp.ones((blk, LANES), BF16)

    kg = kg_ref[...]
    for kv in range(N_KV_HEADS):
        k = proj_ref[:, ATT_WIDTH + kv * HEAD_DIM:ATT_WIDTH + (kv + 1) * HEAD_DIM]
        kn = _head_norm(k, kg)
        kout_ref[:, kv * HEAD_DIM:(kv + 1) * HEAD_DIM] = kn
        kp_ref[kv, :, :HEAD_DIM] = kn.astype(BF16)
        kp_ref[kv, :, HEAD_DIM:] = kfeat
        ksum = jnp.sum(kn, axis=0, keepdims=True) * (1.0 / blk)
        km_ref[kv] = jnp.where(sq_row == i, jnp.broadcast_to(ksum, (LANES, LANES)), km_ref[kv])
        v = proj_ref[:, ATT_WIDTH + KV_WIDTH + kv * HEAD_DIM:ATT_WIDTH + KV_WIDTH + (kv + 1) * HEAD_DIM]
        vout_ref[:, kv * HEAD_DIM:(kv + 1) * HEAD_DIM] = v
        vp_ref[kv, :, :HEAD_DIM] = v.astype(BF16)
        vp_ref[kv, :, HEAD_DIM:] = vfeat

    qg = qg_ref[...]
    blk_id = lax.broadcasted_iota(jnp.int32, (LANES, blk), 0)
    blk_id_f = blk_id.astype(F32)
    valid = blk_id < i
    for h in range(ATT_HEADS):
        q = proj_ref[:, h * HEAD_DIM:(h + 1) * HEAD_DIM]
        qn = _head_norm(q, qg)
        gate = _dot3(km_ref[h // KV_GROUP], qn, NT_DIMS)
        sel_t, _ = _top3(jnp.where(valid, gate, -jnp.inf), blk_id_f, 0)
        sel = sel_t.T
        slope = _alibi_slope(h)
        qfeat = jnp.where(jnp.logical_and(lane < FEAT_KBLK, sel == 0.0), NEG_BIG, 0.0)
        qfeat = jnp.where(lane == FEAT_KBLK, slope * blk, qfeat)
        qfeat = jnp.where(lane == FEAT_KOFF, slope, qfeat)
        qfeat = jnp.where(lane == FEAT_ONE_A, -(slope * blk) * i_f, qfeat)
        qfeat = jnp.where(lane == FEAT_ONE_B, -slope * row_f, qfeat)
        qp_ref[h, :, :HEAD_DIM] = (qn * ATT_SCALE).astype(BF16)
        qp_ref[h, :, HEAD_DIM:] = qfeat.astype(BF16)


def _prompt_prep(proj, q_gain, k_gain):
    m = proj.shape[0]
    nb = m // MOBA_BLOCK
    assert m % MOBA_BLOCK == 0 and nb <= FEAT_KBLK
    blk = MOBA_BLOCK
    qkv_w = ATT_WIDTH + 2 * KV_WIDTH
    return pl.pallas_call(
        _prompt_prep_kernel,
        grid=(nb,),
        in_specs=[pl.BlockSpec((blk, qkv_w), lambda i: (i, 0)),
                  pl.BlockSpec((1, HEAD_DIM), lambda i: (0, 0)),
                  pl.BlockSpec((1, HEAD_DIM), lambda i: (0, 0))],
        out_specs=[pl.BlockSpec((blk, KV_WIDTH), lambda i: (i, 0)),
                   pl.BlockSpec((blk, KV_WIDTH), lambda i: (i, 0)),
                   pl.BlockSpec((N_KV_HEADS, blk, 2 * HEAD_DIM), lambda i: (0, i, 0)),
                   pl.BlockSpec((N_KV_HEADS, blk, 2 * HEAD_DIM), lambda i: (0, i, 0)),
                   pl.BlockSpec((ATT_HEADS, blk, 2 * HEAD_DIM), lambda i: (0, i, 0))],
        out_shape=[jax.ShapeDtypeStruct((m, KV_WIDTH), F32),
                   jax.ShapeDtypeStruct((m, KV_WIDTH), F32),
                   jax.ShapeDtypeStruct((N_KV_HEADS, m, 2 * HEAD_DIM), BF16),
                   jax.ShapeDtypeStruct((N_KV_HEADS, m, 2 * HEAD_DIM), BF16),
                   jax.ShapeDtypeStruct((ATT_HEADS, m, 2 * HEAD_DIM), BF16)],
        scratch_shapes=[pltpu.VMEM((N_KV_HEADS, LANES, LANES), F32)],
        compiler_params=_params("arbitrary"),
        name="prompt_qk_prep",
    )(proj, q_gain.reshape(1, HEAD_DIM), k_gain.reshape(1, HEAD_DIM))


ATTN_UNROLL = 4
LOG2E = 1.4426950408889634


def _prompt_attn_kernel(pps, stream_steps, steps_per_seq, pt_ref, q_ref, k_ref, v_ref, *rest):
    page_refs = rest[:pps]
    o_ref, km_ref, s_ref, acc_ref, m_ref = rest[pps:]
    i = pl.program_id(1)
    _stream_block_means(pl.program_id(0) * pl.num_programs(1) + i, stream_steps, steps_per_seq, page_refs, km_ref)
    blk = MOBA_BLOCK
    rows = KV_GROUP * blk
    span = ATTN_UNROLL * blk
    qs = q_ref[...].reshape(rows, 2 * HEAD_DIM)

    def lane_fold(s):
        out = s[:, :LANES]
        for t in range(1, s.shape[1] // LANES):
            out = jnp.maximum(out, s[:, t * LANES:(t + 1) * LANES])
        return out

    def probs(s):
        mb = m_ref[...]
        return jnp.concatenate([jnp.exp2(s[:, t * LANES:(t + 1) * LANES] - mb) for t in range(s.shape[1] // LANES)],
                               axis=1).astype(BF16)

    own = pl.ds(pl.multiple_of(i * blk, blk), blk)
    r = lax.broadcasted_iota(jnp.int32, (rows, blk), 0)
    c = lax.broadcasted_iota(jnp.int32, (rows, blk), 1)
    dist = jnp.bitwise_and(r, blk - 1) - c
    slope = qs[:, HEAD_DIM + FEAT_KOFF:HEAD_DIM + FEAT_KOFF + 1].astype(F32)
    s_own = lax.dot_general(qs[:, :HEAD_DIM], k_ref[own, :HEAD_DIM], NT_DIMS, preferred_element_type=F32)
    s_own = jnp.where(dist >= 0, (s_own - slope * dist.astype(F32)) * LOG2E, -jnp.inf)
    m_ref[...] = lane_fold(s_own)

    trips = (i + ATTN_UNROLL - 1) // ATTN_UNROLL

    def two_trips_per_iteration(trip):
        def pair(u, carry):
            trip(2 * u)
            trip(2 * u + 1)
            return carry

        lax.fori_loop(0, trips // 2, pair, 0)

        @pl.when(trips % 2 == 1)
        def _():
            trip(trips - 1)

    def pass1(t):
        ks = k_ref[pl.ds(pl.multiple_of(t * span, span), span), :]
        s = lax.dot_general(qs, ks, NT_DIMS, preferred_element_type=F32) * LOG2E
        s_ref[t] = s
        m_ref[...] = jnp.maximum(m_ref[...], lane_fold(s))

    two_trips_per_iteration(pass1)
    m_ref[...] = jnp.broadcast_to(jnp.max(m_ref[...], axis=1, keepdims=True), (rows, LANES))

    acc_ref[...] = jnp.dot(probs(s_own), v_ref[own, :], preferred_element_type=F32)

    def pass2(t):
        vs = v_ref[pl.ds(pl.multiple_of(t * span, span), span), :]
        acc_ref[...] += jnp.dot(probs(s_ref[t]), vs, preferred_element_type=F32)

    two_trips_per_iteration(pass2)
    acc = acc_ref[...]
    o = acc[:, :HEAD_DIM] / acc[:, HEAD_DIM:]
    for g in range(KV_GROUP):
        o_ref[:, g * HEAD_DIM:(g + 1) * HEAD_DIM] = o[g * blk:(g + 1) * blk]


def _prompt_attn(qp, kp, vp, cache_k, layer, page_table):
    m = kp.shape[1]
    nb = m // MOBA_BLOCK
    assert nb % ATTN_UNROLL == 0
    blk = MOBA_BLOCK
    rows = KV_GROUP * blk
    view, page_specs, km_spec, km_shape, stream_steps, steps_per_seq = _page_stream(
        cache_k, layer, page_table, N_KV_HEADS * nb, lambda kv, i: kv * nb + i)
    resident = lambda: pl.BlockSpec((None, m, 2 * HEAD_DIM), lambda kv, i, pt: (kv, 0, 0), pipeline_mode=pl.Buffered(1))
    grid_spec = pltpu.PrefetchScalarGridSpec(
        num_scalar_prefetch=1,
        grid=(N_KV_HEADS, nb),
        in_specs=[pl.BlockSpec((KV_GROUP, blk, 2 * HEAD_DIM), lambda kv, i, pt: (kv, i, 0)), resident(), resident()]
        + page_specs,
        out_specs=[pl.BlockSpec((blk, KV_GROUP * HEAD_DIM), lambda kv, i, pt: (i, kv)), km_spec],
        scratch_shapes=[pltpu.VMEM((nb // ATTN_UNROLL, rows, ATTN_UNROLL * blk), F32),
                        pltpu.VMEM((rows, 2 * HEAD_DIM), F32),
                        pltpu.VMEM((rows, LANES), F32)],
    )
    o, km = pl.pallas_call(
        functools.partial(_prompt_attn_kernel, len(page_specs), stream_steps, steps_per_seq),
        grid_spec=grid_spec,
        out_shape=[jax.ShapeDtypeStruct((m, ATT_WIDTH), F32), km_shape],
        compiler_params=_params("arbitrary", "arbitrary"),
        name="prompt_moba_attn",
    )(page_table, qp, kp, vp, *([view] * len(page_specs)))
    return o, km.reshape(km.shape[0], N_KV_HEADS, -1, HEAD_DIM)


def _expand_heads(v, lane_lo):
    r = v.shape[0]
    parts = []
    for k in range(SSM_HEADS // 2):
        a0 = jnp.broadcast_to(v[:, 2 * k:2 * k + 1], (r, LANES))
        a1 = jnp.broadcast_to(v[:, 2 * k + 1:2 * k + 2], (r, LANES))
        parts.append(jnp.where(lane_lo, a0, a1))
    return jnp.concatenate(parts, axis=1)


def _gated_group_norm(y, z, g):
    yz = y * _silu(z)
    gw = D_INNER // SSM_GROUPS
    outs = []
    for grp in range(SSM_GROUPS):
        t = yz[:, grp * gw:(grp + 1) * gw]
        t = t * lax.rsqrt(jnp.mean(t * t, axis=-1, keepdims=True) + EPS)
        outs.append(t * g[:, grp * gw:(grp + 1) * gw])
    return jnp.concatenate(outs, axis=1)


def _ssd_kernel(z_ref, xbc_ref, dt_ref, cw_ref, cb_ref, dtb_ref, alog_ref, dsk_ref, g_ref,
                y_ref, tail_out_ref, st_out_ref, tail_ref, st_ref):
    c = pl.program_id(0)
    cs = SSD_CHUNK
    gw = D_INNER // SSM_GROUPS

    @pl.when(c == 0)
    def _():
        tail_ref[...] = jnp.zeros_like(tail_ref)
        st_ref[...] = jnp.zeros_like(st_ref)

    xr = xbc_ref[...]
    xp = jnp.concatenate([tail_ref[...], xr], axis=0)
    cw = cw_ref[...]
    conv = cb_ref[...] + cw[3:4] * xr
    for t in range(CONV_WIDTH - 1):
        conv = conv + cw[t:t + 1] * xp[8 - (CONV_WIDTH - 1) + t:8 - (CONV_WIDTH - 1) + t + cs]
    tail_ref[...] = xr[cs - 8:]
    tail_out_ref[...] = xr[cs - 8:]
    xc = _silu(conv)
    xs = xc[:, :D_INNER]
    bm = xc[:, D_INNER:D_INNER + SSM_GROUPS * D_STATE]
    cm = xc[:, D_INNER + SSM_GROUPS * D_STATE:]

    lane = lax.broadcasted_iota(jnp.int32, (cs, LANES), 1)
    rowi = lax.broadcasted_iota(jnp.int32, (cs, LANES), 0)
    lane_lo = lane < SSM_HEAD_DIM
    tri = rowi >= lane

    dt = _softplus(dt_ref[...] + dtb_ref[...])
    a = jnp.where(lane[:1] < SSM_HEADS, -jnp.exp(alog_ref[...]), 0.0)
    da = dt * a
    tril = jnp.where(tri, 1.0, 0.0).astype(BF16)
    p1 = da.astype(BF16)
    r1 = da - p1.astype(F32)
    p2 = r1.astype(BF16)
    p3 = (r1 - p2.astype(F32)).astype(BF16)
    acum = (jnp.dot(tril, p1, preferred_element_type=F32) + jnp.dot(tril, p2, preferred_element_type=F32)
            + jnp.dot(tril, p3, preferred_element_type=F32))
    acum_t = acum.T

    dt_e = _expand_heads(dt, lane_lo)
    ac_e = _expand_heads(acum, lane_lo)
    xdt = xs * dt_e
    ea_e = jnp.exp(ac_e)
    dend_e = jnp.exp(ac_e[cs - 1:cs, :] - ac_e)
    cdec = ea_e[cs - 1:cs, :]
    xdt_bf = xdt.astype(BF16)
    xdec_bf = (xdt * dend_e).astype(BF16)

    y_parts = []
    for grp in range(SSM_GROUPS):
        bg = bm[:, grp * D_STATE:(grp + 1) * D_STATE]
        cg = cm[:, grp * D_STATE:(grp + 1) * D_STATE].astype(BF16)
        cb = lax.dot_general(cg, bg.astype(BF16), NT_DIMS, preferred_element_type=F32)
        hpg = SSM_HEADS // SSM_GROUPS
        intra = []
        for k in range(hpg // 2):
            pair = grp * (hpg // 2) + k
            xpair = xdt_bf[:, pair * LANES:(pair + 1) * LANES]
            acc = None
            for hh in range(2):
                h = 2 * pair + hh
                seg = jnp.broadcast_to(acum[:, h:h + 1], (cs, cs)) - acum_t[h:h + 1, :]
                lmat = jnp.exp(jnp.where(tri, seg, -jnp.inf))
                mh = (cb * lmat).astype(BF16)
                xh = jnp.where(lane_lo if hh == 0 else jnp.logical_not(lane_lo), xpair, jnp.zeros_like(xpair))
                part = jnp.dot(mh, xh, preferred_element_type=F32)
                acc = part if acc is None else acc + part
            intra.append(acc)
        y_intra = jnp.concatenate(intra, axis=1)
        st = st_ref[grp]
        y_inter = jnp.dot(cg, st.astype(BF16), preferred_element_type=F32) * ea_e[:, grp * gw:(grp + 1) * gw]
        new_st = cdec[:, grp * gw:(grp + 1) * gw] * st + jnp.dot(
            bg.T.astype(BF16), xdec_bf[:, grp * gw:(grp + 1) * gw], preferred_element_type=F32)
        st_ref[grp] = new_st
        st_out_ref[grp] = new_st
        y_parts.append(y_intra + y_inter)
    y = jnp.concatenate(y_parts, axis=1) + dsk_ref[...] * xs
    y_ref[...] = _gated_group_norm(y, z_ref[...], g_ref[...]).astype(y_ref.dtype)


def _pad_lanes(v):
    return jnp.pad(v.reshape(1, -1), ((0, 0), (0, LANES - v.size)))


def _ssd_prompt(proj, dtp, conv_w, conv_b, dt_bias, a_log, d_skip, g_ssm):
    m = proj.shape[0]
    cs = SSD_CHUNK
    assert m % cs == 0
    gw = D_INNER // SSM_GROUPS
    z_blk = (ATT_WIDTH + 2 * KV_WIDTH) // D_INNER
    x_blk = (ATT_WIDTH + 2 * KV_WIDTH + D_INNER) // CONV_DIM
    assert z_blk * D_INNER == ATT_WIDTH + 2 * KV_WIDTH and x_blk * CONV_DIM == ATT_WIDTH + 2 * KV_WIDTH + D_INNER
    const = lambda shape: pl.BlockSpec(shape, lambda c: tuple(0 for _ in shape))
    return pl.pallas_call(
        _ssd_kernel,
        grid=(m // cs,),
        in_specs=[pl.BlockSpec((cs, D_INNER), lambda c: (c, z_blk)),
                  pl.BlockSpec((cs, CONV_DIM), lambda c: (c, x_blk)),
                  pl.BlockSpec((cs, LANES), lambda c: (c, 0)),
                  const((CONV_WIDTH, CONV_DIM)), const((1, CONV_DIM)), const((1, LANES)), const((1, LANES)),
                  const((1, D_INNER)), const((1, D_INNER))],
        out_specs=[pl.BlockSpec((cs, D_INNER), lambda c: (c, 0)),
                   const((8, CONV_DIM)), const((SSM_GROUPS, D_STATE, gw))],
        out_shape=[jax.ShapeDtypeStruct((m, D_INNER), BF16),
                   jax.ShapeDtypeStruct((8, CONV_DIM), F32),
                   jax.ShapeDtypeStruct((SSM_GROUPS, D_STATE, gw), F32)],
        scratch_shapes=[pltpu.VMEM((8, CONV_DIM), F32), pltpu.VMEM((SSM_GROUPS, D_STATE, gw), F32)],
        compiler_params=_params("arbitrary"),
        name="prompt_ssd",
    )(proj, proj, dtp, conv_w, conv_b.reshape(1, CONV_DIM), _pad_lanes(dt_bias), _pad_lanes(a_log),
      jnp.repeat(d_skip, SSM_HEAD_DIM).reshape(1, D_INNER), g_ssm.reshape(1, D_INNER))


def _sample_prep_kernel(proj_ref, qg_ref, kg_ref, q_ref, k_ref, v_ref):
    for h in range(ATT_HEADS):
        q_ref[:, h * HEAD_DIM:(h + 1) * HEAD_DIM] = _head_norm(proj_ref[:, h * HEAD_DIM:(h + 1) * HEAD_DIM], qg_ref[...])
    for kv in range(N_KV_HEADS):
        lo = ATT_WIDTH + kv * HEAD_DIM
        k_ref[:, kv * HEAD_DIM:(kv + 1) * HEAD_DIM] = _head_norm(proj_ref[:, lo:lo + HEAD_DIM], kg_ref[...])
    v_ref[...] = proj_ref[:, ATT_WIDTH + KV_WIDTH:ATT_WIDTH + 2 * KV_WIDTH]


def _sample_prep(proj, q_gain, k_gain):
    n = proj.shape[0]
    qkv_w = ATT_WIDTH + 2 * KV_WIDTH
    return pl.pallas_call(
        _sample_prep_kernel,
        grid=(1,),
        in_specs=[pl.BlockSpec((n, qkv_w), lambda i: (0, 0)),
                  pl.BlockSpec((1, HEAD_DIM), lambda i: (0, 0)),
                  pl.BlockSpec((1, HEAD_DIM), lambda i: (0, 0))],
        out_specs=[pl.BlockSpec((n, ATT_WIDTH), lambda i: (0, 0)),
                   pl.BlockSpec((n, KV_WIDTH), lambda i: (0, 0)),
                   pl.BlockSpec((n, KV_WIDTH), lambda i: (0, 0))],
        out_shape=[jax.ShapeDtypeStruct((n, ATT_WIDTH), F32),
                   jax.ShapeDtypeStruct((n, KV_WIDTH), F32),
                   jax.ShapeDtypeStruct((n, KV_WIDTH), F32)],
        compiler_params=_params("arbitrary"),
        name="sample_qk_prep",
    )(proj, q_gain.reshape(1, HEAD_DIM), k_gain.reshape(1, HEAD_DIM))


def _sample_topk_kernel(q_ref, km_ref, idx_ref):
    nblk = km_ref.shape[2]
    q = q_ref[0]
    hrow = lax.broadcasted_iota(jnp.int32, (ATT_HEADS, nblk), 0)
    gate = jnp.zeros((ATT_HEADS, nblk), F32)
    for kv in range(N_KV_HEADS):
        gk = _dot3(q, km_ref[0, kv], NT_DIMS)
        gate = jnp.where(hrow // KV_GROUP == kv, gk, gate)
    lane = lax.broadcasted_iota(jnp.int32, (ATT_HEADS, LANES), 1)
    if nblk < LANES:
        gate = jnp.concatenate([gate, jnp.full((ATT_HEADS, LANES - nblk), -jnp.inf, F32)], axis=1)
    _, firsts = _top3(gate, lane.astype(F32), 1)
    out = jnp.zeros((ATT_HEADS, LANES), jnp.int32)
    for t, first in enumerate(firsts):
        out = jnp.where(lane == t, first.astype(jnp.int32), out)
    idx_ref[0] = out


def _sample_topk(q3, kmean):
    n, _, nblk, _ = kmean.shape
    assert MOBA_TOPK <= nblk <= LANES
    return pl.pallas_call(
        _sample_topk_kernel,
        grid=(n,),
        in_specs=[pl.BlockSpec((1, ATT_HEADS, HEAD_DIM), lambda s: (s, 0, 0)),
                  pl.BlockSpec((1, N_KV_HEADS, nblk, HEAD_DIM), lambda s: (s, 0, 0, 0))],
        out_specs=pl.BlockSpec((1, ATT_HEADS, LANES), lambda s: (s, 0, 0)),
        out_shape=jax.ShapeDtypeStruct((n, ATT_HEADS, LANES), jnp.int32),
        compiler_params=_params("arbitrary"),
        name="sample_gate_topk",
    )(q3, kmean)


def _sample_attn_kernel(past, ppb, layer, pt_ref, idx_ref, q_ref, kn_ref, vn_ref, ck_ref, cv_ref, o_ref,
                        kbuf, vbuf, sems):
    s = pl.program_id(0)
    page = MOBA_BLOCK // ppb
    slot = s % 2

    def copies(seq, buf, h, t, p):
        kv = h // KV_GROUP
        blk = idx_ref[seq, h * MOBA_TOPK + t]
        phys = pt_ref[seq, blk * ppb + p]
        dst = pl.ds((t * ppb + p) * page, page)
        return (pltpu.make_async_copy(ck_ref.at[layer, phys, :, kv, :], kbuf.at[buf, h, dst, :],
                                      sems.at[buf, 0, h, t * ppb + p]),
                pltpu.make_async_copy(cv_ref.at[layer, phys, :, kv, :], vbuf.at[buf, h, dst, :],
                                      sems.at[buf, 1, h, t * ppb + p]))

    triples = [(h, t, p) for h in range(ATT_HEADS) for t in range(MOBA_TOPK) for p in range(ppb)]

    def start_all(seq, buf):
        for h, t, p in triples:
            ck, cv = copies(seq, buf, h, t, p)
            ck.start()
            cv.start()

    @pl.when(s == 0)
    def _():
        start_all(0, 0)

    @pl.when(s + 1 < pl.num_programs(0))
    def _():
        start_all(s + 1, 1 - slot)

    for h, t, p in triples:
        ck, cv = copies(s, slot, h, t, p)
        ck.wait()
        cv.wait()

    nsel = MOBA_TOPK * MOBA_BLOCK
    rowi = lax.broadcasted_iota(jnp.int32, (nsel, 1), 0)
    off = jnp.bitwise_and(rowi, MOBA_BLOCK - 1)
    for h in range(ATT_HEADS):
        kv = h // KV_GROUP
        slope = _alibi_slope(h)
        q = q_ref[0, h:h + 1, :]
        sc = jnp.sum(kbuf[slot, h] * q, axis=1, keepdims=True) * ATT_SCALE
        pos = jnp.zeros((nsel, 1), jnp.int32)
        for t in range(MOBA_TOPK):
            pos = jnp.where(rowi // MOBA_BLOCK == t, idx_ref[s, h * MOBA_TOPK + t] * MOBA_BLOCK, pos)
        dist = (past - (pos + off)).astype(F32)
        sc = sc - slope * dist
        s_own = jnp.sum(kn_ref[0, kv:kv + 1, :] * q, axis=1, keepdims=True) * ATT_SCALE
        mx = jnp.maximum(jnp.max(sc, axis=0, keepdims=True), s_own)
        p = jnp.exp(sc - mx)
        p_own = jnp.exp(s_own - mx)
        denom = jnp.sum(p, axis=0, keepdims=True) + p_own
        num = jnp.sum(p * vbuf[slot, h], axis=0, keepdims=True) + p_own * vn_ref[0, kv:kv + 1, :]
        o_ref[0, h:h + 1, :] = num / denom


def _sample_attn(q3, k_new3, v_new3, cache_k, cache_v, layer, page_table, idx):
    n, n_pages = page_table.shape
    page = cache_k.shape[2]
    ppb = MOBA_BLOCK // page
    past = n_pages * page
    assert past % MOBA_BLOCK == 0
    nsel = MOBA_TOPK * MOBA_BLOCK
    grid_spec = pltpu.PrefetchScalarGridSpec(
        num_scalar_prefetch=2,
        grid=(n,),
        in_specs=[pl.BlockSpec((1, ATT_HEADS, HEAD_DIM), lambda s, pt, ix: (s, 0, 0)),
                  pl.BlockSpec((1, N_KV_HEADS, HEAD_DIM), lambda s, pt, ix: (s, 0, 0)),
                  pl.BlockSpec((1, N_KV_HEADS, HEAD_DIM), lambda s, pt, ix: (s, 0, 0)),
                  pl.BlockSpec(memory_space=pl.ANY),
                  pl.BlockSpec(memory_space=pl.ANY)],
        out_specs=pl.BlockSpec((1, ATT_HEADS, HEAD_DIM), lambda s, pt, ix: (s, 0, 0)),
        scratch_shapes=[pltpu.VMEM((2, ATT_HEADS, nsel, HEAD_DIM), F32),
                        pltpu.VMEM((2, ATT_HEADS, nsel, HEAD_DIM), F32),
                        pltpu.SemaphoreType.DMA((2, 2, ATT_HEADS, MOBA_TOPK * ppb))],
    )
    return pl.pallas_call(
        functools.partial(_sample_attn_kernel, past, ppb, layer),
        grid_spec=grid_spec,
        out_shape=jax.ShapeDtypeStruct((n, ATT_HEADS, HEAD_DIM), F32),
        compiler_params=_params("arbitrary"),
        name="sample_moba_attn",
    )(page_table, idx, q3, k_new3, v_new3, cache_k, cache_v)


def _sample_ssd_kernel(z_ref, xbc_ref, dt_ref, buf_ref, h0_ref, cw_ref, cb_ref, dtb_ref, alog_ref, dsk_ref, g_ref,
                       y_ref, buf_out_ref, h_out_ref):
    x = xbc_ref[0]
    buf = buf_ref[0]
    cw = cw_ref[...]
    conv = cb_ref[...] + cw[CONV_WIDTH - 1:CONV_WIDTH] * x
    for t in range(CONV_WIDTH - 1):
        conv = conv + cw[t:t + 1] * buf[t:t + 1]
    buf_out_ref[0, 0:CONV_WIDTH - 2, :] = buf[1:CONV_WIDTH - 1]
    buf_out_ref[0, CONV_WIDTH - 2:CONV_WIDTH - 1, :] = x
    xc = _silu(conv)
    xs = xc[:, :D_INNER]
    bm = xc[:, D_INNER:D_INNER + SSM_GROUPS * D_STATE]
    cm = xc[:, D_INNER + SSM_GROUPS * D_STATE:]

    lane1 = lax.broadcasted_iota(jnp.int32, (1, LANES), 1)
    rowi = lax.broadcasted_iota(jnp.int32, (LANES, LANES), 0)
    dt = _softplus(dt_ref[0] + dtb_ref[...])
    a = jnp.where(lane1 < SSM_HEADS, -jnp.exp(alog_ref[...]), 0.0)
    dec = jnp.exp(dt * a)
    dt_e = _expand_heads(dt, lane1 < SSM_HEAD_DIM)
    xdt = xs * dt_e
    xdt_rows = jnp.broadcast_to(xdt, (LANES, D_INNER))

    hpg = SSM_HEADS // SSM_GROUPS
    y_parts = []
    for pair in range(SSM_HEADS // 2):
        grp = (2 * pair) // hpg
        xcol = xdt_rows[:, pair * LANES:(pair + 1) * LANES].T
        dcol = jnp.where(rowi < SSM_HEAD_DIM,
                         jnp.broadcast_to(dec[:, 2 * pair:2 * pair + 1], (LANES, LANES)),
                         jnp.broadcast_to(dec[:, 2 * pair + 1:2 * pair + 2], (LANES, LANES)))
        h0 = h0_ref[0, 2 * pair:2 * pair + 2].reshape(LANES, D_STATE)
        hn = dcol * h0 + xcol * bm[:, grp * D_STATE:(grp + 1) * D_STATE]
        h_out_ref[0, 2 * pair:2 * pair + 2] = hn.reshape(2, SSM_HEAD_DIM, D_STATE)
        cgrow = jnp.broadcast_to(cm[:, grp * D_STATE:(grp + 1) * D_STATE], (8, D_STATE))
        ypair = _dot3(cgrow, hn, NT_DIMS)
        y_parts.append(ypair[0:1])
    y = jnp.concatenate(y_parts, axis=1) + dsk_ref[...] * xs
    y_ref[0] = _gated_group_norm(y, z_ref[0], g_ref[...]).astype(y_ref.dtype)


def _ssd_sample(proj, dtp, state_conv, state_ssm, conv_w, conv_b, dt_bias, a_log, d_skip, g_ssm):
    n = proj.shape[0]
    z0 = ATT_WIDTH + 2 * KV_WIDTH
    z3 = proj[:, z0:z0 + D_INNER].reshape(n, 1, D_INNER)
    x3 = proj[:, z0 + D_INNER:z0 + D_INNER + CONV_DIM].reshape(n, 1, CONV_DIM)
    dt3 = dtp.reshape(n, 1, LANES)
    const = lambda shape: pl.BlockSpec(shape, lambda s: tuple(0 for _ in shape))
    per_seq = lambda shape: pl.BlockSpec((1,) + shape, lambda s: (s,) + tuple(0 for _ in shape))
    y, buf, h = pl.pallas_call(
        _sample_ssd_kernel,
        grid=(n,),
        in_specs=[per_seq((1, D_INNER)), per_seq((1, CONV_DIM)), per_seq((1, LANES)),
                  per_seq((CONV_WIDTH - 1, CONV_DIM)), per_seq((SSM_HEADS, SSM_HEAD_DIM, D_STATE)),
                  const((CONV_WIDTH, CONV_DIM)), const((1, CONV_DIM)), const((1, LANES)), const((1, LANES)),
                  const((1, D_INNER)), const((1, D_INNER))],
        out_specs=[per_seq((1, D_INNER)), per_seq((CONV_WIDTH - 1, CONV_DIM)),
                   per_seq((SSM_HEADS, SSM_HEAD_DIM, D_STATE))],
        out_shape=[jax.ShapeDtypeStruct((n, 1, D_INNER), BF16),
                   jax.ShapeDtypeStruct((n, CONV_WIDTH - 1, CONV_DIM), F32),
                   jax.ShapeDtypeStruct((n, SSM_HEADS, SSM_HEAD_DIM, D_STATE), F32)],
        compiler_params=_params("arbitrary"),
        name="sample_ssd",
    )(z3, x3, dt3, state_conv, state_ssm, conv_w, conv_b.reshape(1, CONV_DIM), _pad_lanes(dt_bias),
      _pad_lanes(a_log), jnp.repeat(d_skip, SSM_HEAD_DIM).reshape(1, D_INNER), g_ssm.reshape(1, D_INNER))
    return y.reshape(n, D_INNER), buf, h


def kernel(x_prompt, x_sample, cache_k, cache_v, state_conv, state_ssm, page_table, c_prompt, c_sample, w_ada, b_ada, g_mix_norm, w_in, q_gain, k_gain, g_att_out, conv_w, conv_b, dt_bias, a_log, d_skip, g_ssm_out, w_out, g_ffn_norm, w_gate, w_up, w_down):
    n_p, seq, d = x_prompt.shape
    n_s, dec_seq, _ = x_sample.shape
    assert n_p == 1 and dec_seq == 1
    depth = w_ada.shape[0]
    main_w = ATT_WIDTH + 2 * KV_WIDTH + D_INNER + CONV_DIM

    yp = x_prompt.reshape(seq, d)
    ys = x_sample.reshape(n_s, d)
    c_rows = n_p + n_s
    c_pad = -(-c_rows // 16) * 16
    c_all = jnp.pad(jnp.concatenate([c_prompt, c_sample], axis=0), ((0, c_pad - c_rows), (0, 0)))
    outs = [[] for _ in range(8)]
    for l in range(depth):
        mod = _ada(c_all, w_ada[l], b_ada[l])
        mod_p, mod_s = mod[0:1], mod[1:1 + n_s]
        w_in_t = jnp.swapaxes(w_in[l], 0, 1)

        proj, dtp = _inproj(yp, g_mix_norm[l], mod_p, w_in_t, main_w, tn=1536, tm=512, name="in_proj")
        k_out, v_out, kp, vp, qp = _prompt_prep(proj, q_gain[l], k_gain[l])
        o_att, kmean_s = _prompt_attn(qp, kp, vp, cache_k, l, page_table)
        y_ssm, tail, st = _ssd_prompt(proj, dtp, conv_w[l], conv_b[l], dt_bias[l], a_log[l], d_skip[l], g_ssm_out[l])
        x1_p, h2_p = _outproj(o_att, g_att_out[l], y_ssm, w_out[l], yp, mod_p, g_ffn_norm[l])
        hpg = SSM_HEADS // SSM_GROUPS
        ssm_p = st.reshape(SSM_GROUPS, D_STATE, hpg, SSM_HEAD_DIM).transpose(0, 2, 3, 1).reshape(
            1, SSM_HEADS, SSM_HEAD_DIM, D_STATE)
        outs[0].append(k_out.reshape(1, seq, N_KV_HEADS, HEAD_DIM))
        outs[1].append(v_out.reshape(1, seq, N_KV_HEADS, HEAD_DIM))
        outs[2].append(tail[8 - (CONV_WIDTH - 1):].reshape(1, CONV_WIDTH - 1, CONV_DIM))
        outs[3].append(ssm_p)

        proj_s, dts = _inproj(ys, g_mix_norm[l], mod_s, w_in_t, main_w, tn=768, tm=512, name="in_proj_sample",
                              split=True)
        q_s, k_s, v_s = _sample_prep(proj_s, q_gain[l], k_gain[l])
        q3 = q_s.reshape(n_s, ATT_HEADS, HEAD_DIM)
        idx = _sample_topk(q3, kmean_s)
        idx_flat = idx[:, :, :MOBA_TOPK].reshape(n_s, ATT_HEADS * MOBA_TOPK)
        o_s = _sample_attn(q3, k_s.reshape(n_s, N_KV_HEADS, HEAD_DIM), v_s.reshape(n_s, N_KV_HEADS, HEAD_DIM),
                           cache_k, cache_v, l, page_table, idx_flat)
        y_s, buf_s, h_s = _ssd_sample(proj_s, dts, state_conv[l], state_ssm[l], conv_w[l], conv_b[l], dt_bias[l],
                                      a_log[l], d_skip[l], g_ssm_out[l])
        x1_s, h2_s = _outproj(o_s.reshape(n_s, ATT_WIDTH), g_att_out[l], y_s, w_out[l], ys, mod_s, g_ffn_norm[l])

        hid_p, hid_s = _gateup(h2_p, h2_s, w_gate[l], w_up[l])
        yp, ys = _down(hid_p, hid_s, w_down[l], x1_p, x1_s, mod_p, mod_s)
        outs[4].append(k_s.reshape(n_s, 1, N_KV_HEADS, HEAD_DIM))
        outs[5].append(v_s.reshape(n_s, 1, N_KV_HEADS, HEAD_DIM))
        outs[6].append(buf_s)
        outs[7].append(h_s)
    stacked = [jnp.stack(o) for o in outs]
    return (yp.reshape(1, seq, d), ys.reshape(n_s, 1, d), *stacked)
```

```python
import functools

import jax
import jax.numpy as jnp
from jax import lax
from jax.experimental import pallas as pl
from jax.experimental.pallas import tpu as pltpu

F32 = jnp.float32
BF16 = jnp.bfloat16

HEAD_DIM = 128
ATT_HEADS = 8
N_KV_HEADS = 4
KV_GROUP = ATT_HEADS // N_KV_HEADS
ATT_WIDTH = ATT_HEADS * HEAD_DIM
KV_WIDTH = N_KV_HEADS * HEAD_DIM
MOBA_BLOCK = 256
MOBA_TOPK = 3
D_INNER = 1024
SSM_HEAD_DIM = 64
SSM_HEADS = D_INNER // SSM_HEAD_DIM
SSM_GROUPS = 2
D_STATE = 128
CONV_WIDTH = 4
CONV_DIM = D_INNER + 2 * SSM_GROUPS * D_STATE
SSD_CHUNK = 128
EPS = 1e-6
ATT_SCALE = HEAD_DIM ** -0.5

LANES = 128
FEAT_KBLK = 96
FEAT_KOFF = 97
FEAT_ONE_A = 98
FEAT_ONE_B = 99
NEG_BIG = -1e30

NT_DIMS = (((1,), (1,)), ((), ()))
VMEM_LIMIT = 56 * 1024 * 1024


def _params(*sem):
    return pltpu.CompilerParams(dimension_semantics=sem, vmem_limit_bytes=VMEM_LIMIT)


def _silu(x):
    return x / (1.0 + jnp.exp(-x))


def _softplus(x):
    return jnp.maximum(x, 0.0) + jnp.log1p(jnp.exp(-jnp.abs(x)))


def _split_bf16(x):
    hi = x.astype(BF16)
    lo = (x - hi.astype(F32)).astype(BF16)
    return hi, lo


def _dot3(a, b, dims):
    ah, al = _split_bf16(a)
    bh, bl = _split_bf16(b)
    d = lambda x, y: lax.dot_general(x, y, dims, preferred_element_type=F32)
    return d(ah, bh) + d(al, bh) + d(ah, bl)


def _alibi_slope(h):
    return 2.0 ** (-8.0 * (h + 1) / ATT_HEADS)


NN_DIMS = (((1,), (0,)), ((), ()))


def _ada_kernel(c_ref, w_ref, b_ref, o_ref):
    rows = c_ref.shape[0]
    ahi, alo = _split_bf16(_silu(c_ref[...]))
    whi, wlo = _split_bf16(w_ref[...])
    both = jnp.dot(jnp.concatenate([ahi, alo], axis=0), whi, preferred_element_type=F32)
    o_ref[...] = both[:rows] + both[rows:] + jnp.dot(ahi, wlo, preferred_element_type=F32) + b_ref[...]


def _ada(c_all, w, b):
    rows, d = c_all.shape
    n = w.shape[1]
    tn = 512
    return pl.pallas_call(
        _ada_kernel,
        grid=(n // tn,),
        in_specs=[pl.BlockSpec((rows, d), lambda j: (0, 0)),
                  pl.BlockSpec((d, tn), lambda j: (0, j)),
                  pl.BlockSpec((1, tn), lambda j: (0, j))],
        out_specs=pl.BlockSpec((rows, tn), lambda j: (0, j)),
        out_shape=jax.ShapeDtypeStruct((rows, n), F32),
        compiler_params=_params("arbitrary"),
        name="ada_mod",
    )(c_all, w, b.reshape(1, n))


def _mod_spec(mod_rows, tm, width, col_of):
    if mod_rows == 1:
        return pl.BlockSpec((1, width), lambda n, i: (0, col_of(n)))
    return pl.BlockSpec((tm, width), lambda n, i: (i, col_of(n)))


def _rms(x):
    return x * lax.rsqrt(jnp.mean(x * x, axis=-1, keepdims=True) + EPS)


def _cast_weight_once(w_ref, wbf_ref):
    @pl.when(pl.program_id(1) == 0)
    def _():
        wbf_ref[...] = w_ref[...].astype(BF16)


STAGE_CHUNK = 384


def _inproj_kernel(n_main, split, x_ref, g_ref, sc_ref, sh_ref, w_ref, wdt_ref, o_ref, odt_ref, *w_scratch):
    n = pl.program_id(0)
    first_row_tile = pl.program_id(1) == 0
    main_refs, dt_refs = w_scratch[:len(w_scratch) // 2], w_scratch[len(w_scratch) // 2:]

    def stage(src_ref, refs, pad_rows):
        cols = src_ref.shape[0]
        chunk = STAGE_CHUNK if cols % STAGE_CHUNK == 0 else cols
        for c0 in range(0, cols, chunk):
            w = src_ref[c0:c0 + chunk, :]
            if pad_rows:
                w = jnp.concatenate([w, jnp.zeros((pad_rows, w.shape[1]), F32)], axis=0)
            w = w.T
            for part, ref in zip(_split_bf16(w) if split else (w.astype(BF16),), refs):
                ref[:, c0:c0 + w.shape[1]] = part

    def product(hn, refs):
        mm = lambda a, b: jnp.dot(a, b, preferred_element_type=F32)
        if not split:
            return mm(hn.astype(BF16), refs[0][...])
        hhi, hlo = _split_bf16(hn)
        return mm(hhi, refs[0][...]) + mm(hlo, refs[0][...]) + mm(hhi, refs[1][...])

    @pl.when(first_row_tile)
    def _():
        stage(w_ref, main_refs, 0)

    @pl.when(jnp.logical_and(n == 0, first_row_tile))
    def _():
        stage(wdt_ref, dt_refs, LANES - wdt_ref.shape[0])

    hn = (_rms(x_ref[...]) * g_ref[...]) * (1.0 + sc_ref[...]) + sh_ref[...]
    o_ref[...] = product(hn, main_refs)

    @pl.when(n == 0)
    def _():
        odt_ref[...] = product(hn, dt_refs)


def _inproj(x, g, mod, w_t, main_w, *, tn, tm, name, split=False):
    m, d = x.shape
    tm = min(tm, m)
    mr = mod.shape[0]
    n_main = main_w // tn
    n_dt = w_t.shape[0] - main_w
    assert n_main * tn == main_w and main_w % n_dt == 0 and n_dt % 16 == 0
    last_i = m // tm - 1
    ncopies = 2 if split else 1
    proj, dtp = pl.pallas_call(
        functools.partial(_inproj_kernel, n_main, split),
        grid=(n_main, m // tm),
        in_specs=[pl.BlockSpec((tm, d), lambda n, i: (i, 0)),
                  pl.BlockSpec((1, d), lambda n, i: (0, 0)),
                  _mod_spec(mr, tm, d, lambda n: 1),
                  _mod_spec(mr, tm, d, lambda n: 0),
                  pl.BlockSpec((tn, d), lambda n, i: (n, 0)),
                  pl.BlockSpec((n_dt, d), lambda n, i: (main_w // n_dt, 0))],
        out_specs=[pl.BlockSpec((tm, tn), lambda n, i: (i, n)),
                   pl.BlockSpec((tm, LANES), lambda n, i: (jnp.where(n == 0, i, last_i), 0))],
        out_shape=[jax.ShapeDtypeStruct((m, main_w), F32), jax.ShapeDtypeStruct((m, LANES), F32)],
        scratch_shapes=[pltpu.VMEM((d, tn), BF16)] * ncopies + [pltpu.VMEM((d, LANES), BF16)] * ncopies,
        compiler_params=_params("arbitrary", "arbitrary"),
        name=name,
    )(x, g.reshape(1, d), mod, mod, w_t, w_t)
    return proj, dtp


def _outproj_kernel(a_ref, ga_ref, b_ref, w_ref, x_ref, gt_ref, gf_ref, sc_ref, sh_ref, x1_ref, h2_ref, wbf_ref):
    _cast_weight_once(w_ref, wbf_ref)
    ka = a_ref.shape[1]
    a = (_rms(a_ref[...]) * ga_ref[...]).astype(BF16)
    acc = jnp.dot(a, wbf_ref[:ka, :], preferred_element_type=F32)
    acc = acc + jnp.dot(b_ref[...], wbf_ref[ka:, :], preferred_element_type=F32)
    x1 = x_ref[...] + gt_ref[...] * acc
    x1_ref[...] = x1
    h2_ref[...] = ((_rms(x1) * gf_ref[...]) * (1.0 + sc_ref[...]) + sh_ref[...]).astype(BF16)


def _outproj(a, g_a, b, w, x, mod, g_ffn):
    m, ka = a.shape
    kb = b.shape[1]
    d = w.shape[1]
    tm = min(256, m)
    mr = mod.shape[0]
    return pl.pallas_call(
        _outproj_kernel,
        grid=(1, m // tm),
        in_specs=[pl.BlockSpec((tm, ka), lambda n, i: (i, 0)),
                  pl.BlockSpec((1, ka), lambda n, i: (0, 0)),
                  pl.BlockSpec((tm, kb), lambda n, i: (i, 0)),
                  pl.BlockSpec((ka + kb, d), lambda n, i: (0, 0), pipeline_mode=pl.Buffered(1)),
                  pl.BlockSpec((tm, d), lambda n, i: (i, 0)),
                  _mod_spec(mr, tm, d, lambda n: 2),
                  pl.BlockSpec((1, d), lambda n, i: (0, 0)),
                  _mod_spec(mr, tm, d, lambda n: 4),
                  _mod_spec(mr, tm, d, lambda n: 3)],
        out_specs=[pl.BlockSpec((tm, d), lambda n, i: (i, 0)),
                   pl.BlockSpec((tm, d), lambda n, i: (i, 0))],
        out_shape=[jax.ShapeDtypeStruct((m, d), F32), jax.ShapeDtypeStruct((m, d), BF16)],
        scratch_shapes=[pltpu.VMEM((ka + kb, d), BF16)],
        compiler_params=_params("arbitrary", "arbitrary"),
        name="out_proj",
    )(a, g_a.reshape(1, ka), b, w, x, mod, g_ffn.reshape(1, d), mod, mod)


def _gateup_kernel(tiles_p, xp_ref, xs_ref, wg_ref, wu_ref, op_ref, os_ref, wgb_ref, wub_ref):
    _cast_weight_once(wg_ref, wgb_ref)
    _cast_weight_once(wu_ref, wub_ref)
    i = pl.program_id(1)

    def swiglu(h2):
        g = jnp.dot(h2, wgb_ref[...], preferred_element_type=F32)
        u = jnp.dot(h2, wub_ref[...], preferred_element_type=F32)
        return (_silu(g) * u).astype(BF16)

    @pl.when(i < tiles_p)
    def _():
        op_ref[...] = swiglu(xp_ref[...])

    @pl.when(i == tiles_p)
    def _():
        os_ref[...] = swiglu(xs_ref[...])


def _gateup(h2_p, h2_s, wg, wu):
    m, d = h2_p.shape
    ms = h2_s.shape[0]
    f = wg.shape[1]
    tn, tm = 512, 1024
    tiles_p = m // tm
    prow = lambda i: jnp.minimum(i, tiles_p - 1)
    return pl.pallas_call(
        functools.partial(_gateup_kernel, tiles_p),
        grid=(f // tn, tiles_p + 1),
        in_specs=[pl.BlockSpec((tm, d), lambda n, i: (prow(i), 0)),
                  pl.BlockSpec((ms, d), lambda n, i: (0, 0)),
                  pl.BlockSpec((d, tn), lambda n, i: (0, n)),
                  pl.BlockSpec((d, tn), lambda n, i: (0, n))],
        out_specs=[pl.BlockSpec((tm, tn), lambda n, i: (prow(i), n)),
                   pl.BlockSpec((ms, tn), lambda n, i: (0, n))],
        out_shape=[jax.ShapeDtypeStruct((m, f), BF16), jax.ShapeDtypeStruct((ms, f), BF16)],
        scratch_shapes=[pltpu.VMEM((d, tn), BF16), pltpu.VMEM((d, tn), BF16)],
        compiler_params=_params("arbitrary", "arbitrary"),
        name="ffn_gate_up",
    )(h2_p, h2_s, wg, wu)


def _stream_block_means(t, stream_steps, steps_per_seq, page_refs, km_ref):
    pps = len(page_refs)

    @pl.when(t < stream_steps)
    def _():
        page_rows = page_refs[0].shape[0]
        ppb = MOBA_BLOCK * N_KV_HEADS // page_rows
        fold = 8 // N_KV_HEADS
        j = t % steps_per_seq
        for b in range(pps // ppb):
            s8 = jnp.sum(page_refs[b * ppb][...].reshape(page_rows // 8, 8, HEAD_DIM), axis=0)
            for p in range(1, ppb):
                s8 = s8 + jnp.sum(page_refs[b * ppb + p][...].reshape(page_rows // 8, 8, HEAD_DIM), axis=0)
            s = s8[0:N_KV_HEADS]
            for p in range(1, fold):
                s = s + s8[p * N_KV_HEADS:(p + 1) * N_KV_HEADS]
            s = s * (1.0 / MOBA_BLOCK)
            for kv in range(N_KV_HEADS):
                km_ref[0, kv, j, b:b + 1, :] = s[kv:kv + 1, :]


def _page_stream(cache_k, layer, page_table, host_steps, step_of):
    n_seq, n_pages = page_table.shape
    depth, pool, page = cache_k.shape[:3]
    ppb = MOBA_BLOCK // page
    pps = min(32, n_pages)
    steps_per_seq = n_pages // pps
    stream_steps = n_seq * steps_per_seq
    assert n_pages % pps == 0 and pps % ppb == 0 and 8 % N_KV_HEADS == 0
    assert stream_steps <= host_steps, "not enough grid steps to stream the cache"
    view = cache_k.reshape(depth, pool, page * N_KV_HEADS, HEAD_DIM)

    def stream_step(*grid_idx):
        return jnp.minimum(step_of(*grid_idx), stream_steps - 1)

    def page_spec(k):
        return pl.BlockSpec((None, None, page * N_KV_HEADS, HEAD_DIM),
                            lambda *args: (layer, args[-1][stream_step(*args[:-1]) * pps + k], 0, 0))

    bps = pps // ppb
    km_spec = pl.BlockSpec((1, N_KV_HEADS, steps_per_seq, bps, HEAD_DIM),
                           lambda *args: (stream_step(*args[:-1]) // steps_per_seq, 0, 0, 0, 0))
    km_shape = jax.ShapeDtypeStruct((n_seq, N_KV_HEADS, steps_per_seq, bps, HEAD_DIM), F32)
    return (view, page_table.reshape(-1), [page_spec(k) for k in range(pps)], km_spec, km_shape, stream_steps,
            steps_per_seq)


def _down_kernel(tiles_p, hp_ref, hs_ref, w_ref, xp_ref, xs_ref, gtp_ref, gts_ref, op_ref, os_ref, wbf_ref):
    _cast_weight_once(w_ref, wbf_ref)
    i = pl.program_id(1)

    @pl.when(i < tiles_p)
    def _():
        acc = jnp.dot(hp_ref[...], wbf_ref[...], preferred_element_type=F32)
        op_ref[...] = xp_ref[...] + gtp_ref[...] * acc

    @pl.when(i == tiles_p)
    def _():
        acc = jnp.dot(hs_ref[...], wbf_ref[...], preferred_element_type=F32)
        os_ref[...] = xs_ref[...] + gts_ref[...] * acc


def _down(h_p, h_s, w, x_p, x_s, mod_p, mod_s):
    m, f = h_p.shape
    ms = h_s.shape[0]
    d = w.shape[1]
    tn, tm = 512, 512
    tiles_p = m // tm
    nb = d // tn
    assert mod_p.shape[0] == 1 and mod_s.shape[0] == ms
    prow = lambda i: jnp.minimum(i, tiles_p - 1)
    return pl.pallas_call(
        functools.partial(_down_kernel, tiles_p),
        grid=(nb, tiles_p + 1),
        in_specs=[pl.BlockSpec((tm, f), lambda n, i: (prow(i), 0)),
                  pl.BlockSpec((ms, f), lambda n, i: (0, 0)),
                  pl.BlockSpec((f, tn), lambda n, i: (0, n)),
                  pl.BlockSpec((tm, tn), lambda n, i: (prow(i), n)),
                  pl.BlockSpec((ms, tn), lambda n, i: (0, n)),
                  pl.BlockSpec((1, tn), lambda n, i: (0, 5 * nb + n)),
                  pl.BlockSpec((ms, tn), lambda n, i: (0, 5 * nb + n))],
        out_specs=[pl.BlockSpec((tm, tn), lambda n, i: (prow(i), n)),
                   pl.BlockSpec((ms, tn), lambda n, i: (0, n))],
        out_shape=[jax.ShapeDtypeStruct((m, d), F32), jax.ShapeDtypeStruct((ms, d), F32)],
        scratch_shapes=[pltpu.VMEM((f, tn), BF16)],
        compiler_params=_params("arbitrary", "arbitrary"),
        name="ffn_down",
    )(h_p, h_s, w, x_p, x_s, mod_p, mod_s)


def _head_norm(x, gain):
    return x * lax.rsqrt(jnp.mean(x * x, axis=-1, keepdims=True) + EPS) * gain


def _top3(g, idx_f, axis):
    sel = jnp.zeros(g.shape, F32)
    firsts = []
    for _ in range(MOBA_TOPK):
        mx = jnp.max(g, axis=axis, keepdims=True)
        ismax = jnp.logical_and(g == mx, mx > -jnp.inf)
        first = jnp.min(jnp.where(ismax, idx_f, float(LANES)), axis=axis, keepdims=True)
        pick = idx_f == first
        sel = jnp.where(pick, 1.0, sel)
        g = jnp.where(pick, -jnp.inf, g)
        firsts.append(first)
    return sel, firsts


def _prompt_prep_kernel(proj_ref, qg_ref, kg_ref, kout_ref, vout_ref, kp_ref, vp_ref, qp_ref, km_ref):
    i = pl.program_id(0)
    blk = MOBA_BLOCK

    @pl.when(i == 0)
    def _():
        km_ref[...] = jnp.zeros_like(km_ref)

    lane = lax.broadcasted_iota(jnp.int32, (blk, LANES), 1)
    row = lax.broadcasted_iota(jnp.int32, (blk, LANES), 0)
    sq_row = lax.broadcasted_iota(jnp.int32, (LANES, LANES), 0)
    i_f = i.astype(F32)
    row_f = row.astype(F32)

    kfeat = jnp.where(lane == i, 1.0, 0.0)
    kfeat = jnp.where(lane == FEAT_KBLK, i_f, kfeat)
    kfeat = jnp.where(lane == FEAT_KOFF, row_f, kfeat)
    kfeat = jnp.where(jnp.logical_or(lane == FEAT_ONE_A, lane == FEAT_ONE_B), 1.0, kfeat).astype(BF16)
    vfeat = jnp.ones((blk, LANES), BF16)

    kg = kg_ref[...]
    for kv in range(N_KV_HEADS):
        k = proj_ref[:, ATT_WIDTH + kv * HEAD_DIM:ATT_WIDTH + (kv + 1) * HEAD_DIM]
        kn = _head_norm(k, kg)
        kout_ref[:, kv * HEAD_DIM:(kv + 1) * HEAD_DIM] = kn
        kp_ref[kv, :, :HEAD_DIM] = kn.astype(BF16)
        kp_ref[kv, :, HEAD_DIM:] = kfeat
        ksum = jnp.sum(kn, axis=0, keepdims=True) * (1.0 / blk)
        km_ref[kv] = jnp.where(sq_row == i, jnp.broadcast_to(ksum, (LANES, LANES)), km_ref[kv])
        v = proj_ref[:, ATT_WIDTH + KV_WIDTH + kv * HEAD_DIM:ATT_WIDTH + KV_WIDTH + (kv + 1) * HEAD_DIM]
        vout_ref[:, kv * HEAD_DIM:(kv + 1) * HEAD_DIM] = v
        vp_ref[kv, :, :HEAD_DIM] = v.astype(BF16)
        vp_ref[kv, :, HEAD_DIM:] = vfeat

    qg = qg_ref[...]
    blk_id = lax.broadcasted_iota(jnp.int32, (LANES, blk), 0)
    blk_id_f = blk_id.astype(F32)
    valid = blk_id < i
    for h in range(ATT_HEADS):
        q = proj_ref[:, h * HEAD_DIM:(h + 1) * HEAD_DIM]
        qn = _head_norm(q, qg)
        gate = _dot3(km_ref[h // KV_GROUP], qn, NT_DIMS)
        sel_t, _ = _top3(jnp.where(valid, gate, -jnp.inf), blk_id_f, 0)
        sel = sel_t.T
        slope = _alibi_slope(h)
        qfeat = jnp.where(jnp.logical_and(lane < FEAT_KBLK, sel == 0.0), NEG_BIG, 0.0)
        qfeat = jnp.where(lane == FEAT_KBLK, slope * blk, qfeat)
        qfeat = jnp.where(lane == FEAT_KOFF, slope, qfeat)
        qfeat = jnp.where(lane == FEAT_ONE_A, -(slope * blk) * i_f, qfeat)
        qfeat = jnp.where(lane == FEAT_ONE_B, -slope * row_f, qfeat)
        qp_ref[h, :, :HEAD_DIM] = (qn * ATT_SCALE).astype(BF16)
        qp_ref[h, :, HEAD_DIM:] = qfeat.astype(BF16)


def _prompt_prep(proj, q_gain, k_gain):
    m = proj.shape[0]
    nb = m // MOBA_BLOCK
    assert m % MOBA_BLOCK == 0 and nb <= FEAT_KBLK
    blk = MOBA_BLOCK
    qkv_w = ATT_WIDTH + 2 * KV_WIDTH
    return pl.pallas_call(
        _prompt_prep_kernel,
        grid=(nb,),
        in_specs=[pl.BlockSpec((blk, qkv_w), lambda i: (i, 0)),
                  pl.BlockSpec((1, HEAD_DIM), lambda i: (0, 0)),
                  pl.BlockSpec((1, HEAD_DIM), lambda i: (0, 0))],
        out_specs=[pl.BlockSpec((blk, KV_WIDTH), lambda i: (i, 0)),
                   pl.BlockSpec((blk, KV_WIDTH), lambda i: (i, 0)),
                   pl.BlockSpec((N_KV_HEADS, blk, 2 * HEAD_DIM), lambda i: (0, i, 0)),
                   pl.BlockSpec((N_KV_HEADS, blk, 2 * HEAD_DIM), lambda i: (0, i, 0)),
                   pl.BlockSpec((ATT_HEADS, blk, 2 * HEAD_DIM), lambda i: (0, i, 0))],
        out_shape=[jax.ShapeDtypeStruct((m, KV_WIDTH), F32),
                   jax.ShapeDtypeStruct((m, KV_WIDTH), F32),
                   jax.ShapeDtypeStruct((N_KV_HEADS, m, 2 * HEAD_DIM), BF16),
                   jax.ShapeDtypeStruct((N_KV_HEADS, m, 2 * HEAD_DIM), BF16),
                   jax.ShapeDtypeStruct((ATT_HEADS, m, 2 * HEAD_DIM), BF16)],
        scratch_shapes=[pltpu.VMEM((N_KV_HEADS, LANES, LANES), F32)],
        compiler_params=_params("arbitrary"),
        name="prompt_qk_prep",
    )(proj, q_gain.reshape(1, HEAD_DIM), k_gain.reshape(1, HEAD_DIM))


ATTN_UNROLL = 4
LOG2E = 1.4426950408889634


def _prompt_attn_kernel(pps, stream_steps, steps_per_seq, pt_ref, q_ref, k_ref, v_ref, *rest):
    page_refs = rest[:pps]
    o_ref, km_ref, s_ref, acc_ref, m_ref = rest[pps:]
    i = pl.program_id(1)
    _stream_block_means(pl.program_id(0) * pl.num_programs(1) + i, stream_steps, steps_per_seq, page_refs, km_ref)
    blk = MOBA_BLOCK
    rows = KV_GROUP * blk
    span = ATTN_UNROLL * blk
    qs = q_ref[...].reshape(rows, 2 * HEAD_DIM)

    def lane_fold(s):
        out = s[:, :LANES]
        for t in range(1, s.shape[1] // LANES):
            out = jnp.maximum(out, s[:, t * LANES:(t + 1) * LANES])
        return out

    def probs(s):
        mb = m_ref[...]
        return jnp.concatenate([jnp.exp2(s[:, t * LANES:(t + 1) * LANES] - mb) for t in range(s.shape[1] // LANES)],
                               axis=1).astype(BF16)

    own = pl.ds(pl.multiple_of(i * blk, blk), blk)
    r = lax.broadcasted_iota(jnp.int32, (rows, blk), 0)
    c = lax.broadcasted_iota(jnp.int32, (rows, blk), 1)
    dist = jnp.bitwise_and(r, blk - 1) - c
    slope = qs[:, HEAD_DIM + FEAT_KOFF:HEAD_DIM + FEAT_KOFF + 1].astype(F32)
    s_own = lax.dot_general(qs[:, :HEAD_DIM], k_ref[own, :HEAD_DIM], NT_DIMS, preferred_element_type=F32)
    s_own = jnp.where(dist >= 0, (s_own - slope * dist.astype(F32)) * LOG2E, -jnp.inf)
    m_ref[...] = lane_fold(s_own)

    trips = (i + ATTN_UNROLL - 1) // ATTN_UNROLL

    def two_trips_per_iteration(trip):
        def pair(u, carry):
            trip(2 * u)
            trip(2 * u + 1)
            return carry

        lax.fori_loop(0, trips // 2, pair, 0)

        @pl.when(trips % 2 == 1)
        def _():
            trip(trips - 1)

    def pass1(t):
        ks = k_ref[pl.ds(pl.multiple_of(t * span, span), span), :]
        s = lax.dot_general(qs, ks, NT_DIMS, preferred_element_type=F32) * LOG2E
        s_ref[t] = s
        m_ref[...] = jnp.maximum(m_ref[...], lane_fold(s))

    two_trips_per_iteration(pass1)
    m_ref[...] = jnp.broadcast_to(jnp.max(m_ref[...], axis=1, keepdims=True), (rows, LANES))

    acc_ref[...] = jnp.dot(probs(s_own), v_ref[own, :], preferred_element_type=F32)

    def pass2(t):
        vs = v_ref[pl.ds(pl.multiple_of(t * span, span), span), :]
        acc_ref[...] += jnp.dot(probs(s_ref[t]), vs, preferred_element_type=F32)

    two_trips_per_iteration(pass2)
    acc = acc_ref[...]
    o = acc[:, :HEAD_DIM] / acc[:, HEAD_DIM:]
    for g in range(KV_GROUP):
        o_ref[:, g * HEAD_DIM:(g + 1) * HEAD_DIM] = o[g * blk:(g + 1) * blk]


def _prompt_attn(qp, kp, vp, cache_k, layer, page_table):
    m = kp.shape[1]
    nb = m // MOBA_BLOCK
    assert nb % ATTN_UNROLL == 0
    blk = MOBA_BLOCK
    rows = KV_GROUP * blk
    view, flat_pages, page_specs, km_spec, km_shape, stream_steps, steps_per_seq = _page_stream(
        cache_k, layer, page_table, N_KV_HEADS * nb, lambda kv, i: kv * nb + i)
    resident = lambda: pl.BlockSpec((None, m, 2 * HEAD_DIM), lambda kv, i, pt: (kv, 0, 0), pipeline_mode=pl.Buffered(1))
    grid_spec = pltpu.PrefetchScalarGridSpec(
        num_scalar_prefetch=1,
        grid=(N_KV_HEADS, nb),
        in_specs=[pl.BlockSpec((KV_GROUP, blk, 2 * HEAD_DIM), lambda kv, i, pt: (kv, i, 0)), resident(), resident()]
        + page_specs,
        out_specs=[pl.BlockSpec((blk, KV_GROUP * HEAD_DIM), lambda kv, i, pt: (i, kv)), km_spec],
        scratch_shapes=[pltpu.VMEM((nb // ATTN_UNROLL, rows, ATTN_UNROLL * blk), F32),
                        pltpu.VMEM((rows, 2 * HEAD_DIM), F32),
                        pltpu.VMEM((rows, LANES), F32)],
    )
    o, km = pl.pallas_call(
        functools.partial(_prompt_attn_kernel, len(page_specs), stream_steps, steps_per_seq),
        grid_spec=grid_spec,
        out_shape=[jax.ShapeDtypeStruct((m, ATT_WIDTH), F32), km_shape],
        compiler_params=_params("arbitrary", "arbitrary"),
        name="prompt_moba_attn",
    )(flat_pages, qp, kp, vp, *([view] * len(page_specs)))
    return o, km.reshape(km.shape[0], N_KV_HEADS, -1, HEAD_DIM)


def _expand_heads(v, lane_lo):
    r = v.shape[0]
    parts = []
    for k in range(SSM_HEADS // 2):
        a0 = jnp.broadcast_to(v[:, 2 * k:2 * k + 1], (r, LANES))
        a1 = jnp.broadcast_to(v[:, 2 * k + 1:2 * k + 2], (r, LANES))
        parts.append(jnp.where(lane_lo, a0, a1))
    return jnp.concatenate(parts, axis=1)


def _gated_group_norm(y, z, g):
    yz = y * _silu(z)
    gw = D_INNER // SSM_GROUPS
    outs = []
    for grp in range(SSM_GROUPS):
        t = yz[:, grp * gw:(grp + 1) * gw]
        t = t * lax.rsqrt(jnp.mean(t * t, axis=-1, keepdims=True) + EPS)
        outs.append(t * g[:, grp * gw:(grp + 1) * gw])
    return jnp.concatenate(outs, axis=1)


def _ssd_kernel(z_ref, xbc_ref, dt_ref, cw_ref, cb_ref, dtb_ref, alog_ref, dsk_ref, g_ref,
                y_ref, tail_out_ref, st_out_ref, tail_ref, st_ref):
    c = pl.program_id(0)
    cs = SSD_CHUNK
    gw = D_INNER // SSM_GROUPS

    @pl.when(c == 0)
    def _():
        tail_ref[...] = jnp.zeros_like(tail_ref)
        st_ref[...] = jnp.zeros_like(st_ref)

    xr = xbc_ref[...]
    xp = jnp.concatenate([tail_ref[...], xr], axis=0)
    cw = cw_ref[...]
    conv = cb_ref[...] + cw[3:4] * xr
    for t in range(CONV_WIDTH - 1):
        conv = conv + cw[t:t + 1] * xp[8 - (CONV_WIDTH - 1) + t:8 - (CONV_WIDTH - 1) + t + cs]
    tail_ref[...] = xr[cs - 8:]
    tail_out_ref[...] = xr[cs - 8:]
    xc = _silu(conv)
    xs = xc[:, :D_INNER]
    bm = xc[:, D_INNER:D_INNER + SSM_GROUPS * D_STATE]
    cm = xc[:, D_INNER + SSM_GROUPS * D_STATE:]

    lane = lax.broadcasted_iota(jnp.int32, (cs, LANES), 1)
    rowi = lax.broadcasted_iota(jnp.int32, (cs, LANES), 0)
    lane_lo = lane < SSM_HEAD_DIM
    tri = rowi >= lane

    dt = _softplus(dt_ref[...] + dtb_ref[...])
    a = jnp.where(lane[:1] < SSM_HEADS, -jnp.exp(alog_ref[...]), 0.0)
    da = dt * a
    tril = jnp.where(tri, 1.0, 0.0).astype(BF16)
    p1 = da.astype(BF16)
    r1 = da - p1.astype(F32)
    p2 = r1.astype(BF16)
    p3 = (r1 - p2.astype(F32)).astype(BF16)
    acum = (jnp.dot(tril, p1, preferred_element_type=F32) + jnp.dot(tril, p2, preferred_element_type=F32)
            + jnp.dot(tril, p3, preferred_element_type=F32))
    acum_t = acum.T

    dt_e = _expand_heads(dt, lane_lo)
    ac_e = _expand_heads(acum, lane_lo)
    xdt = xs * dt_e
    ea_e = jnp.exp(ac_e)
    dend_e = jnp.exp(ac_e[cs - 1:cs, :] - ac_e)
    cdec = ea_e[cs - 1:cs, :]
    xdt_bf = xdt.astype(BF16)
    xdec_bf = (xdt * dend_e).astype(BF16)

    y_parts = []
    for grp in range(SSM_GROUPS):
        bg = bm[:, grp * D_STATE:(grp + 1) * D_STATE]
        cg = cm[:, grp * D_STATE:(grp + 1) * D_STATE].astype(BF16)
        cb = lax.dot_general(cg, bg.astype(BF16), NT_DIMS, preferred_element_type=F32)
        hpg = SSM_HEADS // SSM_GROUPS
        intra = []
        for k in range(hpg // 2):
            pair = grp * (hpg // 2) + k
            xpair = xdt_bf[:, pair * LANES:(pair + 1) * LANES]
            acc = None
            for hh in range(2):
                h = 2 * pair + hh
                seg = jnp.broadcast_to(acum[:, h:h + 1], (cs, cs)) - acum_t[h:h + 1, :]
                lmat = jnp.exp(jnp.where(tri, seg, -jnp.inf))
                mh = (cb * lmat).astype(BF16)
                xh = jnp.where(lane_lo if hh == 0 else jnp.logical_not(lane_lo), xpair, jnp.zeros_like(xpair))
                part = jnp.dot(mh, xh, preferred_element_type=F32)
                acc = part if acc is None else acc + part
            intra.append(acc)
        y_intra = jnp.concatenate(intra, axis=1)
        st = st_ref[grp]
        y_inter = jnp.dot(cg, st.astype(BF16), preferred_element_type=F32) * ea_e[:, grp * gw:(grp + 1) * gw]
        new_st = cdec[:, grp * gw:(grp + 1) * gw] * st + jnp.dot(
            bg.T.astype(BF16), xdec_bf[:, grp * gw:(grp + 1) * gw], preferred_element_type=F32)
        st_ref[grp] = new_st
        st_out_ref[grp] = new_st
        y_parts.append(y_intra + y_inter)
    y = jnp.concatenate(y_parts, axis=1) + dsk_ref[...] * xs
    y_ref[...] = _gated_group_norm(y, z_ref[...], g_ref[...]).astype(y_ref.dtype)


def _pad_lanes(v):
    return jnp.pad(v.reshape(1, -1), ((0, 0), (0, LANES - v.size)))


def _ssd_prompt(proj, dtp, conv_w, conv_b, dt_bias, a_log, d_skip, g_ssm):
    m = proj.shape[0]
    cs = SSD_CHUNK
    assert m % cs == 0
    gw = D_INNER // SSM_GROUPS
    z_blk = (ATT_WIDTH + 2 * KV_WIDTH) // D_INNER
    x_blk = (ATT_WIDTH + 2 * KV_WIDTH + D_INNER) // CONV_DIM
    assert z_blk * D_INNER == ATT_WIDTH + 2 * KV_WIDTH and x_blk * CONV_DIM == ATT_WIDTH + 2 * KV_WIDTH + D_INNER
    const = lambda shape: pl.BlockSpec(shape, lambda c: tuple(0 for _ in shape))
    return pl.pallas_call(
        _ssd_kernel,
        grid=(m // cs,),
        in_specs=[pl.BlockSpec((cs, D_INNER), lambda c: (c, z_blk)),
                  pl.BlockSpec((cs, CONV_DIM), lambda c: (c, x_blk)),
                  pl.BlockSpec((cs, LANES), lambda c: (c, 0)),
                  const((CONV_WIDTH, CONV_DIM)), const((1, CONV_DIM)), const((1, LANES)), const((1, LANES)),
                  const((1, D_INNER)), const((1, D_INNER))],
        out_specs=[pl.BlockSpec((cs, D_INNER), lambda c: (c, 0)),
                   const((8, CONV_DIM)), const((SSM_GROUPS, D_STATE, gw))],
        out_shape=[jax.ShapeDtypeStruct((m, D_INNER), BF16),
                   jax.ShapeDtypeStruct((8, CONV_DIM), F32),
                   jax.ShapeDtypeStruct((SSM_GROUPS, D_STATE, gw), F32)],
        scratch_shapes=[pltpu.VMEM((8, CONV_DIM), F32), pltpu.VMEM((SSM_GROUPS, D_STATE, gw), F32)],
        compiler_params=_params("arbitrary"),
        name="prompt_ssd",
    )(proj, proj, dtp, conv_w, conv_b.reshape(1, CONV_DIM), _pad_lanes(dt_bias), _pad_lanes(a_log),
      jnp.repeat(d_skip, SSM_HEAD_DIM).reshape(1, D_INNER), g_ssm.reshape(1, D_INNER))


def _sample_prep_kernel(proj_ref, qg_ref, kg_ref, q_ref, k_ref, v_ref):
    for h in range(ATT_HEADS):
        q_ref[:, h * HEAD_DIM:(h + 1) * HEAD_DIM] = _head_norm(proj_ref[:, h * HEAD_DIM:(h + 1) * HEAD_DIM], qg_ref[...])
    for kv in range(N_KV_HEADS):
        lo = ATT_WIDTH + kv * HEAD_DIM
        k_ref[:, kv * HEAD_DIM:(kv + 1) * HEAD_DIM] = _head_norm(proj_ref[:, lo:lo + HEAD_DIM], kg_ref[...])
    v_ref[...] = proj_ref[:, ATT_WIDTH + KV_WIDTH:ATT_WIDTH + 2 * KV_WIDTH]


def _sample_prep(proj, q_gain, k_gain):
    n = proj.shape[0]
    qkv_w = ATT_WIDTH + 2 * KV_WIDTH
    return pl.pallas_call(
        _sample_prep_kernel,
        grid=(1,),
        in_specs=[pl.BlockSpec((n, qkv_w), lambda i: (0, 0)),
                  pl.BlockSpec((1, HEAD_DIM), lambda i: (0, 0)),
                  pl.BlockSpec((1, HEAD_DIM), lambda i: (0, 0))],
        out_specs=[pl.BlockSpec((n, ATT_WIDTH), lambda i: (0, 0)),
                   pl.BlockSpec((n, KV_WIDTH), lambda i: (0, 0)),
                   pl.BlockSpec((n, KV_WIDTH), lambda i: (0, 0))],
        out_shape=[jax.ShapeDtypeStruct((n, ATT_WIDTH), F32),
                   jax.ShapeDtypeStruct((n, KV_WIDTH), F32),
                   jax.ShapeDtypeStruct((n, KV_WIDTH), F32)],
        compiler_params=_params("arbitrary"),
        name="sample_qk_prep",
    )(proj, q_gain.reshape(1, HEAD_DIM), k_gain.reshape(1, HEAD_DIM))


def _sample_topk_kernel(q_ref, km_ref, idx_ref):
    nblk = km_ref.shape[2]
    hrow = lax.broadcasted_iota(jnp.int32, (ATT_HEADS, nblk), 0)
    lane = lax.broadcasted_iota(jnp.int32, (ATT_HEADS, LANES), 1)
    for s in range(q_ref.shape[0]):
        q = q_ref[s]
        gate = jnp.zeros((ATT_HEADS, nblk), F32)
        for kv in range(N_KV_HEADS):
            gk = _dot3(q, km_ref[s, kv], NT_DIMS)
            gate = jnp.where(hrow // KV_GROUP == kv, gk, gate)
        if nblk < LANES:
            gate = jnp.concatenate([gate, jnp.full((ATT_HEADS, LANES - nblk), -jnp.inf, F32)], axis=1)
        _, firsts = _top3(gate, lane.astype(F32), 1)
        out = jnp.zeros((ATT_HEADS, LANES), jnp.int32)
        for t, first in enumerate(firsts):
            out = jnp.where(lane == t, first.astype(jnp.int32), out)
        idx_ref[s] = out


def _sample_topk(q3, kmean):
    n, _, nblk, _ = kmean.shape
    assert MOBA_TOPK <= nblk <= LANES
    sb = 8 if n % 8 == 0 else 1
    return pl.pallas_call(
        _sample_topk_kernel,
        grid=(n // sb,),
        in_specs=[pl.BlockSpec((sb, ATT_HEADS, HEAD_DIM), lambda s: (s, 0, 0)),
                  pl.BlockSpec((sb, N_KV_HEADS, nblk, HEAD_DIM), lambda s: (s, 0, 0, 0))],
        out_specs=pl.BlockSpec((sb, ATT_HEADS, LANES), lambda s: (s, 0, 0)),
        out_shape=jax.ShapeDtypeStruct((n, ATT_HEADS, LANES), jnp.int32),
        compiler_params=_params("arbitrary"),
        name="sample_gate_topk",
    )(q3, kmean)


def _sample_attn_kernel(past, ppb, layer, pt_ref, idx_ref, q_ref, kn_ref, vn_ref, ck_ref, cv_ref, o_ref,
                        kbuf, vbuf, sems):
    s = pl.program_id(0)
    page = MOBA_BLOCK // ppb
    slot = s % 2

    def copies(seq, buf, h, t, p):
        kv = h // KV_GROUP
        blk = idx_ref[seq, h * MOBA_TOPK + t]
        phys = pt_ref[seq, blk * ppb + p]
        dst = pl.ds((t * ppb + p) * page, page)
        return (pltpu.make_async_copy(ck_ref.at[layer, phys, :, kv, :], kbuf.at[buf, h, dst, :],
                                      sems.at[buf, 0, h, t * ppb + p]),
                pltpu.make_async_copy(cv_ref.at[layer, phys, :, kv, :], vbuf.at[buf, h, dst, :],
                                      sems.at[buf, 1, h, t * ppb + p]))

    triples = [(h, t, p) for h in range(ATT_HEADS) for t in range(MOBA_TOPK) for p in range(ppb)]

    def start_all(seq, buf):
        for h, t, p in triples:
            ck, cv = copies(seq, buf, h, t, p)
            ck.start()
            cv.start()

    @pl.when(s == 0)
    def _():
        start_all(0, 0)

    @pl.when(s + 1 < pl.num_programs(0))
    def _():
        start_all(s + 1, 1 - slot)

    for h, t, p in triples:
        ck, cv = copies(s, slot, h, t, p)
        ck.wait()
        cv.wait()

    nsel = MOBA_TOPK * MOBA_BLOCK
    rowi = lax.broadcasted_iota(jnp.int32, (nsel, 1), 0)
    off = jnp.bitwise_and(rowi, MOBA_BLOCK - 1)
    for h in range(ATT_HEADS):
        kv = h // KV_GROUP
        slope = _alibi_slope(h)
        q = q_ref[0, h:h + 1, :]
        sc = jnp.sum(kbuf[slot, h] * q, axis=1, keepdims=True) * ATT_SCALE
        pos = jnp.zeros((nsel, 1), jnp.int32)
        for t in range(MOBA_TOPK):
            pos = jnp.where(rowi // MOBA_BLOCK == t, idx_ref[s, h * MOBA_TOPK + t] * MOBA_BLOCK, pos)
        dist = (past - (pos + off)).astype(F32)
        sc = sc - slope * dist
        s_own = jnp.sum(kn_ref[0, kv:kv + 1, :] * q, axis=1, keepdims=True) * ATT_SCALE
        mx = jnp.maximum(jnp.max(sc, axis=0, keepdims=True), s_own)
        p = jnp.exp(sc - mx)
        p_own = jnp.exp(s_own - mx)
        denom = jnp.sum(p, axis=0, keepdims=True) + p_own
        num = jnp.sum(p * vbuf[slot, h], axis=0, keepdims=True) + p_own * vn_ref[0, kv:kv + 1, :]
        o_ref[0, h:h + 1, :] = num / denom


def _sample_attn(q3, k_new3, v_new3, cache_k, cache_v, layer, page_table, idx):
    n, n_pages = page_table.shape
    page = cache_k.shape[2]
    ppb = MOBA_BLOCK // page
    past = n_pages * page
    assert past % MOBA_BLOCK == 0
    nsel = MOBA_TOPK * MOBA_BLOCK
    grid_spec = pltpu.PrefetchScalarGridSpec(
        num_scalar_prefetch=2,
        grid=(n,),
        in_specs=[pl.BlockSpec((1, ATT_HEADS, HEAD_DIM), lambda s, pt, ix: (s, 0, 0)),
                  pl.BlockSpec((1, N_KV_HEADS, HEAD_DIM), lambda s, pt, ix: (s, 0, 0)),
                  pl.BlockSpec((1, N_KV_HEADS, HEAD_DIM), lambda s, pt, ix: (s, 0, 0)),
                  pl.BlockSpec(memory_space=pl.ANY),
                  pl.BlockSpec(memory_space=pl.ANY)],
        out_specs=pl.BlockSpec((1, ATT_HEADS, HEAD_DIM), lambda s, pt, ix: (s, 0, 0)),
        scratch_shapes=[pltpu.VMEM((2, ATT_HEADS, nsel, HEAD_DIM), F32),
                        pltpu.VMEM((2, ATT_HEADS, nsel, HEAD_DIM), F32),
                        pltpu.SemaphoreType.DMA((2, 2, ATT_HEADS, MOBA_TOPK * ppb))],
    )
    return pl.pallas_call(
        functools.partial(_sample_attn_kernel, past, ppb, layer),
        grid_spec=grid_spec,
        out_shape=jax.ShapeDtypeStruct((n, ATT_HEADS, HEAD_DIM), F32),
        compiler_params=_params("arbitrary"),
        name="sample_moba_attn",
    )(page_table, idx, q3, k_new3, v_new3, cache_k, cache_v)


def _sample_ssd_kernel(z_ref, xbc_ref, dt_ref, buf_ref, h0_ref, cw_ref, cb_ref, dtb_ref, alog_ref, dsk_ref, g_ref,
                       y_ref, buf_out_ref, h_out_ref):
    x = xbc_ref[0]
    buf = buf_ref[0]
    cw = cw_ref[...]
    conv = cb_ref[...] + cw[CONV_WIDTH - 1:CONV_WIDTH] * x
    for t in range(CONV_WIDTH - 1):
        conv = conv + cw[t:t + 1] * buf[t:t + 1]
    buf_out_ref[0, 0:CONV_WIDTH - 2, :] = buf[1:CONV_WIDTH - 1]
    buf_out_ref[0, CONV_WIDTH - 2:CONV_WIDTH - 1, :] = x
    xc = _silu(conv)
    xs = xc[:, :D_INNER]
    bm = xc[:, D_INNER:D_INNER + SSM_GROUPS * D_STATE]
    cm = xc[:, D_INNER + SSM_GROUPS * D_STATE:]

    lane1 = lax.broadcasted_iota(jnp.int32, (1, LANES), 1)
    rowi = lax.broadcasted_iota(jnp.int32, (LANES, LANES), 0)
    dt = _softplus(dt_ref[0] + dtb_ref[...])
    a = jnp.where(lane1 < SSM_HEADS, -jnp.exp(alog_ref[...]), 0.0)
    dec = jnp.exp(dt * a)
    dt_e = _expand_heads(dt, lane1 < SSM_HEAD_DIM)
    xdt = xs * dt_e
    xdt_rows = jnp.broadcast_to(xdt, (LANES, D_INNER))

    hpg = SSM_HEADS // SSM_GROUPS
    y_parts = []
    for pair in range(SSM_HEADS // 2):
        grp = (2 * pair) // hpg
        xcol = xdt_rows[:, pair * LANES:(pair + 1) * LANES].T
        dcol = jnp.where(rowi < SSM_HEAD_DIM,
                         jnp.broadcast_to(dec[:, 2 * pair:2 * pair + 1], (LANES, LANES)),
                         jnp.broadcast_to(dec[:, 2 * pair + 1:2 * pair + 2], (LANES, LANES)))
        h0 = h0_ref[0, 2 * pair:2 * pair + 2].reshape(LANES, D_STATE)
        hn = dcol * h0 + xcol * bm[:, grp * D_STATE:(grp + 1) * D_STATE]
        h_out_ref[0, 2 * pair:2 * pair + 2] = hn.reshape(2, SSM_HEAD_DIM, D_STATE)
        cgrow = jnp.broadcast_to(cm[:, grp * D_STATE:(grp + 1) * D_STATE], (8, D_STATE))
        ypair = _dot3(cgrow, hn, NT_DIMS)
        y_parts.append(ypair[0:1])
    y = jnp.concatenate(y_parts, axis=1) + dsk_ref[...] * xs
    y_ref[0] = _gated_group_norm(y, z_ref[0], g_ref[...]).astype(y_ref.dtype)


def _ssd_sample(proj, dtp, state_conv, state_ssm, conv_w, conv_b, dt_bias, a_log, d_skip, g_ssm):
    n = proj.shape[0]
    z0 = ATT_WIDTH + 2 * KV_WIDTH
    z3 = proj[:, z0:z0 + D_INNER].reshape(n, 1, D_INNER)
    x3 = proj[:, z0 + D_INNER:z0 + D_INNER + CONV_DIM].reshape(n, 1, CONV_DIM)
    dt3 = dtp.reshape(n, 1, LANES)
    const = lambda shape: pl.BlockSpec(shape, lambda s: tuple(0 for _ in shape))
    per_seq = lambda shape: pl.BlockSpec((1,) + shape, lambda s: (s,) + tuple(0 for _ in shape))
    y, buf, h = pl.pallas_call(
        _sample_ssd_kernel,
        grid=(n,),
        in_specs=[per_seq((1, D_INNER)), per_seq((1, CONV_DIM)), per_seq((1, LANES)),
                  per_seq((CONV_WIDTH - 1, CONV_DIM)), per_seq((SSM_HEADS, SSM_HEAD_DIM, D_STATE)),
                  const((CONV_WIDTH, CONV_DIM)), const((1, CONV_DIM)), const((1, LANES)), const((1, LANES)),
                  const((1, D_INNER)), const((1, D_INNER))],
        out_specs=[per_seq((1, D_INNER)), per_seq((CONV_WIDTH - 1, CONV_DIM)),
                   per_seq((SSM_HEADS, SSM_HEAD_DIM, D_STATE))],
        out_shape=[jax.ShapeDtypeStruct((n, 1, D_INNER), BF16),
                   jax.ShapeDtypeStruct((n, CONV_WIDTH - 1, CONV_DIM), F32),
                   jax.ShapeDtypeStruct((n, SSM_HEADS, SSM_HEAD_DIM, D_STATE), F32)],
        compiler_params=_params("arbitrary"),
        name="sample_ssd",
    )(z3, x3, dt3, state_conv, state_ssm, conv_w, conv_b.reshape(1, CONV_DIM), _pad_lanes(dt_bias),
      _pad_lanes(a_log), jnp.repeat(d_skip, SSM_HEAD_DIM).reshape(1, D_INNER), g_ssm.reshape(1, D_INNER))
    return y.reshape(n, D_INNER), buf, h


def kernel(x_prompt, x_sample, cache_k, cache_v, state_conv, state_ssm, page_table, c_prompt, c_sample, w_ada, b_ada, g_mix_norm, w_in, q_gain, k_gain, g_att_out, conv_w, conv_b, dt_bias, a_log, d_skip, g_ssm_out, w_out, g_ffn_norm, w_gate, w_up, w_down):
    n_p, seq, d = x_prompt.shape
    n_s, dec_seq, _ = x_sample.shape
    assert n_p == 1 and dec_seq == 1
    depth = w_ada.shape[0]
    main_w = ATT_WIDTH + 2 * KV_WIDTH + D_INNER + CONV_DIM

    yp = x_prompt.reshape(seq, d)
    ys = x_sample.reshape(n_s, d)
    c_rows = n_p + n_s
    c_pad = -(-c_rows // 16) * 16
    c_all = jnp.pad(jnp.concatenate([c_prompt, c_sample], axis=0), ((0, c_pad - c_rows), (0, 0)))
    outs = [[] for _ in range(8)]
    for l in range(depth):
        mod = _ada(c_all, w_ada[l], b_ada[l])
        mod_p, mod_s = mod[0:1], mod[1:1 + n_s]
        w_in_t = jnp.swapaxes(w_in[l], 0, 1)

        proj, dtp = _inproj(yp, g_mix_norm[l], mod_p, w_in_t, main_w, tn=1536, tm=512, name="in_proj")
        k_out, v_out, kp, vp, qp = _prompt_prep(proj, q_gain[l], k_gain[l])
        o_att, kmean_s = _prompt_attn(qp, kp, vp, cache_k, l, page_table)
        y_ssm, tail, st = _ssd_prompt(proj, dtp, conv_w[l], conv_b[l], dt_bias[l], a_log[l], d_skip[l], g_ssm_out[l])
        x1_p, h2_p = _outproj(o_att, g_att_out[l], y_ssm, w_out[l], yp, mod_p, g_ffn_norm[l])
        hpg = SSM_HEADS // SSM_GROUPS
        ssm_p = st.reshape(SSM_GROUPS, D_STATE, hpg, SSM_HEAD_DIM).transpose(0, 2, 3, 1).reshape(
            1, SSM_HEADS, SSM_HEAD_DIM, D_STATE)
        outs[0].append(k_out.reshape(1, seq, N_KV_HEADS, HEAD_DIM))
        outs[1].append(v_out.reshape(1, seq, N_KV_HEADS, HEAD_DIM))
        outs[2].append(tail[8 - (CONV_WIDTH - 1):].reshape(1, CONV_WIDTH - 1, CONV_DIM))
        outs[3].append(ssm_p)

        proj_s, dts = _inproj(ys, g_mix_norm[l], mod_s, w_in_t, main_w, tn=768, tm=512, name="in_proj_sample",
                              split=True)
        q_s, k_s, v_s = _sample_prep(proj_s, q_gain[l], k_gain[l])
        q3 = q_s.reshape(n_s, ATT_HEADS, HEAD_DIM)
        idx = _sample_topk(q3, kmean_s)
        idx_flat = idx[:, :, :MOBA_TOPK].reshape(n_s, ATT_HEADS * MOBA_TOPK)
        o_s = _sample_attn(q3, k_s.reshape(n_s, N_KV_HEADS, HEAD_DIM), v_s.reshape(n_s, N_KV_HEADS, HEAD_DIM),
                           cache_k, cache_v, l, page_table, idx_flat)
        y_s, buf_s, h_s = _ssd_sample(proj_s, dts, state_conv[l], state_ssm[l], conv_w[l], conv_b[l], dt_bias[l],
                                      a_log[l], d_skip[l], g_ssm_out[l])
        x1_s, h2_s = _outproj(o_s.reshape(n_s, ATT_WIDTH), g_att_out[l], y_s, w_out[l], ys, mod_s, g_ffn_norm[l])

        hid_p, hid_s = _gateup(h2_p, h2_s, w_gate[l], w_up[l])
        yp, ys = _down(hid_p, hid_s, w_down[l], x1_p, x1_s, mod_p, mod_s)
        outs[4].append(k_s.reshape(n_s, 1, N_KV_HEADS, HEAD_DIM))
        outs[5].append(v_s.reshape(n_s, 1, N_KV_HEADS, HEAD_DIM))
        outs[6].append(buf_s)
        outs[7].append(h_s)
    stacked = [jnp.stack(o) for o in outs]
    return (yp.reshape(1, seq, d), ys.reshape(n_s, 1, d), *stacked)
```

```python
import functools

import jax
import jax.numpy as jnp
from jax import lax
from jax.experimental import pallas as pl
from jax.experimental.pallas import tpu as pltpu

F32 = jnp.float32
BF16 = jnp.bfloat16

HEAD_DIM = 128
ATT_HEADS = 8
N_KV_HEADS = 4
KV_GROUP = ATT_HEADS // N_KV_HEADS
ATT_WIDTH = ATT_HEADS * HEAD_DIM
KV_WIDTH = N_KV_HEADS * HEAD_DIM
MOBA_BLOCK = 256
MOBA_TOPK = 3
D_INNER = 1024
SSM_HEAD_DIM = 64
SSM_HEADS = D_INNER // SSM_HEAD_DIM
SSM_GROUPS = 2
D_STATE = 128
CONV_WIDTH = 4
CONV_DIM = D_INNER + 2 * SSM_GROUPS * D_STATE
SSD_CHUNK = 128
EPS = 1e-6
ATT_SCALE = HEAD_DIM ** -0.5

LANES = 128
FEAT_KBLK = 96
FEAT_KOFF = 97
FEAT_ONE_A = 98
FEAT_ONE_B = 99
NEG_BIG = -1e30

NT_DIMS = (((1,), (1,)), ((), ()))
VMEM_LIMIT = 56 * 1024 * 1024


def _params(*sem):
    return pltpu.CompilerParams(dimension_semantics=sem, vmem_limit_bytes=VMEM_LIMIT)


def _silu(x):
    return x / (1.0 + jnp.exp(-x))


def _softplus(x):
    return jnp.maximum(x, 0.0) + jnp.log1p(jnp.exp(-jnp.abs(x)))


def _split_bf16(x):
    hi = x.astype(BF16)
    lo = (x - hi.astype(F32)).astype(BF16)
    return hi, lo


def _dot3(a, b, dims):
    ah, al = _split_bf16(a)
    bh, bl = _split_bf16(b)
    d = lambda x, y: lax.dot_general(x, y, dims, preferred_element_type=F32)
    return d(ah, bh) + d(al, bh) + d(ah, bl)


def _alibi_slope(h):
    return 2.0 ** (-8.0 * (h + 1) / ATT_HEADS)


NN_DIMS = (((1,), (0,)), ((), ()))


def _ada_kernel(c_ref, w_ref, b_ref, o_ref):
    rows = c_ref.shape[0]
    ahi, alo = _split_bf16(_silu(c_ref[...]))
    whi, wlo = _split_bf16(w_ref[...])
    both = jnp.dot(jnp.concatenate([ahi, alo], axis=0), whi, preferred_element_type=F32)
    o_ref[...] = both[:rows] + both[rows:] + jnp.dot(ahi, wlo, preferred_element_type=F32) + b_ref[...]


def _ada(c_all, w, b):
    rows, d = c_all.shape
    n = w.shape[1]
    tn = 512
    return pl.pallas_call(
        _ada_kernel,
        grid=(n // tn,),
        in_specs=[pl.BlockSpec((rows, d), lambda j: (0, 0)),
                  pl.BlockSpec((d, tn), lambda j: (0, j)),
                  pl.BlockSpec((1, tn), lambda j: (0, j))],
        out_specs=pl.BlockSpec((rows, tn), lambda j: (0, j)),
        out_shape=jax.ShapeDtypeStruct((rows, n), F32),
        compiler_params=_params("arbitrary"),
        name="ada_mod",
    )(c_all, w, b.reshape(1, n))


def _mod_spec(mod_rows, tm, width, col_of):
    if mod_rows == 1:
        return pl.BlockSpec((1, width), lambda n, i: (0, col_of(n)))
    return pl.BlockSpec((tm, width), lambda n, i: (i, col_of(n)))


def _rms(x):
    return x * lax.rsqrt(jnp.mean(x * x, axis=-1, keepdims=True) + EPS)


def _cast_weight_once(w_ref, wbf_ref):
    @pl.when(pl.program_id(1) == 0)
    def _():
        wbf_ref[...] = w_ref[...].astype(BF16)


STAGE_CHUNK = 384


def _inproj_kernel(n_main, split, x_ref, g_ref, sc_ref, sh_ref, w_ref, wdt_ref, o_ref, odt_ref, *w_scratch):
    n = pl.program_id(0)
    first_row_tile = pl.program_id(1) == 0
    main_refs, dt_refs = w_scratch[:len(w_scratch) // 2], w_scratch[len(w_scratch) // 2:]

    def stage(src_ref, refs, pad_rows):
        cols = src_ref.shape[0]
        chunk = STAGE_CHUNK if cols % STAGE_CHUNK == 0 else cols
        for c0 in range(0, cols, chunk):
            w = src_ref[c0:c0 + chunk, :]
            if pad_rows:
                w = jnp.concatenate([w, jnp.zeros((pad_rows, w.shape[1]), F32)], axis=0)
            w = w.T
            for part, ref in zip(_split_bf16(w) if split else (w.astype(BF16),), refs):
                ref[:, c0:c0 + w.shape[1]] = part

    def product(hn, refs):
        mm = lambda a, b: jnp.dot(a, b, preferred_element_type=F32)
        if not split:
            return mm(hn.astype(BF16), refs[0][...])
        hhi, hlo = _split_bf16(hn)
        return mm(hhi, refs[0][...]) + mm(hlo, refs[0][...]) + mm(hhi, refs[1][...])

    @pl.when(first_row_tile)
    def _():
        stage(w_ref, main_refs, 0)

    @pl.when(jnp.logical_and(n == 0, first_row_tile))
    def _():
        stage(wdt_ref, dt_refs, LANES - wdt_ref.shape[0])

    hn = (_rms(x_ref[...]) * g_ref[...]) * (1.0 + sc_ref[...]) + sh_ref[...]
    o_ref[...] = product(hn, main_refs)

    @pl.when(n == 0)
    def _():
        odt_ref[...] = product(hn, dt_refs)


def _inproj(x, g, mod, w_t, main_w, *, tn, tm, name, split=False):
    m, d = x.shape
    tm = min(tm, m)
    mr = mod.shape[0]
    n_main = main_w // tn
    n_dt = w_t.shape[0] - main_w
    assert n_main * tn == main_w and main_w % n_dt == 0 and n_dt % 16 == 0
    last_i = m // tm - 1
    ncopies = 2 if split else 1
    proj, dtp = pl.pallas_call(
        functools.partial(_inproj_kernel, n_main, split),
        grid=(n_main, m // tm),
        in_specs=[pl.BlockSpec((tm, d), lambda n, i: (i, 0)),
                  pl.BlockSpec((1, d), lambda n, i: (0, 0)),
                  _mod_spec(mr, tm, d, lambda n: 1),
                  _mod_spec(mr, tm, d, lambda n: 0),
                  pl.BlockSpec((tn, d), lambda n, i: (n, 0)),
                  pl.BlockSpec((n_dt, d), lambda n, i: (main_w // n_dt, 0))],
        out_specs=[pl.BlockSpec((tm, tn), lambda n, i: (i, n)),
                   pl.BlockSpec((tm, LANES), lambda n, i: (jnp.where(n == 0, i, last_i), 0))],
        out_shape=[jax.ShapeDtypeStruct((m, main_w), F32), jax.ShapeDtypeStruct((m, LANES), F32)],
        scratch_shapes=[pltpu.VMEM((d, tn), BF16)] * ncopies + [pltpu.VMEM((d, LANES), BF16)] * ncopies,
        compiler_params=_params("arbitrary", "arbitrary"),
        name=name,
    )(x, g.reshape(1, d), mod, mod, w_t, w_t)
    return proj, dtp


def _outproj_kernel(a_ref, ga_ref, b_ref, w_ref, x_ref, gt_ref, gf_ref, sc_ref, sh_ref, x1_ref, h2_ref, wbf_ref):
    _cast_weight_once(w_ref, wbf_ref)
    ka = a_ref.shape[1]
    a = (_rms(a_ref[...]) * ga_ref[...]).astype(BF16)
    acc = jnp.dot(a, wbf_ref[:ka, :], preferred_element_type=F32)
    acc = acc + jnp.dot(b_ref[...], wbf_ref[ka:, :], preferred_element_type=F32)
    x1 = x_ref[...] + gt_ref[...] * acc
    x1_ref[...] = x1
    h2_ref[...] = ((_rms(x1) * gf_ref[...]) * (1.0 + sc_ref[...]) + sh_ref[...]).astype(BF16)


def _outproj(a, g_a, b, w, x, mod, g_ffn):
    m, ka = a.shape
    kb = b.shape[1]
    d = w.shape[1]
    tm = min(256, m)
    mr = mod.shape[0]
    return pl.pallas_call(
        _outproj_kernel,
        grid=(1, m // tm),
        in_specs=[pl.BlockSpec((tm, ka), lambda n, i: (i, 0)),
                  pl.BlockSpec((1, ka), lambda n, i: (0, 0)),
                  pl.BlockSpec((tm, kb), lambda n, i: (i, 0)),
                  pl.BlockSpec((ka + kb, d), lambda n, i: (0, 0), pipeline_mode=pl.Buffered(1)),
                  pl.BlockSpec((tm, d), lambda n, i: (i, 0)),
                  _mod_spec(mr, tm, d, lambda n: 2),
                  pl.BlockSpec((1, d), lambda n, i: (0, 0)),
                  _mod_spec(mr, tm, d, lambda n: 4),
                  _mod_spec(mr, tm, d, lambda n: 3)],
        out_specs=[pl.BlockSpec((tm, d), lambda n, i: (i, 0)),
                   pl.BlockSpec((tm, d), lambda n, i: (i, 0))],
        out_shape=[jax.ShapeDtypeStruct((m, d), F32), jax.ShapeDtypeStruct((m, d), BF16)],
        scratch_shapes=[pltpu.VMEM((ka + kb, d), BF16)],
        compiler_params=_params("arbitrary", "arbitrary"),
        name="out_proj",
    )(a, g_a.reshape(1, ka), b, w, x, mod, g_ffn.reshape(1, d), mod, mod)


def _gateup_kernel(tiles_p, xp_ref, xs_ref, wg_ref, wu_ref, op_ref, os_ref, wgb_ref, wub_ref):
    _cast_weight_once(wg_ref, wgb_ref)
    _cast_weight_once(wu_ref, wub_ref)
    i = pl.program_id(1)

    def swiglu(h2):
        g = jnp.dot(h2, wgb_ref[...], preferred_element_type=F32)
        u = jnp.dot(h2, wub_ref[...], preferred_element_type=F32)
        return (_silu(g) * u).astype(BF16)

    @pl.when(i < tiles_p)
    def _():
        op_ref[...] = swiglu(xp_ref[...])

    @pl.when(i == tiles_p)
    def _():
        os_ref[...] = swiglu(xs_ref[...])


def _gateup(h2_p, h2_s, wg, wu):
    m, d = h2_p.shape
    ms = h2_s.shape[0]
    f = wg.shape[1]
    tn, tm = 512, min(2048, m)
    tiles_p = m // tm
    prow = lambda i: jnp.minimum(i, tiles_p - 1)
    return pl.pallas_call(
        functools.partial(_gateup_kernel, tiles_p),
        grid=(f // tn, tiles_p + 1),
        in_specs=[pl.BlockSpec((tm, d), lambda n, i: (prow(i), 0)),
                  pl.BlockSpec((ms, d), lambda n, i: (0, 0)),
                  pl.BlockSpec((d, tn), lambda n, i: (0, n)),
                  pl.BlockSpec((d, tn), lambda n, i: (0, n))],
        out_specs=[pl.BlockSpec((tm, tn), lambda n, i: (prow(i), n)),
                   pl.BlockSpec((ms, tn), lambda n, i: (0, n))],
        out_shape=[jax.ShapeDtypeStruct((m, f), BF16), jax.ShapeDtypeStruct((ms, f), BF16)],
        scratch_shapes=[pltpu.VMEM((d, tn), BF16), pltpu.VMEM((d, tn), BF16)],
        compiler_params=_params("arbitrary", "arbitrary"),
        name="ffn_gate_up",
    )(h2_p, h2_s, wg, wu)


def _stream_block_means(t, stream_steps, steps_per_seq, page_refs, km_ref):
    pps = len(page_refs)

    @pl.when(t < stream_steps)
    def _():
        page_rows = page_refs[0].shape[0]
        ppb = MOBA_BLOCK * N_KV_HEADS // page_rows
        fold = 8 // N_KV_HEADS
        j = t % steps_per_seq
        for b in range(pps // ppb):
            s8 = jnp.sum(page_refs[b * ppb][...].reshape(page_rows // 8, 8, HEAD_DIM), axis=0)
            for p in range(1, ppb):
                s8 = s8 + jnp.sum(page_refs[b * ppb + p][...].reshape(page_rows // 8, 8, HEAD_DIM), axis=0)
            s = s8[0:N_KV_HEADS]
            for p in range(1, fold):
                s = s + s8[p * N_KV_HEADS:(p + 1) * N_KV_HEADS]
            s = s * (1.0 / MOBA_BLOCK)
            for kv in range(N_KV_HEADS):
                km_ref[0, kv, j, b:b + 1, :] = s[kv:kv + 1, :]


def _page_stream(cache_k, layer, page_table, host_steps, step_of):
    n_seq, n_pages = page_table.shape
    depth, pool, page = cache_k.shape[:3]
    ppb = MOBA_BLOCK // page
    pps = min(32, n_pages)
    steps_per_seq = n_pages // pps
    stream_steps = n_seq * steps_per_seq
    assert n_pages % pps == 0 and pps % ppb == 0 and 8 % N_KV_HEADS == 0
    assert stream_steps <= host_steps, "not enough grid steps to stream the cache"
    view = cache_k.reshape(depth, pool, page * N_KV_HEADS, HEAD_DIM)

    def stream_step(*grid_idx):
        return jnp.minimum(step_of(*grid_idx), stream_steps - 1)

    def page_spec(k):
        return pl.BlockSpec((None, None, page * N_KV_HEADS, HEAD_DIM),
                            lambda *args: (layer, args[-1][stream_step(*args[:-1]) * pps + k], 0, 0))

    bps = pps // ppb
    km_spec = pl.BlockSpec((1, N_KV_HEADS, steps_per_seq, bps, HEAD_DIM),
                           lambda *args: (stream_step(*args[:-1]) // steps_per_seq, 0, 0, 0, 0))
    km_shape = jax.ShapeDtypeStruct((n_seq, N_KV_HEADS, steps_per_seq, bps, HEAD_DIM), F32)
    return (view, page_table.reshape(-1), [page_spec(k) for k in range(pps)], km_spec, km_shape, stream_steps,
            steps_per_seq)


def _down_kernel(tiles_p, hp_ref, hs_ref, w_ref, xp_ref, xs_ref, gtp_ref, gts_ref, op_ref, os_ref, wbf_ref):
    _cast_weight_once(w_ref, wbf_ref)
    i = pl.program_id(1)

    @pl.when(i < tiles_p)
    def _():
        acc = jnp.dot(hp_ref[...], wbf_ref[...], preferred_element_type=F32)
        op_ref[...] = xp_ref[...] + gtp_ref[...] * acc

    @pl.when(i == tiles_p)
    def _():
        acc = jnp.dot(hs_ref[...], wbf_ref[...], preferred_element_type=F32)
        os_ref[...] = xs_ref[...] + gts_ref[...] * acc


def _down(h_p, h_s, w, x_p, x_s, mod_p, mod_s):
    m, f = h_p.shape
    ms = h_s.shape[0]
    d = w.shape[1]
    tn, tm = 512, 512
    tiles_p = m // tm
    nb = d // tn
    assert mod_p.shape[0] == 1 and mod_s.shape[0] == ms
    prow = lambda i: jnp.minimum(i, tiles_p - 1)
    return pl.pallas_call(
        functools.partial(_down_kernel, tiles_p),
        grid=(nb, tiles_p + 1),
        in_specs=[pl.BlockSpec((tm, f), lambda n, i: (prow(i), 0)),
                  pl.BlockSpec((ms, f), lambda n, i: (0, 0)),
                  pl.BlockSpec((f, tn), lambda n, i: (0, n)),
                  pl.BlockSpec((tm, tn), lambda n, i: (prow(i), n)),
                  pl.BlockSpec((ms, tn), lambda n, i: (0, n)),
                  pl.BlockSpec((1, tn), lambda n, i: (0, 5 * nb + n)),
                  pl.BlockSpec((ms, tn), lambda n, i: (0, 5 * nb + n))],
        out_specs=[pl.BlockSpec((tm, tn), lambda n, i: (prow(i), n)),
                   pl.BlockSpec((ms, tn), lambda n, i: (0, n))],
        out_shape=[jax.ShapeDtypeStruct((m, d), F32), jax.ShapeDtypeStruct((ms, d), F32)],
        scratch_shapes=[pltpu.VMEM((f, tn), BF16)],
        compiler_params=_params("arbitrary", "arbitrary"),
        name="ffn_down",
    )(h_p, h_s, w, x_p, x_s, mod_p, mod_s)


def _head_norm(x, gain):
    return x * lax.rsqrt(jnp.mean(x * x, axis=-1, keepdims=True) + EPS) * gain


def _top3(g, idx_f, axis):
    sel = jnp.zeros(g.shape, F32)
    firsts = []
    for _ in range(MOBA_TOPK):
        mx = jnp.max(g, axis=axis, keepdims=True)
        ismax = jnp.logical_and(g == mx, mx > -jnp.inf)
        first = jnp.min(jnp.where(ismax, idx_f, float(LANES)), axis=axis, keepdims=True)
        pick = idx_f == first
        sel = jnp.where(pick, 1.0, sel)
        g = jnp.where(pick, -jnp.inf, g)
        firsts.append(first)
    return sel, firsts


def _prompt_prep_kernel(proj_ref, qg_ref, kg_ref, kout_ref, vout_ref, kp_ref, vp_ref, qp_ref, km_ref):
    i = pl.program_id(0)
    blk = MOBA_BLOCK

    @pl.when(i == 0)
    def _():
        km_ref[...] = jnp.zeros_like(km_ref)

    lane = lax.broadcasted_iota(jnp.int32, (blk, LANES), 1)
    row = lax.broadcasted_iota(jnp.int32, (blk, LANES), 0)
    sq_row = lax.broadcasted_iota(jnp.int32, (LANES, LANES), 0)
    i_f = i.astype(F32)
    row_f = row.astype(F32)

    kfeat = jnp.where(lane == i, 1.0, 0.0)
    kfeat = jnp.where(lane == FEAT_KBLK, i_f, kfeat)
    kfeat = jnp.where(lane == FEAT_KOFF, row_f, kfeat)
    kfeat = jnp.where(jnp.logical_or(lane == FEAT_ONE_A, lane == FEAT_ONE_B), 1.0, kfeat).astype(BF16)
    vfeat = jnp.ones((blk, LANES), BF16)

    kg = kg_ref[...]
    for kv in range(N_KV_HEADS):
        k = proj_ref[:, ATT_WIDTH + kv * HEAD_DIM:ATT_WIDTH + (kv + 1) * HEAD_DIM]
        kn = _head_norm(k, kg)
        kout_ref[:, kv * HEAD_DIM:(kv + 1) * HEAD_DIM] = kn
        kp_ref[kv, :, :HEAD_DIM] = kn.astype(BF16)
        kp_ref[kv, :, HEAD_DIM:] = kfeat
        ksum = jnp.sum(kn, axis=0, keepdims=True) * (1.0 / blk)
        km_ref[kv] = jnp.where(sq_row == i, jnp.broadcast_to(ksum, (LANES, LANES)), km_ref[kv])
        v = proj_ref[:, ATT_WIDTH + KV_WIDTH + kv * HEAD_DIM:ATT_WIDTH + KV_WIDTH + (kv + 1) * HEAD_DIM]
        vout_ref[:, kv * HEAD_DIM:(kv + 1) * HEAD_DIM] = v
        vp_ref[kv, :, :HEAD_DIM] = v.astype(BF16)
        vp_ref[kv, :, HEAD_DIM:] = vfeat

    qg = qg_ref[...]
    blk_id = lax.broadcasted_iota(jnp.int32, (LANES, blk), 0)
    blk_id_f = blk_id.astype(F32)
    valid = blk_id < i
    for h in range(ATT_HEADS):
        q = proj_ref[:, h * HEAD_DIM:(h + 1) * HEAD_DIM]
        qn = _head_norm(q, qg)
        gate = _dot3(km_ref[h // KV_GROUP], qn, NT_DIMS)
        sel_t, _ = _top3(jnp.where(valid, gate, -jnp.inf), blk_id_f, 0)
        sel = sel_t.T
        slope = _alibi_slope(h)
        qfeat = jnp.where(jnp.logical_and(lane < FEAT_KBLK, sel == 0.0), NEG_BIG, 0.0)
        qfeat = jnp.where(lane == FEAT_KBLK, slope * blk, qfeat)
        qfeat = jnp.where(lane == FEAT_KOFF, slope, qfeat)
        qfeat = jnp.where(lane == FEAT_ONE_A, -(slope * blk) * i_f, qfeat)
        qfeat = jnp.where(lane == FEAT_ONE_B, -slope * row_f, qfeat)
        qp_ref[h, :, :HEAD_DIM] = (qn * ATT_SCALE).astype(BF16)
        qp_ref[h, :, HEAD_DIM:] = qfeat.astype(BF16)


def _prompt_prep(proj, q_gain, k_gain):
    m = proj.shape[0]
    nb = m // MOBA_BLOCK
    assert m % MOBA_BLOCK == 0 and nb <= FEAT_KBLK
    blk = MOBA_BLOCK
    qkv_w = ATT_WIDTH + 2 * KV_WIDTH
    return pl.pallas_call(
        _prompt_prep_kernel,
        grid=(nb,),
        in_specs=[pl.BlockSpec((blk, qkv_w), lambda i: (i, 0)),
                  pl.BlockSpec((1, HEAD_DIM), lambda i: (0, 0)),
                  pl.BlockSpec((1, HEAD_DIM), lambda i: (0, 0))],
        out_specs=[pl.BlockSpec((blk, KV_WIDTH), lambda i: (i, 0)),
                   pl.BlockSpec((blk, KV_WIDTH), lambda i: (i, 0)),
                   pl.BlockSpec((N_KV_HEADS, blk, 2 * HEAD_DIM), lambda i: (0, i, 0)),
                   pl.BlockSpec((N_KV_HEADS, blk, 2 * HEAD_DIM), lambda i: (0, i, 0)),
                   pl.BlockSpec((ATT_HEADS, blk, 2 * HEAD_DIM), lambda i: (0, i, 0))],
        out_shape=[jax.ShapeDtypeStruct((m, KV_WIDTH), F32),
                   jax.ShapeDtypeStruct((m, KV_WIDTH), F32),
                   jax.ShapeDtypeStruct((N_KV_HEADS, m, 2 * HEAD_DIM), BF16),
                   jax.ShapeDtypeStruct((N_KV_HEADS, m, 2 * HEAD_DIM), BF16),
                   jax.ShapeDtypeStruct((ATT_HEADS, m, 2 * HEAD_DIM), BF16)],
        scratch_shapes=[pltpu.VMEM((N_KV_HEADS, LANES, LANES), F32)],
        compiler_params=_params("arbitrary"),
        name="prompt_qk_prep",
    )(proj, q_gain.reshape(1, HEAD_DIM), k_gain.reshape(1, HEAD_DIM))


ATTN_UNROLL = 4
LOG2E = 1.4426950408889634


def _prompt_attn_kernel(pps, stream_steps, steps_per_seq, pt_ref, q_ref, k_ref, v_ref, *rest):
    page_refs = rest[:pps]
    o_ref, km_ref, s_ref, acc_ref, m_ref = rest[pps:]
    i = pl.program_id(1)
    _stream_block_means(pl.program_id(0) * pl.num_programs(1) + i, stream_steps, steps_per_seq, page_refs, km_ref)
    blk = MOBA_BLOCK
    rows = KV_GROUP * blk
    span = ATTN_UNROLL * blk
    qs = q_ref[...].reshape(rows, 2 * HEAD_DIM)

    def lane_fold(s):
        out = s[:, :LANES]
        for t in range(1, s.shape[1] // LANES):
            out = jnp.maximum(out, s[:, t * LANES:(t + 1) * LANES])
        return out

    def probs(s):
        mb = m_ref[...]
        return jnp.concatenate([jnp.exp2(s[:, t * LANES:(t + 1) * LANES] - mb) for t in range(s.shape[1] // LANES)],
                               axis=1).astype(BF16)

    own = pl.ds(pl.multiple_of(i * blk, blk), blk)
    r = lax.broadcasted_iota(jnp.int32, (rows, blk), 0)
    c = lax.broadcasted_iota(jnp.int32, (rows, blk), 1)
    dist = jnp.bitwise_and(r, blk - 1) - c
    slope = qs[:, HEAD_DIM + FEAT_KOFF:HEAD_DIM + FEAT_KOFF + 1].astype(F32)
    s_own = lax.dot_general(qs[:, :HEAD_DIM], k_ref[own, :HEAD_DIM], NT_DIMS, preferred_element_type=F32)
    s_own = jnp.where(dist >= 0, (s_own - slope * dist.astype(F32)) * LOG2E, -jnp.inf)
    m_ref[...] = lane_fold(s_own)

    trips = (i + ATTN_UNROLL - 1) // ATTN_UNROLL

    def two_trips_per_iteration(trip):
        def pair(u, carry):
            trip(2 * u)
            trip(2 * u + 1)
            return carry

        lax.fori_loop(0, trips // 2, pair, 0)

        @pl.when(trips % 2 == 1)
        def _():
            trip(trips - 1)

    def pass1(t):
        ks = k_ref[pl.ds(pl.multiple_of(t * span, span), span), :]
        s = lax.dot_general(qs, ks, NT_DIMS, preferred_element_type=F32) * LOG2E
        s_ref[t] = s
        m_ref[...] = jnp.maximum(m_ref[...], lane_fold(s))

    two_trips_per_iteration(pass1)
    m_ref[...] = jnp.broadcast_to(jnp.max(m_ref[...], axis=1, keepdims=True), (rows, LANES))

    acc_ref[...] = jnp.dot(probs(s_own), v_ref[own, :], preferred_element_type=F32)

    def pass2(t):
        vs = v_ref[pl.ds(pl.multiple_of(t * span, span), span), :]
        acc_ref[...] += jnp.dot(probs(s_ref[t]), vs, preferred_element_type=F32)

    two_trips_per_iteration(pass2)
    acc = acc_ref[...]
    o = acc[:, :HEAD_DIM] / acc[:, HEAD_DIM:]
    for g in range(KV_GROUP):
        o_ref[:, g * HEAD_DIM:(g + 1) * HEAD_DIM] = o[g * blk:(g + 1) * blk]


def _prompt_attn(qp, kp, vp, cache_k, layer, page_table):
    m = kp.shape[1]
    nb = m // MOBA_BLOCK
    assert nb % ATTN_UNROLL == 0
    blk = MOBA_BLOCK
    rows = KV_GROUP * blk
    view, flat_pages, page_specs, km_spec, km_shape, stream_steps, steps_per_seq = _page_stream(
        cache_k, layer, page_table, N_KV_HEADS * nb, lambda kv, i: kv * nb + i)
    resident = lambda: pl.BlockSpec((None, m, 2 * HEAD_DIM), lambda kv, i, pt: (kv, 0, 0), pipeline_mode=pl.Buffered(1))
    grid_spec = pltpu.PrefetchScalarGridSpec(
        num_scalar_prefetch=1,
        grid=(N_KV_HEADS, nb),
        in_specs=[pl.BlockSpec((KV_GROUP, blk, 2 * HEAD_DIM), lambda kv, i, pt: (kv, i, 0)), resident(), resident()]
        + page_specs,
        out_specs=[pl.BlockSpec((blk, KV_GROUP * HEAD_DIM), lambda kv, i, pt: (i, kv)), km_spec],
        scratch_shapes=[pltpu.VMEM((nb // ATTN_UNROLL, rows, ATTN_UNROLL * blk), F32),
                        pltpu.VMEM((rows, 2 * HEAD_DIM), F32),
                        pltpu.VMEM((rows, LANES), F32)],
    )
    o, km = pl.pallas_call(
        functools.partial(_prompt_attn_kernel, len(page_specs), stream_steps, steps_per_seq),
        grid_spec=grid_spec,
        out_shape=[jax.ShapeDtypeStruct((m, ATT_WIDTH), F32), km_shape],
        compiler_params=_params("arbitrary", "arbitrary"),
        name="prompt_moba_attn",
    )(flat_pages, qp, kp, vp, *([view] * len(page_specs)))
    return o, km.reshape(km.shape[0], N_KV_HEADS, -1, HEAD_DIM)


def _expand_heads(v, lane_lo):
    r = v.shape[0]
    parts = []
    for k in range(SSM_HEADS // 2):
        a0 = jnp.broadcast_to(v[:, 2 * k:2 * k + 1], (r, LANES))
        a1 = jnp.broadcast_to(v[:, 2 * k + 1:2 * k + 2], (r, LANES))
        parts.append(jnp.where(lane_lo, a0, a1))
    return jnp.concatenate(parts, axis=1)


def _gated_group_norm(y, z, g):
    yz = y * _silu(z)
    gw = D_INNER // SSM_GROUPS
    outs = []
    for grp in range(SSM_GROUPS):
        t = yz[:, grp * gw:(grp + 1) * gw]
        t = t * lax.rsqrt(jnp.mean(t * t, axis=-1, keepdims=True) + EPS)
        outs.append(t * g[:, grp * gw:(grp + 1) * gw])
    return jnp.concatenate(outs, axis=1)


def _ssd_kernel(z_ref, xbc_ref, dt_ref, cw_ref, cb_ref, dtb_ref, alog_ref, dsk_ref, g_ref,
                y_ref, tail_out_ref, st_out_ref, tail_ref, st_ref):
    c = pl.program_id(0)
    cs = SSD_CHUNK
    gw = D_INNER // SSM_GROUPS

    @pl.when(c == 0)
    def _():
        tail_ref[...] = jnp.zeros_like(tail_ref)
        st_ref[...] = jnp.zeros_like(st_ref)

    xr = xbc_ref[...]
    xp = jnp.concatenate([tail_ref[...], xr], axis=0)
    cw = cw_ref[...]
    conv = cb_ref[...] + cw[3:4] * xr
    for t in range(CONV_WIDTH - 1):
        conv = conv + cw[t:t + 1] * xp[8 - (CONV_WIDTH - 1) + t:8 - (CONV_WIDTH - 1) + t + cs]
    tail_ref[...] = xr[cs - 8:]
    tail_out_ref[...] = xr[cs - 8:]
    xc = _silu(conv)
    xs = xc[:, :D_INNER]
    bm = xc[:, D_INNER:D_INNER + SSM_GROUPS * D_STATE]
    cm = xc[:, D_INNER + SSM_GROUPS * D_STATE:]

    lane = lax.broadcasted_iota(jnp.int32, (cs, LANES), 1)
    rowi = lax.broadcasted_iota(jnp.int32, (cs, LANES), 0)
    lane_lo = lane < SSM_HEAD_DIM
    tri = rowi >= lane

    dt = _softplus(dt_ref[...] + dtb_ref[...])
    a = jnp.where(lane[:1] < SSM_HEADS, -jnp.exp(alog_ref[...]), 0.0)
    da = dt * a
    tril = jnp.where(tri, 1.0, 0.0).astype(BF16)
    p1 = da.astype(BF16)
    r1 = da - p1.astype(F32)
    p2 = r1.astype(BF16)
    p3 = (r1 - p2.astype(F32)).astype(BF16)
    acum = (jnp.dot(tril, p1, preferred_element_type=F32) + jnp.dot(tril, p2, preferred_element_type=F32)
            + jnp.dot(tril, p3, preferred_element_type=F32))
    acum_t = acum.T

    dt_e = _expand_heads(dt, lane_lo)
    ac_e = _expand_heads(acum, lane_lo)
    xdt = xs * dt_e
    ea_e = jnp.exp(ac_e)
    dend_e = jnp.exp(ac_e[cs - 1:cs, :] - ac_e)
    cdec = ea_e[cs - 1:cs, :]
    xdt_bf = xdt.astype(BF16)
    xdec_bf = (xdt * dend_e).astype(BF16)

    y_parts = []
    for grp in range(SSM_GROUPS):
        bg = bm[:, grp * D_STATE:(grp + 1) * D_STATE]
        cg = cm[:, grp * D_STATE:(grp + 1) * D_STATE].astype(BF16)
        cb = lax.dot_general(cg, bg.astype(BF16), NT_DIMS, preferred_element_type=F32)
        hpg = SSM_HEADS // SSM_GROUPS
        intra = []
        for k in range(hpg // 2):
            pair = grp * (hpg // 2) + k
            xpair = xdt_bf[:, pair * LANES:(pair + 1) * LANES]
            acc = None
            for hh in range(2):
                h = 2 * pair + hh
                seg = jnp.broadcast_to(acum[:, h:h + 1], (cs, cs)) - acum_t[h:h + 1, :]
                lmat = jnp.exp(jnp.where(tri, seg, -jnp.inf))
                mh = (cb * lmat).astype(BF16)
                xh = jnp.where(lane_lo if hh == 0 else jnp.logical_not(lane_lo), xpair, jnp.zeros_like(xpair))
                part = jnp.dot(mh, xh, preferred_element_type=F32)
                acc = part if acc is None else acc + part
            intra.append(acc)
        y_intra = jnp.concatenate(intra, axis=1)
        st = st_ref[grp]
        y_inter = jnp.dot(cg, st.astype(BF16), preferred_element_type=F32) * ea_e[:, grp * gw:(grp + 1) * gw]
        new_st = cdec[:, grp * gw:(grp + 1) * gw] * st + jnp.dot(
            bg.T.astype(BF16), xdec_bf[:, grp * gw:(grp + 1) * gw], preferred_element_type=F32)
        st_ref[grp] = new_st
        st_out_ref[grp] = new_st
        y_parts.append(y_intra + y_inter)
    y = jnp.concatenate(y_parts, axis=1) + dsk_ref[...] * xs
    y_ref[...] = _gated_group_norm(y, z_ref[...], g_ref[...]).astype(y_ref.dtype)


def _pad_lanes(v):
    return jnp.pad(v.reshape(1, -1), ((0, 0), (0, LANES - v.size)))


def _ssd_prompt(proj, dtp, conv_w, conv_b, dt_bias, a_log, d_skip, g_ssm):
    m = proj.shape[0]
    cs = SSD_CHUNK
    assert m % cs == 0
    gw = D_INNER // SSM_GROUPS
    z_blk = (ATT_WIDTH + 2 * KV_WIDTH) // D_INNER
    x_blk = (ATT_WIDTH + 2 * KV_WIDTH + D_INNER) // CONV_DIM
    assert z_blk * D_INNER == ATT_WIDTH + 2 * KV_WIDTH and x_blk * CONV_DIM == ATT_WIDTH + 2 * KV_WIDTH + D_INNER
    const = lambda shape: pl.BlockSpec(shape, lambda c: tuple(0 for _ in shape))
    return pl.pallas_call(
        _ssd_kernel,
        grid=(m // cs,),
        in_specs=[pl.BlockSpec((cs, D_INNER), lambda c: (c, z_blk)),
                  pl.BlockSpec((cs, CONV_DIM), lambda c: (c, x_blk)),
                  pl.BlockSpec((cs, LANES), lambda c: (c, 0)),
                  const((CONV_WIDTH, CONV_DIM)), const((1, CONV_DIM)), const((1, LANES)), const((1, LANES)),
                  const((1, D_INNER)), const((1, D_INNER))],
        out_specs=[pl.BlockSpec((cs, D_INNER), lambda c: (c, 0)),
                   const((8, CONV_DIM)), const((SSM_GROUPS, D_STATE, gw))],
        out_shape=[jax.ShapeDtypeStruct((m, D_INNER), BF16),
                   jax.ShapeDtypeStruct((8, CONV_DIM), F32),
                   jax.ShapeDtypeStruct((SSM_GROUPS, D_STATE, gw), F32)],
        scratch_shapes=[pltpu.VMEM((8, CONV_DIM), F32), pltpu.VMEM((SSM_GROUPS, D_STATE, gw), F32)],
        compiler_params=_params("arbitrary"),
        name="prompt_ssd",
    )(proj, proj, dtp, conv_w, conv_b.reshape(1, CONV_DIM), _pad_lanes(dt_bias), _pad_lanes(a_log),
      jnp.repeat(d_skip, SSM_HEAD_DIM).reshape(1, D_INNER), g_ssm.reshape(1, D_INNER))


def _sample_prep_kernel(proj_ref, qg_ref, kg_ref, q_ref, k_ref, v_ref):
    for h in range(ATT_HEADS):
        q_ref[:, h * HEAD_DIM:(h + 1) * HEAD_DIM] = _head_norm(proj_ref[:, h * HEAD_DIM:(h + 1) * HEAD_DIM], qg_ref[...])
    for kv in range(N_KV_HEADS):
        lo = ATT_WIDTH + kv * HEAD_DIM
        k_ref[:, kv * HEAD_DIM:(kv + 1) * HEAD_DIM] = _head_norm(proj_ref[:, lo:lo + HEAD_DIM], kg_ref[...])
    v_ref[...] = proj_ref[:, ATT_WIDTH + KV_WIDTH:ATT_WIDTH + 2 * KV_WIDTH]


def _sample_prep(proj, q_gain, k_gain):
    n = proj.shape[0]
    qkv_w = ATT_WIDTH + 2 * KV_WIDTH
    return pl.pallas_call(
        _sample_prep_kernel,
        grid=(1,),
        in_specs=[pl.BlockSpec((n, qkv_w), lambda i: (0, 0)),
                  pl.BlockSpec((1, HEAD_DIM), lambda i: (0, 0)),
                  pl.BlockSpec((1, HEAD_DIM), lambda i: (0, 0))],
        out_specs=[pl.BlockSpec((n, ATT_WIDTH), lambda i: (0, 0)),
                   pl.BlockSpec((n, KV_WIDTH), lambda i: (0, 0)),
                   pl.BlockSpec((n, KV_WIDTH), lambda i: (0, 0))],
        out_shape=[jax.ShapeDtypeStruct((n, ATT_WIDTH), F32),
                   jax.ShapeDtypeStruct((n, KV_WIDTH), F32),
                   jax.ShapeDtypeStruct((n, KV_WIDTH), F32)],
        compiler_params=_params("arbitrary"),
        name="sample_qk_prep",
    )(proj, q_gain.reshape(1, HEAD_DIM), k_gain.reshape(1, HEAD_DIM))


def _sample_topk_kernel(q_ref, km_ref, idx_ref):
    nblk = km_ref.shape[2]
    hrow = lax.broadcasted_iota(jnp.int32, (ATT_HEADS, nblk), 0)
    lane = lax.broadcasted_iota(jnp.int32, (ATT_HEADS, LANES), 1)
    for s in range(q_ref.shape[0]):
        q = q_ref[s]
        gate = jnp.zeros((ATT_HEADS, nblk), F32)
        for kv in range(N_KV_HEADS):
            gk = _dot3(q, km_ref[s, kv], NT_DIMS)
            gate = jnp.where(hrow // KV_GROUP == kv, gk, gate)
        if nblk < LANES:
            gate = jnp.concatenate([gate, jnp.full((ATT_HEADS, LANES - nblk), -jnp.inf, F32)], axis=1)
        _, firsts = _top3(gate, lane.astype(F32), 1)
        out = jnp.zeros((ATT_HEADS, LANES), jnp.int32)
        for t, first in enumerate(firsts):
            out = jnp.where(lane == t, first.astype(jnp.int32), out)
        idx_ref[s] = out


def _sample_topk(q3, kmean):
    n, _, nblk, _ = kmean.shape
    assert MOBA_TOPK <= nblk <= LANES
    sb = 8 if n % 8 == 0 else 1
    return pl.pallas_call(
        _sample_topk_kernel,
        grid=(n // sb,),
        in_specs=[pl.BlockSpec((sb, ATT_HEADS, HEAD_DIM), lambda s: (s, 0, 0)),
                  pl.BlockSpec((sb, N_KV_HEADS, nblk, HEAD_DIM), lambda s: (s, 0, 0, 0))],
        out_specs=pl.BlockSpec((sb, ATT_HEADS, LANES), lambda s: (s, 0, 0)),
        out_shape=jax.ShapeDtypeStruct((n, ATT_HEADS, LANES), jnp.int32),
        compiler_params=_params("arbitrary"),
        name="sample_gate_topk",
    )(q3, kmean)


def _sample_attn_kernel(past, ppb, layer, pt_ref, idx_ref, q_ref, kn_ref, vn_ref, ck_ref, cv_ref, o_ref,
                        kbuf, vbuf, sems):
    s = pl.program_id(0)
    page = MOBA_BLOCK // ppb
    slot = s % 2

    def copies(seq, buf, h, t, p):
        kv = h // KV_GROUP
        blk = idx_ref[seq, h * MOBA_TOPK + t]
        phys = pt_ref[seq, blk * ppb + p]
        dst = pl.ds((t * ppb + p) * page, page)
        return (pltpu.make_async_copy(ck_ref.at[layer, phys, :, kv, :], kbuf.at[buf, h, dst, :],
                                      sems.at[buf, 0, h, t * ppb + p]),
                pltpu.make_async_copy(cv_ref.at[layer, phys, :, kv, :], vbuf.at[buf, h, dst, :],
                                      sems.at[buf, 1, h, t * ppb + p]))

    triples = [(h, t, p) for h in range(ATT_HEADS) for t in range(MOBA_TOPK) for p in range(ppb)]

    def start_all(seq, buf):
        for h, t, p in triples:
            ck, cv = copies(seq, buf, h, t, p)
            ck.start()
            cv.start()

    @pl.when(s == 0)
    def _():
        start_all(0, 0)

    @pl.when(s + 1 < pl.num_programs(0))
    def _():
        start_all(s + 1, 1 - slot)

    for h, t, p in triples:
        ck, cv = copies(s, slot, h, t, p)
        ck.wait()
        cv.wait()

    nsel = MOBA_TOPK * MOBA_BLOCK
    rowi = lax.broadcasted_iota(jnp.int32, (nsel, 1), 0)
    off = jnp.bitwise_and(rowi, MOBA_BLOCK - 1)
    for h in range(ATT_HEADS):
        kv = h // KV_GROUP
        slope = _alibi_slope(h)
        q = q_ref[0, h:h + 1, :]
        sc = jnp.sum(kbuf[slot, h] * q, axis=1, keepdims=True) * ATT_SCALE
        pos = jnp.zeros((nsel, 1), jnp.int32)
        for t in range(MOBA_TOPK):
            pos = jnp.where(rowi // MOBA_BLOCK == t, idx_ref[s, h * MOBA_TOPK + t] * MOBA_BLOCK, pos)
        dist = (past - (pos + off)).astype(F32)
        sc = sc - slope * dist
        s_own = jnp.sum(kn_ref[0, kv:kv + 1, :] * q, axis=1, keepdims=True) * ATT_SCALE
        mx = jnp.maximum(jnp.max(sc, axis=0, keepdims=True), s_own)
        p = jnp.exp(sc - mx)
        p_own = jnp.exp(s_own - mx)
        denom = jnp.sum(p, axis=0, keepdims=True) + p_own
        num = jnp.sum(p * vbuf[slot, h], axis=0, keepdims=True) + p_own * vn_ref[0, kv:kv + 1, :]
        o_ref[0, h:h + 1, :] = num / denom


def _sample_attn(q3, k_new3, v_new3, cache_k, cache_v, layer, page_table, idx):
    n, n_pages = page_table.shape
    page = cache_k.shape[2]
    ppb = MOBA_BLOCK // page
    past = n_pages * page
    assert past % MOBA_BLOCK == 0
    nsel = MOBA_TOPK * MOBA_BLOCK
    grid_spec = pltpu.PrefetchScalarGridSpec(
        num_scalar_prefetch=2,
        grid=(n,),
        in_specs=[pl.BlockSpec((1, ATT_HEADS, HEAD_DIM), lambda s, pt, ix: (s, 0, 0)),
                  pl.BlockSpec((1, N_KV_HEADS, HEAD_DIM), lambda s, pt, ix: (s, 0, 0)),
                  pl.BlockSpec((1, N_KV_HEADS, HEAD_DIM), lambda s, pt, ix: (s, 0, 0)),
                  pl.BlockSpec(memory_space=pl.ANY),
                  pl.BlockSpec(memory_space=pl.ANY)],
        out_specs=pl.BlockSpec((1, ATT_HEADS, HEAD_DIM), lambda s, pt, ix: (s, 0, 0)),
        scratch_shapes=[pltpu.VMEM((2, ATT_HEADS, nsel, HEAD_DIM), F32),
                        pltpu.VMEM((2, ATT_HEADS, nsel, HEAD_DIM), F32),
                        pltpu.SemaphoreType.DMA((2, 2, ATT_HEADS, MOBA_TOPK * ppb))],
    )
    return pl.pallas_call(
        functools.partial(_sample_attn_kernel, past, ppb, layer),
        grid_spec=grid_spec,
        out_shape=jax.ShapeDtypeStruct((n, ATT_HEADS, HEAD_DIM), F32),
        compiler_params=_params("arbitrary"),
        name="sample_moba_attn",
    )(page_table, idx, q3, k_new3, v_new3, cache_k, cache_v)


def _sample_ssd_kernel(z_ref, xbc_ref, dt_ref, buf_ref, h0_ref, cw_ref, cb_ref, dtb_ref, alog_ref, dsk_ref, g_ref,
                       y_ref, buf_out_ref, h_out_ref):
    x = xbc_ref[0]
    buf = buf_ref[0]
    cw = cw_ref[...]
    conv = cb_ref[...] + cw[CONV_WIDTH - 1:CONV_WIDTH] * x
    for t in range(CONV_WIDTH - 1):
        conv = conv + cw[t:t + 1] * buf[t:t + 1]
    buf_out_ref[0, 0:CONV_WIDTH - 2, :] = buf[1:CONV_WIDTH - 1]
    buf_out_ref[0, CONV_WIDTH - 2:CONV_WIDTH - 1, :] = x
    xc = _silu(conv)
    xs = xc[:, :D_INNER]
    bm = xc[:, D_INNER:D_INNER + SSM_GROUPS * D_STATE]
    cm = xc[:, D_INNER + SSM_GROUPS * D_STATE:]

    lane1 = lax.broadcasted_iota(jnp.int32, (1, LANES), 1)
    rowi = lax.broadcasted_iota(jnp.int32, (LANES, LANES), 0)
    dt = _softplus(dt_ref[0] + dtb_ref[...])
    a = jnp.where(lane1 < SSM_HEADS, -jnp.exp(alog_ref[...]), 0.0)
    dec = jnp.exp(dt * a)
    dt_e = _expand_heads(dt, lane1 < SSM_HEAD_DIM)
    xdt = xs * dt_e
    xdt_rows = jnp.broadcast_to(xdt, (LANES, D_INNER))

    hpg = SSM_HEADS // SSM_GROUPS
    y_parts = []
    for pair in range(SSM_HEADS // 2):
        grp = (2 * pair) // hpg
        xcol = xdt_rows[:, pair * LANES:(pair + 1) * LANES].T
        dcol = jnp.where(rowi < SSM_HEAD_DIM,
                         jnp.broadcast_to(dec[:, 2 * pair:2 * pair + 1], (LANES, LANES)),
                         jnp.broadcast_to(dec[:, 2 * pair + 1:2 * pair + 2], (LANES, LANES)))
        h0 = h0_ref[0, 2 * pair:2 * pair + 2].reshape(LANES, D_STATE)
        hn = dcol * h0 + xcol * bm[:, grp * D_STATE:(grp + 1) * D_STATE]
        h_out_ref[0, 2 * pair:2 * pair + 2] = hn.reshape(2, SSM_HEAD_DIM, D_STATE)
        cgrow = jnp.broadcast_to(cm[:, grp * D_STATE:(grp + 1) * D_STATE], (8, D_STATE))
        ypair = _dot3(cgrow, hn, NT_DIMS)
        y_parts.append(ypair[0:1])
    y = jnp.concatenate(y_parts, axis=1) + dsk_ref[...] * xs
    y_ref[0] = _gated_group_norm(y, z_ref[0], g_ref[...]).astype(y_ref.dtype)


def _ssd_sample(proj, dtp, state_conv, state_ssm, conv_w, conv_b, dt_bias, a_log, d_skip, g_ssm):
    n = proj.shape[0]
    z0 = ATT_WIDTH + 2 * KV_WIDTH
    z3 = proj[:, z0:z0 + D_INNER].reshape(n, 1, D_INNER)
    x3 = proj[:, z0 + D_INNER:z0 + D_INNER + CONV_DIM].reshape(n, 1, CONV_DIM)
    dt3 = dtp.reshape(n, 1, LANES)
    const = lambda shape: pl.BlockSpec(shape, lambda s: tuple(0 for _ in shape))
    per_seq = lambda shape: pl.BlockSpec((1,) + shape, lambda s: (s,) + tuple(0 for _ in shape))
    y, buf, h = pl.pallas_call(
        _sample_ssd_kernel,
        grid=(n,),
        in_specs=[per_seq((1, D_INNER)), per_seq((1, CONV_DIM)), per_seq((1, LANES)),
                  per_seq((CONV_WIDTH - 1, CONV_DIM)), per_seq((SSM_HEADS, SSM_HEAD_DIM, D_STATE)),
                  const((CONV_WIDTH, CONV_DIM)), const((1, CONV_DIM)), const((1, LANES)), const((1, LANES)),
                  const((1, D_INNER)), const((1, D_INNER))],
        out_specs=[per_seq((1, D_INNER)), per_seq((CONV_WIDTH - 1, CONV_DIM)),
                   per_seq((SSM_HEADS, SSM_HEAD_DIM, D_STATE))],
        out_shape=[jax.ShapeDtypeStruct((n, 1, D_INNER), BF16),
                   jax.ShapeDtypeStruct((n, CONV_WIDTH - 1, CONV_DIM), F32),
                   jax.ShapeDtypeStruct((n, SSM_HEADS, SSM_HEAD_DIM, D_STATE), F32)],
        compiler_params=_params("arbitrary"),
        name="sample_ssd",
    )(z3, x3, dt3, state_conv, state_ssm, conv_w, conv_b.reshape(1, CONV_DIM), _pad_lanes(dt_bias),
      _pad_lanes(a_log), jnp.repeat(d_skip, SSM_HEAD_DIM).reshape(1, D_INNER), g_ssm.reshape(1, D_INNER))
    return y.reshape(n, D_INNER), buf, h


def kernel(x_prompt, x_sample, cache_k, cache_v, state_conv, state_ssm, page_table, c_prompt, c_sample, w_ada, b_ada, g_mix_norm, w_in, q_gain, k_gain, g_att_out, conv_w, conv_b, dt_bias, a_log, d_skip, g_ssm_out, w_out, g_ffn_norm, w_gate, w_up, w_down):
    n_p, seq, d = x_prompt.shape
    n_s, dec_seq, _ = x_sample.shape
    assert n_p == 1 and dec_seq == 1
    depth = w_ada.shape[0]
    main_w = ATT_WIDTH + 2 * KV_WIDTH + D_INNER + CONV_DIM

    yp = x_prompt.reshape(seq, d)
    ys = x_sample.reshape(n_s, d)
    c_rows = n_p + n_s
    c_pad = -(-c_rows // 16) * 16
    c_all = jnp.pad(jnp.concatenate([c_prompt, c_sample], axis=0), ((0, c_pad - c_rows), (0, 0)))
    outs = [[] for _ in range(8)]
    for l in range(depth):
        mod = _ada(c_all, w_ada[l], b_ada[l])
        mod_p, mod_s = mod[0:1], mod[1:1 + n_s]
        w_in_t = jnp.swapaxes(w_in[l], 0, 1)

        proj, dtp = _inproj(yp, g_mix_norm[l], mod_p, w_in_t, main_w, tn=1536, tm=512, name="in_proj")
        k_out, v_out, kp, vp, qp = _prompt_prep(proj, q_gain[l], k_gain[l])
        o_att, kmean_s = _prompt_attn(qp, kp, vp, cache_k, l, page_table)
        y_ssm, tail, st = _ssd_prompt(proj, dtp, conv_w[l], conv_b[l], dt_bias[l], a_log[l], d_skip[l], g_ssm_out[l])
        x1_p, h2_p = _outproj(o_att, g_att_out[l], y_ssm, w_out[l], yp, mod_p, g_ffn_norm[l])
        hpg = SSM_HEADS // SSM_GROUPS
        ssm_p = st.reshape(SSM_GROUPS, D_STATE, hpg, SSM_HEAD_DIM).transpose(0, 2, 3, 1).reshape(
            1, SSM_HEADS, SSM_HEAD_DIM, D_STATE)
        outs[0].append(k_out.reshape(1, seq, N_KV_HEADS, HEAD_DIM))
        outs[1].append(v_out.reshape(1, seq, N_KV_HEADS, HEAD_DIM))
        outs[2].append(tail[8 - (CONV_WIDTH - 1):].reshape(1, CONV_WIDTH - 1, CONV_DIM))
        outs[3].append(ssm_p)

        proj_s, dts = _inproj(ys, g_mix_norm[l], mod_s, w_in_t, main_w, tn=768, tm=512, name="in_proj_sample",
                              split=True)
        q_s, k_s, v_s = _sample_prep(proj_s, q_gain[l], k_gain[l])
        q3 = q_s.reshape(n_s, ATT_HEADS, HEAD_DIM)
        idx = _sample_topk(q3, kmean_s)
        idx_flat = idx[:, :, :MOBA_TOPK].reshape(n_s, ATT_HEADS * MOBA_TOPK)
        o_s = _sample_attn(q3, k_s.reshape(n_s, N_KV_HEADS, HEAD_DIM), v_s.reshape(n_s, N_KV_HEADS, HEAD_DIM),
                           cache_k, cache_v, l, page_table, idx_flat)
        y_s, buf_s, h_s = _ssd_sample(proj_s, dts, state_conv[l], state_ssm[l], conv_w[l], conv_b[l], dt_bias[l],
                                      a_log[l], d_skip[l], g_ssm_out[l])
        x1_s, h2_s = _outproj(o_s.reshape(n_s, ATT_WIDTH), g_att_out[l], y_s, w_out[l], ys, mod_s, g_ffn_norm[l])

        hid_p, hid_s = _gateup(h2_p, h2_s, w_gate[l], w_up[l])
        yp, ys = _down(hid_p, hid_s, w_down[l], x1_p, x1_s, mod_p, mod_s)
        outs[4].append(k_s.reshape(n_s, 1, N_KV_HEADS, HEAD_DIM))
        outs[5].append(v_s.reshape(n_s, 1, N_KV_HEADS, HEAD_DIM))
        outs[6].append(buf_s)
        outs[7].append(h_s)
    stacked = [jnp.stack(o) for o in outs]
    return (yp.reshape(1, seq, d), ys.reshape(n_s, 1, d), *stacked)
```

```python
import functools

import jax
import jax.numpy as jnp
from jax import lax
from jax.experimental import pallas as pl
from jax.experimental.pallas import tpu as pltpu

F32 = jnp.float32
BF16 = jnp.bfloat16

HEAD_DIM = 128
ATT_HEADS = 8
N_KV_HEADS = 4
KV_GROUP = ATT_HEADS // N_KV_HEADS
ATT_WIDTH = ATT_HEADS * HEAD_DIM
KV_WIDTH = N_KV_HEADS * HEAD_DIM
MOBA_BLOCK = 256
MOBA_TOPK = 3
D_INNER = 1024
SSM_HEAD_DIM = 64
SSM_HEADS = D_INNER // SSM_HEAD_DIM
SSM_GROUPS = 2
D_STATE = 128
CONV_WIDTH = 4
CONV_DIM = D_INNER + 2 * SSM_GROUPS * D_STATE
SSD_CHUNK = 128
EPS = 1e-6
ATT_SCALE = HEAD_DIM ** -0.5

LANES = 128
FEAT_KBLK = 96
FEAT_KOFF = 97
FEAT_ONE_A = 98
FEAT_ONE_B = 99
NEG_BIG = -1e30

NT_DIMS = (((1,), (1,)), ((), ()))
VMEM_LIMIT = 56 * 1024 * 1024


def _params(*sem):
    return pltpu.CompilerParams(dimension_semantics=sem, vmem_limit_bytes=VMEM_LIMIT)


def _silu(x):
    return x / (1.0 + jnp.exp(-x))


def _softplus(x):
    return jnp.maximum(x, 0.0) + jnp.log1p(jnp.exp(-jnp.abs(x)))


def _split_bf16(x):
    hi = x.astype(BF16)
    lo = (x - hi.astype(F32)).astype(BF16)
    return hi, lo


def _dot3(a, b, dims):
    ah, al = _split_bf16(a)
    bh, bl = _split_bf16(b)
    d = lambda x, y: lax.dot_general(x, y, dims, preferred_element_type=F32)
    return d(ah, bh) + d(al, bh) + d(ah, bl)


def _alibi_slope(h):
    return 2.0 ** (-8.0 * (h + 1) / ATT_HEADS)


NN_DIMS = (((1,), (0,)), ((), ()))


def _ada_kernel(c_ref, w_ref, b_ref, o_ref):
    rows = c_ref.shape[0]
    ahi, alo = _split_bf16(_silu(c_ref[...]))
    whi, wlo = _split_bf16(w_ref[...])
    both = jnp.dot(jnp.concatenate([ahi, alo], axis=0), whi, preferred_element_type=F32)
    o_ref[...] = both[:rows] + both[rows:] + jnp.dot(ahi, wlo, preferred_element_type=F32) + b_ref[...]


def _ada(c_all, w, b):
    rows, d = c_all.shape
    n = w.shape[1]
    tn = 512
    return pl.pallas_call(
        _ada_kernel,
        grid=(n // tn,),
        in_specs=[pl.BlockSpec((rows, d), lambda j: (0, 0)),
                  pl.BlockSpec((d, tn), lambda j: (0, j)),
                  pl.BlockSpec((1, tn), lambda j: (0, j))],
        out_specs=pl.BlockSpec((rows, tn), lambda j: (0, j)),
        out_shape=jax.ShapeDtypeStruct((rows, n), F32),
        compiler_params=_params("arbitrary"),
        name="ada_mod",
    )(c_all, w, b.reshape(1, n))


def _mod_spec(mod_rows, tm, width, col_of):
    if mod_rows == 1:
        return pl.BlockSpec((1, width), lambda n, i: (0, col_of(n)))
    return pl.BlockSpec((tm, width), lambda n, i: (i, col_of(n)))


def _rms(x):
    return x * lax.rsqrt(jnp.mean(x * x, axis=-1, keepdims=True) + EPS)


def _cast_weight_once(w_ref, wbf_ref):
    @pl.when(pl.program_id(1) == 0)
    def _():
        wbf_ref[...] = w_ref[...].astype(BF16)


STAGE_CHUNK = 384


def _inproj_kernel(n_main, split, x_ref, g_ref, sc_ref, sh_ref, w_ref, wdt_ref, o_ref, odt_ref, *w_scratch):
    n = pl.program_id(0)
    first_row_tile = pl.program_id(1) == 0
    main_refs, dt_refs = w_scratch[:len(w_scratch) // 2], w_scratch[len(w_scratch) // 2:]

    def stage(src_ref, refs, pad_rows):
        cols = src_ref.shape[0]
        chunk = STAGE_CHUNK if cols % STAGE_CHUNK == 0 else cols
        for c0 in range(0, cols, chunk):
            w = src_ref[c0:c0 + chunk, :]
            if pad_rows:
                w = jnp.concatenate([w, jnp.zeros((pad_rows, w.shape[1]), F32)], axis=0)
            w = w.T
            for part, ref in zip(_split_bf16(w) if split else (w.astype(BF16),), refs):
                ref[:, c0:c0 + w.shape[1]] = part

    def product(hn, refs):
        mm = lambda a, b: jnp.dot(a, b, preferred_element_type=F32)
        if not split:
            return mm(hn.astype(BF16), refs[0][...])
        hhi, hlo = _split_bf16(hn)
        return mm(hhi, refs[0][...]) + mm(hlo, refs[0][...]) + mm(hhi, refs[1][...])

    @pl.when(first_row_tile)
    def _():
        stage(w_ref, main_refs, 0)

    @pl.when(jnp.logical_and(n == 0, first_row_tile))
    def _():
        stage(wdt_ref, dt_refs, LANES - wdt_ref.shape[0])

    hn = (_rms(x_ref[...]) * g_ref[...]) * (1.0 + sc_ref[...]) + sh_ref[...]
    o_ref[...] = product(hn, main_refs)

    @pl.when(n == 0)
    def _():
        odt_ref[...] = product(hn, dt_refs)


def _inproj(x, g, mod, w_t, main_w, *, tn, tm, name, split=False):
    m, d = x.shape
    tm = min(tm, m)
    mr = mod.shape[0]
    n_main = main_w // tn
    n_dt = w_t.shape[0] - main_w
    assert n_main * tn == main_w and main_w % n_dt == 0 and n_dt % 16 == 0
    last_i = m // tm - 1
    ncopies = 2 if split else 1
    proj, dtp = pl.pallas_call(
        functools.partial(_inproj_kernel, n_main, split),
        grid=(n_main, m // tm),
        in_specs=[pl.BlockSpec((tm, d), lambda n, i: (i, 0)),
                  pl.BlockSpec((1, d), lambda n, i: (0, 0)),
                  _mod_spec(mr, tm, d, lambda n: 1),
                  _mod_spec(mr, tm, d, lambda n: 0),
                  pl.BlockSpec((tn, d), lambda n, i: (n, 0)),
                  pl.BlockSpec((n_dt, d), lambda n, i: (main_w // n_dt, 0))],
        out_specs=[pl.BlockSpec((tm, tn), lambda n, i: (i, n)),
                   pl.BlockSpec((tm, LANES), lambda n, i: (jnp.where(n == 0, i, last_i), 0))],
        out_shape=[jax.ShapeDtypeStruct((m, main_w), F32), jax.ShapeDtypeStruct((m, LANES), F32)],
        scratch_shapes=[pltpu.VMEM((d, tn), BF16)] * ncopies + [pltpu.VMEM((d, LANES), BF16)] * ncopies,
        compiler_params=_params("arbitrary", "arbitrary"),
        name=name,
    )(x, g.reshape(1, d), mod, mod, w_t, w_t)
    return proj, dtp


def _outproj_kernel(a_ref, ga_ref, b_ref, w_ref, x_ref, gt_ref, gf_ref, sc_ref, sh_ref, x1_ref, h2_ref, wbf_ref):
    _cast_weight_once(w_ref, wbf_ref)
    ka = a_ref.shape[1]
    a = (_rms(a_ref[...]) * ga_ref[...]).astype(BF16)
    acc = jnp.dot(a, wbf_ref[:ka, :], preferred_element_type=F32)
    acc = acc + jnp.dot(b_ref[...], wbf_ref[ka:, :], preferred_element_type=F32)
    x1 = x_ref[...] + gt_ref[...] * acc
    x1_ref[...] = x1
    h2_ref[...] = ((_rms(x1) * gf_ref[...]) * (1.0 + sc_ref[...]) + sh_ref[...]).astype(BF16)


def _outproj(a, g_a, b, w, x, mod, g_ffn):
    m, ka = a.shape
    kb = b.shape[1]
    d = w.shape[1]
    tm = min(256, m)
    mr = mod.shape[0]
    return pl.pallas_call(
        _outproj_kernel,
        grid=(1, m // tm),
        in_specs=[pl.BlockSpec((tm, ka), lambda n, i: (i, 0)),
                  pl.BlockSpec((1, ka), lambda n, i: (0, 0)),
                  pl.BlockSpec((tm, kb), lambda n, i: (i, 0)),
                  pl.BlockSpec((ka + kb, d), lambda n, i: (0, 0), pipeline_mode=pl.Buffered(1)),
                  pl.BlockSpec((tm, d), lambda n, i: (i, 0)),
                  _mod_spec(mr, tm, d, lambda n: 2),
                  pl.BlockSpec((1, d), lambda n, i: (0, 0)),
                  _mod_spec(mr, tm, d, lambda n: 4),
                  _mod_spec(mr, tm, d, lambda n: 3)],
        out_specs=[pl.BlockSpec((tm, d), lambda n, i: (i, 0)),
                   pl.BlockSpec((tm, d), lambda n, i: (i, 0))],
        out_shape=[jax.ShapeDtypeStruct((m, d), F32), jax.ShapeDtypeStruct((m, d), BF16)],
        scratch_shapes=[pltpu.VMEM((ka + kb, d), BF16)],
        compiler_params=_params("arbitrary", "arbitrary"),
        name="out_proj",
    )(a, g_a.reshape(1, ka), b, w, x, mod, g_ffn.reshape(1, d), mod, mod)


def _gateup_kernel(tiles_p, xp_ref, xs_ref, wg_ref, wu_ref, op_ref, os_ref, wgb_ref, wub_ref):
    _cast_weight_once(wg_ref, wgb_ref)
    _cast_weight_once(wu_ref, wub_ref)
    i = pl.program_id(1)

    def swiglu(h2):
        g = jnp.dot(h2, wgb_ref[...], preferred_element_type=F32)
        u = jnp.dot(h2, wub_ref[...], preferred_element_type=F32)
        return (_silu(g) * u).astype(BF16)

    @pl.when(i < tiles_p)
    def _():
        op_ref[...] = swiglu(xp_ref[...])

    @pl.when(i == tiles_p)
    def _():
        os_ref[...] = swiglu(xs_ref[...])


def _gateup(h2_p, h2_s, wg, wu):
    m, d = h2_p.shape
    ms = h2_s.shape[0]
    f = wg.shape[1]
    tn, tm = 512, 1024
    tiles_p = m // tm
    prow = lambda i: jnp.minimum(i, tiles_p - 1)
    return pl.pallas_call(
        functools.partial(_gateup_kernel, tiles_p),
        grid=(f // tn, tiles_p + 1),
        in_specs=[pl.BlockSpec((tm, d), lambda n, i: (prow(i), 0)),
                  pl.BlockSpec((ms, d), lambda n, i: (0, 0)),
                  pl.BlockSpec((d, tn), lambda n, i: (0, n)),
                  pl.BlockSpec((d, tn), lambda n, i: (0, n))],
        out_specs=[pl.BlockSpec((tm, tn), lambda n, i: (prow(i), n)),
                   pl.BlockSpec((ms, tn), lambda n, i: (0, n))],
        out_shape=[jax.ShapeDtypeStruct((m, f), BF16), jax.ShapeDtypeStruct((ms, f), BF16)],
        scratch_shapes=[pltpu.VMEM((d, tn), BF16), pltpu.VMEM((d, tn), BF16)],
        compiler_params=_params("arbitrary", "arbitrary"),
        name="ffn_gate_up",
    )(h2_p, h2_s, wg, wu)


def _stream_block_means(t, stream_steps, steps_per_seq, page_refs, km_ref):
    pps = len(page_refs)

    @pl.when(t < stream_steps)
    def _():
        page_rows = page_refs[0].shape[0]
        ppb = MOBA_BLOCK * N_KV_HEADS // page_rows
        fold = 8 // N_KV_HEADS
        j = t % steps_per_seq
        for b in range(pps // ppb):
            s8 = jnp.sum(page_refs[b * ppb][...].reshape(page_rows // 8, 8, HEAD_DIM), axis=0)
            for p in range(1, ppb):
                s8 = s8 + jnp.sum(page_refs[b * ppb + p][...].reshape(page_rows // 8, 8, HEAD_DIM), axis=0)
            s = s8[0:N_KV_HEADS]
            for p in range(1, fold):
                s = s + s8[p * N_KV_HEADS:(p + 1) * N_KV_HEADS]
            s = s * (1.0 / MOBA_BLOCK)
            for kv in range(N_KV_HEADS):
                km_ref[0, kv, j, b:b + 1, :] = s[kv:kv + 1, :]


def _page_stream(cache_k, layer, page_table, host_steps, step_of):
    n_seq, n_pages = page_table.shape
    depth, pool, page = cache_k.shape[:3]
    ppb = MOBA_BLOCK // page
    pps = min(32, n_pages)
    steps_per_seq = n_pages // pps
    stream_steps = n_seq * steps_per_seq
    assert n_pages % pps == 0 and pps % ppb == 0 and 8 % N_KV_HEADS == 0
    assert stream_steps <= host_steps, "not enough grid steps to stream the cache"
    view = cache_k.reshape(depth, pool, page * N_KV_HEADS, HEAD_DIM)

    def stream_step(*grid_idx):
        return jnp.minimum(step_of(*grid_idx), stream_steps - 1)

    def page_spec(k):
        return pl.BlockSpec((None, None, page * N_KV_HEADS, HEAD_DIM),
                            lambda *args: (layer, args[-1][stream_step(*args[:-1]) * pps + k], 0, 0))

    bps = pps // ppb
    km_spec = pl.BlockSpec((1, N_KV_HEADS, steps_per_seq, bps, HEAD_DIM),
                           lambda *args: (stream_step(*args[:-1]) // steps_per_seq, 0, 0, 0, 0))
    km_shape = jax.ShapeDtypeStruct((n_seq, N_KV_HEADS, steps_per_seq, bps, HEAD_DIM), F32)
    return (view, page_table.reshape(-1), [page_spec(k) for k in range(pps)], km_spec, km_shape, stream_steps,
            steps_per_seq)


def _down_kernel(tiles_p, hp_ref, hs_ref, w_ref, xp_ref, xs_ref, gtp_ref, gts_ref, op_ref, os_ref, wbf_ref):
    _cast_weight_once(w_ref, wbf_ref)
    i = pl.program_id(1)

    @pl.when(i < tiles_p)
    def _():
        acc = jnp.dot(hp_ref[...], wbf_ref[...], preferred_element_type=F32)
        op_ref[...] = xp_ref[...] + gtp_ref[...] * acc

    @pl.when(i == tiles_p)
    def _():
        acc = jnp.dot(hs_ref[...], wbf_ref[...], preferred_element_type=F32)
        os_ref[...] = xs_ref[...] + gts_ref[...] * acc


def _down(h_p, h_s, w, x_p, x_s, mod_p, mod_s):
    m, f = h_p.shape
    ms = h_s.shape[0]
    d = w.shape[1]
    tn, tm = 512, 512
    tiles_p = m // tm
    nb = d // tn
    assert mod_p.shape[0] == 1 and mod_s.shape[0] == ms
    prow = lambda i: jnp.minimum(i, tiles_p - 1)
    return pl.pallas_call(
        functools.partial(_down_kernel, tiles_p),
        grid=(nb, tiles_p + 1),
        in_specs=[pl.BlockSpec((tm, f), lambda n, i: (prow(i), 0)),
                  pl.BlockSpec((ms, f), lambda n, i: (0, 0)),
                  pl.BlockSpec((f, tn), lambda n, i: (0, n)),
                  pl.BlockSpec((tm, tn), lambda n, i: (prow(i), n)),
                  pl.BlockSpec((ms, tn), lambda n, i: (0, n)),
                  pl.BlockSpec((1, tn), lambda n, i: (0, 5 * nb + n)),
                  pl.BlockSpec((ms, tn), lambda n, i: (0, 5 * nb + n))],
        out_specs=[pl.BlockSpec((tm, tn), lambda n, i: (prow(i), n)),
                   pl.BlockSpec((ms, tn), lambda n, i: (0, n))],
        out_shape=[jax.ShapeDtypeStruct((m, d), F32), jax.ShapeDtypeStruct((ms, d), F32)],
        scratch_shapes=[pltpu.VMEM((f, tn), BF16)],
        compiler_params=_params("arbitrary", "arbitrary"),
        name="ffn_down",
    )(h_p, h_s, w, x_p, x_s, mod_p, mod_s)


def _head_norm(x, gain):
    return x * lax.rsqrt(jnp.mean(x * x, axis=-1, keepdims=True) + EPS) * gain


def _top3(g, idx_f, axis):
    sel = jnp.zeros(g.shape, F32)
    firsts = []
    for _ in range(MOBA_TOPK):
        mx = jnp.max(g, axis=axis, keepdims=True)
        ismax = jnp.logical_and(g == mx, mx > -jnp.inf)
        first = jnp.min(jnp.where(ismax, idx_f, float(LANES)), axis=axis, keepdims=True)
        pick = idx_f == first
        sel = jnp.where(pick, 1.0, sel)
        g = jnp.where(pick, -jnp.inf, g)
        firsts.append(first)
    return sel, firsts


def _prompt_prep_kernel(proj_ref, qg_ref, kg_ref, kout_ref, vout_ref, kp_ref, vp_ref, qp_ref, km_ref):
    i = pl.program_id(0)
    blk = MOBA_BLOCK

    @pl.when(i == 0)
    def _():
        km_ref[...] = jnp.zeros_like(km_ref)

    lane = lax.broadcasted_iota(jnp.int32, (blk, LANES), 1)
    row = lax.broadcasted_iota(jnp.int32, (blk, LANES), 0)
    sq_row = lax.broadcasted_iota(jnp.int32, (LANES, LANES), 0)
    i_f = i.astype(F32)
    row_f = row.astype(F32)

    kfeat = jnp.where(lane == i, 1.0, 0.0)
    kfeat = jnp.where(lane == FEAT_KBLK, i_f, kfeat)
    kfeat = jnp.where(lane == FEAT_KOFF, row_f, kfeat)
    kfeat = jnp.where(jnp.logical_or(lane == FEAT_ONE_A, lane == FEAT_ONE_B), 1.0, kfeat).astype(BF16)
    vfeat = jnp.ones((blk, LANES), BF16)

    kg = kg_ref[...]
    for kv in range(N_KV_HEADS):
        k = proj_ref[:, ATT_WIDTH + kv * HEAD_DIM:ATT_WIDTH + (kv + 1) * HEAD_DIM]
        kn = _head_norm(k, kg)
        kout_ref[:, kv * HEAD_DIM:(kv + 1) * HEAD_DIM] = kn
        kp_ref[kv, :, :HEAD_DIM] = kn.astype(BF16)
        kp_ref[kv, :, HEAD_DIM:] = kfeat
        ksum = jnp.sum(kn, axis=0, keepdims=True) * (1.0 / blk)
        km_ref[kv] = jnp.where(sq_row == i, jnp.broadcast_to(ksum, (LANES, LANES)), km_ref[kv])
        v = proj_ref[:, ATT_WIDTH + KV_WIDTH + kv * HEAD_DIM:ATT_WIDTH + KV_WIDTH + (kv + 1) * HEAD_DIM]
        vout_ref[:, kv * HEAD_DIM:(kv + 1) * HEAD_DIM] = v
        vp_ref[kv, :, :HEAD_DIM] = v.astype(BF16)
        vp_ref[kv, :, HEAD_DIM:] = vfeat

    qg = qg_ref[...]
    blk_id = lax.broadcasted_iota(jnp.int32, (LANES, blk), 0)
    blk_id_f = blk_id.astype(F32)
    valid = blk_id < i
    for h in range(ATT_HEADS):
        q = proj_ref[:, h * HEAD_DIM:(h + 1) * HEAD_DIM]
        qn = _head_norm(q, qg)
        gate = _dot3(km_ref[h // KV_GROUP], qn, NT_DIMS)
        sel_t, _ = _top3(jnp.where(valid, gate, -jnp.inf), blk_id_f, 0)
        sel = sel_t.T
        slope = _alibi_slope(h)
        qfeat = jnp.where(jnp.logical_and(lane < FEAT_KBLK, sel == 0.0), NEG_BIG, 0.0)
        qfeat = jnp.where(lane == FEAT_KBLK, slope * blk, qfeat)
        qfeat = jnp.where(lane == FEAT_KOFF, slope, qfeat)
        qfeat = jnp.where(lane == FEAT_ONE_A, -(slope * blk) * i_f, qfeat)
        qfeat = jnp.where(lane == FEAT_ONE_B, -slope * row_f, qfeat)
        qp_ref[h, :, :HEAD_DIM] = (qn * ATT_SCALE).astype(BF16)
        qp_ref[h, :, HEAD_DIM:] = qfeat.astype(BF16)


def _prompt_prep(proj, q_gain, k_gain):
    m = proj.shape[0]
    nb = m // MOBA_BLOCK
    assert m % MOBA_BLOCK == 0 and nb <= FEAT_KBLK
    blk = MOBA_BLOCK
    qkv_w = ATT_WIDTH + 2 * KV_WIDTH
    return pl.pallas_call(
        _prompt_prep_kernel,
        grid=(nb,),
        in_specs=[pl.BlockSpec((blk, qkv_w), lambda i: (i, 0)),
                  pl.BlockSpec((1, HEAD_DIM), lambda i: (0, 0)),
                  pl.BlockSpec((1, HEAD_DIM), lambda i: (0, 0))],
        out_specs=[pl.BlockSpec((blk, KV_WIDTH), lambda i: (i, 0)),
                   pl.BlockSpec((blk, KV_WIDTH), lambda i: (i, 0)),
                   pl.BlockSpec((N_KV_HEADS, blk, 2 * HEAD_DIM), lambda i: (0, i, 0)),
                   pl.BlockSpec((N_KV_HEADS, blk, 2 * HEAD_DIM), lambda i: (0, i, 0)),
                   pl.BlockSpec((ATT_HEADS, blk, 2 * HEAD_DIM), lambda i: (0, i, 0))],
        out_shape=[jax.ShapeDtypeStruct((m, KV_WIDTH), F32),
                   jax.ShapeDtypeStruct((m, KV_WIDTH), F32),
                   jax.ShapeDtypeStruct((N_KV_HEADS, m, 2 * HEAD_DIM), BF16),
                   jax.ShapeDtypeStruct((N_KV_HEADS, m, 2 * HEAD_DIM), BF16),
                   jax.ShapeDtypeStruct((ATT_HEADS, m, 2 * HEAD_DIM), BF16)],
        scratch_shapes=[pltpu.VMEM((N_KV_HEADS, LANES, LANES), F32)],
        compiler_params=_params("arbitrary"),
        name="prompt_qk_prep",
    )(proj, q_gain.reshape(1, HEAD_DIM), k_gain.reshape(1, HEAD_DIM))


ATTN_UNROLL = 4
LOG2E = 1.4426950408889634


def _prompt_attn_kernel(pps, stream_steps, steps_per_seq, pt_ref, q_ref, k_ref, v_ref, *rest):
    page_refs = rest[:pps]
    o_ref, km_ref, s_ref, acc_ref, m_ref = rest[pps:]
    i = pl.program_id(1)
    _stream_block_means(pl.program_id(0) * pl.num_programs(1) + i, stream_steps, steps_per_seq, page_refs, km_ref)
    blk = MOBA_BLOCK
    rows = KV_GROUP * blk
    span = ATTN_UNROLL * blk
    qs = q_ref[...].reshape(rows, 2 * HEAD_DIM)

    def lane_fold(s):
        out = s[:, :LANES]
        for t in range(1, s.shape[1] // LANES):
            out = jnp.maximum(out, s[:, t * LANES:(t + 1) * LANES])
        return out

    def probs(s):
        mb = m_ref[...]
        return jnp.concatenate([jnp.exp2(s[:, t * LANES:(t + 1) * LANES] - mb) for t in range(s.shape[1] // LANES)],
                               axis=1).astype(BF16)

    own = pl.ds(pl.multiple_of(i * blk, blk), blk)
    r = lax.broadcasted_iota(jnp.int32, (rows, blk), 0)
    c = lax.broadcasted_iota(jnp.int32, (rows, blk), 1)
    dist = jnp.bitwise_and(r, blk - 1) - c
    slope = qs[:, HEAD_DIM + FEAT_KOFF:HEAD_DIM + FEAT_KOFF + 1].astype(F32)
    s_own = lax.dot_general(qs[:, :HEAD_DIM], k_ref[own, :HEAD_DIM], NT_DIMS, preferred_element_type=F32)
    s_own = jnp.where(dist >= 0, (s_own - slope * dist.astype(F32)) * LOG2E, -jnp.inf)
    m_ref[...] = lane_fold(s_own)

    trips = (i + ATTN_UNROLL - 1) // ATTN_UNROLL

    def two_trips_per_iteration(trip):
        def pair(u, carry):
            trip(2 * u)
            trip(2 * u + 1)
            return carry

        lax.fori_loop(0, trips // 2, pair, 0)

        @pl.when(trips % 2 == 1)
        def _():
            trip(trips - 1)

    def pass1(t):
        ks = k_ref[pl.ds(pl.multiple_of(t * span, span), span), :]
        s = lax.dot_general(qs, ks, NT_DIMS, preferred_element_type=F32) * LOG2E
        s_ref[t] = s
        m_ref[...] = jnp.maximum(m_ref[...], lane_fold(s))

    two_trips_per_iteration(pass1)
    m_ref[...] = jnp.broadcast_to(jnp.max(m_ref[...], axis=1, keepdims=True), (rows, LANES))

    acc_ref[...] = jnp.dot(probs(s_own), v_ref[own, :], preferred_element_type=F32)

    def pass2(t):
        vs = v_ref[pl.ds(pl.multiple_of(t * span, span), span), :]
        acc_ref[...] += jnp.dot(probs(s_ref[t]), vs, preferred_element_type=F32)

    two_trips_per_iteration(pass2)
    acc = acc_ref[...]
    o = acc[:, :HEAD_DIM] / acc[:, HEAD_DIM:]
    for g in range(KV_GROUP):
        o_ref[:, g * HEAD_DIM:(g + 1) * HEAD_DIM] = o[g * blk:(g + 1) * blk]


def _prompt_attn(qp, kp, vp, cache_k, layer, page_table):
    m = kp.shape[1]
    nb = m // MOBA_BLOCK
    assert nb % ATTN_UNROLL == 0
    blk = MOBA_BLOCK
    rows = KV_GROUP * blk
    view, flat_pages, page_specs, km_spec, km_shape, stream_steps, steps_per_seq = _page_stream(
        cache_k, layer, page_table, N_KV_HEADS * nb, lambda kv, i: kv * nb + i)
    resident = lambda: pl.BlockSpec((None, m, 2 * HEAD_DIM), lambda kv, i, pt: (kv, 0, 0), pipeline_mode=pl.Buffered(1))
    grid_spec = pltpu.PrefetchScalarGridSpec(
        num_scalar_prefetch=1,
        grid=(N_KV_HEADS, nb),
        in_specs=[pl.BlockSpec((KV_GROUP, blk, 2 * HEAD_DIM), lambda kv, i, pt: (kv, i, 0)), resident(), resident()]
        + page_specs,
        out_specs=[pl.BlockSpec((blk, KV_GROUP * HEAD_DIM), lambda kv, i, pt: (i, kv)), km_spec],
        scratch_shapes=[pltpu.VMEM((nb // ATTN_UNROLL, rows, ATTN_UNROLL * blk), F32),
                        pltpu.VMEM((rows, 2 * HEAD_DIM), F32),
                        pltpu.VMEM((rows, LANES), F32)],
    )
    o, km = pl.pallas_call(
        functools.partial(_prompt_attn_kernel, len(page_specs), stream_steps, steps_per_seq),
        grid_spec=grid_spec,
        out_shape=[jax.ShapeDtypeStruct((m, ATT_WIDTH), F32), km_shape],
        compiler_params=_params("arbitrary", "arbitrary"),
        name="prompt_moba_attn",
    )(flat_pages, qp, kp, vp, *([view] * len(page_specs)))
    return o, km.reshape(km.shape[0], N_KV_HEADS, -1, HEAD_DIM)


def _expand_heads(v, lane_lo):
    r = v.shape[0]
    parts = []
    for k in range(SSM_HEADS // 2):
        a0 = jnp.broadcast_to(v[:, 2 * k:2 * k + 1], (r, LANES))
        a1 = jnp.broadcast_to(v[:, 2 * k + 1:2 * k + 2], (r, LANES))
        parts.append(jnp.where(lane_lo, a0, a1))
    return jnp.concatenate(parts, axis=1)


def _gated_group_norm(y, z, g):
    yz = y * _silu(z)
    gw = D_INNER // SSM_GROUPS
    outs = []
    for grp in range(SSM_GROUPS):
        t = yz[:, grp * gw:(grp + 1) * gw]
        t = t * lax.rsqrt(jnp.mean(t * t, axis=-1, keepdims=True) + EPS)
        outs.append(t * g[:, grp * gw:(grp + 1) * gw])
    return jnp.concatenate(outs, axis=1)


def _ssd_kernel(z_ref, xbc_ref, dt_ref, cw_ref, cb_ref, dtb_ref, alog_ref, dsk_ref, g_ref,
                y_ref, tail_out_ref, st_out_ref, tail_ref, st_ref):
    c = pl.program_id(0)
    cs = SSD_CHUNK
    gw = D_INNER // SSM_GROUPS

    @pl.when(c == 0)
    def _():
        tail_ref[...] = jnp.zeros_like(tail_ref)
        st_ref[...] = jnp.zeros_like(st_ref)

    xr = xbc_ref[...]
    xp = jnp.concatenate([tail_ref[...], xr], axis=0)
    cw = cw_ref[...]
    conv = cb_ref[...] + cw[3:4] * xr
    for t in range(CONV_WIDTH - 1):
        conv = conv + cw[t:t + 1] * xp[8 - (CONV_WIDTH - 1) + t:8 - (CONV_WIDTH - 1) + t + cs]
    tail_ref[...] = xr[cs - 8:]
    tail_out_ref[...] = xr[cs - 8:]
    xc = _silu(conv)
    xs = xc[:, :D_INNER]
    bm = xc[:, D_INNER:D_INNER + SSM_GROUPS * D_STATE]
    cm = xc[:, D_INNER + SSM_GROUPS * D_STATE:]

    lane = lax.broadcasted_iota(jnp.int32, (cs, LANES), 1)
    rowi = lax.broadcasted_iota(jnp.int32, (cs, LANES), 0)
    lane_lo = lane < SSM_HEAD_DIM
    tri = rowi >= lane

    dt = _softplus(dt_ref[...] + dtb_ref[...])
    a = jnp.where(lane[:1] < SSM_HEADS, -jnp.exp(alog_ref[...]), 0.0)
    da = dt * a
    tril = jnp.where(tri, 1.0, 0.0).astype(BF16)
    p1 = da.astype(BF16)
    r1 = da - p1.astype(F32)
    p2 = r1.astype(BF16)
    p3 = (r1 - p2.astype(F32)).astype(BF16)
    acum = (jnp.dot(tril, p1, preferred_element_type=F32) + jnp.dot(tril, p2, preferred_element_type=F32)
            + jnp.dot(tril, p3, preferred_element_type=F32))
    acum_t = acum.T

    dt_e = _expand_heads(dt, lane_lo)
    ac_e = _expand_heads(acum, lane_lo)
    xdt = xs * dt_e
    ea_e = jnp.exp(ac_e)
    dend_e = jnp.exp(ac_e[cs - 1:cs, :] - ac_e)
    cdec = ea_e[cs - 1:cs, :]
    xdt_bf = xdt.astype(BF16)
    xdec_bf = (xdt * dend_e).astype(BF16)

    y_parts = []
    for grp in range(SSM_GROUPS):
        bg = bm[:, grp * D_STATE:(grp + 1) * D_STATE]
        cg = cm[:, grp * D_STATE:(grp + 1) * D_STATE].astype(BF16)
        cb = lax.dot_general(cg, bg.astype(BF16), NT_DIMS, preferred_element_type=F32)
        hpg = SSM_HEADS // SSM_GROUPS
        intra = []
        for k in range(hpg // 2):
            pair = grp * (hpg // 2) + k
            xpair = xdt_bf[:, pair * LANES:(pair + 1) * LANES]
            acc = None
            for hh in range(2):
                h = 2 * pair + hh
                seg = jnp.broadcast_to(acum[:, h:h + 1], (cs, cs)) - acum_t[h:h + 1, :]
                lmat = jnp.exp(jnp.where(tri, seg, -jnp.inf))
                mh = (cb * lmat).astype(BF16)
                xh = jnp.where(lane_lo if hh == 0 else jnp.logical_not(lane_lo), xpair, jnp.zeros_like(xpair))
                part = jnp.dot(mh, xh, preferred_element_type=F32)
                acc = part if acc is None else acc + part
            intra.append(acc)
        y_intra = jnp.concatenate(intra, axis=1)
        st = st_ref[grp]
        y_inter = jnp.dot(cg, st.astype(BF16), preferred_element_type=F32) * ea_e[:, grp * gw:(grp + 1) * gw]
        new_st = cdec[:, grp * gw:(grp + 1) * gw] * st + jnp.dot(
            bg.T.astype(BF16), xdec_bf[:, grp * gw:(grp + 1) * gw], preferred_element_type=F32)
        st_ref[grp] = new_st
        st_out_ref[grp] = new_st
        y_parts.append(y_intra + y_inter)
    y = jnp.concatenate(y_parts, axis=1) + dsk_ref[...] * xs
    y_ref[...] = _gated_group_norm(y, z_ref[...], g_ref[...]).astype(y_ref.dtype)


def _pad_lanes(v):
    return jnp.pad(v.reshape(1, -1), ((0, 0), (0, LANES - v.size)))


def _ssd_prompt(proj, dtp, conv_w, conv_b, dt_bias, a_log, d_skip, g_ssm):
    m = proj.shape[0]
    cs = SSD_CHUNK
    assert m % cs == 0
    gw = D_INNER // SSM_GROUPS
    z_blk = (ATT_WIDTH + 2 * KV_WIDTH) // D_INNER
    x_blk = (ATT_WIDTH + 2 * KV_WIDTH + D_INNER) // CONV_DIM
    assert z_blk * D_INNER == ATT_WIDTH + 2 * KV_WIDTH and x_blk * CONV_DIM == ATT_WIDTH + 2 * KV_WIDTH + D_INNER
    const = lambda shape: pl.BlockSpec(shape, lambda c: tuple(0 for _ in shape))
    return pl.pallas_call(
        _ssd_kernel,
        grid=(m // cs,),
        in_specs=[pl.BlockSpec((cs, D_INNER), lambda c: (c, z_blk)),
                  pl.BlockSpec((cs, CONV_DIM), lambda c: (c, x_blk)),
                  pl.BlockSpec((cs, LANES), lambda c: (c, 0)),
                  const((CONV_WIDTH, CONV_DIM)), const((1, CONV_DIM)), const((1, LANES)), const((1, LANES)),
                  const((1, D_INNER)), const((1, D_INNER))],
        out_specs=[pl.BlockSpec((cs, D_INNER), lambda c: (c, 0)),
                   const((8, CONV_DIM)), const((SSM_GROUPS, D_STATE, gw))],
        out_shape=[jax.ShapeDtypeStruct((m, D_INNER), BF16),
                   jax.ShapeDtypeStruct((8, CONV_DIM), F32),
                   jax.ShapeDtypeStruct((SSM_GROUPS, D_STATE, gw), F32)],
        scratch_shapes=[pltpu.VMEM((8, CONV_DIM), F32), pltpu.VMEM((SSM_GROUPS, D_STATE, gw), F32)],
        compiler_params=_params("arbitrary"),
        name="prompt_ssd",
    )(proj, proj, dtp, conv_w, conv_b.reshape(1, CONV_DIM), _pad_lanes(dt_bias), _pad_lanes(a_log),
      jnp.repeat(d_skip, SSM_HEAD_DIM).reshape(1, D_INNER), g_ssm.reshape(1, D_INNER))


def _sample_prep_kernel(proj_ref, qg_ref, kg_ref, q_ref, k_ref, v_ref):
    for h in range(ATT_HEADS):
        q_ref[:, h * HEAD_DIM:(h + 1) * HEAD_DIM] = _head_norm(proj_ref[:, h * HEAD_DIM:(h + 1) * HEAD_DIM], qg_ref[...])
    for kv in range(N_KV_HEADS):
        lo = ATT_WIDTH + kv * HEAD_DIM
        k_ref[:, kv * HEAD_DIM:(kv + 1) * HEAD_DIM] = _head_norm(proj_ref[:, lo:lo + HEAD_DIM], kg_ref[...])
    v_ref[...] = proj_ref[:, ATT_WIDTH + KV_WIDTH:ATT_WIDTH + 2 * KV_WIDTH]


def _sample_prep(proj, q_gain, k_gain):
    n = proj.shape[0]
    qkv_w = ATT_WIDTH + 2 * KV_WIDTH
    return pl.pallas_call(
        _sample_prep_kernel,
        grid=(1,),
        in_specs=[pl.BlockSpec((n, qkv_w), lambda i: (0, 0)),
                  pl.BlockSpec((1, HEAD_DIM), lambda i: (0, 0)),
                  pl.BlockSpec((1, HEAD_DIM), lambda i: (0, 0))],
        out_specs=[pl.BlockSpec((n, ATT_WIDTH), lambda i: (0, 0)),
                   pl.BlockSpec((n, KV_WIDTH), lambda i: (0, 0)),
                   pl.BlockSpec((n, KV_WIDTH), lambda i: (0, 0))],
        out_shape=[jax.ShapeDtypeStruct((n, ATT_WIDTH), F32),
                   jax.ShapeDtypeStruct((n, KV_WIDTH), F32),
                   jax.ShapeDtypeStruct((n, KV_WIDTH), F32)],
        compiler_params=_params("arbitrary"),
        name="sample_qk_prep",
    )(proj, q_gain.reshape(1, HEAD_DIM), k_gain.reshape(1, HEAD_DIM))


def _sample_topk_kernel(q_ref, km_ref, idx_ref):
    nblk = km_ref.shape[2]
    hrow = lax.broadcasted_iota(jnp.int32, (ATT_HEADS, nblk), 0)
    lane = lax.broadcasted_iota(jnp.int32, (ATT_HEADS, LANES), 1)
    for s in range(q_ref.shape[0]):
        q = q_ref[s]
        gate = jnp.zeros((ATT_HEADS, nblk), F32)
        for kv in range(N_KV_HEADS):
            gk = _dot3(q, km_ref[s, kv], NT_DIMS)
            gate = jnp.where(hrow // KV_GROUP == kv, gk, gate)
        if nblk < LANES:
            gate = jnp.concatenate([gate, jnp.full((ATT_HEADS, LANES - nblk), -jnp.inf, F32)], axis=1)
        _, firsts = _top3(gate, lane.astype(F32), 1)
        out = jnp.zeros((ATT_HEADS, LANES), jnp.int32)
        for t, first in enumerate(firsts):
            out = jnp.where(lane == t, first.astype(jnp.int32), out)
        idx_ref[s] = out


def _sample_topk(q3, kmean):
    n, _, nblk, _ = kmean.shape
    assert MOBA_TOPK <= nblk <= LANES
    sb = 8 if n % 8 == 0 else 1
    return pl.pallas_call(
        _sample_topk_kernel,
        grid=(n // sb,),
        in_specs=[pl.BlockSpec((sb, ATT_HEADS, HEAD_DIM), lambda s: (s, 0, 0)),
                  pl.BlockSpec((sb, N_KV_HEADS, nblk, HEAD_DIM), lambda s: (s, 0, 0, 0))],
        out_specs=pl.BlockSpec((sb, ATT_HEADS, LANES), lambda s: (s, 0, 0)),
        out_shape=jax.ShapeDtypeStruct((n, ATT_HEADS, LANES), jnp.int32),
        compiler_params=_params("arbitrary"),
        name="sample_gate_topk",
    )(q3, kmean)


def _sample_attn_kernel(past, ppb, layer, pt_ref, idx_ref, q_ref, kn_ref, vn_ref, ck_ref, cv_ref, o_ref,
                        kbuf, vbuf, sems):
    s = pl.program_id(0)
    page = MOBA_BLOCK // ppb
    slot = s % 2

    def copies(seq, buf, h, t, p):
        kv = h // KV_GROUP
        blk = idx_ref[seq, h * MOBA_TOPK + t]
        phys = pt_ref[seq, blk * ppb + p]
        dst = pl.ds((t * ppb + p) * page, page)
        return (pltpu.make_async_copy(ck_ref.at[layer, phys, :, kv, :], kbuf.at[buf, h, dst, :],
                                      sems.at[buf, 0, h, t * ppb + p]),
                pltpu.make_async_copy(cv_ref.at[layer, phys, :, kv, :], vbuf.at[buf, h, dst, :],
                                      sems.at[buf, 1, h, t * ppb + p]))

    triples = [(h, t, p) for h in range(ATT_HEADS) for t in range(MOBA_TOPK) for p in range(ppb)]

    def start_all(seq, buf):
        for h, t, p in triples:
            ck, cv = copies(seq, buf, h, t, p)
            ck.start(priority=0)
            cv.start(priority=1)

    @pl.when(s == 0)
    def _():
        start_all(0, 0)

    @pl.when(s + 1 < pl.num_programs(0))
    def _():
        start_all(s + 1, 1 - slot)

    for h, t, p in triples:
        ck, cv = copies(s, slot, h, t, p)
        ck.wait()
        cv.wait()

    nsel = MOBA_TOPK * MOBA_BLOCK
    rowi = lax.broadcasted_iota(jnp.int32, (nsel, 1), 0)
    off = jnp.bitwise_and(rowi, MOBA_BLOCK - 1)
    for h in range(ATT_HEADS):
        kv = h // KV_GROUP
        slope = _alibi_slope(h)
        q = q_ref[0, h:h + 1, :]
        sc = jnp.sum(kbuf[slot, h] * q, axis=1, keepdims=True) * ATT_SCALE
        pos = jnp.zeros((nsel, 1), jnp.int32)
        for t in range(MOBA_TOPK):
            pos = jnp.where(rowi // MOBA_BLOCK == t, idx_ref[s, h * MOBA_TOPK + t] * MOBA_BLOCK, pos)
        dist = (past - (pos + off)).astype(F32)
        sc = sc - slope * dist
        s_own = jnp.sum(kn_ref[0, kv:kv + 1, :] * q, axis=1, keepdims=True) * ATT_SCALE
        mx = jnp.maximum(jnp.max(sc, axis=0, keepdims=True), s_own)
        p = jnp.exp(sc - mx)
        p_own = jnp.exp(s_own - mx)
        denom = jnp.sum(p, axis=0, keepdims=True) + p_own
        num = jnp.sum(p * vbuf[slot, h], axis=0, keepdims=True) + p_own * vn_ref[0, kv:kv + 1, :]
        o_ref[0, h:h + 1, :] = num / denom


def _sample_attn(q3, k_new3, v_new3, cache_k, cache_v, layer, page_table, idx):
    n, n_pages = page_table.shape
    page = cache_k.shape[2]
    ppb = MOBA_BLOCK // page
    past = n_pages * page
    assert past % MOBA_BLOCK == 0
    nsel = MOBA_TOPK * MOBA_BLOCK
    grid_spec = pltpu.PrefetchScalarGridSpec(
        num_scalar_prefetch=2,
        grid=(n,),
        in_specs=[pl.BlockSpec((1, ATT_HEADS, HEAD_DIM), lambda s, pt, ix: (s, 0, 0)),
                  pl.BlockSpec((1, N_KV_HEADS, HEAD_DIM), lambda s, pt, ix: (s, 0, 0)),
                  pl.BlockSpec((1, N_KV_HEADS, HEAD_DIM), lambda s, pt, ix: (s, 0, 0)),
                  pl.BlockSpec(memory_space=pl.ANY),
                  pl.BlockSpec(memory_space=pl.ANY)],
        out_specs=pl.BlockSpec((1, ATT_HEADS, HEAD_DIM), lambda s, pt, ix: (s, 0, 0)),
        scratch_shapes=[pltpu.VMEM((2, ATT_HEADS, nsel, HEAD_DIM), F32),
                        pltpu.VMEM((2, ATT_HEADS, nsel, HEAD_DIM), F32),
                        pltpu.SemaphoreType.DMA((2, 2, ATT_HEADS, MOBA_TOPK * ppb))],
    )
    return pl.pallas_call(
        functools.partial(_sample_attn_kernel, past, ppb, layer),
        grid_spec=grid_spec,
        out_shape=jax.ShapeDtypeStruct((n, ATT_HEADS, HEAD_DIM), F32),
        compiler_params=_params("arbitrary"),
        name="sample_moba_attn",
    )(page_table, idx, q3, k_new3, v_new3, cache_k, cache_v)


def _sample_ssd_kernel(z_ref, xbc_ref, dt_ref, buf_ref, h0_ref, cw_ref, cb_ref, dtb_ref, alog_ref, dsk_ref, g_ref,
                       y_ref, buf_out_ref, h_out_ref):
    x = xbc_ref[0]
    buf = buf_ref[0]
    cw = cw_ref[...]
    conv = cb_ref[...] + cw[CONV_WIDTH - 1:CONV_WIDTH] * x
    for t in range(CONV_WIDTH - 1):
        conv = conv + cw[t:t + 1] * buf[t:t + 1]
    buf_out_ref[0, 0:CONV_WIDTH - 2, :] = buf[1:CONV_WIDTH - 1]
    buf_out_ref[0, CONV_WIDTH - 2:CONV_WIDTH - 1, :] = x
    xc = _silu(conv)
    xs = xc[:, :D_INNER]
    bm = xc[:, D_INNER:D_INNER + SSM_GROUPS * D_STATE]
    cm = xc[:, D_INNER + SSM_GROUPS * D_STATE:]

    lane1 = lax.broadcasted_iota(jnp.int32, (1, LANES), 1)
    rowi = lax.broadcasted_iota(jnp.int32, (LANES, LANES), 0)
    dt = _softplus(dt_ref[0] + dtb_ref[...])
    a = jnp.where(lane1 < SSM_HEADS, -jnp.exp(alog_ref[...]), 0.0)
    dec = jnp.exp(dt * a)
    dt_e = _expand_heads(dt, lane1 < SSM_HEAD_DIM)
    xdt = xs * dt_e
    xdt_rows = jnp.broadcast_to(xdt, (LANES, D_INNER))

    hpg = SSM_HEADS // SSM_GROUPS
    y_parts = []
    for pair in range(SSM_HEADS // 2):
        grp = (2 * pair) // hpg
        xcol = xdt_rows[:, pair * LANES:(pair + 1) * LANES].T
        dcol = jnp.where(rowi < SSM_HEAD_DIM,
                         jnp.broadcast_to(dec[:, 2 * pair:2 * pair + 1], (LANES, LANES)),
                         jnp.broadcast_to(dec[:, 2 * pair + 1:2 * pair + 2], (LANES, LANES)))
        h0 = h0_ref[0, 2 * pair:2 * pair + 2].reshape(LANES, D_STATE)
        hn = dcol * h0 + xcol * bm[:, grp * D_STATE:(grp + 1) * D_STATE]
        h_out_ref[0, 2 * pair:2 * pair + 2] = hn.reshape(2, SSM_HEAD_DIM, D_STATE)
        cgrow = jnp.broadcast_to(cm[:, grp * D_STATE:(grp + 1) * D_STATE], (8, D_STATE))
        ypair = _dot3(cgrow, hn, NT_DIMS)
        y_parts.append(ypair[0:1])
    y = jnp.concatenate(y_parts, axis=1) + dsk_ref[...] * xs
    y_ref[0] = _gated_group_norm(y, z_ref[0], g_ref[...]).astype(y_ref.dtype)


def _ssd_sample(proj, dtp, state_conv, state_ssm, conv_w, conv_b, dt_bias, a_log, d_skip, g_ssm):
    n = proj.shape[0]
    z0 = ATT_WIDTH + 2 * KV_WIDTH
    z3 = proj[:, z0:z0 + D_INNER].reshape(n, 1, D_INNER)
    x3 = proj[:, z0 + D_INNER:z0 + D_INNER + CONV_DIM].reshape(n, 1, CONV_DIM)
    dt3 = dtp.reshape(n, 1, LANES)
    const = lambda shape: pl.BlockSpec(shape, lambda s: tuple(0 for _ in shape))
    per_seq = lambda shape: pl.BlockSpec((1,) + shape, lambda s: (s,) + tuple(0 for _ in shape))
    y, buf, h = pl.pallas_call(
        _sample_ssd_kernel,
        grid=(n,),
        in_specs=[per_seq((1, D_INNER)), per_seq((1, CONV_DIM)), per_seq((1, LANES)),
                  per_seq((CONV_WIDTH - 1, CONV_DIM)), per_seq((SSM_HEADS, SSM_HEAD_DIM, D_STATE)),
                  const((CONV_WIDTH, CONV_DIM)), const((1, CONV_DIM)), const((1, LANES)), const((1, LANES)),
                  const((1, D_INNER)), const((1, D_INNER))],
        out_specs=[per_seq((1, D_INNER)), per_seq((CONV_WIDTH - 1, CONV_DIM)),
                   per_seq((SSM_HEADS, SSM_HEAD_DIM, D_STATE))],
        out_shape=[jax.ShapeDtypeStruct((n, 1, D_INNER), BF16),
                   jax.ShapeDtypeStruct((n, CONV_WIDTH - 1, CONV_DIM), F32),
                   jax.ShapeDtypeStruct((n, SSM_HEADS, SSM_HEAD_DIM, D_STATE), F32)],
        compiler_params=_params("arbitrary"),
        name="sample_ssd",
    )(z3, x3, dt3, state_conv, state_ssm, conv_w, conv_b.reshape(1, CONV_DIM), _pad_lanes(dt_bias),
      _pad_lanes(a_log), jnp.repeat(d_skip, SSM_HEAD_DIM).reshape(1, D_INNER), g_ssm.reshape(1, D_INNER))
    return y.reshape(n, D_INNER), buf, h


def kernel(x_prompt, x_sample, cache_k, cache_v, state_conv, state_ssm, page_table, c_prompt, c_sample, w_ada, b_ada, g_mix_norm, w_in, q_gain, k_gain, g_att_out, conv_w, conv_b, dt_bias, a_log, d_skip, g_ssm_out, w_out, g_ffn_norm, w_gate, w_up, w_down):
    n_p, seq, d = x_prompt.shape
    n_s, dec_seq, _ = x_sample.shape
    assert n_p == 1 and dec_seq == 1
    depth = w_ada.shape[0]
    main_w = ATT_WIDTH + 2 * KV_WIDTH + D_INNER + CONV_DIM

    yp = x_prompt.reshape(seq, d)
    ys = x_sample.reshape(n_s, d)
    c_rows = n_p + n_s
    c_pad = -(-c_rows // 16) * 16
    c_all = jnp.pad(jnp.concatenate([c_prompt, c_sample], axis=0), ((0, c_pad - c_rows), (0, 0)))
    outs = [[] for _ in range(8)]
    for l in range(depth):
        mod = _ada(c_all, w_ada[l], b_ada[l])
        mod_p, mod_s = mod[0:1], mod[1:1 + n_s]
        w_in_t = jnp.swapaxes(w_in[l], 0, 1)

        proj, dtp = _inproj(yp, g_mix_norm[l], mod_p, w_in_t, main_w, tn=1536, tm=512, name="in_proj")
        k_out, v_out, kp, vp, qp = _prompt_prep(proj, q_gain[l], k_gain[l])
        o_att, kmean_s = _prompt_attn(qp, kp, vp, cache_k, l, page_table)
        y_ssm, tail, st = _ssd_prompt(proj, dtp, conv_w[l], conv_b[l], dt_bias[l], a_log[l], d_skip[l], g_ssm_out[l])
        x1_p, h2_p = _outproj(o_att, g_att_out[l], y_ssm, w_out[l], yp, mod_p, g_ffn_norm[l])
        hpg = SSM_HEADS // SSM_GROUPS
        ssm_p = st.reshape(SSM_GROUPS, D_STATE, hpg, SSM_HEAD_DIM).transpose(0, 2, 3, 1).reshape(
            1, SSM_HEADS, SSM_HEAD_DIM, D_STATE)
        outs[0].append(k_out.reshape(1, seq, N_KV_HEADS, HEAD_DIM))
        outs[1].append(v_out.reshape(1, seq, N_KV_HEADS, HEAD_DIM))
        outs[2].append(tail[8 - (CONV_WIDTH - 1):].reshape(1, CONV_WIDTH - 1, CONV_DIM))
        outs[3].append(ssm_p)

        proj_s, dts = _inproj(ys, g_mix_norm[l], mod_s, w_in_t, main_w, tn=768, tm=512, name="in_proj_sample",
                              split=True)
        q_s, k_s, v_s = _sample_prep(proj_s, q_gain[l], k_gain[l])
        q3 = q_s.reshape(n_s, ATT_HEADS, HEAD_DIM)
        idx = _sample_topk(q3, kmean_s)
        idx_flat = idx[:, :, :MOBA_TOPK].reshape(n_s, ATT_HEADS * MOBA_TOPK)
        o_s = _sample_attn(q3, k_s.reshape(n_s, N_KV_HEADS, HEAD_DIM), v_s.reshape(n_s, N_KV_HEADS, HEAD_DIM),
                           cache_k, cache_v, l, page_table, idx_flat)
        y_s, buf_s, h_s = _ssd_sample(proj_s, dts, state_conv[l], state_ssm[l], conv_w[l], conv_b[l], dt_bias[l],
                                      a_log[l], d_skip[l], g_ssm_out[l])
        x1_s, h2_s = _outproj(o_s.reshape(n_s, ATT_WIDTH), g_att_out[l], y_s, w_out[l], ys, mod_s, g_ffn_norm[l])

        hid_p, hid_s = _gateup(h2_p, h2_s, w_gate[l], w_up[l])
        yp, ys = _down(hid_p, hid_s, w_down[l], x1_p, x1_s, mod_p, mod_s)
        outs[4].append(k_s.reshape(n_s, 1, N_KV_HEADS, HEAD_DIM))
        outs[5].append(v_s.reshape(n_s, 1, N_KV_HEADS, HEAD_DIM))
        outs[6].append(buf_s)
        outs[7].append(h_s)
    stacked = [jnp.stack(o) for o in outs]
    return (yp.reshape(1, seq, d), ys.reshape(n_s, 1, d), *stacked)
```
